```python
import jax, jax.numpy as jnp
from jax import lax
import numpy as np

D_MODEL = 2048
BATCH = 2
SEQ = 4096
DEPTH = 2
DEC_BATCH = 8
DEC_SEQ = 1
PAST_LEN = 16384
PAGE_SIZE = 128

ML_HEADS = 8
ML_DQK = 128
ML_DV = 256
ML_CHUNK = 64
SW_GROUPS = ((128, 1), (512, 4), (2048, 16))
SW_HEADS_PER_GROUP = 4
SW_HEADS = 12
SW_HD = 128
SW_ROT = SW_HD // 4
ROPE_THETA = 500000.0
CM_CHUNK = 128
CM_GROUPS = 8
CM_WIDTH = 2048
CM_GD = CM_WIDTH // CM_GROUPS
N_BRANCH = 3
D_FF = -(-8 * D_MODEL // (3 * 256)) * 256
IN_WIDTH = 2 * ML_HEADS * ML_DQK + 2 * ML_HEADS * ML_DV + 2 * ML_HEADS + 3 * SW_HEADS * SW_HD + 2 * CM_WIDTH + N_BRANCH * D_MODEL
RMS_EPS = 1e-6
NEG = -1e30

kernel_name = 'hybrid_mlstm_dilated_swa_chunkmlp_decode_step'


def rmsnorm(x, g):
    xf = x.astype(jnp.float32)
    y = xf * lax.rsqrt(jnp.mean(xf * xf, axis=-1, keepdims=True) + RMS_EPS)
    return (y * g.astype(jnp.float32)).astype(x.dtype)


def partial_rope(x, pos):
    half = SW_ROT // 2
    freq = ROPE_THETA ** (-jnp.arange(half, dtype=jnp.float32) * 2.0 / SW_ROT)
    ang = pos.astype(jnp.float32)[:, None] * freq[None, :]
    cos = jnp.cos(ang)[None, :, None, :]
    sin = jnp.sin(ang)[None, :, None, :]
    xf = x.astype(jnp.float32)
    x1, x2, rest = xf[..., :half], xf[..., half:SW_ROT], xf[..., SW_ROT:]
    out = jnp.concatenate([x1 * cos - x2 * sin, x1 * sin + x2 * cos, rest], axis=-1)
    return out.astype(x.dtype)


def mlstm_chunkwise(q, k, v, ig, lf, C0, n0, m0):
    f32 = jnp.float32
    B, T, H, _ = q.shape
    L = min(ML_CHUNK, T)
    nc = -(-T // L)
    pad = nc * L - T

    def prep(a, fill):
        a = a.astype(f32)
        a = jnp.pad(a, [(0, 0), (0, pad)] + [(0, 0)] * (a.ndim - 2), constant_values=fill)
        return jnp.moveaxis(a.reshape((B, nc, L) + a.shape[2:]), 1, 0)

    xs = (prep(q, 0.0), prep(k, 0.0), prep(v, 0.0), prep(ig, NEG), prep(lf, 0.0))
    causal = jnp.tril(jnp.ones((L, L), dtype=bool))[None, :, :, None]

    def step(carry, xc):
        C, n, m = carry
        qc, kc, vc, igc, lfc = xc
        F = jnp.cumsum(lfc, axis=1)
        logD = F[:, :, None, :] - F[:, None, :, :] + igc[:, None, :, :]
        logD = jnp.where(causal, logD, NEG)
        inter = F + m[:, None, :]
        m_t = jnp.maximum(inter, jnp.max(logD, axis=2))
        Dw = jnp.exp(logD - m_t[:, :, None, :])
        w_inter = jnp.exp(inter - m_t)
        s = jnp.einsum('bthd,bshd->btsh', qc, kc) * Dw
        num = jnp.einsum('btsh,bshv->bthv', s, vc) + w_inter[..., None] * jnp.einsum('bthd,bhdv->bthv', qc, C)
        den = jnp.sum(s, axis=2) + w_inter * jnp.einsum('bthd,bhd->bth', qc, n)
        h = num / jnp.maximum(jnp.abs(den), jnp.exp(-m_t))[..., None]
        m_new = m_t[:, -1]
        w_last = jnp.exp(F[:, -1:] - F + igc - m_new[:, None, :])
        decay = jnp.exp(F[:, -1] + m - m_new)
        C_new = decay[..., None, None] * C + jnp.einsum('bsh,bshd,bshv->bhdv', w_last, kc, vc)
        n_new = decay[..., None] * n + jnp.einsum('bsh,bshd->bhd', w_last, kc)
        return (C_new, n_new, m_new), h

    (C, n, m), hs = lax.scan(step, (C0.astype(f32), n0.astype(f32), m0.astype(f32)), xs)
    h = jnp.moveaxis(hs, 0, 1).reshape(B, nc * L, H, -1)[:, :T]
    return h, C, n, m


def dilated_window_prompt(q, k, v, win, dil):
    f32 = jnp.float32
    B, T, Hg, hd = q.shape
    blk = win // dil
    span = dil * blk
    Tp = -(-T // span) * span
    Ld = Tp // dil
    nb = Ld // blk

    def blocks(a):
        a = jnp.pad(a, ((0, 0), (0, Tp - T), (0, 0), (0, 0)))
        a = a.reshape(B, Ld, dil, Hg, hd).transpose(0, 2, 1, 3, 4)
        return a.reshape(B, dil, nb, blk, Hg, hd)

    def with_prev(a):
        prev = jnp.pad(a[:, :, :-1], ((0, 0), (0, 0), (1, 0), (0, 0), (0, 0), (0, 0)))
        return jnp.concatenate([prev, a], axis=3)

    qb = blocks(q)
    kk = with_prev(blocks(k))
    vv = with_prev(blocks(v))
    s = jnp.einsum('brnqhd,brnkhd->brnhqk', qb, kk, preferred_element_type=f32) * (SW_HD ** -0.5)
    qi = jnp.arange(blk)[:, None]
    ki = jnp.arange(2 * blk)[None, :]
    dist = blk + qi - ki
    band = (dist >= 0) & (dist <= blk)
    has_prev = (jnp.arange(nb) > 0)[:, None, None]
    valid = band[None] & ((ki >= blk)[None] | has_prev)
    s = jnp.where(valid[None, None, :, None], s, NEG)
    lse = jax.nn.logsumexp(s, axis=-1)
    p = jnp.exp(s - lse[..., None])
    o = jnp.einsum('brnhqk,brnkhd->brnqhd', p.astype(v.dtype), vv, preferred_element_type=f32)
    o = o.reshape(B, dil, Ld, Hg, hd).transpose(0, 2, 1, 3, 4).reshape(B, Tp, Hg, hd)[:, :T]
    lse = lse.transpose(0, 1, 2, 4, 3).reshape(B, dil, Ld, Hg).transpose(0, 2, 1, 3).reshape(B, Tp, Hg)[:, :T]
    return o, lse


def dilated_window_sample(q, k_all, v_all, win, dil):
    f32 = jnp.float32
    B, S, Hg, hd = q.shape
    buf = k_all.shape[1] - S
    nk = win // dil + 1
    idx = buf + jnp.arange(S)[:, None] - dil * jnp.arange(nk)[None, :]
    valid = idx >= 0
    idx = jnp.maximum(idx, 0)
    kg = k_all[:, idx]
    vg = v_all[:, idx]
    s = jnp.einsum('bshd,bskhd->bshk', q, kg, preferred_element_type=f32) * (SW_HD ** -0.5)
    s = jnp.where(valid[None, :, None, :], s, NEG)
    lse = jax.nn.logsumexp(s, axis=-1)
    p = jnp.exp(s - lse[..., None])
    o = jnp.einsum('bshk,bskhd->bshd', p.astype(v_all.dtype), vg, preferred_element_type=f32)
    return o, lse


def chunk_spatial_gating(u, v, g_v, w_s, b_s):
    B, T, _ = u.shape
    nc = -(-T // CM_CHUNK)
    Tp = nc * CM_CHUNK
    v_n = rmsnorm(v.reshape(B, T, CM_GROUPS, CM_GD), g_v)
    vp = jnp.pad(v_n, ((0, 0), (0, Tp - T), (0, 0), (0, 0))).reshape(B, nc, CM_CHUNK, CM_GROUPS, CM_GD)
    w_causal = w_s * jnp.tril(jnp.ones((CM_CHUNK, CM_CHUNK), w_s.dtype))[None]
    mixed = jnp.einsum('gts,bnsgc->bntgc', w_causal, vp) + b_s.T[None, None, :, :, None]
    mixed = mixed.reshape(B, Tp, CM_WIDTH)[:, :T]
    return u * mixed, v_n


def _in_split_points():
    widths = [ML_HEADS * ML_DQK, ML_HEADS * ML_DQK, ML_HEADS * ML_DV, ML_HEADS * ML_DV, ML_HEADS, ML_HEADS,
              SW_HEADS * SW_HD, SW_HEADS * SW_HD, SW_HEADS * SW_HD, CM_WIDTH, CM_WIDTH]
    return np.cumsum(widths).tolist()


def mixer_branches(h, pos, C0, n0, m0, swa_bufs, w_in_l, b_i_l, b_f_l, g_v_l, w_s_l, b_s_l,
                   w_br_a_l, w_br_b_l, w_br_c_l, w_out_l):
    f32 = jnp.float32
    B, T, _ = h.shape
    proj = h @ w_in_l
    (q_a, k_a, v_a, o_a, i_a, f_a, q_b, k_b, v_b, u_c, v_c, g_all) = jnp.split(proj, _in_split_points(), axis=-1)
    q_a = q_a.reshape(B, T, ML_HEADS, ML_DQK)
    k_a = k_a.reshape(B, T, ML_HEADS, ML_DQK) * (ML_DQK ** -0.5)
    v_a = v_a.reshape(B, T, ML_HEADS, ML_DV)
    ig = i_a.astype(f32) + b_i_l.astype(f32)
    lf = jax.nn.log_sigmoid(f_a.astype(f32) + b_f_l.astype(f32))
    h_a, C1, n1, m1 = mlstm_chunkwise(q_a, k_a, v_a, ig, lf, C0, n0, m0)
    out_a = (jax.nn.sigmoid(o_a) * h_a.reshape(B, T, ML_HEADS * ML_DV).astype(h.dtype)) @ w_br_a_l
    q_b = partial_rope(q_b.reshape(B, T, SW_HEADS, SW_HD), pos)
    k_b = partial_rope(k_b.reshape(B, T, SW_HEADS, SW_HD), pos)
    v_b = v_b.reshape(B, T, SW_HEADS, SW_HD)
    outs, lses, kv_new = [], [], []
    for gi, (win, dil) in enumerate(SW_GROUPS):
        sl = slice(gi * SW_HEADS_PER_GROUP, (gi + 1) * SW_HEADS_PER_GROUP)
        qg, kg, vg = q_b[:, :, sl], k_b[:, :, sl], v_b[:, :, sl]
        if swa_bufs is None:
            o_g, lse_g = dilated_window_prompt(qg, kg, vg, win, dil)
            kv_new.append(jnp.stack([kg, vg], axis=2)[:, T - min(win, T):])
        else:
            buf = swa_bufs[gi]
            k_all = jnp.concatenate([buf[:, :, 0], kg], axis=1)
            v_all = jnp.concatenate([buf[:, :, 1], vg], axis=1)
            o_g, lse_g = dilated_window_sample(qg, k_all, v_all, win, dil)
            kv_new.append(jnp.stack([kg, vg], axis=2))
        outs.append(o_g)
        lses.append(lse_g)
    wgt = jax.nn.softmax(jnp.stack(lses, axis=0), axis=0)
    o_b = jnp.sum(wgt[..., None] * jnp.stack(outs, axis=0), axis=0)
    out_b = o_b.reshape(B, T, SW_HEADS_PER_GROUP * SW_HD).astype(h.dtype) @ w_br_b_l
    gated, v_rows = chunk_spatial_gating(u_c, v_c, g_v_l, w_s_l, b_s_l)
    out_c = gated @ w_br_c_l
    g = jax.nn.sigmoid(g_all).reshape(B, T, N_BRANCH, D_MODEL)
    merged = g[:, :, 0] * out_a + g[:, :, 1] * out_b + g[:, :, 2] * out_c
    return merged @ w_out_l, (C1, n1, m1), kv_new, v_rows


def trunk(x, c, pos, C0s, n0s, m0s, swa_bufs, w_ada, b_ada, g_norm1, g_norm2, g_final, w_in, b_ml_i, b_ml_f,
          g_cm_v, w_s, b_s, w_br_a, w_br_b, w_br_c, w_out, w_ffn_gate_up, w_ffn_down):
    new_C, new_n, new_m, new_v = [], [], [], []
    new_kv = [[] for _ in SW_GROUPS]
    for l in range(DEPTH):
        mod = jax.nn.silu(c) @ w_ada[l] + b_ada[l]
        sh1, sc1, gt1, sh2, sc2, gt2 = [t_[:, None, :] for t_ in jnp.split(mod, 6, axis=-1)]
        h = rmsnorm(x, g_norm1[l]) * (1 + sc1) + sh1
        bufs_l = None if swa_bufs is None else [b_[l] for b_ in swa_bufs]
        mix, (C1, n1, m1), kv_l, v_rows = mixer_branches(
            h, pos, C0s[l], n0s[l], m0s[l], bufs_l, w_in[l], b_ml_i[l], b_ml_f[l], g_cm_v[l], w_s[l], b_s[l],
            w_br_a[l], w_br_b[l], w_br_c[l], w_out[l])
        x = x + gt1 * mix
        h = rmsnorm(x, g_norm2[l]) * (1 + sc2) + sh2
        gate, up = jnp.split(h @ w_ffn_gate_up[l], 2, axis=-1)
        x = x + gt2 * ((jax.nn.silu(gate) * up) @ w_ffn_down[l])
        new_C.append(C1.astype(x.dtype))
        new_n.append(n1.astype(x.dtype))
        new_m.append(m1.astype(x.dtype))
        for gi in range(len(SW_GROUPS)):
            new_kv[gi].append(kv_l[gi])
        new_v.append(v_rows)
    y = rmsnorm(x, g_final)
    return y, jnp.stack(new_C), jnp.stack(new_n), jnp.stack(new_m), [jnp.stack(a_) for a_ in new_kv], jnp.stack(new_v)


def setup_inputs(seed: int = 0) -> dict:
    key = jax.random.key(seed)
    ks = jax.random.split(key, 32)
    f32 = jnp.float32

    def nrm(k, shape, s):
        return s * jax.random.normal(k, shape, f32)

    def kv_buf(k, win):
        return nrm(k, (DEPTH, DEC_BATCH, min(win, PAST_LEN), 2, SW_HEADS_PER_GROUP, SW_HD), 1.0)

    return {
        'x_prompt': nrm(ks[0], (BATCH, SEQ, D_MODEL), 1.0),
        'x_sample': nrm(ks[1], (DEC_BATCH, DEC_SEQ, D_MODEL), 1.0),
        'state_mlstm_C': nrm(ks[2], (DEPTH, DEC_BATCH, ML_HEADS, ML_DQK, ML_DV), 0.1),
        'state_mlstm_n': nrm(ks[3], (DEPTH, DEC_BATCH, ML_HEADS, ML_DQK), 0.5),
        'state_mlstm_m': nrm(ks[4], (DEPTH, DEC_BATCH, ML_HEADS), 0.5),
        'cache_swa_kv_w128': kv_buf(ks[5], SW_GROUPS[0][0]),
        'cache_swa_kv_w512': kv_buf(ks[6], SW_GROUPS[1][0]),
        'cache_swa_kv_w2048': kv_buf(ks[7], SW_GROUPS[2][0]),
        'c_prompt': nrm(ks[8], (BATCH, D_MODEL), 1.0),
        'c_sample': nrm(ks[9], (DEC_BATCH, D_MODEL), 1.0),
        'w_ada': nrm(ks[10], (DEPTH, D_MODEL, 6 * D_MODEL), 0.5 * D_MODEL ** -0.5),
        'b_ada': nrm(ks[11], (DEPTH, 6 * D_MODEL), 0.02),
        'g_norm1': 1.0 + nrm(ks[12], (DEPTH, D_MODEL), 0.02),
        'g_norm2': 1.0 + nrm(ks[13], (DEPTH, D_MODEL), 0.02),
        'g_final': 1.0 + nrm(ks[14], (D_MODEL,), 0.02),
        'w_in': nrm(ks[15], (DEPTH, D_MODEL, IN_WIDTH), D_MODEL ** -0.5),
        'b_ml_i': nrm(ks[16], (DEPTH, ML_HEADS), 0.1),
        'b_ml_f': 3.0 + nrm(ks[17], (DEPTH, ML_HEADS), 0.5),
        'g_cm_v': 1.0 + nrm(ks[18], (DEPTH, CM_GROUPS, CM_GD), 0.02),
        'w_s': nrm(ks[19], (DEPTH, CM_GROUPS, CM_CHUNK, CM_CHUNK), CM_CHUNK ** -0.5),
        'b_s': 1.0 + nrm(ks[20], (DEPTH, CM_GROUPS, CM_CHUNK), 0.02),
        'w_br_a': nrm(ks[21], (DEPTH, ML_HEADS * ML_DV, D_MODEL), (ML_HEADS * ML_DV) ** -0.5),
        'w_br_b': nrm(ks[22], (DEPTH, SW_HEADS_PER_GROUP * SW_HD, D_MODEL), (SW_HEADS_PER_GROUP * SW_HD) ** -0.5),
        'w_br_c': nrm(ks[23], (DEPTH, CM_WIDTH, D_MODEL), CM_WIDTH ** -0.5),
        'w_out': nrm(ks[24], (DEPTH, D_MODEL, D_MODEL), D_MODEL ** -0.5),
        'w_ffn_gate_up': nrm(ks[25], (DEPTH, D_MODEL, 2 * D_FF), D_MODEL ** -0.5),
        'w_ffn_down': nrm(ks[26], (DEPTH, D_FF, D_MODEL), D_FF ** -0.5),
    }


def reference(x_prompt, x_sample, state_mlstm_C, state_mlstm_n, state_mlstm_m, cache_swa_kv_w128,
              cache_swa_kv_w512, cache_swa_kv_w2048, c_prompt, c_sample, w_ada, b_ada, g_norm1, g_norm2, g_final,
              w_in, b_ml_i, b_ml_f, g_cm_v, w_s, b_s, w_br_a, w_br_b, w_br_c, w_out, w_ffn_gate_up, w_ffn_down):
    Bp, Tp, _ = x_prompt.shape
    Ts = x_sample.shape[1]
    dt = x_prompt.dtype
    zC = jnp.zeros((DEPTH, Bp, ML_HEADS, ML_DQK, ML_DV), dt)
    zn = jnp.zeros((DEPTH, Bp, ML_HEADS, ML_DQK), dt)
    zm = jnp.zeros((DEPTH, Bp, ML_HEADS), dt)
    pos_p = jnp.arange(Tp, dtype=jnp.int32)
    y_prompt, p_C, p_n, p_m, p_kv, _ = trunk(
        x_prompt, c_prompt, pos_p, zC, zn, zm, None, w_ada, b_ada, g_norm1, g_norm2, g_final, w_in, b_ml_i,
        b_ml_f, g_cm_v, w_s, b_s, w_br_a, w_br_b, w_br_c, w_out, w_ffn_gate_up, w_ffn_down)
    pos_s = PAST_LEN + jnp.arange(Ts, dtype=jnp.int32)
    y_sample, s_C, s_n, s_m, s_kv, s_cm_v = trunk(
        x_sample, c_sample, pos_s, state_mlstm_C, state_mlstm_n, state_mlstm_m,
        [cache_swa_kv_w128, cache_swa_kv_w512, cache_swa_kv_w2048], w_ada, b_ada, g_norm1, g_norm2, g_final,
        w_in, b_ml_i, b_ml_f, g_cm_v, w_s, b_s, w_br_a, w_br_b, w_br_c, w_out, w_ffn_gate_up, w_ffn_down)
    p_kv128, p_kv512, p_kv2048 = p_kv
    s_kv128, s_kv512, s_kv2048 = s_kv
    return (y_prompt, y_sample, p_C, p_n, p_m, p_kv128, p_kv512, p_kv2048,
            s_C, s_n, s_m, s_kv128, s_kv512, s_kv2048, s_cm_v)
```

```python
import functools

import jax
import jax.numpy as jnp
from jax import lax
from jax.experimental import pallas as pl
from jax.experimental.pallas import tpu as pltpu

BF = jnp.bfloat16
F32 = jnp.float32

D_MODEL = 2048
ML_HEADS = 8
ML_DQK = 128
ML_DV = 256
SW_GROUPS = ((128, 1), (512, 4), (2048, 16))
SW_HPG = 4
SW_HEADS = 12
SW_HD = 128
SW_ROT = SW_HD // 4
SW_BLK = 128
ROPE_THETA = 500000.0
CM_CHUNK = 128
CM_GROUPS = 8
CM_GD = 256
D_FF = 5632
RMS_EPS = 1e-6
NEG = -1e30

C_QA, C_KA, C_VA, C_OA, C_UC, C_VC, C_G, C_QB, C_KB, C_VB = (
    0, 1024, 2048, 4096, 6144, 8192, 10240, 16384, 17920, 19456)
N_MAIN = 21504
N_GATE = 128
O_GATES, O_QB, O_UC, O_G = 6144, 6160, 10768, 14864

TM = 512
ML_L = 128
VMEM_LIMIT = 48 * 1024 * 1024


def _cparams(*sem):
    return pltpu.CompilerParams(dimension_semantics=sem, vmem_limit_bytes=VMEM_LIMIT)


def _sigmoid(x):
    return 1.0 / (1.0 + jnp.exp(-x))


def _log_sigmoid(x):
    return jnp.minimum(x, 0.0) - jnp.log1p(jnp.exp(-jnp.abs(x)))


def _ada_kernel(c_ref, w_ref, b_ref, o_ref):
    c = c_ref[...]
    a = (c * _sigmoid(c)).astype(BF)
    o_ref[...] = jnp.dot(a, w_ref[...].astype(BF), preferred_element_type=F32) + b_ref[...]


def _ada(c_all, w_ada, b_ada):
    depth, d, n = w_ada.shape
    rows = c_all.shape[0]
    tn = 1024
    return pl.pallas_call(
        _ada_kernel,
        grid=(depth, n // tn),
        in_specs=[pl.BlockSpec((rows, d), lambda l, j: (0, 0)),
                  pl.BlockSpec((None, d, tn), lambda l, j: (l, 0, j)),
                  pl.BlockSpec((None, 1, tn), lambda l, j: (l, 0, j))],
        out_specs=pl.BlockSpec((None, rows, tn), lambda l, j: (l, 0, j)),
        out_shape=jax.ShapeDtypeStruct((depth, rows, n), F32),
        compiler_params=_cparams("parallel", "parallel"),
        name="ada",
    )(c_all, w_ada, b_ada.reshape(depth, 1, n))


def _mod_spec(mod, chunk, width, tm, rows_per_batch, row_of, col_of):
    per = D_MODEL // width
    if mod.ndim == 3:
        tiles_per_batch = rows_per_batch // tm
        return pl.BlockSpec((None, 1, width),
                            lambda *g: (row_of(*g) // tiles_per_batch, 0, chunk * per + col_of(*g)))
    return pl.BlockSpec((tm, width), lambda *g: (row_of(*g), chunk * per + col_of(*g)))


def _norm_mod_kernel(x_ref, g_ref, sc_ref, sh_ref, o_ref):
    x = x_ref[...]
    y = x * lax.rsqrt(jnp.mean(x * x, axis=1, keepdims=True) + RMS_EPS) * g_ref[...]
    o_ref[...] = (y * (1.0 + sc_ref[...]) + sh_ref[...]).astype(o_ref.dtype)


def _norm_kernel(x_ref, g_ref, o_ref):
    x = x_ref[...]
    o_ref[...] = x * lax.rsqrt(jnp.mean(x * x, axis=1, keepdims=True) + RMS_EPS) * g_ref[...]


def _norm_mod(x, g, mod, sc_chunk, sh_chunk, rows_per_batch):
    m, d = x.shape
    tm = min(256, m)
    row_of = lambda i: i
    col_of = lambda i: 0
    return pl.pallas_call(
        _norm_mod_kernel,
        grid=(m // tm,),
        in_specs=[pl.BlockSpec((tm, d), lambda i: (i, 0)),
                  pl.BlockSpec((1, d), lambda i: (0, 0)),
                  _mod_spec(mod, sc_chunk, d, tm, rows_per_batch, row_of, col_of),
                  _mod_spec(mod, sh_chunk, d, tm, rows_per_batch, row_of, col_of)],
        out_specs=pl.BlockSpec((tm, d), lambda i: (i, 0)),
        out_shape=jax.ShapeDtypeStruct((m, d), BF),
        compiler_params=_cparams("parallel"),
        name="norm_mod",
    )(x, g.reshape(1, d), mod, mod)


def _norm(x, g):
    m, d = x.shape
    tm = min(256, m)
    return pl.pallas_call(
        _norm_kernel,
        grid=(m // tm,),
        in_specs=[pl.BlockSpec((tm, d), lambda i: (i, 0)),
                  pl.BlockSpec((1, d), lambda i: (0, 0))],
        out_specs=pl.BlockSpec((tm, d), lambda i: (i, 0)),
        out_shape=jax.ShapeDtypeStruct((m, d), F32),
        compiler_params=_cparams("parallel"),
        name="norm_final",
    )(x, g.reshape(1, d))


def _cast_weights_once(pairs):
    @pl.when(pl.program_id(1) == 0)
    def _():
        for src, dst in pairs:
            dst[...] = src[...].astype(BF)


def _mm_kernel(x_ref, w_ref, o_ref, wbf):
    _cast_weights_once([(w_ref, wbf)])
    o_ref[...] = jnp.dot(x_ref[...], wbf[...], preferred_element_type=F32).astype(o_ref.dtype)


def _mm(x, w3, l, tn, out_dtype, name):
    m, k = x.shape
    n = w3.shape[2]
    tm = min(TM, m)
    return pl.pallas_call(
        _mm_kernel,
        grid=(n // tn, m // tm),
        in_specs=[pl.BlockSpec((tm, k), lambda j, i: (i, 0)),
                  pl.BlockSpec((None, k, tn), lambda j, i: (l, 0, j))],
        out_specs=pl.BlockSpec((tm, tn), lambda j, i: (i, j)),
        out_shape=jax.ShapeDtypeStruct((m, n), out_dtype),
        scratch_shapes=[pltpu.VMEM((k, tn), BF)],
        compiler_params=_cparams("arbitrary", "arbitrary"),
        name=name,
    )(x, w3)


def _mm_res_kernel(x_ref, w_ref, r_ref, g_ref, o_ref, wbf):
    _cast_weights_once([(w_ref, wbf)])
    acc = jnp.dot(x_ref[...], wbf[...], preferred_element_type=F32)
    o_ref[...] = r_ref[...] + g_ref[...] * acc


def _mm_res(x, w3, l, res, mod, gate_chunk, rows_per_batch, tn, name, tm=TM):
    m, k = x.shape
    n = w3.shape[2]
    tm = min(tm, m)
    return pl.pallas_call(
        _mm_res_kernel,
        grid=(n // tn, m // tm),
        in_specs=[pl.BlockSpec((tm, k), lambda j, i: (i, 0)),
                  pl.BlockSpec((None, k, tn), lambda j, i: (l, 0, j)),
                  pl.BlockSpec((tm, tn), lambda j, i: (i, j)),
                  _mod_spec(mod, gate_chunk, tn, tm, rows_per_batch, lambda j, i: i, lambda j, i: j)],
        out_specs=pl.BlockSpec((tm, tn), lambda j, i: (i, j)),
        out_shape=jax.ShapeDtypeStruct((m, n), F32),
        scratch_shapes=[pltpu.VMEM((k, tn), BF)],
        compiler_params=_cparams("arbitrary", "arbitrary"),
        name=name,
    )(x, w3, res, mod)


def _ffn1_kernel(x_ref, wg_ref, wu_ref, o_ref, wgbf, wubf):
    _cast_weights_once([(wg_ref, wgbf), (wu_ref, wubf)])
    x = x_ref[...]
    g = jnp.dot(x, wgbf[...], preferred_element_type=F32)
    u = jnp.dot(x, wubf[...], preferred_element_type=F32)
    o_ref[...] = (g * _sigmoid(g) * u).astype(o_ref.dtype)


def _ffn1(x, w_gu, l):
    m, k = x.shape
    tn = 512
    nj = D_FF // tn
    tm = min(TM, m)
    return pl.pallas_call(
        _ffn1_kernel,
        grid=(nj, m // tm),
        in_specs=[pl.BlockSpec((tm, k), lambda j, i: (i, 0)),
                  pl.BlockSpec((None, k, tn), lambda j, i: (l, 0, j)),
                  pl.BlockSpec((None, k, tn), lambda j, i: (l, 0, nj + j))],
        out_specs=pl.BlockSpec((tm, tn), lambda j, i: (i, j)),
        out_shape=jax.ShapeDtypeStruct((m, D_FF), BF),
        scratch_shapes=[pltpu.VMEM((k, tn), BF), pltpu.VMEM((k, tn), BF)],
        compiler_params=_cparams("arbitrary", "arbitrary"),
        name="ffn_gate_up",
    )(x, w_gu, w_gu)


def _merge_kernel(xa_ref, xb_ref, xc_ref, g0_ref, g1_ref, g2_ref, wa_ref, wb_ref, wc_ref, o_ref,
                  wabf, wbbf, wcbf):
    _cast_weights_once([(wa_ref, wabf), (wb_ref, wbbf), (wc_ref, wcbf)])
    da = jnp.dot(xa_ref[...].astype(BF), wabf[...], preferred_element_type=F32)
    db = jnp.dot(xb_ref[...].astype(BF), wbbf[...], preferred_element_type=F32)
    dc = jnp.dot(xc_ref[...].astype(BF), wcbf[...], preferred_element_type=F32)
    o_ref[...] = (_sigmoid(g0_ref[...].astype(F32)) * da + _sigmoid(g1_ref[...].astype(F32)) * db
                  + _sigmoid(g2_ref[...].astype(F32)) * dc).astype(o_ref.dtype)


def _merge(xa, xb, xc, p, w_a, w_b, w_c, l):
    m = xa.shape[0]
    tn = 512
    tm = min(TM, m)
    ka, kb, kc = xa.shape[1], xb.shape[1], xc.shape[1]
    g_blk = C_G // tn
    per = D_MODEL // tn
    return pl.pallas_call(
        _merge_kernel,
        grid=(D_MODEL // tn, m // tm),
        in_specs=[pl.BlockSpec((tm, ka), lambda j, i: (i, 0)),
                  pl.BlockSpec((tm, kb), lambda j, i: (i, 0)),
                  pl.BlockSpec((tm, kc), lambda j, i: (i, 0)),
                  pl.BlockSpec((tm, tn), lambda j, i: (i, g_blk + j)),
                  pl.BlockSpec((tm, tn), lambda j, i: (i, g_blk + per + j)),
                  pl.BlockSpec((tm, tn), lambda j, i: (i, g_blk + 2 * per + j)),
                  pl.BlockSpec((None, ka, tn), lambda j, i: (l, 0, j)),
                  pl.BlockSpec((None, kb, tn), lambda j, i: (l, 0, j)),
                  pl.BlockSpec((None, kc, tn), lambda j, i: (l, 0, j))],
        out_specs=pl.BlockSpec((tm, tn), lambda j, i: (i, j)),
        out_shape=jax.ShapeDtypeStruct((m, D_MODEL), BF),
        scratch_shapes=[pltpu.VMEM((ka, tn), BF), pltpu.VMEM((kb, tn), BF), pltpu.VMEM((kc, tn), BF)],
        compiler_params=_cparams("arbitrary", "arbitrary"),
        name="branch_merge",
    )(xa, xb, xc, p, p, p, w_a, w_b, w_c)


def _mlstm_kernel(q_ref, k_ref, v_ref, o_ref, gc_ref, gr_ref, brow_ref, bcol_ref,
                  xa_ref, c_ref, n_ref, m_ref):
    @pl.when(pl.program_id(1) == 0)
    def _():
        c_ref[...] = jnp.zeros_like(c_ref)
        n_ref[...] = jnp.zeros_like(n_ref)
        m_ref[...] = jnp.zeros_like(m_ref)

    L = q_ref.shape[0]
    scale = ML_DQK ** -0.5
    row = lax.broadcasted_iota(jnp.int32, (L, L), 0)
    col = lax.broadcasted_iota(jnp.int32, (L, L), 1)
    tri = col <= row
    gc = gc_ref[...] + brow_ref[...]
    gr = gr_ref[...] + bcol_ref[...]
    for h in range(ML_HEADS):
        ig_c = gc[:, h:h + 1]
        lf_c = _log_sigmoid(gc[:, ML_HEADS + h:ML_HEADS + h + 1])
        ig_r = gr[h:h + 1, :]
        lf_r = _log_sigmoid(gr[ML_HEADS + h:ML_HEADS + h + 1, :])
        f_col = jnp.sum(jnp.where(tri, lf_r, 0.0), axis=1, keepdims=True)
        f_row = jnp.sum(jnp.where(row <= col, lf_c, 0.0), axis=0, keepdims=True)
        m_prev = m_ref[h:h + 1, 0:1]
        log_d = jnp.where(tri, f_col - f_row + ig_r, NEG)
        inter = f_col + m_prev
        m_t = jnp.maximum(inter, jnp.max(log_d, axis=1, keepdims=True))
        d_w = jnp.exp(log_d - m_t)
        w_inter = jnp.exp(inter - m_t)
        q = q_ref[:, h * ML_DQK:(h + 1) * ML_DQK]
        k = k_ref[:, h * ML_DQK:(h + 1) * ML_DQK]
        v = v_ref[:, h * ML_DV:(h + 1) * ML_DV]
        s = lax.dot_general(q, k, (((1,), (1,)), ((), ())), preferred_element_type=F32) * (scale * d_w)
        c_old = c_ref[h]
        n_old = n_ref[h:h + 1, :]
        num = (jnp.dot(s.astype(BF), v, preferred_element_type=F32)
               + w_inter * jnp.dot(q, c_old.astype(BF), preferred_element_type=F32))
        den = (jnp.sum(s, axis=1, keepdims=True)
               + w_inter * jnp.sum(q.astype(F32) * n_old, axis=1, keepdims=True))
        hh = num / jnp.maximum(jnp.abs(den), jnp.exp(-m_t))
        og = o_ref[:, h * ML_DV:(h + 1) * ML_DV].astype(F32)
        xa_ref[:, h * ML_DV:(h + 1) * ML_DV] = (_sigmoid(og) * hh).astype(xa_ref.dtype)
        m_new = m_t[L - 1:L, :]
        f_last = f_col[L - 1:L, :]
        w_last = jnp.exp(f_last - f_col + ig_c - m_new) * scale
        decay = jnp.exp(f_last + m_prev - m_new)
        kw = k.astype(F32) * w_last
        c_ref[h] = decay * c_old + jnp.dot(kw.T.astype(BF), v, preferred_element_type=F32)
        n_ref[h:h + 1, :] = decay * n_old + jnp.sum(kw, axis=0, keepdims=True)
        m_ref[h:h + 1, :] = jnp.broadcast_to(m_new, (1, ML_DQK))


def _mlstm_prompt(p, gates, b_i, b_f, batch, seq):
    m_rows = p.shape[0]
    L = ML_L
    nc = seq // L
    bias = jnp.concatenate([b_i, b_f]).astype(F32)
    bias_row = jnp.zeros((1, N_GATE), F32).at[0, :2 * ML_HEADS].set(bias)
    bias_col = bias.reshape(2 * ML_HEADS, 1)
    gates_t = gates[:, :2 * ML_HEADS].T
    hq, hv = ML_HEADS * ML_DQK, ML_HEADS * ML_DV
    return pl.pallas_call(
        _mlstm_kernel,
        grid=(batch, nc),
        in_specs=[pl.BlockSpec((L, hq), lambda b, c: (b * nc + c, C_QA // hq)),
                  pl.BlockSpec((L, hq), lambda b, c: (b * nc + c, C_KA // hq)),
                  pl.BlockSpec((L, hv), lambda b, c: (b * nc + c, C_VA // hv)),
                  pl.BlockSpec((L, hv), lambda b, c: (b * nc + c, C_OA // hv)),
                  pl.BlockSpec((L, N_GATE), lambda b, c: (b * nc + c, 0)),
                  pl.BlockSpec((2 * ML_HEADS, L), lambda b, c: (0, b * nc + c)),
                  pl.BlockSpec((1, N_GATE), lambda b, c: (0, 0)),
                  pl.BlockSpec((2 * ML_HEADS, 1), lambda b, c: (0, 0))],
        out_specs=[pl.BlockSpec((L, hv), lambda b, c: (b * nc + c, 0)),
                   pl.BlockSpec((None, ML_HEADS, ML_DQK, ML_DV), lambda b, c: (b, 0, 0, 0)),
                   pl.BlockSpec((None, ML_HEADS, ML_DQK), lambda b, c: (b, 0, 0)),
                   pl.BlockSpec((None, ML_HEADS, ML_DQK), lambda b, c: (b, 0, 0))],
        out_shape=[jax.ShapeDtypeStruct((m_rows, hv), BF),
                   jax.ShapeDtypeStruct((batch, ML_HEADS, ML_DQK, ML_DV), F32),
                   jax.ShapeDtypeStruct((batch, ML_HEADS, ML_DQK), F32),
                   jax.ShapeDtypeStruct((batch, ML_HEADS, ML_DQK), F32)],
        compiler_params=_cparams("parallel", "arbitrary"),
        name="mlstm_prompt",
    )(p, p, p, p, gates, gates_t, bias_row, bias_col)


def _rope_table(pos):
    half = SW_ROT // 2
    freq = ROPE_THETA ** (-jnp.arange(half, dtype=F32) * 2.0 / SW_ROT)
    ang = pos.astype(F32)[:, None] * freq[None, :]
    cos, sin = jnp.cos(ang), jnp.sin(ang)
    t = pos.shape[0]
    z = lambda w: jnp.zeros((t, w), F32)
    return jnp.concatenate([cos, cos, jnp.ones((t, SW_HD - SW_ROT), F32),
                            -sin, z(SW_HD - half),
                            z(half), sin, z(SW_HD - SW_ROT)], axis=1)


def _rope(x, tab):
    half = SW_ROT // 2
    return (x * tab[:, 0:SW_HD]
            + pltpu.roll(x, SW_HD - half, 1) * tab[:, SW_HD:2 * SW_HD]
            + pltpu.roll(x, half, 1) * tab[:, 2 * SW_HD:3 * SW_HD])


def _swa_kernel(*refs, has_carry, final):
    q_ref, ko_ref, kp_ref, vo_ref, vp_ref, to_ref, tp_ref = refs[:7]
    pos = 7
    if has_carry:
        oc_ref, lc_ref = refs[7:9]
        pos = 9
    if final:
        xb_ref, kv_ref = refs[pos:pos + 2]
    else:
        og_ref, lg_ref, kv_ref = refs[pos:pos + 3]
    n = pl.program_id(2)
    last = pl.num_programs(2) - 1
    scale = SW_HD ** -0.5
    blk = SW_BLK
    hw = SW_HPG * SW_HD
    row = lax.broadcasted_iota(jnp.int32, (blk, blk), 0)
    col = lax.broadcasted_iota(jnp.int32, (blk, blk), 1)
    own_ok = col <= row
    prev_ok = col >= row + jnp.where(n > 0, 0, blk)
    t_own = to_ref[...]
    t_prev = tp_ref[...]
    nt = (((1,), (1,)), ((), ()))
    for hh in range(SW_HPG):
        sl = slice(hh * SW_HD, (hh + 1) * SW_HD)
        q = _rope(q_ref[:, sl].astype(F32), t_own).astype(BF)
        k_own = _rope(ko_ref[:, sl].astype(F32), t_own)
        k_prev = _rope(kp_ref[:, sl].astype(F32), t_prev).astype(BF)
        v_own = vo_ref[:, sl]
        s_o = jnp.where(own_ok, lax.dot_general(q, k_own.astype(BF), nt, preferred_element_type=F32) * scale, NEG)
        s_p = jnp.where(prev_ok, lax.dot_general(q, k_prev, nt, preferred_element_type=F32) * scale, NEG)
        mx = jnp.maximum(jnp.max(s_o, axis=1, keepdims=True), jnp.max(s_p, axis=1, keepdims=True))
        p_o = jnp.exp(s_o - mx)
        p_p = jnp.exp(s_p - mx)
        den = jnp.sum(p_o, axis=1, keepdims=True) + jnp.sum(p_p, axis=1, keepdims=True)
        acc = (jnp.dot(p_o.astype(BF), v_own, preferred_element_type=F32)
               + jnp.dot(p_p.astype(BF), vp_ref[:, sl], preferred_element_type=F32))
        o = acc / den
        lse = jnp.broadcast_to(mx + jnp.log(den), (blk, SW_HD))
        if has_carry:
            lc = lc_ref[:, sl]
            top = jnp.maximum(lc, lse)
            tot = top + jnp.log(jnp.exp(lc - top) + jnp.exp(lse - top))
            o = jnp.exp(lc - tot) * oc_ref[:, sl] + jnp.exp(lse - tot) * o
            lse = tot
        if final:
            xb_ref[:, sl] = o.astype(xb_ref.dtype)
        else:
            og_ref[:, sl] = o
            lg_ref[:, sl] = lse

        @pl.when(n == last)
        def _():
            kv_ref[:, sl] = k_own
            kv_ref[:, hw + hh * SW_HD:hw + (hh + 1) * SW_HD] = v_own.astype(F32)


def _swa_group(p, tab, gi, carry, batch, seq, final):
    win, dil = SW_GROUPS[gi]
    blk = SW_BLK
    assert win // dil == blk and seq % (dil * blk) == 0 and seq >= win
    ld = seq // dil
    nb = ld // blk
    hw = SW_HPG * SW_HD
    npb = N_MAIN // hw
    m_rows = batch * seq
    pv = p.reshape(batch * ld, dil * N_MAIN)
    tv = tab.reshape(ld, dil * 3 * SW_HD)
    qb, kb, vb = C_QB // hw + gi, C_KB // hw + gi, C_VB // hw + gi
    own = lambda b, r, n: b * nb + n
    prv = lambda b, r, n: b * nb + jnp.maximum(n - 1, 0)
    in_specs = [pl.BlockSpec((blk, hw), lambda b, r, n: (own(b, r, n), r * npb + qb)),
                pl.BlockSpec((blk, hw), lambda b, r, n: (own(b, r, n), r * npb + kb)),
                pl.BlockSpec((blk, hw), lambda b, r, n: (prv(b, r, n), r * npb + kb)),
                pl.BlockSpec((blk, hw), lambda b, r, n: (own(b, r, n), r * npb + vb)),
                pl.BlockSpec((blk, hw), lambda b, r, n: (prv(b, r, n), r * npb + vb)),
                pl.BlockSpec((blk, 3 * SW_HD), lambda b, r, n: (n, r)),
                pl.BlockSpec((blk, 3 * SW_HD), lambda b, r, n: (jnp.maximum(n - 1, 0), r))]
    args = [pv, pv, pv, pv, pv, tv, tv]
    tok_spec = pl.BlockSpec((blk, hw), lambda b, r, n: (own(b, r, n), r))
    if carry is not None:
        in_specs += [tok_spec, tok_spec]
        args += [carry[0].reshape(batch * ld, dil * hw), carry[1].reshape(batch * ld, dil * hw)]
    kv_spec = pl.BlockSpec((None, blk, 2 * hw), lambda b, r, n: (b, 0, r))
    kv_shape = jax.ShapeDtypeStruct((batch, blk, dil * 2 * hw), F32)
    if final:
        out_specs = [tok_spec, kv_spec]
        out_shape = [jax.ShapeDtypeStruct((batch * ld, dil * hw), BF), kv_shape]
    else:
        out_specs = [tok_spec, tok_spec, kv_spec]
        out_shape = [jax.ShapeDtypeStruct((batch * ld, dil * hw), F32)] * 2 + [kv_shape]
    outs = pl.pallas_call(
        functools.partial(_swa_kernel, has_carry=carry is not None, final=final),
        grid=(batch, dil, nb),
        in_specs=in_specs,
        out_specs=out_specs,
        out_shape=out_shape,
        compiler_params=_cparams("parallel", "parallel", "arbitrary"),
        name=f"swa_w{win}",
    )(*args)
    kv = outs[-1].reshape(batch, win, 2, SW_HPG, SW_HD)
    return [o.reshape(m_rows, hw) for o in outs[:-1]], kv


def _swa_prompt(p, tab, batch, seq):
    carry = None
    kvs = []
    for gi in range(len(SW_GROUPS)):
        final = gi == len(SW_GROUPS) - 1
        outs, kv = _swa_group(p, tab, gi, carry, batch, seq, final)
        kvs.append(kv)
        carry = outs
    return carry[0], kvs


def _cm_kernel(u_ref, v_ref, gv_ref, ws_ref, bs_ref, o_ref):
    ch = CM_CHUNK
    row = lax.broadcasted_iota(jnp.int32, (ch, ch), 0)
    col = lax.broadcasted_iota(jnp.int32, (ch, ch), 1)
    tri = col <= row
    for g in range(CM_GROUPS):
        sl = slice(g * CM_GD, (g + 1) * CM_GD)
        vg = v_ref[:, sl].astype(F32)
        vn = vg * lax.rsqrt(jnp.mean(vg * vg, axis=1, keepdims=True) + RMS_EPS) * gv_ref[g:g + 1, :]
        w = jnp.where(tri, ws_ref[g], 0.0).astype(BF)
        mixed = jnp.dot(w, vn.astype(BF), preferred_element_type=F32) + bs_ref[:, g:g + 1]
        o_ref[:, sl] = (u_ref[:, sl].astype(F32) * mixed).astype(o_ref.dtype)


def _cm_prompt(p, g_v, w_s, b_s, l):
    m_rows = p.shape[0]
    ch = CM_CHUNK
    width = CM_GROUPS * CM_GD
    return pl.pallas_call(
        _cm_kernel,
        grid=(m_rows // ch,),
        in_specs=[pl.BlockSpec((ch, width), lambda i: (i, C_UC // width)),
                  pl.BlockSpec((ch, width), lambda i: (i, C_VC // width)),
                  pl.BlockSpec((None, CM_GROUPS, CM_GD), lambda i: (l, 0, 0)),
                  pl.BlockSpec((None, CM_GROUPS, ch, ch), lambda i: (l, 0, 0, 0)),
                  pl.BlockSpec((ch, CM_GROUPS), lambda i: (0, 0))],
        out_specs=pl.BlockSpec((ch, width), lambda i: (i, 0)),
        out_shape=jax.ShapeDtypeStruct((m_rows, width), BF),
        compiler_params=_cparams("parallel"),
        name="cm_prompt",
    )(p, p, g_v, w_s, b_s[l].T)


def _mlstm_step_kernel(qr_ref, kr_ref, vr_ref, or_ref, qkc_ref, g_ref, brow_ref, m0_ref, c0_ref, n0_ref,
                       xa_ref, c1_ref, n1_ref, m1_ref):
    scale = ML_DQK ** -0.5
    gates = g_ref[...] + brow_ref[...]
    m0 = m0_ref[...]
    for h in range(ML_HEADS):
        ig = gates[:, h:h + 1]
        lf = _log_sigmoid(gates[:, ML_HEADS + h:ML_HEADS + h + 1])
        m_prev = m0[:, h:h + 1]
        inter = lf + m_prev
        m_t = jnp.maximum(inter, ig)
        d_w = jnp.exp(ig - m_t)
        w_inter = jnp.exp(inter - m_t)
        q_row = qr_ref[h:h + 1, :]
        k_row = kr_ref[h:h + 1, :]
        v_row = vr_ref[h:h + 1, :]
        q_col = qkc_ref[:, h:h + 1]
        k_col = qkc_ref[:, ML_HEADS + h:ML_HEADS + h + 1]
        c_old = c0_ref[h]
        n_old = n0_ref[h:h + 1, :]
        s = jnp.sum(q_row * k_row, axis=1, keepdims=True) * scale * d_w
        qc = jnp.sum(q_col * c_old, axis=0, keepdims=True)
        num = s * v_row + w_inter * qc
        den = s + w_inter * jnp.sum(q_row * n_old, axis=1, keepdims=True)
        hh = num / jnp.maximum(jnp.abs(den), jnp.exp(-m_t))
        xa_ref[:, h * ML_DV:(h + 1) * ML_DV] = _sigmoid(or_ref[h:h + 1, :]) * hh
        w_last = d_w * scale
        decay = w_inter
        c1_ref[h] = decay * c_old + (k_col * w_last) * v_row
        n1_ref[h:h + 1, :] = decay * n_old + w_last * k_row
        m1_ref[h:h + 1, :] = jnp.broadcast_to(m_t, (1, ML_DQK))


def _mlstm_sample(ps, gates, b_i, b_f, c0, n0, m0, l):
    nb = c0.shape[1]
    q = ps[:nb, C_QA:C_QA + 1024].reshape(nb, ML_HEADS, ML_DQK)
    k = ps[:nb, C_KA:C_KA + 1024].reshape(nb, ML_HEADS, ML_DQK)
    v = ps[:nb, C_VA:C_VA + 2048].reshape(nb, ML_HEADS, ML_DV)
    o = ps[:nb, C_OA:C_OA + 2048].reshape(nb, ML_HEADS, ML_DV)
    qk_col = jnp.concatenate([q, k], axis=1).transpose(0, 2, 1)
    bias = jnp.concatenate([b_i, b_f]).astype(F32)
    bias_row = jnp.zeros((1, N_GATE), F32).at[0, :2 * ML_HEADS].set(bias)
    g3 = gates[:nb].reshape(nb, 1, N_GATE)
    m03 = m0[l].reshape(nb, 1, ML_HEADS)
    per_b3 = lambda shape: pl.BlockSpec((None,) + shape, lambda b: (b, 0, 0))
    return pl.pallas_call(
        _mlstm_step_kernel,
        grid=(nb,),
        in_specs=[per_b3((ML_HEADS, ML_DQK)), per_b3((ML_HEADS, ML_DQK)),
                  per_b3((ML_HEADS, ML_DV)), per_b3((ML_HEADS, ML_DV)),
                  per_b3((ML_DQK, 2 * ML_HEADS)), per_b3((1, N_GATE)),
                  pl.BlockSpec((1, N_GATE), lambda b: (0, 0)),
                  per_b3((1, ML_HEADS)),
                  pl.BlockSpec((None, None, ML_HEADS, ML_DQK, ML_DV), lambda b: (l, b, 0, 0, 0)),
                  pl.BlockSpec((None, None, ML_HEADS, ML_DQK), lambda b: (l, b, 0, 0))],
        out_specs=[per_b3((1, ML_HEADS * ML_DV)),
                   pl.BlockSpec((None, ML_HEADS, ML_DQK, ML_DV), lambda b: (b, 0, 0, 0)),
                   per_b3((ML_HEADS, ML_DQK)), per_b3((ML_HEADS, ML_DQK))],
        out_shape=[jax.ShapeDtypeStruct((nb, 1, ML_HEADS * ML_DV), F32),
                   jax.ShapeDtypeStruct((nb, ML_HEADS, ML_DQK, ML_DV), F32),
                   jax.ShapeDtypeStruct((nb, ML_HEADS, ML_DQK), F32),
                   jax.ShapeDtypeStruct((nb, ML_HEADS, ML_DQK), F32)],
        compiler_params=_cparams("parallel"),
        name="mlstm_sample",
    )(q, k, v, o, qk_col, g3, bias_row, m03, c0, n0)


def _swa_step_kernel(q_ref, k_ref, v_ref, tab_ref, c0_ref, c1_ref, c2_ref, xb_ref, kr_ref):
    scale = SW_HD ** -0.5
    hw = SW_HPG * SW_HD
    tab = tab_ref[...]
    caches = (c0_ref, c1_ref, c2_ref)
    for hh in range(SW_HPG):
        m_run = l_run = acc = None
        for gi in range(len(SW_GROUPS)):
            lo = (gi * SW_HPG + hh) * SW_HD
            q = _rope(q_ref[:, lo:lo + SW_HD], tab)
            k_new = _rope(k_ref[:, lo:lo + SW_HD], tab)
            v_new = v_ref[:, lo:lo + SW_HD]
            kr_ref[:, lo:lo + SW_HD] = k_new
            kc = caches[gi][:, hh * SW_HD:(hh + 1) * SW_HD]
            vc = caches[gi][:, hw + hh * SW_HD:hw + (hh + 1) * SW_HD]
            s_c = jnp.sum(kc * q, axis=1, keepdims=True) * scale
            s_n = jnp.sum(k_new * q, axis=1, keepdims=True) * scale
            mx = jnp.maximum(jnp.max(s_c, axis=0, keepdims=True), s_n)
            p_c = jnp.exp(s_c - mx)
            p_n = jnp.exp(s_n - mx)
            den = jnp.sum(p_c, axis=0, keepdims=True) + p_n
            part = jnp.sum(p_c * vc, axis=0, keepdims=True) + p_n * v_new
            if gi == 0:
                m_run, l_run, acc = mx, den, part
            else:
                top = jnp.maximum(m_run, mx)
                a, b = jnp.exp(m_run - top), jnp.exp(mx - top)
                m_run, l_run, acc = top, a * l_run + b * den, a * acc + b * part
        xb_ref[:, hh * SW_HD:(hh + 1) * SW_HD] = acc / l_run


def _swa_sample(ps, tab_s, caches, l):
    nb = caches[0].shape[1]
    hw = SW_HPG * SW_HD
    wide = SW_HEADS * SW_HD
    q = ps[:nb, C_QB:C_QB + wide].reshape(nb, 1, wide)
    k = ps[:nb, C_KB:C_KB + wide].reshape(nb, 1, wide)
    v = ps[:nb, C_VB:C_VB + wide].reshape(nb, 1, wide)
    views = []
    specs = []
    for (win, dil), cache in zip(SW_GROUPS, caches):
        assert cache.shape[2] == win and win // dil == SW_BLK
        views.append(cache.reshape(cache.shape[0], nb, win // dil, dil * 2 * hw))
        specs.append(pl.BlockSpec((None, None, SW_BLK, 2 * hw), lambda b: (l, b, 0, 0)))
    row3 = pl.BlockSpec((None, 1, wide), lambda b: (b, 0, 0))
    xb, k_rot = pl.pallas_call(
        _swa_step_kernel,
        grid=(nb,),
        in_specs=[row3, row3, row3, pl.BlockSpec((1, 3 * SW_HD), lambda b: (0, 0))] + specs,
        out_specs=[pl.BlockSpec((None, 1, hw), lambda b: (b, 0, 0)), row3],
        out_shape=[jax.ShapeDtypeStruct((nb, 1, hw), F32), jax.ShapeDtypeStruct((nb, 1, wide), F32)],
        compiler_params=_cparams("parallel"),
        name="swa_sample",
    )(q, k, v, tab_s, *views)
    kvs = []
    for gi in range(len(SW_GROUPS)):
        sl = slice(gi * hw, (gi + 1) * hw)
        kvs.append(jnp.stack([k_rot[:, :, sl].reshape(nb, 1, SW_HPG, SW_HD),
                              v[:, :, sl].reshape(nb, 1, SW_HPG, SW_HD)], axis=2))
    return xb.reshape(nb, hw), kvs


def _cm_step_kernel(u_ref, v_ref, gv_ref, ws_ref, bs_ref, o_ref, vn_ref):
    for g in range(CM_GROUPS):
        sl = slice(g * CM_GD, (g + 1) * CM_GD)
        vg = v_ref[:, sl]
        vn = vg * lax.rsqrt(jnp.mean(vg * vg, axis=1, keepdims=True) + RMS_EPS) * gv_ref[g:g + 1, :]
        vn_ref[:, sl] = vn
        w00 = ws_ref[g][0:1, 0:1]
        o_ref[:, sl] = u_ref[:, sl] * (w00 * vn + bs_ref[g:g + 1, 0:1])


def _cm_sample(ps, g_v, w_s, b_s, l):
    rows = ps.shape[0]
    width = CM_GROUPS * CM_GD
    ch = CM_CHUNK
    return pl.pallas_call(
        _cm_step_kernel,
        grid=(1,),
        in_specs=[pl.BlockSpec((rows, width), lambda i: (0, C_UC // width)),
                  pl.BlockSpec((rows, width), lambda i: (0, C_VC // width)),
                  pl.BlockSpec((None, CM_GROUPS, CM_GD), lambda i: (l, 0, 0)),
                  pl.BlockSpec((None, CM_GROUPS, ch, ch), lambda i: (l, 0, 0, 0)),
                  pl.BlockSpec((None, CM_GROUPS, ch), lambda i: (l, 0, 0))],
        out_specs=[pl.BlockSpec((rows, width), lambda i: (0, 0))] * 2,
        out_shape=[jax.ShapeDtypeStruct((rows, width), F32)] * 2,
        compiler_params=_cparams("arbitrary"),
        name="cm_sample",
    )(ps, ps, g_v, w_s, b_s)


def _relayout_w_in(w_in):
    depth, d, _ = w_in.shape
    main = jnp.concatenate(
        [w_in[:, :, :O_GATES],
         w_in[:, :, O_UC:],
         w_in[:, :, O_QB:O_UC],
         jnp.zeros((depth, d, N_MAIN - (w_in.shape[2] - 2 * ML_HEADS)), w_in.dtype)], axis=2).astype(BF)
    gates = jnp.concatenate([w_in[:, :, O_GATES:O_QB],
                             jnp.zeros((depth, d, N_GATE - 2 * ML_HEADS), w_in.dtype)], axis=2)
    return main, gates


def _dense_tail(x, mixers, p, l, mod, rows_per_batch, wts):
    xa, xb, xc = mixers
    merged = _merge(xa, xb, xc, p, wts["w_br_a"], wts["w_br_b"], wts["w_br_c"], l)
    x = _mm_res(merged, wts["w_out"], l, x, mod, 2, rows_per_batch, 512, "out_proj")
    h2 = _norm_mod(x, wts["g_norm2"][l], mod, 4, 3, rows_per_batch)
    act = _ffn1(h2, wts["w_ffn_gate_up"], l)
    return _mm_res(act, wts["w_ffn_down"], l, x, mod, 5, rows_per_batch, 512, "ffn_down", tm=256)


def kernel(x_prompt, x_sample, state_mlstm_C, state_mlstm_n, state_mlstm_m, cache_swa_kv_w128, cache_swa_kv_w512, cache_swa_kv_w2048, c_prompt, c_sample, w_ada, b_ada, g_norm1, g_norm2, g_final, w_in, b_ml_i, b_ml_f, g_cm_v, w_s, b_s, w_br_a, w_br_b, w_br_c, w_out, w_ffn_gate_up, w_ffn_down):
    bp, tp, d = x_prompt.shape
    ns, ts, _ = x_sample.shape
    depth = w_in.shape[0]
    past_len = 16384
    assert d == D_MODEL and ts == 1 and ns == 8 and tp % max(ML_L, SW_GROUPS[-1][0]) == 0
    s_rows = 16
    wts = dict(w_br_a=w_br_a, w_br_b=w_br_b, w_br_c=w_br_c, w_out=w_out, g_norm2=g_norm2,
               w_ffn_gate_up=w_ffn_gate_up, w_ffn_down=w_ffn_down)

    c_all = jnp.zeros((s_rows, d), F32).at[:ns].set(c_sample).at[ns:ns + bp].set(c_prompt)
    mod_all = _ada(c_all, w_ada, b_ada)
    w_main, w_gates = _relayout_w_in(w_in)
    tab_p = _rope_table(jnp.arange(tp, dtype=jnp.int32))
    tab_s = _rope_table(past_len + jnp.arange(ts, dtype=jnp.int32))
    caches = (cache_swa_kv_w128, cache_swa_kv_w512, cache_swa_kv_w2048)

    xp = x_prompt.reshape(bp * tp, d)
    xs = jnp.zeros((s_rows, d), F32).at[:ns].set(x_sample.reshape(ns, d))
    p_c, p_n, p_m, p_kv = [], [], [], [[], [], []]
    s_c, s_n, s_m, s_kv, s_v = [], [], [], [[], [], []], []
    for l in range(depth):
        mod_p = mod_all[l, ns:ns + bp].reshape(bp, 1, 6 * d)
        mod_s = mod_all[l]
        h = _norm_mod(xp, g_norm1[l], mod_p, 1, 0, tp)
        p = _mm(h, w_main, l, 1024, BF, "in_proj")
        gates = _mm(h, w_gates, l, N_GATE, F32, "gate_proj")
        xa, c1, n1, m1 = _mlstm_prompt(p, gates, b_ml_i[l], b_ml_f[l], bp, tp)
        xb, kvs = _swa_prompt(p, tab_p, bp, tp)
        xc = _cm_prompt(p, g_cm_v, w_s, b_s, l)
        xp = _dense_tail(xp, (xa, xb, xc), p, l, mod_p, tp, wts)
        p_c.append(c1)
        p_n.append(n1)
        p_m.append(m1[:, :, 0])
        for gi in range(3):
            p_kv[gi].append(kvs[gi])
        h = _norm_mod(xs, g_norm1[l], mod_s, 1, 0, None)
        ps = _mm(h, w_main, l, 1024, F32, "in_proj_s")
        gates = _mm(h, w_gates, l, N_GATE, F32, "gate_proj_s")
        xa, c1, n1, m1 = _mlstm_sample(ps, gates, b_ml_i[l], b_ml_f[l], state_mlstm_C, state_mlstm_n,
                                       state_mlstm_m, l)
        xb, kvs = _swa_sample(ps, tab_s, caches, l)
        xc, vn = _cm_sample(ps, g_cm_v, w_s, b_s, l)
        pad = lambda a: jnp.zeros((s_rows, a.shape[-1]), F32).at[:ns].set(a.reshape(ns, -1))
        xs = _dense_tail(xs, (pad(xa), pad(xb), xc), ps, l, mod_s, None, wts)
        s_c.append(c1)
        s_n.append(n1)
        s_m.append(m1[:, :, 0])
        for gi in range(3):
            s_kv[gi].append(kvs[gi])
        s_v.append(vn[:ns].reshape(ns, 1, CM_GROUPS, CM_GD))
    y_p = _norm(xp, g_final).reshape(bp, tp, d)
    y_s = _norm(xs, g_final)[:ns].reshape(ns, ts, d)
    st = jnp.stack
    return (y_p, y_s, st(p_c), st(p_n), st(p_m), st(p_kv[0]), st(p_kv[1]), st(p_kv[2]),
            st(s_c), st(s_n), st(s_m), st(s_kv[0]), st(s_kv[1]), st(s_kv[2]), st(s_v))
```

```python
import functools

import jax
import jax.numpy as jnp
from jax import lax
from jax.experimental import pallas as pl
from jax.experimental.pallas import tpu as pltpu

BF = jnp.bfloat16
F32 = jnp.float32

D_MODEL = 2048
ML_HEADS = 8
ML_DQK = 128
ML_DV = 256
SW_GROUPS = ((128, 1), (512, 4), (2048, 16))
SW_HPG = 4
SW_HEADS = 12
SW_HD = 128
SW_ROT = SW_HD // 4
SW_BLK = 128
ROPE_THETA = 500000.0
CM_CHUNK = 128
CM_GROUPS = 8
CM_GD = 256
D_FF = 5632
RMS_EPS = 1e-6
NEG = -1e30

LANE = 128
O_GATES = 6144
O_QB = O_GATES + 2 * ML_HEADS
O_UC = O_QB + 3 * SW_HEADS * SW_HD
N_REST = 2 * CM_GROUPS * CM_GD + 3 * D_MODEL
N_GATE = LANE
C_QA, C_KA, C_VA, C_OA = 0, 1024, 2048, 4096
C_UC, C_VC, C_G = 0, 2048, 4096

TM = 512
ML_L = 128
VMEM_LIMIT = 48 * 1024 * 1024


def _cparams(*sem):
    return pltpu.CompilerParams(dimension_semantics=sem, vmem_limit_bytes=VMEM_LIMIT)


def _sigmoid(x):
    return 1.0 / (1.0 + jnp.exp(-x))


def _log_sigmoid(x):
    return jnp.minimum(x, 0.0) - jnp.log1p(jnp.exp(-jnp.abs(x)))


def _ada_kernel(c_ref, w_ref, b_ref, o_ref):
    c = c_ref[...]
    a = (c * _sigmoid(c)).astype(BF)
    o_ref[...] = jnp.dot(a, w_ref[...].astype(BF), preferred_element_type=F32) + b_ref[...]


def _ada(c_all, w_ada, b_ada):
    depth, d, n = w_ada.shape
    rows = c_all.shape[0]
    tn = 1024
    return pl.pallas_call(
        _ada_kernel,
        grid=(depth, n // tn),
        in_specs=[pl.BlockSpec((rows, d), lambda l, j: (0, 0)),
                  pl.BlockSpec((None, d, tn), lambda l, j: (l, 0, j)),
                  pl.BlockSpec((None, 1, tn), lambda l, j: (l, 0, j))],
        out_specs=pl.BlockSpec((None, rows, tn), lambda l, j: (l, 0, j)),
        out_shape=jax.ShapeDtypeStruct((depth, rows, n), F32),
        compiler_params=_cparams("parallel", "parallel"),
        name="ada",
    )(c_all, w_ada, b_ada.reshape(depth, 1, n))


def _mod_spec(mod, chunk, width, tm, rows_per_batch, row_of, col_of):
    per = D_MODEL // width
    if mod.ndim == 3:
        tiles_per_batch = rows_per_batch // tm
        return pl.BlockSpec((None, 1, width),
                            lambda *g: (row_of(*g) // tiles_per_batch, 0, chunk * per + col_of(*g)))
    return pl.BlockSpec((tm, width), lambda *g: (row_of(*g), chunk * per + col_of(*g)))


def _norm_mod_kernel(x_ref, g_ref, sc_ref, sh_ref, o_ref):
    x = x_ref[...]
    y = x * lax.rsqrt(jnp.mean(x * x, axis=1, keepdims=True) + RMS_EPS) * g_ref[...]
    o_ref[...] = (y * (1.0 + sc_ref[...]) + sh_ref[...]).astype(o_ref.dtype)


def _norm_kernel(x_ref, g_ref, o_ref):
    x = x_ref[...]
    o_ref[...] = x * lax.rsqrt(jnp.mean(x * x, axis=1, keepdims=True) + RMS_EPS) * g_ref[...]


def _norm_mod(x, g, mod, sc_chunk, sh_chunk, rows_per_batch):
    m, d = x.shape
    tm = min(256, m)
    row_of = lambda i: i
    col_of = lambda i: 0
    return pl.pallas_call(
        _norm_mod_kernel,
        grid=(m // tm,),
        in_specs=[pl.BlockSpec((tm, d), lambda i: (i, 0)),
                  pl.BlockSpec((1, d), lambda i: (0, 0)),
                  _mod_spec(mod, sc_chunk, d, tm, rows_per_batch, row_of, col_of),
                  _mod_spec(mod, sh_chunk, d, tm, rows_per_batch, row_of, col_of)],
        out_specs=pl.BlockSpec((tm, d), lambda i: (i, 0)),
        out_shape=jax.ShapeDtypeStruct((m, d), BF),
        compiler_params=_cparams("parallel"),
        name="norm_mod",
    )(x, g.reshape(1, d), mod, mod)


def _norm(x, g):
    m, d = x.shape
    tm = min(256, m)
    return pl.pallas_call(
        _norm_kernel,
        grid=(m // tm,),
        in_specs=[pl.BlockSpec((tm, d), lambda i: (i, 0)),
                  pl.BlockSpec((1, d), lambda i: (0, 0))],
        out_specs=pl.BlockSpec((tm, d), lambda i: (i, 0)),
        out_shape=jax.ShapeDtypeStruct((m, d), F32),
        compiler_params=_cparams("parallel"),
        name="norm_final",
    )(x, g.reshape(1, d))


def _cast_weights_once(pairs):
    @pl.when(pl.program_id(1) == 0)
    def _():
        for src, dst in pairs:
            dst[...] = src[...].astype(BF)


_NT =(((1,), (1,)), ((), ()))


def _wt_spec(k, l, tn, start_of):
    return pl.BlockSpec((pl.Element(1), pl.Element(tn), pl.Element(k)),
                        lambda j, i: (l, pl.multiple_of(start_of(j), 8), 0))


def _mm_t_kernel(x_ref, w_ref, o_ref, wbf):
    _cast_weights_once([(w_ref.at[0], wbf)])
    o_ref[...] = lax.dot_general(x_ref[...], wbf[...], _NT, preferred_element_type=F32).astype(o_ref.dtype)


def _mm_t(x, wt, l, tn, col0, n_tiles, out_dtype, name):
    m, k = x.shape
    tm = min(TM, m)
    return pl.pallas_call(
        _mm_t_kernel,
        grid=(n_tiles, m // tm),
        in_specs=[pl.BlockSpec((tm, k), lambda j, i: (i, 0)), _wt_spec(k, l, tn, lambda j: col0 + j * tn)],
        out_specs=pl.BlockSpec((tm, tn), lambda j, i: (i, j)),
        out_shape=jax.ShapeDtypeStruct((m, n_tiles * tn), out_dtype),
        scratch_shapes=[pltpu.VMEM((tn, k), BF)],
        compiler_params=_cparams("arbitrary", "arbitrary"),
        name=name,
    )(x, wt)


def _qkv_kernel(x_ref, w_ref, tab_ref, o_ref, wbf, acc_ref, *, dil):
    _cast_weights_once([(w_ref.at[0], wbf)])
    acc = lax.dot_general(x_ref[...], wbf[...], _NT, preferred_element_type=F32)
    j = pl.program_id(0)

    heads = [slice(hh * SW_HD, (hh + 1) * SW_HD) for hh in range(SW_HPG)]

    @pl.when(j < 2)
    def _():
        tab = tab_ref[...]
        for hh, sl in enumerate(heads):
            acc_ref[hh] = _rope(acc[:, sl], tab)

    @pl.when(j == 2)
    def _():
        for hh, sl in enumerate(heads):
            acc_ref[hh] = acc[:, sl]

    rows = acc_ref.shape[1] // dil
    for r in range(dil):
        src = pl.ds(r, rows, stride=dil) if dil > 1 else pl.ds(0, rows)
        for hh, sl in enumerate(heads):
            o_ref[r, :, sl] = acc_ref[hh, src, :].astype(o_ref.dtype)


def _qkv_proj(x, wt, l, gi, dil, tabs, batch, out_dtype, name):
    m, k = x.shape
    seq = m // batch
    tm = min(TM, seq)
    tpb = seq // tm
    hw = SW_HPG * SW_HD
    wide = len(SW_GROUPS) * hw
    return pl.pallas_call(
        functools.partial(_qkv_kernel, dil=dil),
        grid=(3, m // tm),
        in_specs=[pl.BlockSpec((tm, k), lambda j, i: (i, 0)),
                  _wt_spec(k, l, hw, lambda j: O_QB + j * wide + gi * hw),
                  pl.BlockSpec((None, tm, 3 * SW_HD), lambda j, i: (jnp.minimum(j, 1), i % tpb, 0))],
        out_specs=pl.BlockSpec((None, dil, tm // dil, hw), lambda j, i: (i // tpb, 0, i % tpb, j)),
        out_shape=jax.ShapeDtypeStruct((batch, dil, seq // dil, 3 * hw), out_dtype),
        scratch_shapes=[pltpu.VMEM((hw, k), BF), pltpu.VMEM((SW_HPG, tm, SW_HD), F32)],
        compiler_params=_cparams("arbitrary", "arbitrary"),
        name=name,
    )(x, wt, tabs)


def _mm_res_kernel(x_ref, w_ref, r_ref, g_ref, o_ref, wbf):
    _cast_weights_once([(w_ref, wbf)])
    acc = jnp.dot(x_ref[...], wbf[...], preferred_element_type=F32)
    o_ref[...] = r_ref[...] + g_ref[...] * acc


def _mm_res(x, w3, l, res, mod, gate_chunk, rows_per_batch, tn, name, tm=TM):
    m, k = x.shape
    n = w3.shape[2]
    tm = min(tm, m)
    return pl.pallas_call(
        _mm_res_kernel,
        grid=(n // tn, m // tm),
        in_specs=[pl.BlockSpec((tm, k), lambda j, i: (i, 0)),
                  pl.BlockSpec((None, k, tn), lambda j, i: (l, 0, j)),
                  pl.BlockSpec((tm, tn), lambda j, i: (i, j)),
                  _mod_spec(mod, gate_chunk, tn, tm, rows_per_batch, lambda j, i: i, lambda j, i: j)],
        out_specs=pl.BlockSpec((tm, tn), lambda j, i: (i, j)),
        out_shape=jax.ShapeDtypeStruct((m, n), F32),
        scratch_shapes=[pltpu.VMEM((k, tn), BF)],
        compiler_params=_cparams("arbitrary", "arbitrary"),
        name=name,
    )(x, w3, res, mod)


def _ffn1_kernel(x_ref, wg_ref, wu_ref, o_ref, wgbf, wubf):
    _cast_weights_once([(wg_ref, wgbf), (wu_ref, wubf)])
    x = x_ref[...]
    g = jnp.dot(x, wgbf[...], preferred_element_type=F32)
    u = jnp.dot(x, wubf[...], preferred_element_type=F32)
    o_ref[...] = (g * _sigmoid(g) * u).astype(o_ref.dtype)


def _ffn1(x, w_gu, l):
    m, k = x.shape
    tn = 512
    nj = D_FF // tn
    tm = min(TM, m)
    return pl.pallas_call(
        _ffn1_kernel,
        grid=(nj, m // tm),
        in_specs=[pl.BlockSpec((tm, k), lambda j, i: (i, 0)),
                  pl.BlockSpec((None, k, tn), lambda j, i: (l, 0, j)),
                  pl.BlockSpec((None, k, tn), lambda j, i: (l, 0, nj + j))],
        out_specs=pl.BlockSpec((tm, tn), lambda j, i: (i, j)),
        out_shape=jax.ShapeDtypeStruct((m, D_FF), BF),
        scratch_shapes=[pltpu.VMEM((k, tn), BF), pltpu.VMEM((k, tn), BF)],
        compiler_params=_cparams("arbitrary", "arbitrary"),
        name="ffn_gate_up",
    )(x, w_gu, w_gu)


def _merge_kernel(xa_ref, xb_ref, xc_ref, g0_ref, g1_ref, g2_ref, wa_ref, wb_ref, wc_ref, o_ref,
                  wabf, wbbf, wcbf):
    _cast_weights_once([(wa_ref, wabf), (wb_ref, wbbf), (wc_ref, wcbf)])
    da = jnp.dot(xa_ref[...].astype(BF), wabf[...], preferred_element_type=F32)
    db = jnp.dot(xb_ref[...].astype(BF), wbbf[...], preferred_element_type=F32)
    dc = jnp.dot(xc_ref[...].astype(BF), wcbf[...], preferred_element_type=F32)
    o_ref[...] = (_sigmoid(g0_ref[...].astype(F32)) * da + _sigmoid(g1_ref[...].astype(F32)) * db
                  + _sigmoid(g2_ref[...].astype(F32)) * dc).astype(o_ref.dtype)


def _merge(xa, xb, xc, p, w_a, w_b, w_c, l):
    m = xa.shape[0]
    tn = 512
    tm = min(TM, m)
    ka, kb, kc = xa.shape[1], xb.shape[1], xc.shape[1]
    g_blk = C_G // tn
    per = D_MODEL // tn
    return pl.pallas_call(
        _merge_kernel,
        grid=(D_MODEL // tn, m // tm),
        in_specs=[pl.BlockSpec((tm, ka), lambda j, i: (i, 0)),
                  pl.BlockSpec((tm, kb), lambda j, i: (i, 0)),
                  pl.BlockSpec((tm, kc), lambda j, i: (i, 0)),
                  pl.BlockSpec((tm, tn), lambda j, i: (i, g_blk + j)),
                  pl.BlockSpec((tm, tn), lambda j, i: (i, g_blk + per + j)),
                  pl.BlockSpec((tm, tn), lambda j, i: (i, g_blk + 2 * per + j)),
                  pl.BlockSpec((None, ka, tn), lambda j, i: (l, 0, j)),
                  pl.BlockSpec((None, kb, tn), lambda j, i: (l, 0, j)),
                  pl.BlockSpec((None, kc, tn), lambda j, i: (l, 0, j))],
        out_specs=pl.BlockSpec((tm, tn), lambda j, i: (i, j)),
        out_shape=jax.ShapeDtypeStruct((m, D_MODEL), BF),
        scratch_shapes=[pltpu.VMEM((ka, tn), BF), pltpu.VMEM((kb, tn), BF), pltpu.VMEM((kc, tn), BF)],
        compiler_params=_cparams("arbitrary", "arbitrary"),
        name="branch_merge",
    )(xa, xb, xc, p, p, p, w_a, w_b, w_c)


def _mlstm_kernel(q_ref, k_ref, v_ref, o_ref, gc_ref, gr_ref, brow_ref, bcol_ref,
                  xa_ref, c_ref, n_ref, m_ref):
    @pl.when(pl.program_id(1) == 0)
    def _():
        c_ref[...] = jnp.zeros_like(c_ref)
        n_ref[...] = jnp.zeros_like(n_ref)
        m_ref[...] = jnp.zeros_like(m_ref)

    L = q_ref.shape[0]
    scale = ML_DQK ** -0.5
    row = lax.broadcasted_iota(jnp.int32, (L, L), 0)
    col = lax.broadcasted_iota(jnp.int32, (L, L), 1)
    tri = col <= row
    gc = gc_ref[...] + brow_ref[...]
    gr = gr_ref[...] + bcol_ref[...]
    for h in range(ML_HEADS):
        ig_c = gc[:, h:h + 1]
        lf_c = _log_sigmoid(gc[:, ML_HEADS + h:ML_HEADS + h + 1])
        ig_r = gr[h:h + 1, :]
        lf_r = _log_sigmoid(gr[ML_HEADS + h:ML_HEADS + h + 1, :])
        f_col = jnp.sum(jnp.where(tri, lf_r, 0.0), axis=1, keepdims=True)
        f_row = jnp.sum(jnp.where(row <= col, lf_c, 0.0), axis=0, keepdims=True)
        m_prev = m_ref[h:h + 1, 0:1]
        log_d = jnp.where(tri, f_col - f_row + ig_r, NEG)
        inter = f_col + m_prev
        m_t = jnp.maximum(inter, jnp.max(log_d, axis=1, keepdims=True))
        d_w = jnp.exp(log_d - m_t)
        w_inter = jnp.exp(inter - m_t)
        q = q_ref[:, h * ML_DQK:(h + 1) * ML_DQK]
        k = k_ref[:, h * ML_DQK:(h + 1) * ML_DQK]
        v = v_ref[:, h * ML_DV:(h + 1) * ML_DV]
        s = lax.dot_general(q, k, (((1,), (1,)), ((), ())), preferred_element_type=F32) * (scale * d_w)
        c_old = c_ref[h]
        n_old = n_ref[h:h + 1, :]
        num = (jnp.dot(s.astype(BF), v, preferred_element_type=F32)
               + w_inter * jnp.dot(q, c_old.astype(BF), preferred_element_type=F32))
        den = (jnp.sum(s, axis=1, keepdims=True)
               + w_inter * jnp.sum(q.astype(F32) * n_old, axis=1, keepdims=True))
        hh = num / jnp.maximum(jnp.abs(den), jnp.exp(-m_t))
        og = o_ref[:, h * ML_DV:(h + 1) * ML_DV].astype(F32)
        xa_ref[:, h * ML_DV:(h + 1) * ML_DV] = (_sigmoid(og) * hh).astype(xa_ref.dtype)
        m_new = m_t[L - 1:L, :]
        f_last = f_col[L - 1:L, :]
        w_last = jnp.exp(f_last - f_col + ig_c - m_new) * scale
        decay = jnp.exp(f_last + m_prev - m_new)
        kw = k.astype(F32) * w_last
        c_ref[h] = decay * c_old + jnp.dot(kw.T.astype(BF), v, preferred_element_type=F32)
        n_ref[h:h + 1, :] = decay * n_old + jnp.sum(kw, axis=0, keepdims=True)
        m_ref[h:h + 1, :] = jnp.broadcast_to(m_new, (1, ML_DQK))


def _mlstm_prompt(p, gates, b_i, b_f, batch, seq):
    m_rows = p.shape[0]
    L = ML_L
    nc = seq // L
    bias = jnp.concatenate([b_i, b_f]).astype(F32)
    bias_row = jnp.zeros((1, N_GATE), F32).at[0, :2 * ML_HEADS].set(bias)
    bias_col = bias.reshape(2 * ML_HEADS, 1)
    gates_t = gates[:, :2 * ML_HEADS].T
    hq, hv = ML_HEADS * ML_DQK, ML_HEADS * ML_DV
    return pl.pallas_call(
        _mlstm_kernel,
        grid=(batch, nc),
        in_specs=[pl.BlockSpec((L, hq), lambda b, c: (b * nc + c, C_QA // hq)),
                  pl.BlockSpec((L, hq), lambda b, c: (b * nc + c, C_KA // hq)),
                  pl.BlockSpec((L, hv), lambda b, c: (b * nc + c, C_VA // hv)),
                  pl.BlockSpec((L, hv), lambda b, c: (b * nc + c, C_OA // hv)),
                  pl.BlockSpec((L, N_GATE), lambda b, c: (b * nc + c, 0)),
                  pl.BlockSpec((2 * ML_HEADS, L), lambda b, c: (0, b * nc + c)),
                  pl.BlockSpec((1, N_GATE), lambda b, c: (0, 0)),
                  pl.BlockSpec((2 * ML_HEADS, 1), lambda b, c: (0, 0))],
        out_specs=[pl.BlockSpec((L, hv), lambda b, c: (b * nc + c, 0)),
                   pl.BlockSpec((None, ML_HEADS, ML_DQK, ML_DV), lambda b, c: (b, 0, 0, 0)),
                   pl.BlockSpec((None, ML_HEADS, ML_DQK), lambda b, c: (b, 0, 0)),
                   pl.BlockSpec((None, ML_HEADS, ML_DQK), lambda b, c: (b, 0, 0))],
        out_shape=[jax.ShapeDtypeStruct((m_rows, hv), BF),
                   jax.ShapeDtypeStruct((batch, ML_HEADS, ML_DQK, ML_DV), F32),
                   jax.ShapeDtypeStruct((batch, ML_HEADS, ML_DQK), F32),
                   jax.ShapeDtypeStruct((batch, ML_HEADS, ML_DQK), F32)],
        compiler_params=_cparams("parallel", "arbitrary"),
        name="mlstm_prompt",
    )(p, p, p, p, gates, gates_t, bias_row, bias_col)


def _rope_table(pos):
    half = SW_ROT // 2
    freq = ROPE_THETA ** (-jnp.arange(half, dtype=F32) * 2.0 / SW_ROT)
    ang = pos.astype(F32)[:, None] * freq[None, :]
    cos, sin = jnp.cos(ang), jnp.sin(ang)
    t = pos.shape[0]
    z = lambda w: jnp.zeros((t, w), F32)
    return jnp.concatenate([cos, cos, jnp.ones((t, SW_HD - SW_ROT), F32),
                            -sin, z(SW_HD - half),
                            z(half), sin, z(SW_HD - SW_ROT)], axis=1)


def _rope(x, tab):
    half = SW_ROT // 2
    return (x * tab[:, 0:SW_HD]
            + pltpu.roll(x, SW_HD - half, 1) * tab[:, SW_HD:2 * SW_HD]
            + pltpu.roll(x, half, 1) * tab[:, 2 * SW_HD:3 * SW_HD])


def _attend(q, k_own, v_own, k_prev, v_prev, prev_shift):
    blk = SW_BLK
    row = lax.broadcasted_iota(jnp.int32, (blk, blk), 0)
    col = lax.broadcasted_iota(jnp.int32, (blk, blk), 1)
    nt = (((1,), (1,)), ((), ()))
    s_o = jnp.where(col <= row, lax.dot_general(q, k_own, nt, preferred_element_type=F32), NEG)
    s_p = jnp.where(col >= row + prev_shift, lax.dot_general(q, k_prev, nt, preferred_element_type=F32), NEG)
    mx = jnp.maximum(jnp.max(s_o, axis=1, keepdims=True), jnp.max(s_p, axis=1, keepdims=True))
    p_o = jnp.exp(s_o - mx)
    p_p = jnp.exp(s_p - mx)
    den = jnp.sum(p_o, axis=1, keepdims=True) + jnp.sum(p_p, axis=1, keepdims=True)
    acc = (jnp.dot(p_o.astype(BF), v_own, preferred_element_type=F32)
           + jnp.dot(p_p.astype(BF), v_prev, preferred_element_type=F32))
    return acc / den, mx + jnp.log(den)


def _swa_kernel(q1, k1, v1, k1p, v1p, q2, k2, v2, k2p, v2p, q3, k3, v3, k3p, v3p, xb_ref, o_acc, l_acc):
    blk = SW_BLK
    first_shift = jnp.where(pl.program_id(1) > 0, 0, blk)
    aligned = lambda i: pl.ds(pl.multiple_of(i * blk, blk), blk)

    def merge(rows, o, lse):
        lc = l_acc[rows, :]
        top = jnp.maximum(lc, lse)
        tot = top + jnp.log(jnp.exp(lc - top) + jnp.exp(lse - top))
        o_acc[rows, :] = jnp.exp(lc - tot) * o_acc[rows, :] + jnp.exp(lse - tot) * o
        l_acc[rows, :] = tot

    n1 = q1.shape[0] // blk

    def g1_unit(bi, k_p, v_p, shift):
        rows = aligned(bi)
        o, lse = _attend(q1[rows, :], k1[rows, :], v1[rows, :], k_p, v_p, shift)
        o_acc[rows, :] = o
        l_acc[rows, :] = jnp.broadcast_to(lse, (blk, SW_HD))

    g1_unit(0, k1p[...], v1p[...], first_shift)

    def g1_body(bi, c):
        g1_unit(bi, k1[aligned(bi - 1), :], v1[aligned(bi - 1), :], 0)
        return c

    lax.fori_loop(1, n1, g1_body, 0)

    d2 = q2.shape[0]
    n2 = q2.shape[1] // blk

    def g2_unit(sub, r, k_p, v_p, shift):
        rows = aligned(sub)
        o, lse = _attend(q2[r, rows, :], k2[r, rows, :], v2[r, rows, :], k_p, v_p, shift)
        merge(pl.ds(sub * blk * d2 + r, blk, stride=d2), o, lse)

    def g2_first(r, c):
        g2_unit(0, r, k2p[r], v2p[r], first_shift)
        return c

    lax.fori_loop(0, d2, g2_first, 0)

    def g2_body(i, c):
        sub = 1 + i // d2
        r = i % d2
        g2_unit(sub, r, k2[r, aligned(sub - 1), :], v2[r, aligned(sub - 1), :], 0)
        return c

    lax.fori_loop(0, (n2 - 1) * d2, g2_body, 0)

    d3 = q3.shape[0]

    def g3_body(r, c):
        o, lse = _attend(q3[r], k3[r], v3[r], k3p[r], v3p[r], first_shift)
        merge(pl.ds(r, blk, stride=d3), o, lse)
        return c

    lax.fori_loop(0, d3, g3_body, 0)
    xb_ref[...] = o_acc[...].astype(xb_ref.dtype)


def _swa_prompt(qkvs, batch, seq):
    blk = SW_BLK
    span = SW_GROUPS[-1][0]
    assert seq % span == 0 and all(w // d == blk for w, d in SW_GROUPS)
    nspan = seq // span
    in_specs = []
    args = []
    for (win, dil), qkv in zip(SW_GROUPS, qkvs):
        rows = span // dil
        per = rows // blk
        own = lambda c: pl.BlockSpec((None, dil, rows, SW_HD), lambda b, s, h, c=c: (b, 0, s, c * SW_HPG + h))
        prev = lambda c: pl.BlockSpec(
            (None, dil, blk, SW_HD), lambda b, s, h, c=c, per=per: (b, 0, jnp.maximum(s * per - 1, 0), c * SW_HPG + h))
        if dil == 1:
            own = lambda c: pl.BlockSpec((None, None, span, SW_HD), lambda b, s, h, c=c: (b, 0, s, c * SW_HPG + h))
            prev = lambda c: pl.BlockSpec(
                (None, None, blk, SW_HD),
                lambda b, s, h, c=c, per=per: (b, 0, jnp.maximum(s * per - 1, 0), c * SW_HPG + h))
        in_specs += [own(0), own(1), own(2), prev(1), prev(2)]
        args += [qkv] * 5
    return pl.pallas_call(
        _swa_kernel,
        grid=(batch, nspan, SW_HPG),
        in_specs=in_specs,
        out_specs=pl.BlockSpec((span, SW_HD), lambda b, s, h: (b * nspan + s, h)),
        out_shape=jax.ShapeDtypeStruct((batch * seq, SW_HPG * SW_HD), BF),
        scratch_shapes=[pltpu.VMEM((span, SW_HD), F32), pltpu.VMEM((span, SW_HD), F32)],
        compiler_params=_cparams("parallel", "parallel", "parallel"),
        name="swa_prompt",
    )(*args)


def _kv_rows_kernel(k_ref, v_ref, o_ref, *, dil):
    for r in range(dil):
        rows = pl.ds(r, SW_BLK, stride=dil) if dil > 1 else pl.ds(0, SW_BLK)
        for hh in range(SW_HPG):
            sl = slice(hh * SW_HD, (hh + 1) * SW_HD)
            o_ref[hh, rows, :] = k_ref[r, :, sl].astype(F32)
            o_ref[SW_HPG + hh, rows, :] = v_ref[r, :, sl].astype(F32)


def _kv_rows(qkv, gi, batch, seq):
    win, dil = SW_GROUPS[gi]
    hw = SW_HPG * SW_HD
    last = seq // dil // SW_BLK - 1
    out = pl.pallas_call(
        functools.partial(_kv_rows_kernel, dil=dil),
        grid=(batch,),
        in_specs=[pl.BlockSpec((None, dil, SW_BLK, hw), lambda b: (b, 0, last, 1)),
                  pl.BlockSpec((None, dil, SW_BLK, hw), lambda b: (b, 0, last, 2))],
        out_specs=pl.BlockSpec((None, 2 * SW_HPG, win, SW_HD), lambda b: (b, 0, 0, 0)),
        out_shape=jax.ShapeDtypeStruct((batch, 2 * SW_HPG, win, SW_HD), F32),
        compiler_params=_cparams("parallel"),
        name=f"kv_rows_w{win}",
    )(qkv, qkv)
    return out.reshape(batch, 2, SW_HPG, win, SW_HD).transpose(0, 3, 1, 2, 4)


def _cm_kernel(u_ref, v_ref, gv_ref, ws_ref, bs_ref, o_ref):
    ch = CM_CHUNK
    row = lax.broadcasted_iota(jnp.int32, (ch, ch), 0)
    col = lax.broadcasted_iota(jnp.int32, (ch, ch), 1)
    tri = col <= row
    for g in range(CM_GROUPS):
        sl = slice(g * CM_GD, (g + 1) * CM_GD)
        vg = v_ref[:, sl].astype(F32)
        vn = vg * lax.rsqrt(jnp.mean(vg * vg, axis=1, keepdims=True) + RMS_EPS) * gv_ref[g:g + 1, :]
        w = jnp.where(tri, ws_ref[g], 0.0).astype(BF)
        mixed = jnp.dot(w, vn.astype(BF), preferred_element_type=F32) + bs_ref[:, g:g + 1]
        o_ref[:, sl] = (u_ref[:, sl].astype(F32) * mixed).astype(o_ref.dtype)


def _cm_prompt(p, g_v, w_s, b_s, l):
    m_rows = p.shape[0]
    ch = CM_CHUNK
    width = CM_GROUPS * CM_GD
    return pl.pallas_call(
        _cm_kernel,
        grid=(m_rows // ch,),
        in_specs=[pl.BlockSpec((ch, width), lambda i: (i, C_UC // width)),
                  pl.BlockSpec((ch, width), lambda i: (i, C_VC // width)),
                  pl.BlockSpec((None, CM_GROUPS, CM_GD), lambda i: (l, 0, 0)),
                  pl.BlockSpec((None, CM_GROUPS, ch, ch), lambda i: (l, 0, 0, 0)),
                  pl.BlockSpec((ch, CM_GROUPS), lambda i: (0, 0))],
        out_specs=pl.BlockSpec((ch, width), lambda i: (i, 0)),
        out_shape=jax.ShapeDtypeStruct((m_rows, width), BF),
        compiler_params=_cparams("parallel"),
        name="cm_prompt",
    )(p, p, g_v, w_s, b_s[l].T)


def _mlstm_step_kernel(qr_ref, kr_ref, vr_ref, or_ref, qkc_ref, g_ref, brow_ref, m0_ref, c0_ref, n0_ref,
                       xa_ref, c1_ref, n1_ref, m1_ref):
    scale = ML_DQK ** -0.5
    gates = g_ref[...] + brow_ref[...]
    m0 = m0_ref[...]
    for h in range(ML_HEADS):
        ig = gates[:, h:h + 1]
        lf = _log_sigmoid(gates[:, ML_HEADS + h:ML_HEADS + h + 1])
        m_prev = m0[:, h:h + 1]
        inter = lf + m_prev
        m_t = jnp.maximum(inter, ig)
        d_w = jnp.exp(ig - m_t)
        w_inter = jnp.exp(inter - m_t)
        q_row = qr_ref[h:h + 1, :]
        k_row = kr_ref[h:h + 1, :]
        v_row = vr_ref[h:h + 1, :]
        q_col = qkc_ref[:, h:h + 1]
        k_col = qkc_ref[:, ML_HEADS + h:ML_HEADS + h + 1]
        c_old = c0_ref[h]
        n_old = n0_ref[h:h + 1, :]
        s = jnp.sum(q_row * k_row, axis=1, keepdims=True) * scale * d_w
        qc = jnp.sum(q_col * c_old, axis=0, keepdims=True)
        num = s * v_row + w_inter * qc
        den = s + w_inter * jnp.sum(q_row * n_old, axis=1, keepdims=True)
        hh = num / jnp.maximum(jnp.abs(den), jnp.exp(-m_t))
        xa_ref[:, h * ML_DV:(h + 1) * ML_DV] = _sigmoid(or_ref[h:h + 1, :]) * hh
        w_last = d_w * scale
        decay = w_inter
        c1_ref[h] = decay * c_old + (k_col * w_last) * v_row
        n1_ref[h:h + 1, :] = decay * n_old + w_last * k_row
        m1_ref[h:h + 1, :] = jnp.broadcast_to(m_t, (1, ML_DQK))


def _mlstm_sample(ps, gates, b_i, b_f, c0, n0, m0, l):
    nb = c0.shape[1]
    q = ps[:nb, C_QA:C_QA + 1024].reshape(nb, ML_HEADS, ML_DQK)
    k = ps[:nb, C_KA:C_KA + 1024].reshape(nb, ML_HEADS, ML_DQK)
    v = ps[:nb, C_VA:C_VA + 2048].reshape(nb, ML_HEADS, ML_DV)
    o = ps[:nb, C_OA:C_OA + 2048].reshape(nb, ML_HEADS, ML_DV)
    qk_col = jnp.concatenate([q, k], axis=1).transpose(0, 2, 1)
    bias = jnp.concatenate([b_i, b_f]).astype(F32)
    bias_row = jnp.zeros((1, N_GATE), F32).at[0, :2 * ML_HEADS].set(bias)
    g3 = gates[:nb].reshape(nb, 1, N_GATE)
    m03 = m0[l].reshape(nb, 1, ML_HEADS)
    per_b3 = lambda shape: pl.BlockSpec((None,) + shape, lambda b: (b, 0, 0))
    return pl.pallas_call(
        _mlstm_step_kernel,
        grid=(nb,),
        in_specs=[per_b3((ML_HEADS, ML_DQK)), per_b3((ML_HEADS, ML_DQK)),
                  per_b3((ML_HEADS, ML_DV)), per_b3((ML_HEADS, ML_DV)),
                  per_b3((ML_DQK, 2 * ML_HEADS)), per_b3((1, N_GATE)),
                  pl.BlockSpec((1, N_GATE), lambda b: (0, 0)),
                  per_b3((1, ML_HEADS)),
                  pl.BlockSpec((None, None, ML_HEADS, ML_DQK, ML_DV), lambda b: (l, b, 0, 0, 0)),
                  pl.BlockSpec((None, None, ML_HEADS, ML_DQK), lambda b: (l, b, 0, 0))],
        out_specs=[per_b3((1, ML_HEADS * ML_DV)),
                   pl.BlockSpec((None, ML_HEADS, ML_DQK, ML_DV), lambda b: (b, 0, 0, 0)),
                   per_b3((ML_HEADS, ML_DQK)), per_b3((ML_HEADS, ML_DQK))],
        out_shape=[jax.ShapeDtypeStruct((nb, 1, ML_HEADS * ML_DV), F32),
                   jax.ShapeDtypeStruct((nb, ML_HEADS, ML_DQK, ML_DV), F32),
                   jax.ShapeDtypeStruct((nb, ML_HEADS, ML_DQK), F32),
                   jax.ShapeDtypeStruct((nb, ML_HEADS, ML_DQK), F32)],
        compiler_params=_cparams("parallel"),
        name="mlstm_sample",
    )(q, k, v, o, qk_col, g3, bias_row, m03, c0, n0)


def _swa_step_kernel(g0_ref, g1_ref, g2_ref, c0_ref, c1_ref, c2_ref, xb_ref):
    hw = SW_HPG * SW_HD
    per_pos = 2 * SW_HPG
    news = (g0_ref, g1_ref, g2_ref)
    caches = (c0_ref, c1_ref, c2_ref)
    for hh in range(SW_HPG):
        m_run = l_run = acc = None
        for gi in range(len(SW_GROUPS)):
            lo = hh * SW_HD
            q = news[gi][:, lo:lo + SW_HD]
            k_new = news[gi][:, hw + lo:hw + lo + SW_HD]
            v_new = news[gi][:, 2 * hw + lo:2 * hw + lo + SW_HD]
            kc = caches[gi][:, hh, :]
            vc = caches[gi][:, SW_HPG + hh, :]
            s_c = jnp.sum(kc * q, axis=1, keepdims=True)
            s_n = jnp.sum(k_new * q, axis=1, keepdims=True)
            mx = jnp.maximum(jnp.max(s_c, axis=0, keepdims=True), s_n)
            p_c = jnp.exp(s_c - mx)
            p_n = jnp.exp(s_n - mx)
            den = jnp.sum(p_c, axis=0, keepdims=True) + p_n
            part = jnp.sum(p_c * vc, axis=0, keepdims=True) + p_n * v_new
            if gi == 0:
                m_run, l_run, acc = mx, den, part
            else:
                top = jnp.maximum(m_run, mx)
                a, b = jnp.exp(m_run - top), jnp.exp(mx - top)
                m_run, l_run, acc = top, a * l_run + b * den, a * acc + b * part
        xb_ref[:, hh * SW_HD:(hh + 1) * SW_HD] = acc / l_run


def _swa_sample(qkvs, caches, l):
    nb = caches[0].shape[1]
    hw = SW_HPG * SW_HD
    per_pos = 2 * SW_HPG
    news, views, specs = [], [], []
    for (win, dil), cache, qkv in zip(SW_GROUPS, caches, qkvs):
        assert cache.shape[2] == win and win // dil == SW_BLK
        news.append(qkv[0, 0, :nb].reshape(nb, 1, 3 * hw))
        views.append(cache.reshape(cache.shape[0], nb, win // dil, dil * per_pos, SW_HD))
        specs.append(pl.BlockSpec((None, None, SW_BLK, per_pos, SW_HD), lambda b: (l, b, 0, 0, 0)))
    row3 = pl.BlockSpec((None, 1, 3 * hw), lambda b: (b, 0, 0))
    xb = pl.pallas_call(
        _swa_step_kernel,
        grid=(nb,),
        in_specs=[row3, row3, row3] + specs,
        out_specs=pl.BlockSpec((None, 1, hw), lambda b: (b, 0, 0)),
        out_shape=jax.ShapeDtypeStruct((nb, 1, hw), F32),
        compiler_params=_cparams("parallel"),
        name="swa_sample",
    )(*news, *views)
    kvs = [jnp.stack([n[:, :, hw:2 * hw].reshape(nb, 1, SW_HPG, SW_HD),
                      n[:, :, 2 * hw:].reshape(nb, 1, SW_HPG, SW_HD)], axis=2) for n in news]
    return xb.reshape(nb, hw), kvs


def _cm_step_kernel(u_ref, v_ref, gv_ref, ws_ref, bs_ref, o_ref, vn_ref):
    for g in range(CM_GROUPS):
        sl = slice(g * CM_GD, (g + 1) * CM_GD)
        vg = v_ref[:, sl]
        vn = vg * lax.rsqrt(jnp.mean(vg * vg, axis=1, keepdims=True) + RMS_EPS) * gv_ref[g:g + 1, :]
        vn_ref[:, sl] = vn
        w00 = ws_ref[g][0:1, 0:1]
        o_ref[:, sl] = u_ref[:, sl] * (w00 * vn + bs_ref[g:g + 1, 0:1])


def _cm_sample(ps, g_v, w_s, b_s, l):
    rows = ps.shape[0]
    width = CM_GROUPS * CM_GD
    ch = CM_CHUNK
    return pl.pallas_call(
        _cm_step_kernel,
        grid=(1,),
        in_specs=[pl.BlockSpec((rows, width), lambda i: (0, C_UC // width)),
                  pl.BlockSpec((rows, width), lambda i: (0, C_VC // width)),
                  pl.BlockSpec((None, CM_GROUPS, CM_GD), lambda i: (l, 0, 0)),
                  pl.BlockSpec((None, CM_GROUPS, ch, ch), lambda i: (l, 0, 0, 0)),
                  pl.BlockSpec((None, CM_GROUPS, ch), lambda i: (l, 0, 0))],
        out_specs=[pl.BlockSpec((rows, width), lambda i: (0, 0))] * 2,
        out_shape=[jax.ShapeDtypeStruct((rows, width), F32)] * 2,
        compiler_params=_cparams("arbitrary"),
        name="cm_sample",
    )(ps, ps, g_v, w_s, b_s)


def _in_proj(h, w_in, l, tabs, batch, out_dtype, dils, tag):
    wt = jnp.swapaxes(w_in, 1, 2)
    p_a = _mm_t(h, wt, l, 1024, 0, O_GATES // 1024, out_dtype, "in_proj_a" + tag)
    gates = _mm_t(h, wt, l, N_GATE, O_GATES, 1, F32, "gate_proj" + tag)
    qkvs = [_qkv_proj(h, wt, l, gi, dil, tabs, batch, out_dtype, f"qkv_proj{gi}" + tag)
            for gi, dil in enumerate(dils)]
    p_r = _mm_t(h, wt, l, 1024, O_UC, N_REST // 1024, out_dtype, "in_proj_r" + tag)
    return p_a, gates, qkvs, p_r


def _qk_tables(pos):
    tab = _rope_table(pos)
    return jnp.stack([tab * (SW_HD ** -0.5), tab])


def _dense_tail(x, mixers, p, l, mod, rows_per_batch, wts):
    xa, xb, xc = mixers
    merged = _merge(xa, xb, xc, p, wts["w_br_a"], wts["w_br_b"], wts["w_br_c"], l)
    x = _mm_res(merged, wts["w_out"], l, x, mod, 2, rows_per_batch, 512, "out_proj")
    h2 = _norm_mod(x, wts["g_norm2"][l], mod, 4, 3, rows_per_batch)
    act = _ffn1(h2, wts["w_ffn_gate_up"], l)
    return _mm_res(act, wts["w_ffn_down"], l, x, mod, 5, rows_per_batch, 512, "ffn_down", tm=256)


def kernel(x_prompt, x_sample, state_mlstm_C, state_mlstm_n, state_mlstm_m, cache_swa_kv_w128, cache_swa_kv_w512, cache_swa_kv_w2048, c_prompt, c_sample, w_ada, b_ada, g_norm1, g_norm2, g_final, w_in, b_ml_i, b_ml_f, g_cm_v, w_s, b_s, w_br_a, w_br_b, w_br_c, w_out, w_ffn_gate_up, w_ffn_down):
    bp, tp, d = x_prompt.shape
    ns, ts, _ = x_sample.shape
    depth = w_in.shape[0]
    past_len = 16384
    assert d == D_MODEL and ts == 1 and ns == 8 and tp % max(ML_L, SW_GROUPS[-1][0]) == 0
    s_rows = 16
    wts = dict(w_br_a=w_br_a, w_br_b=w_br_b, w_br_c=w_br_c, w_out=w_out, g_norm2=g_norm2,
               w_ffn_gate_up=w_ffn_gate_up, w_ffn_down=w_ffn_down)

    c_all = jnp.zeros((s_rows, d), F32).at[:ns].set(c_sample).at[ns:ns + bp].set(c_prompt)
    mod_all = _ada(c_all, w_ada, b_ada)
    tabs_p = _qk_tables(jnp.arange(tp, dtype=jnp.int32))
    tabs_s = _qk_tables(jnp.full((s_rows,), past_len, jnp.int32))
    caches = (cache_swa_kv_w128, cache_swa_kv_w512, cache_swa_kv_w2048)
    dils = [dil for _, dil in SW_GROUPS]

    xp = x_prompt.reshape(bp * tp, d)
    xs = jnp.zeros((s_rows, d), F32).at[:ns].set(x_sample.reshape(ns, d))
    p_c, p_n, p_m, p_kv = [], [], [], [[], [], []]
    s_c, s_n, s_m, s_kv, s_v = [], [], [], [[], [], []], []
    for l in range(depth):
        mod_p = mod_all[l, ns:ns + bp].reshape(bp, 1, 6 * d)
        mod_s = mod_all[l]
        h = _norm_mod(xp, g_norm1[l], mod_p, 1, 0, tp)
        p_a, gates, qkvs, p_r = _in_proj(h, w_in, l, tabs_p, bp, BF, dils, "")
        xa, c1, n1, m1 = _mlstm_prompt(p_a, gates, b_ml_i[l], b_ml_f[l], bp, tp)
        xb = _swa_prompt(qkvs, bp, tp)
        kvs = [_kv_rows(qkvs[gi], gi, bp, tp) for gi in range(len(SW_GROUPS))]
        xc = _cm_prompt(p_r, g_cm_v, w_s, b_s, l)
        xp = _dense_tail(xp, (xa, xb, xc), p_r, l, mod_p, tp, wts)
        p_c.append(c1)
        p_n.append(n1)
        p_m.append(m1[:, :, 0])
        for gi in range(3):
            p_kv[gi].append(kvs[gi])
        h = _norm_mod(xs, g_norm1[l], mod_s, 1, 0, None)
        p_a, gates, qkvs, p_r = _in_proj(h, w_in, l, tabs_s, 1, F32, [1] * len(SW_GROUPS), "_s")
        xa, c1, n1, m1 = _mlstm_sample(p_a, gates, b_ml_i[l], b_ml_f[l], state_mlstm_C, state_mlstm_n,
                                       state_mlstm_m, l)
        xb, kvs = _swa_sample(qkvs, caches, l)
        xc, vn = _cm_sample(p_r, g_cm_v, w_s, b_s, l)
        pad = lambda a: jnp.zeros((s_rows, a.shape[-1]), F32).at[:ns].set(a.reshape(ns, -1))
        xs = _dense_tail(xs, (pad(xa), pad(xb), xc), p_r, l, mod_s, None, wts)
        s_c.append(c1)
        s_n.append(n1)
        s_m.append(m1[:, :, 0])
        for gi in range(3):
            s_kv[gi].append(kvs[gi])
        s_v.append(vn[:ns].reshape(ns, 1, CM_GROUPS, CM_GD))
    y_p = _norm(xp, g_final).reshape(bp, tp, d)
    y_s = _norm(xs, g_final)[:ns].reshape(ns, ts, d)
    st = jnp.stack
    return (y_p, y_s, st(p_c), st(p_n), st(p_m), st(p_kv[0]), st(p_kv[1]), st(p_kv[2]),
            st(s_c), st(s_n), st(s_m), st(s_kv[0]), st(s_kv[1]), st(s_kv[2]), st(s_v))
```

```python
import functools

import jax
import jax.numpy as jnp
from jax import lax
from jax.experimental import pallas as pl
from jax.experimental.pallas import tpu as pltpu

BF = jnp.bfloat16
F32 = jnp.float32

D_MODEL = 2048
ML_HEADS = 8
ML_DQK = 128
ML_DV = 256
SW_GROUPS = ((128, 1), (512, 4), (2048, 16))
SW_HPG = 4
SW_HEADS = 12
SW_HD = 128
SW_ROT = SW_HD // 4
SW_BLK = 128
ROPE_THETA = 500000.0
CM_CHUNK = 128
CM_GROUPS = 8
CM_GD = 256
D_FF = 5632
RMS_EPS = 1e-6
NEG = -1e30

LANE = 128
O_GATES = 6144
O_QB = O_GATES + 2 * ML_HEADS
O_UC = O_QB + 3 * SW_HEADS * SW_HD
N_REST = 2 * CM_GROUPS * CM_GD + 3 * D_MODEL
N_GATE = LANE
C_QA, C_KA, C_VA, C_OA = 0, 1024, 2048, 4096
C_UC, C_VC, C_G = 0, 2048, 4096

TM = 512
TM_WIDE = 1024
QKV_SUB = 256
SWA_UNITS = 4
ML_L = 128
VMEM_LIMIT = 48 * 1024 * 1024


def _cparams(*sem):
    return pltpu.CompilerParams(dimension_semantics=sem, vmem_limit_bytes=VMEM_LIMIT)


def _sigmoid(x):
    return 1.0 / (1.0 + jnp.exp(-x))


def _log_sigmoid(x):
    return jnp.minimum(x, 0.0) - jnp.log1p(jnp.exp(-jnp.abs(x)))


def _ada_kernel(c_ref, w_ref, b_ref, o_ref):
    c = c_ref[...]
    a = (c * _sigmoid(c)).astype(BF)
    o_ref[...] = jnp.dot(a, w_ref[...].astype(BF), preferred_element_type=F32) + b_ref[...]


def _ada(c_all, w_ada, b_ada):
    depth, d, n = w_ada.shape
    rows = c_all.shape[0]
    tn = 1024
    return pl.pallas_call(
        _ada_kernel,
        grid=(depth, n // tn),
        in_specs=[pl.BlockSpec((rows, d), lambda l, j: (0, 0)),
                  pl.BlockSpec((None, d, tn), lambda l, j: (l, 0, j)),
                  pl.BlockSpec((None, 1, tn), lambda l, j: (l, 0, j))],
        out_specs=pl.BlockSpec((None, rows, tn), lambda l, j: (l, 0, j)),
        out_shape=jax.ShapeDtypeStruct((depth, rows, n), F32),
        compiler_params=_cparams("parallel", "parallel"),
        name="ada",
    )(c_all, w_ada, b_ada.reshape(depth, 1, n))


def _mod_spec(mod, chunk, width, tm, rows_per_batch, row_of, col_of):
    per = D_MODEL // width
    if mod.ndim == 3:
        tiles_per_batch = rows_per_batch // tm
        return pl.BlockSpec((None, 1, width),
                            lambda *g: (row_of(*g) // tiles_per_batch, 0, chunk * per + col_of(*g)))
    return pl.BlockSpec((tm, width), lambda *g: (row_of(*g), chunk * per + col_of(*g)))


def _norm_mod_kernel(x_ref, g_ref, sc_ref, sh_ref, o_ref):
    x = x_ref[...]
    y = x * lax.rsqrt(jnp.mean(x * x, axis=1, keepdims=True) + RMS_EPS) * g_ref[...]
    o_ref[...] = (y * (1.0 + sc_ref[...]) + sh_ref[...]).astype(o_ref.dtype)


def _norm_kernel(x_ref, g_ref, o_ref):
    x = x_ref[...]
    o_ref[...] = x * lax.rsqrt(jnp.mean(x * x, axis=1, keepdims=True) + RMS_EPS) * g_ref[...]


def _norm_mod(x, g, mod, sc_chunk, sh_chunk, rows_per_batch):
    m, d = x.shape
    tm = min(256, m)
    row_of = lambda i: i
    col_of = lambda i: 0
    return pl.pallas_call(
        _norm_mod_kernel,
        grid=(m // tm,),
        in_specs=[pl.BlockSpec((tm, d), lambda i: (i, 0)),
                  pl.BlockSpec((1, d), lambda i: (0, 0)),
                  _mod_spec(mod, sc_chunk, d, tm, rows_per_batch, row_of, col_of),
                  _mod_spec(mod, sh_chunk, d, tm, rows_per_batch, row_of, col_of)],
        out_specs=pl.BlockSpec((tm, d), lambda i: (i, 0)),
        out_shape=jax.ShapeDtypeStruct((m, d), BF),
        compiler_params=_cparams("parallel"),
        name="norm_mod",
    )(x, g.reshape(1, d), mod, mod)


def _norm(x, g):
    m, d = x.shape
    tm = min(256, m)
    return pl.pallas_call(
        _norm_kernel,
        grid=(m // tm,),
        in_specs=[pl.BlockSpec((tm, d), lambda i: (i, 0)),
                  pl.BlockSpec((1, d), lambda i: (0, 0))],
        out_specs=pl.BlockSpec((tm, d), lambda i: (i, 0)),
        out_shape=jax.ShapeDtypeStruct((m, d), F32),
        compiler_params=_cparams("parallel"),
        name="norm_final",
    )(x, g.reshape(1, d))


def _cast_weights_once(pairs):
    @pl.when(pl.program_id(1) == 0)
    def _():
        for src, dst in pairs:
            dst[...] = src[...].astype(BF)


_NT =(((1,), (1,)), ((), ()))


def _wt_spec(k, l, tn, start_of):
    return pl.BlockSpec((pl.Element(1), pl.Element(tn), pl.Element(k)),
                        lambda j, i: (l, pl.multiple_of(start_of(j), 8), 0))


def _mm_t_kernel(x_ref, w_ref, o_ref, wbf):
    _cast_weights_once([(w_ref.at[0], wbf)])
    o_ref[...] = lax.dot_general(x_ref[...], wbf[...], _NT, preferred_element_type=F32).astype(o_ref.dtype)


def _mm_t(x, wt, l, tn, col0, n_tiles, out_dtype, name):
    m, k = x.shape
    tm = min(TM_WIDE, m)
    return pl.pallas_call(
        _mm_t_kernel,
        grid=(n_tiles, m // tm),
        in_specs=[pl.BlockSpec((tm, k), lambda j, i: (i, 0)), _wt_spec(k, l, tn, lambda j: col0 + j * tn)],
        out_specs=pl.BlockSpec((tm, tn), lambda j, i: (i, j)),
        out_shape=jax.ShapeDtypeStruct((m, n_tiles * tn), out_dtype),
        scratch_shapes=[pltpu.VMEM((tn, k), BF)],
        compiler_params=_cparams("arbitrary", "arbitrary"),
        name=name,
    )(x, wt)


def _qkv_kernel(x_ref, w_ref, tab_ref, o_ref, wbf, acc_ref, *, dil):
    _cast_weights_once([(w_ref.at[0], wbf)])
    heads = [slice(hh * SW_HD, (hh + 1) * SW_HD) for hh in range(SW_HPG)]
    tm = x_ref.shape[0]
    sub = min(tm, QKV_SUB)
    for s in range(tm // sub):
        rows_in = slice(s * sub, (s + 1) * sub)
        acc = lax.dot_general(x_ref[rows_in, :], wbf[...], _NT, preferred_element_type=F32)
        tab = tab_ref[rows_in, :]
        for hh, sl in enumerate(heads):
            acc_ref[hh, rows_in, :] = _rope(acc[:, sl], tab)
        n = sub // dil
        for r in range(dil):
            src = pl.ds(s * sub + r, n, stride=dil) if dil > 1 else pl.ds(s * sub, n)
            for hh, sl in enumerate(heads):
                o_ref[r, s * n:(s + 1) * n, sl] = acc_ref[hh, src, :].astype(o_ref.dtype)


def _qkv_proj(x, wt, l, gi, dil, tabs, batch, out_dtype, name):
    m, k = x.shape
    seq = m // batch
    tm = min(TM, seq)
    tpb = seq // tm
    hw = SW_HPG * SW_HD
    wide = len(SW_GROUPS) * hw
    return pl.pallas_call(
        functools.partial(_qkv_kernel, dil=dil),
        grid=(3, m // tm),
        in_specs=[pl.BlockSpec((tm, k), lambda j, i: (i, 0)),
                  _wt_spec(k, l, hw, lambda j: O_QB + j * wide + gi * hw),
                  pl.BlockSpec((None, tm, 3 * SW_HD), lambda j, i: (j, i % tpb, 0))],
        out_specs=pl.BlockSpec((None, dil, tm // dil, hw), lambda j, i: (i // tpb, 0, i % tpb, j)),
        out_shape=jax.ShapeDtypeStruct((batch, dil, seq // dil, 3 * hw), out_dtype),
        scratch_shapes=[pltpu.VMEM((hw, k), BF), pltpu.VMEM((SW_HPG, tm, SW_HD), F32)],
        compiler_params=_cparams("arbitrary", "arbitrary"),
        name=name,
    )(x, wt, tabs)


def _mm_res_kernel(x_ref, w_ref, r_ref, g_ref, o_ref, wbf):
    _cast_weights_once([(w_ref, wbf)])
    acc = jnp.dot(x_ref[...], wbf[...], preferred_element_type=F32)
    o_ref[...] = r_ref[...] + g_ref[...] * acc


def _mm_res(x, w3, l, res, mod, gate_chunk, rows_per_batch, tn, name, tm=TM):
    m, k = x.shape
    n = w3.shape[2]
    tm = min(tm, m)
    return pl.pallas_call(
        _mm_res_kernel,
        grid=(n // tn, m // tm),
        in_specs=[pl.BlockSpec((tm, k), lambda j, i: (i, 0)),
                  pl.BlockSpec((None, k, tn), lambda j, i: (l, 0, j)),
                  pl.BlockSpec((tm, tn), lambda j, i: (i, j)),
                  _mod_spec(mod, gate_chunk, tn, tm, rows_per_batch, lambda j, i: i, lambda j, i: j)],
        out_specs=pl.BlockSpec((tm, tn), lambda j, i: (i, j)),
        out_shape=jax.ShapeDtypeStruct((m, n), F32),
        scratch_shapes=[pltpu.VMEM((k, tn), BF)],
        compiler_params=_cparams("arbitrary", "arbitrary"),
        name=name,
    )(x, w3, res, mod)


def _ffn1_kernel(x_ref, wg_ref, wu_ref, o_ref, wgbf, wubf):
    _cast_weights_once([(wg_ref, wgbf), (wu_ref, wubf)])
    x = x_ref[...]
    g = jnp.dot(x, wgbf[...], preferred_element_type=F32)
    u = jnp.dot(x, wubf[...], preferred_element_type=F32)
    o_ref[...] = (g * _sigmoid(g) * u).astype(o_ref.dtype)


def _ffn1(x, w_gu, l):
    m, k = x.shape
    tn = 512
    nj = D_FF // tn
    tm = min(TM_WIDE, m)
    return pl.pallas_call(
        _ffn1_kernel,
        grid=(nj, m // tm),
        in_specs=[pl.BlockSpec((tm, k), lambda j, i: (i, 0)),
                  pl.BlockSpec((None, k, tn), lambda j, i: (l, 0, j)),
                  pl.BlockSpec((None, k, tn), lambda j, i: (l, 0, nj + j))],
        out_specs=pl.BlockSpec((tm, tn), lambda j, i: (i, j)),
        out_shape=jax.ShapeDtypeStruct((m, D_FF), BF),
        scratch_shapes=[pltpu.VMEM((k, tn), BF), pltpu.VMEM((k, tn), BF)],
        compiler_params=_cparams("arbitrary", "arbitrary"),
        name="ffn_gate_up",
    )(x, w_gu, w_gu)


def _merge_kernel(xa_ref, xb_ref, xc_ref, g0_ref, g1_ref, g2_ref, wa_ref, wb_ref, wc_ref, o_ref,
                  wabf, wbbf, wcbf):
    _cast_weights_once([(wa_ref, wabf), (wb_ref, wbbf), (wc_ref, wcbf)])
    da = jnp.dot(xa_ref[...].astype(BF), wabf[...], preferred_element_type=F32)
    db = jnp.dot(xb_ref[...].astype(BF), wbbf[...], preferred_element_type=F32)
    dc = jnp.dot(xc_ref[...].astype(BF), wcbf[...], preferred_element_type=F32)
    o_ref[...] = (_sigmoid(g0_ref[...].astype(F32)) * da + _sigmoid(g1_ref[...].astype(F32)) * db
                  + _sigmoid(g2_ref[...].astype(F32)) * dc).astype(o_ref.dtype)


def _merge(xa, xb, xc, p, w_a, w_b, w_c, l):
    m = xa.shape[0]
    tn = 512
    tm = min(TM, m)
    ka, kb, kc = xa.shape[1], xb.shape[1], xc.shape[1]
    g_blk = C_G // tn
    per = D_MODEL // tn
    return pl.pallas_call(
        _merge_kernel,
        grid=(D_MODEL // tn, m // tm),
        in_specs=[pl.BlockSpec((tm, ka), lambda j, i: (i, 0)),
                  pl.BlockSpec((tm, kb), lambda j, i: (i, 0)),
                  pl.BlockSpec((tm, kc), lambda j, i: (i, 0)),
                  pl.BlockSpec((tm, tn), lambda j, i: (i, g_blk + j)),
                  pl.BlockSpec((tm, tn), lambda j, i: (i, g_blk + per + j)),
                  pl.BlockSpec((tm, tn), lambda j, i: (i, g_blk + 2 * per + j)),
                  pl.BlockSpec((None, ka, tn), lambda j, i: (l, 0, j)),
                  pl.BlockSpec((None, kb, tn), lambda j, i: (l, 0, j)),
                  pl.BlockSpec((None, kc, tn), lambda j, i: (l, 0, j))],
        out_specs=pl.BlockSpec((tm, tn), lambda j, i: (i, j)),
        out_shape=jax.ShapeDtypeStruct((m, D_MODEL), BF),
        scratch_shapes=[pltpu.VMEM((ka, tn), BF), pltpu.VMEM((kb, tn), BF), pltpu.VMEM((kc, tn), BF)],
        compiler_params=_cparams("arbitrary", "arbitrary"),
        name="branch_merge",
    )(xa, xb, xc, p, p, p, w_a, w_b, w_c)


def _scan(x, axis, op, fill):
    idx = lax.broadcasted_iota(jnp.int32, x.shape, axis)
    shift = 1
    while shift < x.shape[axis]:
        x = op(x, jnp.where(idx >= shift, pltpu.roll(x, shift, axis), fill))
        shift *= 2
    return x


def _mlstm_kernel(q_ref, k_ref, v_ref, o_ref, gc_ref, gr_ref, brow_ref, bcol_ref,
                  xa_ref, c_ref, n_ref, m_ref, nn_ref):
    chunk = pl.program_id(1)

    @pl.when(chunk == 0)
    def _():
        c_ref[...] = jnp.zeros_like(c_ref)
        m_ref[...] = jnp.zeros_like(m_ref)
        nn_ref[...] = jnp.zeros_like(nn_ref)

    L = q_ref.shape[0]
    nh = ML_HEADS
    scale = ML_DQK ** -0.5
    gc = gc_ref[...] + brow_ref[...]
    gr = gr_ref[...] + bcol_ref[...]
    f_c = pltpu.roll(_scan(_log_sigmoid(gc), 0, jnp.add, 0.0), LANE - nh, 1)
    m_prev = m_ref[...]
    m_t = f_c + jnp.maximum(m_prev, _scan(gc - f_c, 0, jnp.maximum, NEG))
    w_inter = jnp.exp(f_c + m_prev - m_t)
    e_neg_m = jnp.exp(-m_t)
    f_minus_m = f_c - m_t
    m_new = m_t[L - 1:L, :]
    f_last = f_c[L - 1:L, :]
    w_last = jnp.exp(f_last - f_c + gc - m_new) * scale
    decay = jnp.exp(f_last + m_prev - m_new)
    m_ref[...] = m_new
    f_r = _scan(_log_sigmoid(gr), 1, jnp.add, 0.0)
    a_r = gr[0:nh, :] - f_r[nh:2 * nh, :]
    row = lax.broadcasted_iota(jnp.int32, (L, L), 0)
    col = lax.broadcasted_iota(jnp.int32, (L, L), 1)
    tri = col <= row
    ones = jnp.ones((L, ML_DQK), BF)
    nt = (((1,), (1,)), ((), ()))
    tn = (((0,), (0,)), ((), ()))
    for h in range(nh):
        lane = lambda x: x[:, h:h + 1]
        d_w = jnp.exp(jnp.where(tri, lane(f_minus_m) + a_r[h:h + 1, :], NEG))
        q = q_ref[:, h * ML_DQK:(h + 1) * ML_DQK]
        k = k_ref[:, h * ML_DQK:(h + 1) * ML_DQK]
        v = v_ref[:, h * ML_DV:(h + 1) * ML_DV]
        s = (lax.dot_general(q, k, nt, preferred_element_type=F32) * (scale * d_w)).astype(BF)
        c_old = c_ref[h]
        nn_old = nn_ref[h]
        wi = lane(w_inter)
        num = (jnp.dot(s, v, preferred_element_type=F32)
               + wi * jnp.dot(q, c_old.astype(BF), preferred_element_type=F32))
        den = (jnp.dot(s, ones, preferred_element_type=F32)
               + wi * jnp.dot(q, nn_old.astype(BF), preferred_element_type=F32))
        inv = 1.0 / jnp.maximum(jnp.abs(den[:, 0:1]), lane(e_neg_m))
        og = o_ref[:, h * ML_DV:(h + 1) * ML_DV].astype(F32)
        xa_ref[:, h * ML_DV:(h + 1) * ML_DV] = (_sigmoid(og) * (num * inv)).astype(xa_ref.dtype)
        kw = (k.astype(F32) * lane(w_last)).astype(BF)
        dk = decay[:, h:h + 1]
        c_ref[h] = dk * c_old + lax.dot_general(kw, v, tn, preferred_element_type=F32)
        nn_ref[h] = dk * nn_old + lax.dot_general(kw, ones, tn, preferred_element_type=F32)

    @pl.when(chunk == pl.num_programs(1) - 1)
    def _():
        for h in range(nh):
            n_ref[h:h + 1, :] = nn_ref[h].T[0:1, :]


def _mlstm_prompt(p, gates, b_i, b_f, batch, seq):
    m_rows = p.shape[0]
    L = ML_L
    nc = seq // L
    bias = jnp.concatenate([b_i, b_f]).astype(F32)
    bias_row = jnp.zeros((1, N_GATE), F32).at[0, :2 * ML_HEADS].set(bias)
    bias_col = bias.reshape(2 * ML_HEADS, 1)
    gates_t = gates[:, :2 * ML_HEADS].T
    hq, hv = ML_HEADS * ML_DQK, ML_HEADS * ML_DV
    return pl.pallas_call(
        _mlstm_kernel,
        grid=(batch, nc),
        in_specs=[pl.BlockSpec((L, hq), lambda b, c: (b * nc + c, C_QA // hq)),
                  pl.BlockSpec((L, hq), lambda b, c: (b * nc + c, C_KA // hq)),
                  pl.BlockSpec((L, hv), lambda b, c: (b * nc + c, C_VA // hv)),
                  pl.BlockSpec((L, hv), lambda b, c: (b * nc + c, C_OA // hv)),
                  pl.BlockSpec((L, N_GATE), lambda b, c: (b * nc + c, 0)),
                  pl.BlockSpec((2 * ML_HEADS, L), lambda b, c: (0, b * nc + c)),
                  pl.BlockSpec((1, N_GATE), lambda b, c: (0, 0)),
                  pl.BlockSpec((2 * ML_HEADS, 1), lambda b, c: (0, 0))],
        out_specs=[pl.BlockSpec((L, hv), lambda b, c: (b * nc + c, 0)),
                   pl.BlockSpec((None, ML_HEADS, ML_DQK, ML_DV), lambda b, c: (b, 0, 0, 0)),
                   pl.BlockSpec((None, ML_HEADS, ML_DQK), lambda b, c: (b, 0, 0)),
                   pl.BlockSpec((None, 1, LANE), lambda b, c: (b, 0, 0))],
        out_shape=[jax.ShapeDtypeStruct((m_rows, hv), BF),
                   jax.ShapeDtypeStruct((batch, ML_HEADS, ML_DQK, ML_DV), F32),
                   jax.ShapeDtypeStruct((batch, ML_HEADS, ML_DQK), F32),
                   jax.ShapeDtypeStruct((batch, 1, LANE), F32)],
        scratch_shapes=[pltpu.VMEM((ML_HEADS, ML_DQK, ML_DQK), F32)],
        compiler_params=_cparams("parallel", "arbitrary"),
        name="mlstm_prompt",
    )(p, p, p, p, gates, gates_t, bias_row, bias_col)


def _rope_table(pos):
    half = SW_ROT // 2
    freq = ROPE_THETA ** (-jnp.arange(half, dtype=F32) * 2.0 / SW_ROT)
    ang = pos.astype(F32)[:, None] * freq[None, :]
    cos, sin = jnp.cos(ang), jnp.sin(ang)
    t = pos.shape[0]
    z = lambda w: jnp.zeros((t, w), F32)
    return jnp.concatenate([cos, cos, jnp.ones((t, SW_HD - SW_ROT), F32),
                            -sin, z(SW_HD - half),
                            z(half), sin, z(SW_HD - SW_ROT)], axis=1)


def _rope(x, tab):
    half = SW_ROT // 2
    return (x * tab[:, 0:SW_HD]
            + pltpu.roll(x, SW_HD - half, 1) * tab[:, SW_HD:2 * SW_HD]
            + pltpu.roll(x, half, 1) * tab[:, 2 * SW_HD:3 * SW_HD])


def _attend(q, k_own, v_own, k_prev, v_prev, prev_shift):
    blk = SW_BLK
    row = lax.broadcasted_iota(jnp.int32, (1, blk, blk), 1)
    col = lax.broadcasted_iota(jnp.int32, (1, blk, blk), 2)
    qk = (((2,), (2,)), ((0,), (0,)))
    pv = (((2,), (1,)), ((0,), (0,)))
    s_o = jnp.where(col <= row, lax.dot_general(q, k_own, qk, preferred_element_type=F32), NEG)
    s_p = jnp.where(col >= row + prev_shift, lax.dot_general(q, k_prev, qk, preferred_element_type=F32), NEG)
    mx = jnp.maximum(jnp.max(s_o, axis=2, keepdims=True), jnp.max(s_p, axis=2, keepdims=True))
    p_o = jnp.exp(s_o - mx)
    p_p = jnp.exp(s_p - mx)
    den = jnp.sum(p_o, axis=2, keepdims=True) + jnp.sum(p_p, axis=2, keepdims=True)
    acc = (lax.dot_general(p_o.astype(BF), v_own, pv, preferred_element_type=F32)
           + lax.dot_general(p_p.astype(BF), v_prev, pv, preferred_element_type=F32))
    return acc, mx, den


def _swa_kernel(q1, k1, v1, k1p, v1p, q2, k2, v2, k2p, v2p, q3, k3, v3, k3p, v3p, xb_ref, o_acc, m_acc, l_acc):
    blk = SW_BLK
    u = SWA_UNITS
    first_shift = jnp.where(pl.program_id(1) > 0, 0, blk)
    wide = lambda x: jnp.broadcast_to(x, x.shape[:-1] + (SW_HD,))

    def merge(rows, acc, mx, den):
        m_old = m_acc[rows, :]
        m_new = jnp.maximum(m_old, mx)
        a = jnp.exp(m_old - m_new)
        b = jnp.exp(mx - m_new)
        o_acc[rows, :] = a * o_acc[rows, :] + b * acc
        l_acc[rows, :] = a * l_acc[rows, :] + b * den
        m_acc[rows, :] = m_new

    def g1_batch(rows, k_p, v_p, shift):
        split = lambda x: x.reshape(u, blk, SW_HD)
        acc, mx, den = _attend(split(q1[rows, :]), split(k1[rows, :]), split(v1[rows, :]),
                               split(k_p), split(v_p), shift)
        o_acc[rows, :] = acc.reshape(u * blk, SW_HD)
        m_acc[rows, :] = wide(mx).reshape(u * blk, SW_HD)
        l_acc[rows, :] = wide(den).reshape(u * blk, SW_HD)

    head = pl.ds(0, (u - 1) * blk)
    unit = lax.broadcasted_iota(jnp.int32, (u, 1, 1), 0)
    g1_batch(pl.ds(0, u * blk),
             jnp.concatenate([k1p[...], k1[head, :]], axis=0), jnp.concatenate([v1p[...], v1[head, :]], axis=0),
             jnp.where(unit == 0, first_shift, 0))

    def g1_body(i, c):
        rows = pl.ds(pl.multiple_of(i * u * blk, blk), u * blk)
        prev = pl.ds(pl.multiple_of(i * u * blk - blk, blk), u * blk)
        g1_batch(rows, k1[prev, :], v1[prev, :], 0)
        return c

    lax.fori_loop(1, q1.shape[0] // (u * blk), g1_body, 0)

    def dilated(q, k, v, kp, vp):
        dil = q.shape[0]
        n_sub = q.shape[1] // blk

        def batch(sub, r0, k_p, v_p, shift):
            rs = pl.ds(r0, u)
            rows = pl.ds(pl.multiple_of(sub * blk, blk), blk)
            acc, mx, den = _attend(q[rs, rows, :], k[rs, rows, :], v[rs, rows, :], k_p, v_p, shift)
            for i in range(u):
                merge(pl.ds(sub * blk * dil + r0 + i, blk, stride=dil), acc[i], wide(mx[i]), wide(den[i]))

        def first(i, c):
            rs = pl.ds(i * u, u)
            batch(0, i * u, kp[rs], vp[rs], first_shift)
            return c

        lax.fori_loop(0, dil // u, first, 0)

        def rest(i, c):
            sub = 1 + i // (dil // u)
            r0 = (i % (dil // u)) * u
            prev = pl.ds(pl.multiple_of((sub - 1) * blk, blk), blk)
            batch(sub, r0, k[pl.ds(r0, u), prev, :], v[pl.ds(r0, u), prev, :], 0)
            return c

        lax.fori_loop(0, (n_sub - 1) * (dil // u), rest, 0)

    dilated(q2, k2, v2, k2p, v2p)
    dilated(q3, k3, v3, k3p, v3p)
    xb_ref[...] = (o_acc[...] * (1.0 / l_acc[...])).astype(xb_ref.dtype)


def _swa_prompt(qkvs, batch, seq):
    blk = SW_BLK
    span = SW_GROUPS[-1][0]
    assert seq % span == 0 and all(w // d == blk for w, d in SW_GROUPS)
    nspan = seq // span
    in_specs = []
    args = []
    for (win, dil), qkv in zip(SW_GROUPS, qkvs):
        rows = span // dil
        per = rows // blk
        own = lambda c: pl.BlockSpec((None, dil, rows, SW_HD), lambda b, s, h, c=c: (b, 0, s, c * SW_HPG + h))
        prev = lambda c: pl.BlockSpec(
            (None, dil, blk, SW_HD), lambda b, s, h, c=c, per=per: (b, 0, jnp.maximum(s * per - 1, 0), c * SW_HPG + h))
        if dil == 1:
            own = lambda c: pl.BlockSpec((None, None, span, SW_HD), lambda b, s, h, c=c: (b, 0, s, c * SW_HPG + h))
            prev = lambda c: pl.BlockSpec(
                (None, None, blk, SW_HD),
                lambda b, s, h, c=c, per=per: (b, 0, jnp.maximum(s * per - 1, 0), c * SW_HPG + h))
        in_specs += [own(0), own(1), own(2), prev(1), prev(2)]
        args += [qkv] * 5
    return pl.pallas_call(
        _swa_kernel,
        grid=(batch, nspan, SW_HPG),
        in_specs=in_specs,
        out_specs=pl.BlockSpec((span, SW_HD), lambda b, s, h: (b * nspan + s, h)),
        out_shape=jax.ShapeDtypeStruct((batch * seq, SW_HPG * SW_HD), BF),
        scratch_shapes=[pltpu.VMEM((span, SW_HD), F32)] * 3,
        compiler_params=_cparams("parallel", "parallel", "parallel"),
        name="swa_prompt",
    )(*args)


def _kv_rows_kernel(k_ref, v_ref, o_ref, *, dil):
    for r in range(dil):
        rows = pl.ds(r, SW_BLK, stride=dil) if dil > 1 else pl.ds(0, SW_BLK)
        for hh in range(SW_HPG):
            sl = slice(hh * SW_HD, (hh + 1) * SW_HD)
            o_ref[hh, rows, :] = k_ref[r, :, sl].astype(F32)
            o_ref[SW_HPG + hh, rows, :] = v_ref[r, :, sl].astype(F32)


def _kv_rows(qkv, gi, batch, seq):
    win, dil = SW_GROUPS[gi]
    hw = SW_HPG * SW_HD
    last = seq // dil // SW_BLK - 1
    out = pl.pallas_call(
        functools.partial(_kv_rows_kernel, dil=dil),
        grid=(batch,),
        in_specs=[pl.BlockSpec((None, dil, SW_BLK, hw), lambda b: (b, 0, last, 1)),
                  pl.BlockSpec((None, dil, SW_BLK, hw), lambda b: (b, 0, last, 2))],
        out_specs=pl.BlockSpec((None, 2 * SW_HPG, win, SW_HD), lambda b: (b, 0, 0, 0)),
        out_shape=jax.ShapeDtypeStruct((batch, 2 * SW_HPG, win, SW_HD), F32),
        compiler_params=_cparams("parallel"),
        name=f"kv_rows_w{win}",
    )(qkv, qkv)
    return out.reshape(batch, 2, SW_HPG, win, SW_HD).transpose(0, 3, 1, 2, 4)


def _cm_kernel(u_ref, v_ref, gv_ref, ws_ref, bs_ref, o_ref):
    ch = CM_CHUNK
    row = lax.broadcasted_iota(jnp.int32, (ch, ch), 0)
    col = lax.broadcasted_iota(jnp.int32, (ch, ch), 1)
    tri = col <= row
    for g in range(CM_GROUPS):
        sl = slice(g * CM_GD, (g + 1) * CM_GD)
        vg = v_ref[:, sl].astype(F32)
        vn = vg * lax.rsqrt(jnp.mean(vg * vg, axis=1, keepdims=True) + RMS_EPS) * gv_ref[g:g + 1, :]
        w = jnp.where(tri, ws_ref[g], 0.0).astype(BF)
        mixed = jnp.dot(w, vn.astype(BF), preferred_element_type=F32) + bs_ref[:, g:g + 1]
        o_ref[:, sl] = (u_ref[:, sl].astype(F32) * mixed).astype(o_ref.dtype)


def _cm_prompt(p, g_v, w_s, b_s, l):
    m_rows = p.shape[0]
    ch = CM_CHUNK
    width = CM_GROUPS * CM_GD
    return pl.pallas_call(
        _cm_kernel,
        grid=(m_rows // ch,),
        in_specs=[pl.BlockSpec((ch, width), lambda i: (i, C_UC // width)),
                  pl.BlockSpec((ch, width), lambda i: (i, C_VC // width)),
                  pl.BlockSpec((None, CM_GROUPS, CM_GD), lambda i: (l, 0, 0)),
                  pl.BlockSpec((None, CM_GROUPS, ch, ch), lambda i: (l, 0, 0, 0)),
                  pl.BlockSpec((ch, CM_GROUPS), lambda i: (0, 0))],
        out_specs=pl.BlockSpec((ch, width), lambda i: (i, 0)),
        out_shape=jax.ShapeDtypeStruct((m_rows, width), BF),
        compiler_params=_cparams("parallel"),
        name="cm_prompt",
    )(p, p, g_v, w_s, b_s[l].T)


def _mlstm_step_kernel(qr_ref, kr_ref, vr_ref, or_ref, qkc_ref, g_ref, brow_ref, m0_ref, c0_ref, n0_ref,
                       xa_ref, c1_ref, n1_ref, m1_ref):
    scale = ML_DQK ** -0.5
    gates = g_ref[...] + brow_ref[...]
    m0 = m0_ref[...]
    for h in range(ML_HEADS):
        ig = gates[:, h:h + 1]
        lf = _log_sigmoid(gates[:, ML_HEADS + h:ML_HEADS + h + 1])
        m_prev = m0[:, h:h + 1]
        inter = lf + m_prev
        m_t = jnp.maximum(inter, ig)
        d_w = jnp.exp(ig - m_t)
        w_inter = jnp.exp(inter - m_t)
        q_row = qr_ref[h:h + 1, :]
        k_row = kr_ref[h:h + 1, :]
        v_row = vr_ref[h:h + 1, :]
        q_col = qkc_ref[:, h:h + 1]
        k_col = qkc_ref[:, ML_HEADS + h:ML_HEADS + h + 1]
        c_old = c0_ref[h]
        n_old = n0_ref[h:h + 1, :]
        s = jnp.sum(q_row * k_row, axis=1, keepdims=True) * scale * d_w
        qc = jnp.sum(q_col * c_old, axis=0, keepdims=True)
        num = s * v_row + w_inter * qc
        den = s + w_inter * jnp.sum(q_row * n_old, axis=1, keepdims=True)
        hh = num / jnp.maximum(jnp.abs(den), jnp.exp(-m_t))
        xa_ref[:, h * ML_DV:(h + 1) * ML_DV] = _sigmoid(or_ref[h:h + 1, :]) * hh
        w_last = d_w * scale
        decay = w_inter
        c1_ref[h] = decay * c_old + (k_col * w_last) * v_row
        n1_ref[h:h + 1, :] = decay * n_old + w_last * k_row
        m1_ref[h:h + 1, :] = jnp.broadcast_to(m_t, (1, ML_DQK))


def _mlstm_sample(ps, gates, b_i, b_f, c0, n0, m0, l):
    nb = c0.shape[1]
    q = ps[:nb, C_QA:C_QA + 1024].reshape(nb, ML_HEADS, ML_DQK)
    k = ps[:nb, C_KA:C_KA + 1024].reshape(nb, ML_HEADS, ML_DQK)
    v = ps[:nb, C_VA:C_VA + 2048].reshape(nb, ML_HEADS, ML_DV)
    o = ps[:nb, C_OA:C_OA + 2048].reshape(nb, ML_HEADS, ML_DV)
    qk_col = jnp.concatenate([q, k], axis=1).transpose(0, 2, 1)
    bias = jnp.concatenate([b_i, b_f]).astype(F32)
    bias_row = jnp.zeros((1, N_GATE), F32).at[0, :2 * ML_HEADS].set(bias)
    g3 = gates[:nb].reshape(nb, 1, N_GATE)
    m03 = m0[l].reshape(nb, 1, ML_HEADS)
    per_b3 = lambda shape: pl.BlockSpec((None,) + shape, lambda b: (b, 0, 0))
    return pl.pallas_call(
        _mlstm_step_kernel,
        grid=(nb,),
        in_specs=[per_b3((ML_HEADS, ML_DQK)), per_b3((ML_HEADS, ML_DQK)),
                  per_b3((ML_HEADS, ML_DV)), per_b3((ML_HEADS, ML_DV)),
                  per_b3((ML_DQK, 2 * ML_HEADS)), per_b3((1, N_GATE)),
                  pl.BlockSpec((1, N_GATE), lambda b: (0, 0)),
                  per_b3((1, ML_HEADS)),
                  pl.BlockSpec((None, None, ML_HEADS, ML_DQK, ML_DV), lambda b: (l, b, 0, 0, 0)),
                  pl.BlockSpec((None, None, ML_HEADS, ML_DQK), lambda b: (l, b, 0, 0))],
        out_specs=[per_b3((1, ML_HEADS * ML_DV)),
                   pl.BlockSpec((None, ML_HEADS, ML_DQK, ML_DV), lambda b: (b, 0, 0, 0)),
                   per_b3((ML_HEADS, ML_DQK)), per_b3((ML_HEADS, ML_DQK))],
        out_shape=[jax.ShapeDtypeStruct((nb, 1, ML_HEADS * ML_DV), F32),
                   jax.ShapeDtypeStruct((nb, ML_HEADS, ML_DQK, ML_DV), F32),
                   jax.ShapeDtypeStruct((nb, ML_HEADS, ML_DQK), F32),
                   jax.ShapeDtypeStruct((nb, ML_HEADS, ML_DQK), F32)],
        compiler_params=_cparams("parallel"),
        name="mlstm_sample",
    )(q, k, v, o, qk_col, g3, bias_row, m03, c0, n0)


def _swa_step_kernel(g0_ref, g1_ref, g2_ref, c0_ref, c1_ref, c2_ref, xb_ref):
    hw = SW_HPG * SW_HD
    per_pos = 2 * SW_HPG
    news = (g0_ref, g1_ref, g2_ref)
    caches = (c0_ref, c1_ref, c2_ref)
    for hh in range(SW_HPG):
        m_run = l_run = acc = None
        for gi in range(len(SW_GROUPS)):
            lo = hh * SW_HD
            q = news[gi][:, lo:lo + SW_HD]
            k_new = news[gi][:, hw + lo:hw + lo + SW_HD]
            v_new = news[gi][:, 2 * hw + lo:2 * hw + lo + SW_HD]
            kc = caches[gi][:, hh, :]
            vc = caches[gi][:, SW_HPG + hh, :]
            s_c = jnp.sum(kc * q, axis=1, keepdims=True)
            s_n = jnp.sum(k_new * q, axis=1, keepdims=True)
            mx = jnp.maximum(jnp.max(s_c, axis=0, keepdims=True), s_n)
            p_c = jnp.exp(s_c - mx)
            p_n = jnp.exp(s_n - mx)
            den = jnp.sum(p_c, axis=0, keepdims=True) + p_n
            part = jnp.sum(p_c * vc, axis=0, keepdims=True) + p_n * v_new
            if gi == 0:
                m_run, l_run, acc = mx, den, part
            else:
                top = jnp.maximum(m_run, mx)
                a, b = jnp.exp(m_run - top), jnp.exp(mx - top)
                m_run, l_run, acc = top, a * l_run + b * den, a * acc + b * part
        xb_ref[:, hh * SW_HD:(hh + 1) * SW_HD] = acc / l_run


def _swa_sample(qkvs, caches, l):
    nb = caches[0].shape[1]
    hw = SW_HPG * SW_HD
    per_pos = 2 * SW_HPG
    news, views, specs = [], [], []
    for (win, dil), cache, qkv in zip(SW_GROUPS, caches, qkvs):
        assert cache.shape[2] == win and win // dil == SW_BLK
        news.append(qkv[0, 0, :nb].reshape(nb, 1, 3 * hw))
        views.append(cache.reshape(cache.shape[0], nb, win // dil, dil * per_pos, SW_HD))
        specs.append(pl.BlockSpec((None, None, SW_BLK, per_pos, SW_HD), lambda b: (l, b, 0, 0, 0)))
    row3 = pl.BlockSpec((None, 1, 3 * hw), lambda b: (b, 0, 0))
    xb = pl.pallas_call(
        _swa_step_kernel,
        grid=(nb,),
        in_specs=[row3, row3, row3] + specs,
        out_specs=pl.BlockSpec((None, 1, hw), lambda b: (b, 0, 0)),
        out_shape=jax.ShapeDtypeStruct((nb, 1, hw), F32),
        compiler_params=_cparams("parallel"),
        name="swa_sample",
    )(*news, *views)
    kvs = [jnp.stack([n[:, :, hw:2 * hw].reshape(nb, 1, SW_HPG, SW_HD),
                      n[:, :, 2 * hw:].reshape(nb, 1, SW_HPG, SW_HD)], axis=2) for n in news]
    return xb.reshape(nb, hw), kvs


def _cm_step_kernel(u_ref, v_ref, gv_ref, ws_ref, bs_ref, o_ref, vn_ref):
    for g in range(CM_GROUPS):
        sl = slice(g * CM_GD, (g + 1) * CM_GD)
        vg = v_ref[:, sl]
        vn = vg * lax.rsqrt(jnp.mean(vg * vg, axis=1, keepdims=True) + RMS_EPS) * gv_ref[g:g + 1, :]
        vn_ref[:, sl] = vn
        w00 = ws_ref[g][0:1, 0:1]
        o_ref[:, sl] = u_ref[:, sl] * (w00 * vn + bs_ref[g:g + 1, 0:1])


def _cm_sample(ps, g_v, w_s, b_s, l):
    rows = ps.shape[0]
    width = CM_GROUPS * CM_GD
    ch = CM_CHUNK
    return pl.pallas_call(
        _cm_step_kernel,
        grid=(1,),
        in_specs=[pl.BlockSpec((rows, width), lambda i: (0, C_UC // width)),
                  pl.BlockSpec((rows, width), lambda i: (0, C_VC // width)),
                  pl.BlockSpec((None, CM_GROUPS, CM_GD), lambda i: (l, 0, 0)),
                  pl.BlockSpec((None, CM_GROUPS, ch, ch), lambda i: (l, 0, 0, 0)),
                  pl.BlockSpec((None, CM_GROUPS, ch), lambda i: (l, 0, 0))],
        out_specs=[pl.BlockSpec((rows, width), lambda i: (0, 0))] * 2,
        out_shape=[jax.ShapeDtypeStruct((rows, width), F32)] * 2,
        compiler_params=_cparams("arbitrary"),
        name="cm_sample",
    )(ps, ps, g_v, w_s, b_s)


def _in_proj(h, w_in, l, tabs, batch, out_dtype, dils, tag):
    wt = jnp.swapaxes(w_in, 1, 2)
    p_a = _mm_t(h, wt, l, 1024, 0, O_GATES // 1024, out_dtype, "in_proj_a" + tag)
    gates = _mm_t(h, wt, l, N_GATE, O_GATES, 1, F32, "gate_proj" + tag)
    qkvs = [_qkv_proj(h, wt, l, gi, dil, tabs, batch, out_dtype, f"qkv_proj{gi}" + tag)
            for gi, dil in enumerate(dils)]
    p_r = _mm_t(h, wt, l, 1024, O_UC, N_REST // 1024, out_dtype, "in_proj_r" + tag)
    return p_a, gates, qkvs, p_r


def _qk_tables(pos):
    tab = _rope_table(pos)
    ident = jnp.zeros_like(tab).at[:, :SW_HD].set(1.0)
    return jnp.stack([tab * (SW_HD ** -0.5), tab, ident])


def _dense_tail(x, mixers, p, l, mod, rows_per_batch, wts):
    xa, xb, xc = mixers
    merged = _merge(xa, xb, xc, p, wts["w_br_a"], wts["w_br_b"], wts["w_br_c"], l)
    x = _mm_res(merged, wts["w_out"], l, x, mod, 2, rows_per_batch, 1024, "out_proj")
    h2 = _norm_mod(x, wts["g_norm2"][l], mod, 4, 3, rows_per_batch)
    act = _ffn1(h2, wts["w_ffn_gate_up"], l)
    return _mm_res(act, wts["w_ffn_down"], l, x, mod, 5, rows_per_batch, 512, "ffn_down", tm=256)


def kernel(x_prompt, x_sample, state_mlstm_C, state_mlstm_n, state_mlstm_m, cache_swa_kv_w128, cache_swa_kv_w512, cache_swa_kv_w2048, c_prompt, c_sample, w_ada, b_ada, g_norm1, g_norm2, g_final, w_in, b_ml_i, b_ml_f, g_cm_v, w_s, b_s, w_br_a, w_br_b, w_br_c, w_out, w_ffn_gate_up, w_ffn_down):
    bp, tp, d = x_prompt.shape
    ns, ts, _ = x_sample.shape
    depth = w_in.shape[0]
    past_len = 16384
    assert d == D_MODEL and ts == 1 and ns == 8 and tp % max(ML_L, SW_GROUPS[-1][0]) == 0
    s_rows = 16
    wts = dict(w_br_a=w_br_a, w_br_b=w_br_b, w_br_c=w_br_c, w_out=w_out, g_norm2=g_norm2,
               w_ffn_gate_up=w_ffn_gate_up, w_ffn_down=w_ffn_down)

    c_all = jnp.zeros((s_rows, d), F32).at[:ns].set(c_sample).at[ns:ns + bp].set(c_prompt)
    mod_all = _ada(c_all, w_ada, b_ada)
    tabs_p = _qk_tables(jnp.arange(tp, dtype=jnp.int32))
    tabs_s = _qk_tables(jnp.full((s_rows,), past_len, jnp.int32))
    caches = (cache_swa_kv_w128, cache_swa_kv_w512, cache_swa_kv_w2048)
    dils = [dil for _, dil in SW_GROUPS]

    xp = x_prompt.reshape(bp * tp, d)
    xs = jnp.zeros((s_rows, d), F32).at[:ns].set(x_sample.reshape(ns, d))
    p_c, p_n, p_m, p_kv = [], [], [], [[], [], []]
    s_c, s_n, s_m, s_kv, s_v = [], [], [], [[], [], []], []
    for l in range(depth):
        mod_p = mod_all[l, ns:ns + bp].reshape(bp, 1, 6 * d)
        mod_s = mod_all[l]
        h = _norm_mod(xp, g_norm1[l], mod_p, 1, 0, tp)
        p_a, gates, qkvs, p_r = _in_proj(h, w_in, l, tabs_p, bp, BF, dils, "")
        xa, c1, n1, m1 = _mlstm_prompt(p_a, gates, b_ml_i[l], b_ml_f[l], bp, tp)
        xb = _swa_prompt(qkvs, bp, tp)
        kvs = [_kv_rows(qkvs[gi], gi, bp, tp) for gi in range(len(SW_GROUPS))]
        xc = _cm_prompt(p_r, g_cm_v, w_s, b_s, l)
        xp = _dense_tail(xp, (xa, xb, xc), p_r, l, mod_p, tp, wts)
        p_c.append(c1)
        p_n.append(n1)
        p_m.append(m1[:, 0, :ML_HEADS])
        for gi in range(3):
            p_kv[gi].append(kvs[gi])
        h = _norm_mod(xs, g_norm1[l], mod_s, 1, 0, None)
        p_a, gates, qkvs, p_r = _in_proj(h, w_in, l, tabs_s, 1, F32, [1] * len(SW_GROUPS), "_s")
        xa, c1, n1, m1 = _mlstm_sample(p_a, gates, b_ml_i[l], b_ml_f[l], state_mlstm_C, state_mlstm_n,
                                       state_mlstm_m, l)
        xb, kvs = _swa_sample(qkvs, caches, l)
        xc, vn = _cm_sample(p_r, g_cm_v, w_s, b_s, l)
        pad = lambda a: jnp.zeros((s_rows, a.shape[-1]), F32).at[:ns].set(a.reshape(ns, -1))
        xs = _dense_tail(xs, (pad(xa), pad(xb), xc), p_r, l, mod_s, None, wts)
        s_c.append(c1)
        s_n.append(n1)
        s_m.append(m1[:, :, 0])
        for gi in range(3):
            s_kv[gi].append(kvs[gi])
        s_v.append(vn[:ns].reshape(ns, 1, CM_GROUPS, CM_GD))
    y_p = _norm(xp, g_final).reshape(bp, tp, d)
    y_s = _norm(xs, g_final)[:ns].reshape(ns, ts, d)
    st = jnp.stack
    return (y_p, y_s, st(p_c), st(p_n), st(p_m), st(p_kv[0]), st(p_kv[1]), st(p_kv[2]),
            st(s_c), st(s_n), st(s_m), st(s_kv[0]), st(s_kv[1]), st(s_kv[2]), st(s_v))
```

```python
import functools

import jax
import jax.numpy as jnp
from jax import lax
from jax.experimental import pallas as pl
from jax.experimental.pallas import tpu as pltpu

BF = jnp.bfloat16
F32 = jnp.float32

D_MODEL = 2048
ML_HEADS = 8
ML_DQK = 128
ML_DV = 256
SW_GROUPS = ((128, 1), (512, 4), (2048, 16))
SW_HPG = 4
SW_HEADS = 12
SW_HD = 128
SW_ROT = SW_HD // 4
SW_BLK = 128
ROPE_THETA = 500000.0
CM_CHUNK = 128
CM_GROUPS = 8
CM_GD = 256
D_FF = 5632
RMS_EPS = 1e-6
NEG = -1e30

LANE = 128
O_GATES = 6144
O_QB = O_GATES + 2 * ML_HEADS
O_UC = O_QB + 3 * SW_HEADS * SW_HD
N_REST = 2 * CM_GROUPS * CM_GD + 3 * D_MODEL
N_GATE = LANE
C_QA, C_KA, C_VA, C_OA = 0, 1024, 2048, 4096
C_UC, C_VC, C_G = 0, 2048, 4096

TM = 512
TM_WIDE = 1024
QKV_SUB = 256
SWA_UNITS = 4
ML_L = 128
VMEM_LIMIT = 48 * 1024 * 1024
VMEM_LIMIT_BIG = 56 * 1024 * 1024


def _cparams(*sem, vmem=VMEM_LIMIT):
    return pltpu.CompilerParams(dimension_semantics=sem, vmem_limit_bytes=vmem)


def _sigmoid(x):
    return 1.0 / (1.0 + jnp.exp(-x))


def _log_sigmoid(x):
    return jnp.minimum(x, 0.0) - jnp.log1p(jnp.exp(-jnp.abs(x)))


def _ada_kernel(c_ref, w_ref, b_ref, o_ref):
    c = c_ref[...]
    a = (c * _sigmoid(c)).astype(BF)
    o_ref[...] = jnp.dot(a, w_ref[...].astype(BF), preferred_element_type=F32) + b_ref[...]


def _ada(c_all, w_ada, b_ada):
    depth, d, n = w_ada.shape
    rows = c_all.shape[0]
    tn = 1024
    return pl.pallas_call(
        _ada_kernel,
        grid=(depth, n // tn),
        in_specs=[pl.BlockSpec((rows, d), lambda l, j: (0, 0)),
                  pl.BlockSpec((None, d, tn), lambda l, j: (l, 0, j)),
                  pl.BlockSpec((None, 1, tn), lambda l, j: (l, 0, j))],
        out_specs=pl.BlockSpec((None, rows, tn), lambda l, j: (l, 0, j)),
        out_shape=jax.ShapeDtypeStruct((depth, rows, n), F32),
        compiler_params=_cparams("parallel", "parallel"),
        name="ada",
    )(c_all, w_ada, b_ada.reshape(depth, 1, n))


def _mod_spec(mod, chunk, width, tm, rows_per_batch, row_of, col_of):
    per = D_MODEL // width
    if mod.ndim == 3:
        tiles_per_batch = rows_per_batch // tm
        return pl.BlockSpec((None, 1, width),
                            lambda *g: (row_of(*g) // tiles_per_batch, 0, chunk * per + col_of(*g)))
    return pl.BlockSpec((tm, width), lambda *g: (row_of(*g), chunk * per + col_of(*g)))


NORM_ROWS = 16


def _row_groups(n_rows, body):
    def step(i, c):
        body(pl.ds(pl.multiple_of(i * NORM_ROWS, NORM_ROWS), NORM_ROWS))
        return c
    groups = n_rows // NORM_ROWS
    lax.fori_loop(0, groups, step, 0, unroll=min(groups, 4))


def _norm_mod_kernel(x_ref, g_ref, sc_ref, sh_ref, o_ref):
    per_row = sc_ref.shape[0] > 1

    def body(rows):
        x = x_ref[rows, :]
        y = x * lax.rsqrt(jnp.mean(x * x, axis=1, keepdims=True) + RMS_EPS) * g_ref[...]
        sc = sc_ref[rows, :] if per_row else sc_ref[...]
        sh = sh_ref[rows, :] if per_row else sh_ref[...]
        o_ref[rows, :] = (y * (1.0 + sc) + sh).astype(o_ref.dtype)

    _row_groups(x_ref.shape[0], body)


def _norm_kernel(x_ref, g_ref, o_ref):
    def body(rows):
        x = x_ref[rows, :]
        o_ref[rows, :] = x * lax.rsqrt(jnp.mean(x * x, axis=1, keepdims=True) + RMS_EPS) * g_ref[...]

    _row_groups(x_ref.shape[0], body)


def _norm_mod(x, g, mod, sc_chunk, sh_chunk, rows_per_batch):
    m, d = x.shape
    tm = min(TM, m)
    row_of = lambda i: i
    col_of = lambda i: 0
    return pl.pallas_call(
        _norm_mod_kernel,
        grid=(m // tm,),
        in_specs=[pl.BlockSpec((tm, d), lambda i: (i, 0)),
                  pl.BlockSpec((1, d), lambda i: (0, 0)),
                  _mod_spec(mod, sc_chunk, d, tm, rows_per_batch, row_of, col_of),
                  _mod_spec(mod, sh_chunk, d, tm, rows_per_batch, row_of, col_of)],
        out_specs=pl.BlockSpec((tm, d), lambda i: (i, 0)),
        out_shape=jax.ShapeDtypeStruct((m, d), BF),
        compiler_params=_cparams("parallel"),
        name="norm_mod",
    )(x, g.reshape(1, d), mod, mod)


def _norm(x, g):
    m, d = x.shape
    tm = min(TM, m)
    return pl.pallas_call(
        _norm_kernel,
        grid=(m // tm,),
        in_specs=[pl.BlockSpec((tm, d), lambda i: (i, 0)),
                  pl.BlockSpec((1, d), lambda i: (0, 0))],
        out_specs=pl.BlockSpec((tm, d), lambda i: (i, 0)),
        out_shape=jax.ShapeDtypeStruct((m, d), F32),
        compiler_params=_cparams("parallel"),
        name="norm_final",
    )(x, g.reshape(1, d))


def _cast_weights_once(pairs):
    @pl.when(pl.program_id(1) == 0)
    def _():
        for src, dst in pairs:
            dst[...] = src[...].astype(BF)


_NT =(((1,), (1,)), ((), ()))


def _wt_spec(k, l, tn, start_of):
    return pl.BlockSpec((pl.Element(1), pl.Element(tn), pl.Element(k)),
                        lambda j, i: (l, pl.multiple_of(start_of(j), 8), 0))


def _mm_t_kernel(x_ref, w_ref, o_ref, wbf):
    _cast_weights_once([(w_ref.at[0], wbf)])
    o_ref[...] = lax.dot_general(x_ref[...], wbf[...], _NT, preferred_element_type=F32).astype(o_ref.dtype)


def _mm_t(x, wt, l, tn, col0, n_tiles, out_dtype, name):
    m, k = x.shape
    tm = min(TM_WIDE, m)
    return pl.pallas_call(
        _mm_t_kernel,
        grid=(n_tiles, m // tm),
        in_specs=[pl.BlockSpec((tm, k), lambda j, i: (i, 0)), _wt_spec(k, l, tn, lambda j: col0 + j * tn)],
        out_specs=pl.BlockSpec((tm, tn), lambda j, i: (i, j)),
        out_shape=jax.ShapeDtypeStruct((m, n_tiles * tn), out_dtype),
        scratch_shapes=[pltpu.VMEM((tn, k), BF)],
        compiler_params=_cparams("arbitrary", "arbitrary"),
        name=name,
    )(x, wt)


def _qkv_kernel(x_ref, w_ref, tab_ref, o_ref, wbf, acc_ref, *, dil):
    _cast_weights_once([(w_ref.at[0], wbf)])
    heads = [slice(hh * SW_HD, (hh + 1) * SW_HD) for hh in range(SW_HPG)]
    tm = x_ref.shape[0]
    sub = min(tm, QKV_SUB)
    for s in range(tm // sub):
        rows_in = slice(s * sub, (s + 1) * sub)
        acc = lax.dot_general(x_ref[rows_in, :], wbf[...], _NT, preferred_element_type=F32)
        tab = tab_ref[rows_in, :]
        for hh, sl in enumerate(heads):
            acc_ref[hh, rows_in, :] = _rope(acc[:, sl], tab)
        n = sub // dil
        for r in range(dil):
            src = pl.ds(s * sub + r, n, stride=dil) if dil > 1 else pl.ds(s * sub, n)
            for hh, sl in enumerate(heads):
                o_ref[r, s * n:(s + 1) * n, sl] = acc_ref[hh, src, :].astype(o_ref.dtype)


def _qkv_proj(x, wt, l, gi, dil, tabs, batch, out_dtype, name):
    m, k = x.shape
    seq = m // batch
    tm = min(TM, seq)
    tpb = seq // tm
    hw = SW_HPG * SW_HD
    wide = len(SW_GROUPS) * hw
    return pl.pallas_call(
        functools.partial(_qkv_kernel, dil=dil),
        grid=(3, m // tm),
        in_specs=[pl.BlockSpec((tm, k), lambda j, i: (i, 0)),
                  _wt_spec(k, l, hw, lambda j: O_QB + j * wide + gi * hw),
                  pl.BlockSpec((None, tm, 3 * SW_HD), lambda j, i: (j, i % tpb, 0))],
        out_specs=pl.BlockSpec((None, dil, tm // dil, hw), lambda j, i: (i // tpb, 0, i % tpb, j)),
        out_shape=jax.ShapeDtypeStruct((batch, dil, seq // dil, 3 * hw), out_dtype),
        scratch_shapes=[pltpu.VMEM((hw, k), BF), pltpu.VMEM((SW_HPG, tm, SW_HD), F32)],
        compiler_params=_cparams("arbitrary", "arbitrary"),
        name=name,
    )(x, wt, tabs)


def _mm_res_kernel(x_ref, w_ref, r_ref, g_ref, o_ref, wbf):
    _cast_weights_once([(w_ref, wbf)])
    acc = jnp.dot(x_ref[...], wbf[...], preferred_element_type=F32)
    o_ref[...] = r_ref[...] + g_ref[...] * acc


def _mm_res(x, w3, l, res, mod, gate_chunk, rows_per_batch, tn, name, tm=TM, big_weight=False):
    m, k = x.shape
    n = w3.shape[2]
    tm = min(tm, m)
    w_mode = dict(pipeline_mode=pl.Buffered(1)) if big_weight else {}
    return pl.pallas_call(
        _mm_res_kernel,
        grid=(n // tn, m // tm),
        in_specs=[pl.BlockSpec((tm, k), lambda j, i: (i, 0)),
                  pl.BlockSpec((None, k, tn), lambda j, i: (l, 0, j), **w_mode),
                  pl.BlockSpec((tm, tn), lambda j, i: (i, j)),
                  _mod_spec(mod, gate_chunk, tn, tm, rows_per_batch, lambda j, i: i, lambda j, i: j)],
        out_specs=pl.BlockSpec((tm, tn), lambda j, i: (i, j)),
        out_shape=jax.ShapeDtypeStruct((m, n), F32),
        scratch_shapes=[pltpu.VMEM((k, tn), BF)],
        compiler_params=_cparams("arbitrary", "arbitrary", vmem=VMEM_LIMIT_BIG if big_weight else VMEM_LIMIT),
        name=name,
    )(x, w3, res, mod)


def _ffn1_kernel(x_ref, wg_ref, wu_ref, o_ref, wgbf, wubf):
    _cast_weights_once([(wg_ref, wgbf), (wu_ref, wubf)])
    x = x_ref[...]
    g = jnp.dot(x, wgbf[...], preferred_element_type=F32)
    u = jnp.dot(x, wubf[...], preferred_element_type=F32)
    o_ref[...] = (g * _sigmoid(g) * u).astype(o_ref.dtype)


def _ffn1(x, w_gu, l):
    m, k = x.shape
    tn = 512
    nj = D_FF // tn
    tm = min(TM_WIDE, m)
    return pl.pallas_call(
        _ffn1_kernel,
        grid=(nj, m // tm),
        in_specs=[pl.BlockSpec((tm, k), lambda j, i: (i, 0)),
                  pl.BlockSpec((None, k, tn), lambda j, i: (l, 0, j)),
                  pl.BlockSpec((None, k, tn), lambda j, i: (l, 0, nj + j))],
        out_specs=pl.BlockSpec((tm, tn), lambda j, i: (i, j)),
        out_shape=jax.ShapeDtypeStruct((m, D_FF), BF),
        scratch_shapes=[pltpu.VMEM((k, tn), BF), pltpu.VMEM((k, tn), BF)],
        compiler_params=_cparams("arbitrary", "arbitrary"),
        name="ffn_gate_up",
    )(x, w_gu, w_gu)


def _merge_kernel(xa_ref, xb_ref, xc_ref, g0_ref, g1_ref, g2_ref, wa_ref, wb_ref, wc_ref, o_ref,
                  wabf, wbbf, wcbf):
    _cast_weights_once([(wa_ref, wabf), (wb_ref, wbbf), (wc_ref, wcbf)])
    da = jnp.dot(xa_ref[...].astype(BF), wabf[...], preferred_element_type=F32)
    db = jnp.dot(xb_ref[...].astype(BF), wbbf[...], preferred_element_type=F32)
    dc = jnp.dot(xc_ref[...].astype(BF), wcbf[...], preferred_element_type=F32)
    o_ref[...] = (_sigmoid(g0_ref[...].astype(F32)) * da + _sigmoid(g1_ref[...].astype(F32)) * db
                  + _sigmoid(g2_ref[...].astype(F32)) * dc).astype(o_ref.dtype)


def _merge(xa, xb, xc, p, w_a, w_b, w_c, l):
    m = xa.shape[0]
    tn = 1024
    tm = min(256, m)
    ka, kb, kc = xa.shape[1], xb.shape[1], xc.shape[1]
    g_blk = C_G // tn
    per = D_MODEL // tn
    once = dict(pipeline_mode=pl.Buffered(1))
    return pl.pallas_call(
        _merge_kernel,
        grid=(D_MODEL // tn, m // tm),
        in_specs=[pl.BlockSpec((tm, ka), lambda j, i: (i, 0)),
                  pl.BlockSpec((tm, kb), lambda j, i: (i, 0)),
                  pl.BlockSpec((tm, kc), lambda j, i: (i, 0)),
                  pl.BlockSpec((tm, tn), lambda j, i: (i, g_blk + j)),
                  pl.BlockSpec((tm, tn), lambda j, i: (i, g_blk + per + j)),
                  pl.BlockSpec((tm, tn), lambda j, i: (i, g_blk + 2 * per + j)),
                  pl.BlockSpec((None, ka, tn), lambda j, i: (l, 0, j), **once),
                  pl.BlockSpec((None, kb, tn), lambda j, i: (l, 0, j), **once),
                  pl.BlockSpec((None, kc, tn), lambda j, i: (l, 0, j), **once)],
        out_specs=pl.BlockSpec((tm, tn), lambda j, i: (i, j)),
        out_shape=jax.ShapeDtypeStruct((m, D_MODEL), BF),
        scratch_shapes=[pltpu.VMEM((ka, tn), BF), pltpu.VMEM((kb, tn), BF), pltpu.VMEM((kc, tn), BF)],
        compiler_params=_cparams("arbitrary", "arbitrary", vmem=VMEM_LIMIT_BIG),
        name="branch_merge",
    )(xa, xb, xc, p, p, p, w_a, w_b, w_c)


def _scan(x, axis, op, fill):
    idx = lax.broadcasted_iota(jnp.int32, x.shape, axis)
    shift = 1
    while shift < x.shape[axis]:
        x = op(x, jnp.where(idx >= shift, pltpu.roll(x, shift, axis), fill))
        shift *= 2
    return x


def _mlstm_kernel(q_ref, k_ref, v_ref, o_ref, gc_ref, gr_ref, brow_ref, bcol_ref,
                  xa_ref, c_ref, n_ref, m_ref, nn_ref):
    chunk = pl.program_id(1)

    @pl.when(chunk == 0)
    def _():
        c_ref[...] = jnp.zeros_like(c_ref)
        m_ref[...] = jnp.zeros_like(m_ref)
        nn_ref[...] = jnp.zeros_like(nn_ref)

    L = q_ref.shape[0]
    nh = ML_HEADS
    scale = ML_DQK ** -0.5
    gc = gc_ref[...] + brow_ref[...]
    gr = gr_ref[...] + bcol_ref[...]
    f_c = pltpu.roll(_scan(_log_sigmoid(gc), 0, jnp.add, 0.0), LANE - nh, 1)
    m_prev = m_ref[...]
    m_t = f_c + jnp.maximum(m_prev, _scan(gc - f_c, 0, jnp.maximum, NEG))
    w_inter = jnp.exp(f_c + m_prev - m_t)
    e_neg_m = jnp.exp(-m_t)
    f_minus_m = f_c - m_t
    m_new = m_t[L - 1:L, :]
    f_last = f_c[L - 1:L, :]
    w_last = jnp.exp(f_last - f_c + gc - m_new) * scale
    decay = jnp.exp(f_last + m_prev - m_new)
    m_ref[...] = m_new
    f_r = _scan(_log_sigmoid(gr), 1, jnp.add, 0.0)
    a_r = gr[0:nh, :] - f_r[nh:2 * nh, :]
    row = lax.broadcasted_iota(jnp.int32, (L, L), 0)
    col = lax.broadcasted_iota(jnp.int32, (L, L), 1)
    tri = col <= row
    ones = jnp.ones((L, ML_DQK), BF)
    nt = (((1,), (1,)), ((), ()))
    tn = (((0,), (0,)), ((), ()))
    for h in range(nh):
        lane = lambda x: x[:, h:h + 1]
        d_w = jnp.exp(jnp.where(tri, lane(f_minus_m) + a_r[h:h + 1, :], NEG))
        q = q_ref[:, h * ML_DQK:(h + 1) * ML_DQK]
        k = k_ref[:, h * ML_DQK:(h + 1) * ML_DQK]
        v = v_ref[:, h * ML_DV:(h + 1) * ML_DV]
        s = (lax.dot_general(q, k, nt, preferred_element_type=F32) * (scale * d_w)).astype(BF)
        c_old = c_ref[h]
        nn_old = nn_ref[h]
        wi = lane(w_inter)
        num = (jnp.dot(s, v, preferred_element_type=F32)
               + wi * jnp.dot(q, c_old.astype(BF), preferred_element_type=F32))
        den = (jnp.dot(s, ones, preferred_element_type=F32)
               + wi * jnp.dot(q, nn_old.astype(BF), preferred_element_type=F32))
        inv = 1.0 / jnp.maximum(jnp.abs(den[:, 0:1]), lane(e_neg_m))
        og = o_ref[:, h * ML_DV:(h + 1) * ML_DV].astype(F32)
        xa_ref[:, h * ML_DV:(h + 1) * ML_DV] = (_sigmoid(og) * (num * inv)).astype(xa_ref.dtype)
        kw = (k.astype(F32) * lane(w_last)).astype(BF)
        dk = decay[:, h:h + 1]
        c_ref[h] = dk * c_old + lax.dot_general(kw, v, tn, preferred_element_type=F32)
        nn_ref[h] = dk * nn_old + lax.dot_general(kw, ones, tn, preferred_element_type=F32)

    @pl.when(chunk == pl.num_programs(1) - 1)
    def _():
        for h in range(nh):
            n_ref[h:h + 1, :] = nn_ref[h].T[0:1, :]


def _mlstm_prompt(p, gates, b_i, b_f, batch, seq):
    m_rows = p.shape[0]
    L = ML_L
    nc = seq // L
    bias = jnp.concatenate([b_i, b_f]).astype(F32)
    bias_row = jnp.zeros((1, N_GATE), F32).at[0, :2 * ML_HEADS].set(bias)
    bias_col = bias.reshape(2 * ML_HEADS, 1)
    gates_t = gates[:, :2 * ML_HEADS].T
    hq, hv = ML_HEADS * ML_DQK, ML_HEADS * ML_DV
    return pl.pallas_call(
        _mlstm_kernel,
        grid=(batch, nc),
        in_specs=[pl.BlockSpec((L, hq), lambda b, c: (b * nc + c, C_QA // hq)),
                  pl.BlockSpec((L, hq), lambda b, c: (b * nc + c, C_KA // hq)),
                  pl.BlockSpec((L, hv), lambda b, c: (b * nc + c, C_VA // hv)),
                  pl.BlockSpec((L, hv), lambda b, c: (b * nc + c, C_OA // hv)),
                  pl.BlockSpec((L, N_GATE), lambda b, c: (b * nc + c, 0)),
                  pl.BlockSpec((2 * ML_HEADS, L), lambda b, c: (0, b * nc + c)),
                  pl.BlockSpec((1, N_GATE), lambda b, c: (0, 0)),
                  pl.BlockSpec((2 * ML_HEADS, 1), lambda b, c: (0, 0))],
        out_specs=[pl.BlockSpec((L, hv), lambda b, c: (b * nc + c, 0)),
                   pl.BlockSpec((None, ML_HEADS, ML_DQK, ML_DV), lambda b, c: (b, 0, 0, 0)),
                   pl.BlockSpec((None, ML_HEADS, ML_DQK), lambda b, c: (b, 0, 0)),
                   pl.BlockSpec((None, 1, LANE), lambda b, c: (b, 0, 0))],
        out_shape=[jax.ShapeDtypeStruct((m_rows, hv), BF),
                   jax.ShapeDtypeStruct((batch, ML_HEADS, ML_DQK, ML_DV), F32),
                   jax.ShapeDtypeStruct((batch, ML_HEADS, ML_DQK), F32),
                   jax.ShapeDtypeStruct((batch, 1, LANE), F32)],
        scratch_shapes=[pltpu.VMEM((ML_HEADS, ML_DQK, ML_DQK), F32)],
        compiler_params=_cparams("parallel", "arbitrary"),
        name="mlstm_prompt",
    )(p, p, p, p, gates, gates_t, bias_row, bias_col)


def _rope_table(pos):
    half = SW_ROT // 2
    freq = ROPE_THETA ** (-jnp.arange(half, dtype=F32) * 2.0 / SW_ROT)
    ang = pos.astype(F32)[:, None] * freq[None, :]
    cos, sin = jnp.cos(ang), jnp.sin(ang)
    t = pos.shape[0]
    z = lambda w: jnp.zeros((t, w), F32)
    return jnp.concatenate([cos, cos, jnp.ones((t, SW_HD - SW_ROT), F32),
                            -sin, z(SW_HD - half),
                            z(half), sin, z(SW_HD - SW_ROT)], axis=1)


def _rope(x, tab):
    half = SW_ROT // 2
    return (x * tab[:, 0:SW_HD]
            + pltpu.roll(x, SW_HD - half, 1) * tab[:, SW_HD:2 * SW_HD]
            + pltpu.roll(x, half, 1) * tab[:, 2 * SW_HD:3 * SW_HD])


def _attend(q, k_own, v_own, k_prev, v_prev, prev_shift):
    blk = SW_BLK
    row = lax.broadcasted_iota(jnp.int32, (1, blk, blk), 1)
    col = lax.broadcasted_iota(jnp.int32, (1, blk, blk), 2)
    qk = (((2,), (2,)), ((0,), (0,)))
    pv = (((2,), (1,)), ((0,), (0,)))
    s_o = jnp.where(col <= row, lax.dot_general(q, k_own, qk, preferred_element_type=F32), NEG)
    s_p = jnp.where(col >= row + prev_shift, lax.dot_general(q, k_prev, qk, preferred_element_type=F32), NEG)
    mx = jnp.maximum(jnp.max(s_o, axis=2, keepdims=True), jnp.max(s_p, axis=2, keepdims=True))
    p_o = jnp.exp(s_o - mx)
    p_p = jnp.exp(s_p - mx)
    den = jnp.sum(p_o, axis=2, keepdims=True) + jnp.sum(p_p, axis=2, keepdims=True)
    acc = (lax.dot_general(p_o.astype(BF), v_own, pv, preferred_element_type=F32)
           + lax.dot_general(p_p.astype(BF), v_prev, pv, preferred_element_type=F32))
    return acc, mx, den


def _swa_kernel(q1, k1, v1, k1p, v1p, q2, k2, v2, k2p, v2p, q3, k3, v3, k3p, v3p, xb_ref, o_acc, m_acc, l_acc):
    blk = SW_BLK
    u = SWA_UNITS
    first_shift = jnp.where(pl.program_id(1) > 0, 0, blk)
    wide = lambda x: jnp.broadcast_to(x, x.shape[:-1] + (SW_HD,))

    def merge(rows, acc, mx, den):
        m_old = m_acc[rows, :]
        m_new = jnp.maximum(m_old, mx)
        a = jnp.exp(m_old - m_new)
        b = jnp.exp(mx - m_new)
        o_acc[rows, :] = a * o_acc[rows, :] + b * acc
        l_acc[rows, :] = a * l_acc[rows, :] + b * den
        m_acc[rows, :] = m_new

    def g1_batch(rows, k_p, v_p, shift):
        split = lambda x: x.reshape(u, blk, SW_HD)
        acc, mx, den = _attend(split(q1[rows, :]), split(k1[rows, :]), split(v1[rows, :]),
                               split(k_p), split(v_p), shift)
        o_acc[rows, :] = acc.reshape(u * blk, SW_HD)
        m_acc[rows, :] = wide(mx).reshape(u * blk, SW_HD)
        l_acc[rows, :] = wide(den).reshape(u * blk, SW_HD)

    head = pl.ds(0, (u - 1) * blk)
    unit = lax.broadcasted_iota(jnp.int32, (u, 1, 1), 0)
    g1_batch(pl.ds(0, u * blk),
             jnp.concatenate([k1p[...], k1[head, :]], axis=0), jnp.concatenate([v1p[...], v1[head, :]], axis=0),
             jnp.where(unit == 0, first_shift, 0))

    def g1_body(i, c):
        rows = pl.ds(pl.multiple_of(i * u * blk, blk), u * blk)
        prev = pl.ds(pl.multiple_of(i * u * blk - blk, blk), u * blk)
        g1_batch(rows, k1[prev, :], v1[prev, :], 0)
        return c

    lax.fori_loop(1, q1.shape[0] // (u * blk), g1_body, 0)

    def dilated(q, k, v, kp, vp):
        dil = q.shape[0]
        n_sub = q.shape[1] // blk

        def batch(sub, r0, k_p, v_p, shift):
            rs = pl.ds(r0, u)
            rows = pl.ds(pl.multiple_of(sub * blk, blk), blk)
            acc, mx, den = _attend(q[rs, rows, :], k[rs, rows, :], v[rs, rows, :], k_p, v_p, shift)
            for i in range(u):
                merge(pl.ds(sub * blk * dil + r0 + i, blk, stride=dil), acc[i], wide(mx[i]), wide(den[i]))

        def first(i, c):
            rs = pl.ds(i * u, u)
            batch(0, i * u, kp[rs], vp[rs], first_shift)
            return c

        lax.fori_loop(0, dil // u, first, 0)

        def rest(i, c):
            sub = 1 + i // (dil // u)
            r0 = (i % (dil // u)) * u
            prev = pl.ds(pl.multiple_of((sub - 1) * blk, blk), blk)
            batch(sub, r0, k[pl.ds(r0, u), prev, :], v[pl.ds(r0, u), prev, :], 0)
            return c

        lax.fori_loop(0, (n_sub - 1) * (dil // u), rest, 0)

    dilated(q2, k2, v2, k2p, v2p)
    dilated(q3, k3, v3, k3p, v3p)
    xb_ref[...] = (o_acc[...] * (1.0 / l_acc[...])).astype(xb_ref.dtype)


def _swa_prompt(qkvs, batch, seq):
    blk = SW_BLK
    span = SW_GROUPS[-1][0]
    assert seq % span == 0 and all(w // d == blk for w, d in SW_GROUPS)
    nspan = seq // span
    in_specs = []
    args = []
    for (win, dil), qkv in zip(SW_GROUPS, qkvs):
        rows = span // dil
        per = rows // blk
        own = lambda c: pl.BlockSpec((None, dil, rows, SW_HD), lambda b, s, h, c=c: (b, 0, s, c * SW_HPG + h))
        prev = lambda c: pl.BlockSpec(
            (None, dil, blk, SW_HD), lambda b, s, h, c=c, per=per: (b, 0, jnp.maximum(s * per - 1, 0), c * SW_HPG + h))
        if dil == 1:
            own = lambda c: pl.BlockSpec((None, None, span, SW_HD), lambda b, s, h, c=c: (b, 0, s, c * SW_HPG + h))
            prev = lambda c: pl.BlockSpec(
                (None, None, blk, SW_HD),
                lambda b, s, h, c=c, per=per: (b, 0, jnp.maximum(s * per - 1, 0), c * SW_HPG + h))
        in_specs += [own(0), own(1), own(2), prev(1), prev(2)]
        args += [qkv] * 5
    return pl.pallas_call(
        _swa_kernel,
        grid=(batch, nspan, SW_HPG),
        in_specs=in_specs,
        out_specs=pl.BlockSpec((span, SW_HD), lambda b, s, h: (b * nspan + s, h)),
        out_shape=jax.ShapeDtypeStruct((batch * seq, SW_HPG * SW_HD), BF),
        scratch_shapes=[pltpu.VMEM((span, SW_HD), F32)] * 3,
        compiler_params=_cparams("parallel", "parallel", "parallel"),
        name="swa_prompt",
    )(*args)


def _kv_rows_kernel(k_ref, v_ref, o_ref, *, dil):
    for r in range(dil):
        rows = pl.ds(r, SW_BLK, stride=dil) if dil > 1 else pl.ds(0, SW_BLK)
        for hh in range(SW_HPG):
            sl = slice(hh * SW_HD, (hh + 1) * SW_HD)
            o_ref[hh, rows, :] = k_ref[r, :, sl].astype(F32)
            o_ref[SW_HPG + hh, rows, :] = v_ref[r, :, sl].astype(F32)


def _kv_rows(qkv, gi, batch, seq):
    win, dil = SW_GROUPS[gi]
    hw = SW_HPG * SW_HD
    last = seq // dil // SW_BLK - 1
    out = pl.pallas_call(
        functools.partial(_kv_rows_kernel, dil=dil),
        grid=(batch,),
        in_specs=[pl.BlockSpec((None, dil, SW_BLK, hw), lambda b: (b, 0, last, 1)),
                  pl.BlockSpec((None, dil, SW_BLK, hw), lambda b: (b, 0, last, 2))],
        out_specs=pl.BlockSpec((None, 2 * SW_HPG, win, SW_HD), lambda b: (b, 0, 0, 0)),
        out_shape=jax.ShapeDtypeStruct((batch, 2 * SW_HPG, win, SW_HD), F32),
        compiler_params=_cparams("parallel"),
        name=f"kv_rows_w{win}",
    )(qkv, qkv)
    return out.reshape(batch, 2, SW_HPG, win, SW_HD).transpose(0, 3, 1, 2, 4)


def _cm_kernel(u_ref, v_ref, gv_ref, ws_ref, bs_ref, o_ref):
    ch = CM_CHUNK
    row = lax.broadcasted_iota(jnp.int32, (ch, ch), 0)
    col = lax.broadcasted_iota(jnp.int32, (ch, ch), 1)
    tri = col <= row
    for g in range(CM_GROUPS):
        sl = slice(g * CM_GD, (g + 1) * CM_GD)
        w = jnp.where(tri, ws_ref[g], 0.0).astype(BF)
        for c in range(u_ref.shape[0] // ch):
            rows = slice(c * ch, (c + 1) * ch)
            vg = v_ref[rows, sl].astype(F32)
            vn = vg * lax.rsqrt(jnp.mean(vg * vg, axis=1, keepdims=True) + RMS_EPS) * gv_ref[g:g + 1, :]
            mixed = jnp.dot(w, vn.astype(BF), preferred_element_type=F32) + bs_ref[:, g:g + 1]
            o_ref[rows, sl] = (u_ref[rows, sl].astype(F32) * mixed).astype(o_ref.dtype)


def _cm_prompt(p, g_v, w_s, b_s, l):
    m_rows = p.shape[0]
    ch = CM_CHUNK
    width = CM_GROUPS * CM_GD
    rows = 2 * ch
    return pl.pallas_call(
        _cm_kernel,
        grid=(m_rows // rows,),
        in_specs=[pl.BlockSpec((rows, width), lambda i: (i, C_UC // width)),
                  pl.BlockSpec((rows, width), lambda i: (i, C_VC // width)),
                  pl.BlockSpec((None, CM_GROUPS, CM_GD), lambda i: (l, 0, 0)),
                  pl.BlockSpec((None, CM_GROUPS, ch, ch), lambda i: (l, 0, 0, 0)),
                  pl.BlockSpec((ch, CM_GROUPS), lambda i: (0, 0))],
        out_specs=pl.BlockSpec((rows, width), lambda i: (i, 0)),
        out_shape=jax.ShapeDtypeStruct((m_rows, width), BF),
        compiler_params=_cparams("parallel"),
        name="cm_prompt",
    )(p, p, g_v, w_s, b_s[l].T)


def _mlstm_step_kernel(qr_ref, kr_ref, vr_ref, or_ref, qkc_ref, g_ref, brow_ref, m0_ref, c0_ref, n0_ref,
                       xa_ref, c1_ref, n1_ref, m1_ref):
    scale = ML_DQK ** -0.5
    gates = g_ref[...] + brow_ref[...]
    m0 = m0_ref[...]
    for h in range(ML_HEADS):
        ig = gates[:, h:h + 1]
        lf = _log_sigmoid(gates[:, ML_HEADS + h:ML_HEADS + h + 1])
        m_prev = m0[:, h:h + 1]
        inter = lf + m_prev
        m_t = jnp.maximum(inter, ig)
        d_w = jnp.exp(ig - m_t)
        w_inter = jnp.exp(inter - m_t)
        q_row = qr_ref[h:h + 1, :]
        k_row = kr_ref[h:h + 1, :]
        v_row = vr_ref[h:h + 1, :]
        q_col = qkc_ref[:, h:h + 1]
        k_col = qkc_ref[:, ML_HEADS + h:ML_HEADS + h + 1]
        c_old = c0_ref[h]
        n_old = n0_ref[h:h + 1, :]
        s = jnp.sum(q_row * k_row, axis=1, keepdims=True) * scale * d_w
        qc = jnp.sum(q_col * c_old, axis=0, keepdims=True)
        num = s * v_row + w_inter * qc
        den = s + w_inter * jnp.sum(q_row * n_old, axis=1, keepdims=True)
        hh = num / jnp.maximum(jnp.abs(den), jnp.exp(-m_t))
        xa_ref[:, h * ML_DV:(h + 1) * ML_DV] = _sigmoid(or_ref[h:h + 1, :]) * hh
        w_last = d_w * scale
        decay = w_inter
        c1_ref[h] = decay * c_old + (k_col * w_last) * v_row
        n1_ref[h:h + 1, :] = decay * n_old + w_last * k_row
        m1_ref[h:h + 1, :] = jnp.broadcast_to(m_t, (1, ML_DQK))


def _mlstm_sample(ps, gates, b_i, b_f, c0, n0, m0, l):
    nb = c0.shape[1]
    q = ps[:nb, C_QA:C_QA + 1024].reshape(nb, ML_HEADS, ML_DQK)
    k = ps[:nb, C_KA:C_KA + 1024].reshape(nb, ML_HEADS, ML_DQK)
    v = ps[:nb, C_VA:C_VA + 2048].reshape(nb, ML_HEADS, ML_DV)
    o = ps[:nb, C_OA:C_OA + 2048].reshape(nb, ML_HEADS, ML_DV)
    qk_col = jnp.concatenate([q, k], axis=1).transpose(0, 2, 1)
    bias = jnp.concatenate([b_i, b_f]).astype(F32)
    bias_row = jnp.zeros((1, N_GATE), F32).at[0, :2 * ML_HEADS].set(bias)
    g3 = gates[:nb].reshape(nb, 1, N_GATE)
    m03 = m0[l].reshape(nb, 1, ML_HEADS)
    per_b3 = lambda shape: pl.BlockSpec((None,) + shape, lambda b: (b, 0, 0))
    return pl.pallas_call(
        _mlstm_step_kernel,
        grid=(nb,),
        in_specs=[per_b3((ML_HEADS, ML_DQK)), per_b3((ML_HEADS, ML_DQK)),
                  per_b3((ML_HEADS, ML_DV)), per_b3((ML_HEADS, ML_DV)),
                  per_b3((ML_DQK, 2 * ML_HEADS)), per_b3((1, N_GATE)),
                  pl.BlockSpec((1, N_GATE), lambda b: (0, 0)),
                  per_b3((1, ML_HEADS)),
                  pl.BlockSpec((None, None, ML_HEADS, ML_DQK, ML_DV), lambda b: (l, b, 0, 0, 0)),
                  pl.BlockSpec((None, None, ML_HEADS, ML_DQK), lambda b: (l, b, 0, 0))],
        out_specs=[per_b3((1, ML_HEADS * ML_DV)),
                   pl.BlockSpec((None, ML_HEADS, ML_DQK, ML_DV), lambda b: (b, 0, 0, 0)),
                   per_b3((ML_HEADS, ML_DQK)), per_b3((ML_HEADS, ML_DQK))],
        out_shape=[jax.ShapeDtypeStruct((nb, 1, ML_HEADS * ML_DV), F32),
                   jax.ShapeDtypeStruct((nb, ML_HEADS, ML_DQK, ML_DV), F32),
                   jax.ShapeDtypeStruct((nb, ML_HEADS, ML_DQK), F32),
                   jax.ShapeDtypeStruct((nb, ML_HEADS, ML_DQK), F32)],
        compiler_params=_cparams("parallel"),
        name="mlstm_sample",
    )(q, k, v, o, qk_col, g3, bias_row, m03, c0, n0)


def _swa_step_kernel(g0_ref, g1_ref, g2_ref, c0_ref, c1_ref, c2_ref, xb_ref):
    hw = SW_HPG * SW_HD
    per_pos = 2 * SW_HPG
    news = (g0_ref, g1_ref, g2_ref)
    caches = (c0_ref, c1_ref, c2_ref)
    for hh in range(SW_HPG):
        m_run = l_run = acc = None
        for gi in range(len(SW_GROUPS)):
            lo = hh * SW_HD
            q = news[gi][:, lo:lo + SW_HD]
            k_new = news[gi][:, hw + lo:hw + lo + SW_HD]
            v_new = news[gi][:, 2 * hw + lo:2 * hw + lo + SW_HD]
            kc = caches[gi][:, hh, :]
            vc = caches[gi][:, SW_HPG + hh, :]
            s_c = jnp.sum(kc * q, axis=1, keepdims=True)
            s_n = jnp.sum(k_new * q, axis=1, keepdims=True)
            mx = jnp.maximum(jnp.max(s_c, axis=0, keepdims=True), s_n)
            p_c = jnp.exp(s_c - mx)
            p_n = jnp.exp(s_n - mx)
            den = jnp.sum(p_c, axis=0, keepdims=True) + p_n
            part = jnp.sum(p_c * vc, axis=0, keepdims=True) + p_n * v_new
            if gi == 0:
                m_run, l_run, acc = mx, den, part
            else:
                top = jnp.maximum(m_run, mx)
                a, b = jnp.exp(m_run - top), jnp.exp(mx - top)
                m_run, l_run, acc = top, a * l_run + b * den, a * acc + b * part
        xb_ref[:, hh * SW_HD:(hh + 1) * SW_HD] = acc / l_run


def _swa_sample(qkvs, caches, l):
    nb = caches[0].shape[1]
    hw = SW_HPG * SW_HD
    per_pos = 2 * SW_HPG
    news, views, specs = [], [], []
    for (win, dil), cache, qkv in zip(SW_GROUPS, caches, qkvs):
        assert cache.shape[2] == win and win // dil == SW_BLK
        news.append(qkv[0, 0, :nb].reshape(nb, 1, 3 * hw))
        views.append(cache.reshape(cache.shape[0], nb, win // dil, dil * per_pos, SW_HD))
        specs.append(pl.BlockSpec((None, None, SW_BLK, per_pos, SW_HD), lambda b: (l, b, 0, 0, 0)))
    row3 = pl.BlockSpec((None, 1, 3 * hw), lambda b: (b, 0, 0))
    xb = pl.pallas_call(
        _swa_step_kernel,
        grid=(nb,),
        in_specs=[row3, row3, row3] + specs,
        out_specs=pl.BlockSpec((None, 1, hw), lambda b: (b, 0, 0)),
        out_shape=jax.ShapeDtypeStruct((nb, 1, hw), F32),
        compiler_params=_cparams("parallel"),
        name="swa_sample",
    )(*news, *views)
    kvs = [jnp.stack([n[:, :, hw:2 * hw].reshape(nb, 1, SW_HPG, SW_HD),
                      n[:, :, 2 * hw:].reshape(nb, 1, SW_HPG, SW_HD)], axis=2) for n in news]
    return xb.reshape(nb, hw), kvs


def _cm_step_kernel(u_ref, v_ref, gv_ref, ws_ref, bs_ref, o_ref, vn_ref):
    for g in range(CM_GROUPS):
        sl = slice(g * CM_GD, (g + 1) * CM_GD)
        vg = v_ref[:, sl]
        vn = vg * lax.rsqrt(jnp.mean(vg * vg, axis=1, keepdims=True) + RMS_EPS) * gv_ref[g:g + 1, :]
        vn_ref[:, sl] = vn
        w00 = ws_ref[g][0:1, 0:1]
        o_ref[:, sl] = u_ref[:, sl] * (w00 * vn + bs_ref[g:g + 1, 0:1])


def _cm_sample(ps, g_v, w_s, b_s, l):
    rows = ps.shape[0]
    width = CM_GROUPS * CM_GD
    ch = CM_CHUNK
    return pl.pallas_call(
        _cm_step_kernel,
        grid=(1,),
        in_specs=[pl.BlockSpec((rows, width), lambda i: (0, C_UC // width)),
                  pl.BlockSpec((rows, width), lambda i: (0, C_VC // width)),
                  pl.BlockSpec((None, CM_GROUPS, CM_GD), lambda i: (l, 0, 0)),
                  pl.BlockSpec((None, CM_GROUPS, ch, ch), lambda i: (l, 0, 0, 0)),
                  pl.BlockSpec((None, CM_GROUPS, ch), lambda i: (l, 0, 0))],
        out_specs=[pl.BlockSpec((rows, width), lambda i: (0, 0))] * 2,
        out_shape=[jax.ShapeDtypeStruct((rows, width), F32)] * 2,
        compiler_params=_cparams("arbitrary"),
        name="cm_sample",
    )(ps, ps, g_v, w_s, b_s)


def _in_proj(h, w_in, l, tabs, batch, out_dtype, dils, tag):
    wt = jnp.swapaxes(w_in, 1, 2)
    p_a = _mm_t(h, wt, l, 1024, 0, O_GATES // 1024, out_dtype, "in_proj_a" + tag)
    gates = _mm_t(h, wt, l, N_GATE, O_GATES, 1, F32, "gate_proj" + tag)
    qkvs = [_qkv_proj(h, wt, l, gi, dil, tabs, batch, out_dtype, f"qkv_proj{gi}" + tag)
            for gi, dil in enumerate(dils)]
    p_r = _mm_t(h, wt, l, 1024, O_UC, N_REST // 1024, out_dtype, "in_proj_r" + tag)
    return p_a, gates, qkvs, p_r


def _qk_tables(pos):
    tab = _rope_table(pos)
    ident = jnp.zeros_like(tab).at[:, :SW_HD].set(1.0)
    return jnp.stack([tab * (SW_HD ** -0.5), tab, ident])


def _dense_tail(x, mixers, p, l, mod, rows_per_batch, wts):
    xa, xb, xc = mixers
    merged = _merge(xa, xb, xc, p, wts["w_br_a"], wts["w_br_b"], wts["w_br_c"], l)
    x = _mm_res(merged, wts["w_out"], l, x, mod, 2, rows_per_batch, 1024, "out_proj")
    h2 = _norm_mod(x, wts["g_norm2"][l], mod, 4, 3, rows_per_batch)
    act = _ffn1(h2, wts["w_ffn_gate_up"], l)
    return _mm_res(act, wts["w_ffn_down"], l, x, mod, 5, rows_per_batch, 1024, "ffn_down", tm=256, big_weight=True)


def kernel(x_prompt, x_sample, state_mlstm_C, state_mlstm_n, state_mlstm_m, cache_swa_kv_w128, cache_swa_kv_w512, cache_swa_kv_w2048, c_prompt, c_sample, w_ada, b_ada, g_norm1, g_norm2, g_final, w_in, b_ml_i, b_ml_f, g_cm_v, w_s, b_s, w_br_a, w_br_b, w_br_c, w_out, w_ffn_gate_up, w_ffn_down):
    bp, tp, d = x_prompt.shape
    ns, ts, _ = x_sample.shape
    depth = w_in.shape[0]
    past_len = 16384
    assert d == D_MODEL and ts == 1 and ns == 8 and tp % max(ML_L, SW_GROUPS[-1][0]) == 0
    s_rows = 16
    wts = dict(w_br_a=w_br_a, w_br_b=w_br_b, w_br_c=w_br_c, w_out=w_out, g_norm2=g_norm2,
               w_ffn_gate_up=w_ffn_gate_up, w_ffn_down=w_ffn_down)

    c_all = jnp.zeros((s_rows, d), F32).at[:ns].set(c_sample).at[ns:ns + bp].set(c_prompt)
    mod_all = _ada(c_all, w_ada, b_ada)
    tabs_p = _qk_tables(jnp.arange(tp, dtype=jnp.int32))
    tabs_s = _qk_tables(jnp.full((s_rows,), past_len, jnp.int32))
    caches = (cache_swa_kv_w128, cache_swa_kv_w512, cache_swa_kv_w2048)
    dils = [dil for _, dil in SW_GROUPS]

    xp = x_prompt.reshape(bp * tp, d)
    xs = jnp.zeros((s_rows, d), F32).at[:ns].set(x_sample.reshape(ns, d))
    p_c, p_n, p_m, p_kv = [], [], [], [[], [], []]
    s_c, s_n, s_m, s_kv, s_v = [], [], [], [[], [], []], []
    for l in range(depth):
        mod_p = mod_all[l, ns:ns + bp].reshape(bp, 1, 6 * d)
        mod_s = mod_all[l]
        h = _norm_mod(xp, g_norm1[l], mod_p, 1, 0, tp)
        p_a, gates, qkvs, p_r = _in_proj(h, w_in, l, tabs_p, bp, BF, dils, "")
        xa, c1, n1, m1 = _mlstm_prompt(p_a, gates, b_ml_i[l], b_ml_f[l], bp, tp)
        xb = _swa_prompt(qkvs, bp, tp)
        kvs = [_kv_rows(qkvs[gi], gi, bp, tp) for gi in range(len(SW_GROUPS))]
        xc = _cm_prompt(p_r, g_cm_v, w_s, b_s, l)
        xp = _dense_tail(xp, (xa, xb, xc), p_r, l, mod_p, tp, wts)
        p_c.append(c1)
        p_n.append(n1)
        p_m.append(m1[:, 0, :ML_HEADS])
        for gi in range(3):
            p_kv[gi].append(kvs[gi])
        h = _norm_mod(xs, g_norm1[l], mod_s, 1, 0, None)
        p_a, gates, qkvs, p_r = _in_proj(h, w_in, l, tabs_s, 1, F32, [1] * len(SW_GROUPS), "_s")
        xa, c1, n1, m1 = _mlstm_sample(p_a, gates, b_ml_i[l], b_ml_f[l], state_mlstm_C, state_mlstm_n,
                                       state_mlstm_m, l)
        xb, kvs = _swa_sample(qkvs, caches, l)
        xc, vn = _cm_sample(p_r, g_cm_v, w_s, b_s, l)
        pad = lambda a: jnp.zeros((s_rows, a.shape[-1]), F32).at[:ns].set(a.reshape(ns, -1))
        xs = _dense_tail(xs, (pad(xa), pad(xb), xc), p_r, l, mod_s, None, wts)
        s_c.append(c1)
        s_n.append(n1)
        s_m.append(m1[:, :, 0])
        for gi in range(3):
            s_kv[gi].append(kvs[gi])
        s_v.append(vn[:ns].reshape(ns, 1, CM_GROUPS, CM_GD))
    y_p = _norm(xp, g_final).reshape(bp, tp, d)
    y_s = _norm(xs, g_final)[:ns].reshape(ns, ts, d)
    st = jnp.stack
    return (y_p, y_s, st(p_c), st(p_n), st(p_m), st(p_kv[0]), st(p_kv[1]), st(p_kv[2]),
            st(s_c), st(s_n), st(s_m), st(s_kv[0]), st(s_kv[1]), st(s_kv[2]), st(s_v))
```

```python
import functools

import jax
import jax.numpy as jnp
from jax import lax
from jax.experimental import pallas as pl
from jax.experimental.pallas import tpu as pltpu

BF = jnp.bfloat16
F32 = jnp.float32

D_MODEL = 2048
ML_HEADS = 8
ML_DQK = 128
ML_DV = 256
SW_GROUPS = ((128, 1), (512, 4), (2048, 16))
SW_HPG = 4
SW_HEADS = 12
SW_HD = 128
SW_ROT = SW_HD // 4
SW_BLK = 128
ROPE_THETA = 500000.0
CM_CHUNK = 128
CM_GROUPS = 8
CM_GD = 256
D_FF = 5632
RMS_EPS = 1e-6
NEG = -1e30

LANE = 128
O_GATES = 6144
O_QB = O_GATES + 2 * ML_HEADS
O_UC = O_QB + 3 * SW_HEADS * SW_HD
N_REST = 2 * CM_GROUPS * CM_GD + 3 * D_MODEL
N_GATE = LANE
C_QA, C_KA, C_VA, C_OA = 0, 1024, 2048, 4096
C_UC, C_VC, C_G = 0, 2048, 4096

TM = 512
TM_WIDE = 1024
QKV_SUB = 256
SWA_UNITS = 4
ML_L = 128
VMEM_LIMIT = 48 * 1024 * 1024
VMEM_LIMIT_BIG = 56 * 1024 * 1024


def _cparams(*sem, vmem=VMEM_LIMIT):
    return pltpu.CompilerParams(dimension_semantics=sem, vmem_limit_bytes=vmem)


def _sigmoid(x):
    return 0.5 * (jnp.tanh(0.5 * x) + 1.0)


def _log_sigmoid(x):
    return jnp.minimum(x, 0.0) - jnp.log1p(jnp.exp(-jnp.abs(x)))


def _ada_kernel(c_ref, w_ref, b_ref, o_ref):
    c = c_ref[...]
    a = (c * _sigmoid(c)).astype(BF)
    o_ref[...] = jnp.dot(a, w_ref[...].astype(BF), preferred_element_type=F32) + b_ref[...]


def _ada(c_all, w_ada, b_ada):
    depth, d, n = w_ada.shape
    rows = c_all.shape[0]
    tn = 1024
    return pl.pallas_call(
        _ada_kernel,
        grid=(depth, n // tn),
        in_specs=[pl.BlockSpec((rows, d), lambda l, j: (0, 0)),
                  pl.BlockSpec((None, d, tn), lambda l, j: (l, 0, j)),
                  pl.BlockSpec((None, 1, tn), lambda l, j: (l, 0, j))],
        out_specs=pl.BlockSpec((None, rows, tn), lambda l, j: (l, 0, j)),
        out_shape=jax.ShapeDtypeStruct((depth, rows, n), F32),
        compiler_params=_cparams("parallel", "parallel"),
        name="ada",
    )(c_all, w_ada, b_ada.reshape(depth, 1, n))


def _mod_spec(mod, chunk, width, tm, rows_per_batch, row_of, col_of):
    per = D_MODEL // width
    if mod.ndim == 3:
        tiles_per_batch = rows_per_batch // tm
        return pl.BlockSpec((None, 1, width),
                            lambda *g: (row_of(*g) // tiles_per_batch, 0, chunk * per + col_of(*g)))
    return pl.BlockSpec((tm, width), lambda *g: (row_of(*g), chunk * per + col_of(*g)))


NORM_ROWS = 16


def _row_groups(n_rows, body):
    def step(i, c):
        body(pl.ds(pl.multiple_of(i * NORM_ROWS, NORM_ROWS), NORM_ROWS))
        return c
    groups = n_rows // NORM_ROWS
    lax.fori_loop(0, groups, step, 0, unroll=min(groups, 4))


def _norm_mod_kernel(x_ref, g_ref, sc_ref, sh_ref, o_ref):
    per_row = sc_ref.shape[0] > 1

    def body(rows):
        x = x_ref[rows, :]
        y = x * lax.rsqrt(jnp.mean(x * x, axis=1, keepdims=True) + RMS_EPS) * g_ref[...]
        sc = sc_ref[rows, :] if per_row else sc_ref[...]
        sh = sh_ref[rows, :] if per_row else sh_ref[...]
        o_ref[rows, :] = (y * (1.0 + sc) + sh).astype(o_ref.dtype)

    _row_groups(x_ref.shape[0], body)


def _norm_mod(x, g, mod, sc_chunk, sh_chunk, rows_per_batch):
    m, d = x.shape
    tm = min(TM, m)
    row_of = lambda i: i
    col_of = lambda i: 0
    return pl.pallas_call(
        _norm_mod_kernel,
        grid=(m // tm,),
        in_specs=[pl.BlockSpec((tm, d), lambda i: (i, 0)),
                  pl.BlockSpec((1, d), lambda i: (0, 0)),
                  _mod_spec(mod, sc_chunk, d, tm, rows_per_batch, row_of, col_of),
                  _mod_spec(mod, sh_chunk, d, tm, rows_per_batch, row_of, col_of)],
        out_specs=pl.BlockSpec((tm, d), lambda i: (i, 0)),
        out_shape=jax.ShapeDtypeStruct((m, d), BF),
        compiler_params=_cparams("parallel"),
        name="norm_mod",
    )(x, g.reshape(1, d), mod, mod)


def _cast_weights_once(pairs):
    @pl.when(pl.program_id(1) == 0)
    def _():
        for src, dst in pairs:
            dst[...] = src[...].astype(BF)


_NT =(((1,), (1,)), ((), ()))


def _wt_spec(k, l, tn, start_of):
    return pl.BlockSpec((pl.Element(1), pl.Element(tn), pl.Element(k)),
                        lambda j, i: (l, pl.multiple_of(start_of(j), 8), 0))


def _mm_t_kernel(x_ref, w_ref, o_ref, wbf):
    _cast_weights_once([(w_ref.at[0], wbf)])
    o_ref[...] = lax.dot_general(x_ref[...], wbf[...], _NT, preferred_element_type=F32).astype(o_ref.dtype)


def _mm_t(x, wt, l, tn, col0, n_tiles, out_dtype, name):
    m, k = x.shape
    tm = min(TM_WIDE, m)
    return pl.pallas_call(
        _mm_t_kernel,
        grid=(n_tiles, m // tm),
        in_specs=[pl.BlockSpec((tm, k), lambda j, i: (i, 0)), _wt_spec(k, l, tn, lambda j: col0 + j * tn)],
        out_specs=pl.BlockSpec((tm, tn), lambda j, i: (i, j)),
        out_shape=jax.ShapeDtypeStruct((m, n_tiles * tn), out_dtype),
        scratch_shapes=[pltpu.VMEM((tn, k), BF)],
        compiler_params=_cparams("arbitrary", "arbitrary"),
        name=name,
    )(x, wt)


def _qkv_kernel(x_ref, w_ref, tab_ref, o_ref, wbf, acc_ref, *, dil):
    _cast_weights_once([(w_ref.at[0], wbf)])
    heads = [slice(hh * SW_HD, (hh + 1) * SW_HD) for hh in range(SW_HPG)]
    tm = x_ref.shape[0]
    sub = min(tm, QKV_SUB)

    def tile(rotary):
        for s in range(tm // sub):
            rows_in = slice(s * sub, (s + 1) * sub)
            acc = lax.dot_general(x_ref[rows_in, :], wbf[...], _NT, preferred_element_type=F32)
            tab = tab_ref[rows_in, :]
            vals = [_rope(acc[:, sl], tab) if rotary else acc[:, sl] for sl in heads]
            if dil == 1:
                for sl, val in zip(heads, vals):
                    o_ref[0, rows_in, sl] = val.astype(o_ref.dtype)
                continue
            for hh, val in enumerate(vals):
                acc_ref[hh, rows_in, :] = val
            n = sub // dil
            for r in range(dil):
                for hh, sl in enumerate(heads):
                    o_ref[r, s * n:(s + 1) * n, sl] = acc_ref[hh, pl.ds(s * sub + r, n, stride=dil), :].astype(
                        o_ref.dtype)

    is_v = pl.program_id(0) == 2
    pl.when(jnp.logical_not(is_v))(lambda: tile(True))
    pl.when(is_v)(lambda: tile(False))


def _qkv_proj(x, wt, l, gi, dil, tabs, batch, out_dtype, name):
    m, k = x.shape
    seq = m // batch
    tm = min(TM_WIDE, seq)
    tpb = seq // tm
    hw = SW_HPG * SW_HD
    wide = len(SW_GROUPS) * hw
    return pl.pallas_call(
        functools.partial(_qkv_kernel, dil=dil),
        grid=(3, m // tm),
        in_specs=[pl.BlockSpec((tm, k), lambda j, i: (i, 0)),
                  _wt_spec(k, l, hw, lambda j: O_QB + j * wide + gi * hw),
                  pl.BlockSpec((None, tm, 3 * SW_HD), lambda j, i: (jnp.minimum(j, 1), i % tpb, 0))],
        out_specs=pl.BlockSpec((None, dil, tm // dil, hw), lambda j, i: (i // tpb, 0, i % tpb, j)),
        out_shape=jax.ShapeDtypeStruct((batch, dil, seq // dil, 3 * hw), out_dtype),
        scratch_shapes=[pltpu.VMEM((hw, k), BF), pltpu.VMEM((SW_HPG, tm, SW_HD), F32)],
        compiler_params=_cparams("arbitrary", "arbitrary"),
        name=name,
    )(x, wt, tabs)


def _cast_kernel(w_ref, o_ref):
    o_ref[...] = w_ref[...].astype(o_ref.dtype)


def _cast_bf16(w3, l):
    _, k, n = w3.shape
    rows = 512
    return pl.pallas_call(
        _cast_kernel,
        grid=(k // rows,),
        in_specs=[pl.BlockSpec((None, rows, n), lambda i: (l, i, 0))],
        out_specs=pl.BlockSpec((rows, n), lambda i: (i, 0)),
        out_shape=jax.ShapeDtypeStruct((k, n), BF),
        compiler_params=_cparams("parallel"),
        name="cast_bf16",
    )(w3)


def _mm_res_norm_kernel(*refs, cast, final):
    x_ref, w_ref, r_ref, g_ref, gn_ref = refs[:5]
    if final:
        y_ref = refs[5]
        o_ref = y_ref
        scratch = refs[6:]
    else:
        sc_ref, sh_ref, o_ref, y_ref = refs[5:9]
        scratch = refs[9:]
    if cast:
        wbf = scratch[0]

        @pl.when(pl.program_id(0) == 0)
        def _():
            wbf[...] = w_ref[...].astype(BF)
        w = wbf[...]
    else:
        w = w_ref[...]
    o_ref[...] = r_ref[...] + g_ref[...] * jnp.dot(x_ref[...], w, preferred_element_type=F32)
    per_row = (not final) and sc_ref.shape[0] > 1

    def body(rows):
        x = o_ref[rows, :]
        y = x * lax.rsqrt(jnp.mean(x * x, axis=1, keepdims=True) + RMS_EPS) * gn_ref[...]
        if not final:
            sc = sc_ref[rows, :] if per_row else sc_ref[...]
            sh = sh_ref[rows, :] if per_row else sh_ref[...]
            y = y * (1.0 + sc) + sh
        y_ref[rows, :] = y.astype(y_ref.dtype)

    _row_groups(o_ref.shape[0], body)


def _mm_res_norm(x, w, l, res, mod, gate_chunk, rows_per_batch, gn, mod_next, sc_chunk, sh_chunk, name):
    m, k = x.shape
    n = D_MODEL
    tm = min(256, m)
    cast = w.ndim == 3
    final = mod_next is None
    once = dict(pipeline_mode=pl.Buffered(1))
    w_spec = (pl.BlockSpec((None, k, n), lambda i: (l, 0, 0), **once) if cast
              else pl.BlockSpec((k, n), lambda i: (0, 0), **once))
    row_of, col_of = (lambda i: i), (lambda i: 0)
    tile = pl.BlockSpec((tm, n), lambda i: (i, 0))
    in_specs = [pl.BlockSpec((tm, k), lambda i: (i, 0)), w_spec, tile,
                _mod_spec(mod, gate_chunk, n, tm, rows_per_batch, row_of, col_of),
                pl.BlockSpec((1, n), lambda i: (0, 0))]
    args = [x, w, res, mod, gn.reshape(1, n)]
    if final:
        out_specs, out_shape = tile, jax.ShapeDtypeStruct((m, n), F32)
    else:
        in_specs += [_mod_spec(mod_next, sc_chunk, n, tm, rows_per_batch, row_of, col_of),
                     _mod_spec(mod_next, sh_chunk, n, tm, rows_per_batch, row_of, col_of)]
        args += [mod_next, mod_next]
        out_specs = [tile, tile]
        out_shape = [jax.ShapeDtypeStruct((m, n), F32), jax.ShapeDtypeStruct((m, n), BF)]
    return pl.pallas_call(
        functools.partial(_mm_res_norm_kernel, cast=cast, final=final),
        grid=(m // tm,),
        in_specs=in_specs,
        out_specs=out_specs,
        out_shape=out_shape,
        scratch_shapes=[pltpu.VMEM((k, n), BF)] if cast else [],
        compiler_params=_cparams("arbitrary", vmem=VMEM_LIMIT_BIG),
        name=name,
    )(*args)


def _ffn1_kernel(x_ref, wg_ref, wu_ref, o_ref, wgbf, wubf):
    _cast_weights_once([(wg_ref, wgbf), (wu_ref, wubf)])
    x = x_ref[...]
    g = jnp.dot(x, wgbf[...], preferred_element_type=F32)
    u = jnp.dot(x, wubf[...], preferred_element_type=F32)
    o_ref[...] = (g * _sigmoid(g) * u).astype(o_ref.dtype)


def _ffn1(x, w_gu, l):
    m, k = x.shape
    tn = 512
    nj = D_FF // tn
    tm = min(TM_WIDE, m)
    return pl.pallas_call(
        _ffn1_kernel,
        grid=(nj, m // tm),
        in_specs=[pl.BlockSpec((tm, k), lambda j, i: (i, 0)),
                  pl.BlockSpec((None, k, tn), lambda j, i: (l, 0, j)),
                  pl.BlockSpec((None, k, tn), lambda j, i: (l, 0, nj + j))],
        out_specs=pl.BlockSpec((tm, tn), lambda j, i: (i, j)),
        out_shape=jax.ShapeDtypeStruct((m, D_FF), BF),
        scratch_shapes=[pltpu.VMEM((k, tn), BF), pltpu.VMEM((k, tn), BF)],
        compiler_params=_cparams("arbitrary", "arbitrary"),
        name="ffn_gate_up",
    )(x, w_gu, w_gu)


def _merge_kernel(xa_ref, xb_ref, xc_ref, g0_ref, g1_ref, g2_ref, wa_ref, wb_ref, wc_ref, o_ref,
                  wabf, wbbf, wcbf):
    _cast_weights_once([(wa_ref, wabf), (wb_ref, wbbf), (wc_ref, wcbf)])
    da = jnp.dot(xa_ref[...].astype(BF), wabf[...], preferred_element_type=F32)
    db = jnp.dot(xb_ref[...].astype(BF), wbbf[...], preferred_element_type=F32)
    dc = jnp.dot(xc_ref[...].astype(BF), wcbf[...], preferred_element_type=F32)
    o_ref[...] = (_sigmoid(g0_ref[...].astype(F32)) * da + _sigmoid(g1_ref[...].astype(F32)) * db
                  + _sigmoid(g2_ref[...].astype(F32)) * dc).astype(o_ref.dtype)


def _merge(xa, xb, xc, p, w_a, w_b, w_c, l):
    m = xa.shape[0]
    tn = 1024
    tm = min(256, m)
    ka, kb, kc = xa.shape[1], xb.shape[1], xc.shape[1]
    g_blk = C_G // tn
    per = D_MODEL // tn
    once = dict(pipeline_mode=pl.Buffered(1))
    return pl.pallas_call(
        _merge_kernel,
        grid=(D_MODEL // tn, m // tm),
        in_specs=[pl.BlockSpec((tm, ka), lambda j, i: (i, 0)),
                  pl.BlockSpec((tm, kb), lambda j, i: (i, 0)),
                  pl.BlockSpec((tm, kc), lambda j, i: (i, 0)),
                  pl.BlockSpec((tm, tn), lambda j, i: (i, g_blk + j)),
                  pl.BlockSpec((tm, tn), lambda j, i: (i, g_blk + per + j)),
                  pl.BlockSpec((tm, tn), lambda j, i: (i, g_blk + 2 * per + j)),
                  pl.BlockSpec((None, ka, tn), lambda j, i: (l, 0, j), **once),
                  pl.BlockSpec((None, kb, tn), lambda j, i: (l, 0, j), **once),
                  pl.BlockSpec((None, kc, tn), lambda j, i: (l, 0, j), **once)],
        out_specs=pl.BlockSpec((tm, tn), lambda j, i: (i, j)),
        out_shape=jax.ShapeDtypeStruct((m, D_MODEL), BF),
        scratch_shapes=[pltpu.VMEM((ka, tn), BF), pltpu.VMEM((kb, tn), BF), pltpu.VMEM((kc, tn), BF)],
        compiler_params=_cparams("arbitrary", "arbitrary", vmem=VMEM_LIMIT_BIG),
        name="branch_merge",
    )(xa, xb, xc, p, p, p, w_a, w_b, w_c)


def _scan(x, axis, op, fill):
    idx = lax.broadcasted_iota(jnp.int32, x.shape, axis)
    shift = 1
    while shift < x.shape[axis]:
        x = op(x, jnp.where(idx >= shift, pltpu.roll(x, shift, axis), fill))
        shift *= 2
    return x


def _mlstm_kernel(q_ref, k_ref, v_ref, o_ref, gc_ref, gr_ref, brow_ref, bcol_ref,
                  xa_ref, c_ref, n_ref, m_ref, nn_ref):
    chunk = pl.program_id(1)

    @pl.when(chunk == 0)
    def _():
        c_ref[...] = jnp.zeros_like(c_ref)
        m_ref[...] = jnp.zeros_like(m_ref)
        nn_ref[...] = jnp.zeros_like(nn_ref)

    L = q_ref.shape[0]
    nh = ML_HEADS
    scale = ML_DQK ** -0.5
    gc = gc_ref[...] + brow_ref[...]
    gr = gr_ref[...] + bcol_ref[...]
    f_c = pltpu.roll(_scan(_log_sigmoid(gc), 0, jnp.add, 0.0), LANE - nh, 1)
    m_prev = m_ref[...]
    m_t = f_c + jnp.maximum(m_prev, _scan(gc - f_c, 0, jnp.maximum, NEG))
    w_inter = jnp.exp(f_c + m_prev - m_t)
    e_neg_m = jnp.exp(-m_t)
    f_minus_m = f_c - m_t
    m_new = m_t[L - 1:L, :]
    f_last = f_c[L - 1:L, :]
    w_last = jnp.exp(f_last - f_c + gc - m_new) * scale
    decay = jnp.exp(f_last + m_prev - m_new)
    m_ref[...] = m_new
    f_r = _scan(_log_sigmoid(gr), 1, jnp.add, 0.0)
    a_r = gr[0:nh, :] - f_r[nh:2 * nh, :]
    row = lax.broadcasted_iota(jnp.int32, (L, L), 0)
    col = lax.broadcasted_iota(jnp.int32, (L, L), 1)
    tri = col <= row
    ones = jnp.ones((L, ML_DQK), BF)
    nt = (((1,), (1,)), ((), ()))
    tn = (((0,), (0,)), ((), ()))
    for h in range(nh):
        lane = lambda x: x[:, h:h + 1]
        d_w = jnp.exp(jnp.where(tri, lane(f_minus_m) + a_r[h:h + 1, :], NEG))
        q = q_ref[:, h * ML_DQK:(h + 1) * ML_DQK]
        k = k_ref[:, h * ML_DQK:(h + 1) * ML_DQK]
        v = v_ref[:, h * ML_DV:(h + 1) * ML_DV]
        s = (lax.dot_general(q, k, nt, preferred_element_type=F32) * (scale * d_w)).astype(BF)
        c_old = c_ref[h]
        nn_old = nn_ref[h]
        wi = lane(w_inter)
        num = (jnp.dot(s, v, preferred_element_type=F32)
               + wi * jnp.dot(q, c_old.astype(BF), preferred_element_type=F32))
        den = (jnp.dot(s, ones, preferred_element_type=F32)
               + wi * jnp.dot(q, nn_old.astype(BF), preferred_element_type=F32))
        inv = 1.0 / jnp.maximum(jnp.abs(den[:, 0:1]), lane(e_neg_m))
        og = o_ref[:, h * ML_DV:(h + 1) * ML_DV].astype(F32)
        xa_ref[:, h * ML_DV:(h + 1) * ML_DV] = (_sigmoid(og) * (num * inv)).astype(xa_ref.dtype)
        kw = (k.astype(F32) * lane(w_last)).astype(BF)
        dk = decay[:, h:h + 1]
        c_ref[h] = dk * c_old + lax.dot_general(kw, v, tn, preferred_element_type=F32)
        nn_ref[h] = dk * nn_old + lax.dot_general(kw, ones, tn, preferred_element_type=F32)

    @pl.when(chunk == pl.num_programs(1) - 1)
    def _():
        for h in range(nh):
            n_ref[h:h + 1, :] = nn_ref[h].T[0:1, :]


def _mlstm_prompt(p, gates, b_i, b_f, batch, seq):
    m_rows = p.shape[0]
    L = ML_L
    nc = seq // L
    bias = jnp.concatenate([b_i, b_f]).astype(F32)
    bias_row = jnp.zeros((1, N_GATE), F32).at[0, :2 * ML_HEADS].set(bias)
    bias_col = bias.reshape(2 * ML_HEADS, 1)
    gates_t = gates[:, :2 * ML_HEADS].T
    hq, hv = ML_HEADS * ML_DQK, ML_HEADS * ML_DV
    return pl.pallas_call(
        _mlstm_kernel,
        grid=(batch, nc),
        in_specs=[pl.BlockSpec((L, hq), lambda b, c: (b * nc + c, C_QA // hq)),
                  pl.BlockSpec((L, hq), lambda b, c: (b * nc + c, C_KA // hq)),
                  pl.BlockSpec((L, hv), lambda b, c: (b * nc + c, C_VA // hv)),
                  pl.BlockSpec((L, hv), lambda b, c: (b * nc + c, C_OA // hv)),
                  pl.BlockSpec((L, N_GATE), lambda b, c: (b * nc + c, 0)),
                  pl.BlockSpec((2 * ML_HEADS, L), lambda b, c: (0, b * nc + c)),
                  pl.BlockSpec((1, N_GATE), lambda b, c: (0, 0)),
                  pl.BlockSpec((2 * ML_HEADS, 1), lambda b, c: (0, 0))],
        out_specs=[pl.BlockSpec((L, hv), lambda b, c: (b * nc + c, 0)),
                   pl.BlockSpec((None, ML_HEADS, ML_DQK, ML_DV), lambda b, c: (b, 0, 0, 0)),
                   pl.BlockSpec((None, ML_HEADS, ML_DQK), lambda b, c: (b, 0, 0)),
                   pl.BlockSpec((None, 1, LANE), lambda b, c: (b, 0, 0))],
        out_shape=[jax.ShapeDtypeStruct((m_rows, hv), BF),
                   jax.ShapeDtypeStruct((batch, ML_HEADS, ML_DQK, ML_DV), F32),
                   jax.ShapeDtypeStruct((batch, ML_HEADS, ML_DQK), F32),
                   jax.ShapeDtypeStruct((batch, 1, LANE), F32)],
        scratch_shapes=[pltpu.VMEM((ML_HEADS, ML_DQK, ML_DQK), F32)],
        compiler_params=_cparams("parallel", "arbitrary"),
        name="mlstm_prompt",
    )(p, p, p, p, gates, gates_t, bias_row, bias_col)


def _rope_table(pos):
    half = SW_ROT // 2
    freq = ROPE_THETA ** (-jnp.arange(half, dtype=F32) * 2.0 / SW_ROT)
    ang = pos.astype(F32)[:, None] * freq[None, :]
    cos, sin = jnp.cos(ang), jnp.sin(ang)
    t = pos.shape[0]
    z = lambda w: jnp.zeros((t, w), F32)
    return jnp.concatenate([cos, cos, jnp.ones((t, SW_HD - SW_ROT), F32),
                            -sin, z(SW_HD - half),
                            z(half), sin, z(SW_HD - SW_ROT)], axis=1)


def _rope(x, tab):
    half = SW_ROT // 2
    return (x * tab[:, 0:SW_HD]
            + pltpu.roll(x, SW_HD - half, 1) * tab[:, SW_HD:2 * SW_HD]
            + pltpu.roll(x, half, 1) * tab[:, 2 * SW_HD:3 * SW_HD])


def _attend(q, k_own, v_own, k_prev, v_prev, prev_shift):
    blk = SW_BLK
    row = lax.broadcasted_iota(jnp.int32, (1, blk, blk), 1)
    col = lax.broadcasted_iota(jnp.int32, (1, blk, blk), 2)
    qk = (((2,), (2,)), ((0,), (0,)))
    pv = (((2,), (1,)), ((0,), (0,)))
    s_o = jnp.where(col <= row, lax.dot_general(q, k_own, qk, preferred_element_type=F32), NEG)
    s_p = jnp.where(col >= row + prev_shift, lax.dot_general(q, k_prev, qk, preferred_element_type=F32), NEG)
    mx = jnp.maximum(jnp.max(s_o, axis=2, keepdims=True), jnp.max(s_p, axis=2, keepdims=True))
    p_o = jnp.exp(s_o - mx)
    p_p = jnp.exp(s_p - mx)
    den = jnp.sum(p_o, axis=2, keepdims=True) + jnp.sum(p_p, axis=2, keepdims=True)
    acc = (lax.dot_general(p_o.astype(BF), v_own, pv, preferred_element_type=F32)
           + lax.dot_general(p_p.astype(BF), v_prev, pv, preferred_element_type=F32))
    return acc, mx, den


def _swa_kernel(q1, k1, v1, k1p, v1p, q2, k2, v2, k2p, v2p, q3, k3, v3, k3p, v3p, xb_ref, o_acc, m_acc, l_acc):
    blk = SW_BLK
    u = SWA_UNITS
    first_shift = jnp.where(pl.program_id(1) > 0, 0, blk)
    wide = lambda x: jnp.broadcast_to(x, x.shape[:-1] + (SW_HD,))

    def merge(rows, acc, mx, den):
        m_old = m_acc[rows, :]
        m_new = jnp.maximum(m_old, mx)
        a = jnp.exp(m_old - m_new)
        b = jnp.exp(mx - m_new)
        o_acc[rows, :] = a * o_acc[rows, :] + b * acc
        l_acc[rows, :] = a * l_acc[rows, :] + b * den
        m_acc[rows, :] = m_new

    def g1_batch(rows, k_p, v_p, shift):
        split = lambda x: x.reshape(u, blk, SW_HD)
        acc, mx, den = _attend(split(q1[rows, :]), split(k1[rows, :]), split(v1[rows, :]),
                               split(k_p), split(v_p), shift)
        o_acc[rows, :] = acc.reshape(u * blk, SW_HD)
        m_acc[rows, :] = wide(mx).reshape(u * blk, SW_HD)
        l_acc[rows, :] = wide(den).reshape(u * blk, SW_HD)

    head = pl.ds(0, (u - 1) * blk)
    unit = lax.broadcasted_iota(jnp.int32, (u, 1, 1), 0)
    g1_batch(pl.ds(0, u * blk),
             jnp.concatenate([k1p[...], k1[head, :]], axis=0), jnp.concatenate([v1p[...], v1[head, :]], axis=0),
             jnp.where(unit == 0, first_shift, 0))

    def g1_body(i, c):
        rows = pl.ds(pl.multiple_of(i * u * blk, blk), u * blk)
        prev = pl.ds(pl.multiple_of(i * u * blk - blk, blk), u * blk)
        g1_batch(rows, k1[prev, :], v1[prev, :], 0)
        return c

    lax.fori_loop(1, q1.shape[0] // (u * blk), g1_body, 0)

    def dilated(q, k, v, kp, vp):
        dil = q.shape[0]
        n_sub = q.shape[1] // blk

        def batch(sub, r0, k_p, v_p, shift):
            rs = pl.ds(r0, u)
            rows = pl.ds(pl.multiple_of(sub * blk, blk), blk)
            acc, mx, den = _attend(q[rs, rows, :], k[rs, rows, :], v[rs, rows, :], k_p, v_p, shift)
            for i in range(u):
                merge(pl.ds(sub * blk * dil + r0 + i, blk, stride=dil), acc[i], wide(mx[i]), wide(den[i]))

        def first(i, c):
            rs = pl.ds(i * u, u)
            batch(0, i * u, kp[rs], vp[rs], first_shift)
            return c

        lax.fori_loop(0, dil // u, first, 0)

        def rest(i, c):
            sub = 1 + i // (dil // u)
            r0 = (i % (dil // u)) * u
            prev = pl.ds(pl.multiple_of((sub - 1) * blk, blk), blk)
            batch(sub, r0, k[pl.ds(r0, u), prev, :], v[pl.ds(r0, u), prev, :], 0)
            return c

        lax.fori_loop(0, (n_sub - 1) * (dil // u), rest, 0)

    dilated(q2, k2, v2, k2p, v2p)
    dilated(q3, k3, v3, k3p, v3p)
    xb_ref[...] = (o_acc[...] * (1.0 / l_acc[...])).astype(xb_ref.dtype)


def _swa_prompt(qkvs, batch, seq):
    blk = SW_BLK
    span = SW_GROUPS[-1][0]
    assert seq % span == 0 and all(w // d == blk for w, d in SW_GROUPS)
    nspan = seq // span
    in_specs = []
    args = []
    for (win, dil), qkv in zip(SW_GROUPS, qkvs):
        rows = span // dil
        per = rows // blk
        own = lambda c: pl.BlockSpec((None, dil, rows, SW_HD), lambda b, s, h, c=c: (b, 0, s, c * SW_HPG + h))
        prev = lambda c: pl.BlockSpec(
            (None, dil, blk, SW_HD), lambda b, s, h, c=c, per=per: (b, 0, jnp.maximum(s * per - 1, 0), c * SW_HPG + h))
        if dil == 1:
            own = lambda c: pl.BlockSpec((None, None, span, SW_HD), lambda b, s, h, c=c: (b, 0, s, c * SW_HPG + h))
            prev = lambda c: pl.BlockSpec(
                (None, None, blk, SW_HD),
                lambda b, s, h, c=c, per=per: (b, 0, jnp.maximum(s * per - 1, 0), c * SW_HPG + h))
        in_specs += [own(0), own(1), own(2), prev(1), prev(2)]
        args += [qkv] * 5
    return pl.pallas_call(
        _swa_kernel,
        grid=(batch, nspan, SW_HPG),
        in_specs=in_specs,
        out_specs=pl.BlockSpec((span, SW_HD), lambda b, s, h: (b * nspan + s, h)),
        out_shape=jax.ShapeDtypeStruct((batch * seq, SW_HPG * SW_HD), BF),
        scratch_shapes=[pltpu.VMEM((span, SW_HD), F32)] * 3,
        compiler_params=_cparams("parallel", "parallel", "parallel"),
        name="swa_prompt",
    )(*args)


def _kv_rows_kernel(k_ref, v_ref, o_ref, *, dil):
    for r in range(dil):
        rows = pl.ds(r, SW_BLK, stride=dil) if dil > 1 else pl.ds(0, SW_BLK)
        for hh in range(SW_HPG):
            sl = slice(hh * SW_HD, (hh + 1) * SW_HD)
            o_ref[hh, rows, :] = k_ref[r, :, sl].astype(F32)
            o_ref[SW_HPG + hh, rows, :] = v_ref[r, :, sl].astype(F32)


def _kv_rows(qkv, gi, batch, seq):
    win, dil = SW_GROUPS[gi]
    hw = SW_HPG * SW_HD
    last = seq // dil // SW_BLK - 1
    out = pl.pallas_call(
        functools.partial(_kv_rows_kernel, dil=dil),
        grid=(batch,),
        in_specs=[pl.BlockSpec((None, dil, SW_BLK, hw), lambda b: (b, 0, last, 1)),
                  pl.BlockSpec((None, dil, SW_BLK, hw), lambda b: (b, 0, last, 2))],
        out_specs=pl.BlockSpec((None, 2 * SW_HPG, win, SW_HD), lambda b: (b, 0, 0, 0)),
        out_shape=jax.ShapeDtypeStruct((batch, 2 * SW_HPG, win, SW_HD), F32),
        compiler_params=_cparams("parallel"),
        name=f"kv_rows_w{win}",
    )(qkv, qkv)
    return out.reshape(batch, 2, SW_HPG, win, SW_HD).transpose(0, 3, 1, 2, 4)


def _cm_kernel(u_ref, v_ref, gv_ref, ws_ref, bs_ref, o_ref):
    ch = CM_CHUNK
    row = lax.broadcasted_iota(jnp.int32, (ch, ch), 0)
    col = lax.broadcasted_iota(jnp.int32, (ch, ch), 1)
    tri = col <= row
    for g in range(CM_GROUPS):
        sl = slice(g * CM_GD, (g + 1) * CM_GD)
        w = jnp.where(tri, ws_ref[g], 0.0).astype(BF)
        for c in range(u_ref.shape[0] // ch):
            rows = slice(c * ch, (c + 1) * ch)
            vg = v_ref[rows, sl].astype(F32)
            vn = vg * lax.rsqrt(jnp.mean(vg * vg, axis=1, keepdims=True) + RMS_EPS) * gv_ref[g:g + 1, :]
            mixed = jnp.dot(w, vn.astype(BF), preferred_element_type=F32) + bs_ref[:, g:g + 1]
            o_ref[rows, sl] = (u_ref[rows, sl].astype(F32) * mixed).astype(o_ref.dtype)


def _cm_prompt(p, g_v, w_s, b_s, l):
    m_rows = p.shape[0]
    ch = CM_CHUNK
    width = CM_GROUPS * CM_GD
    rows = 2 * ch
    return pl.pallas_call(
        _cm_kernel,
        grid=(m_rows // rows,),
        in_specs=[pl.BlockSpec((rows, width), lambda i: (i, C_UC // width)),
                  pl.BlockSpec((rows, width), lambda i: (i, C_VC // width)),
                  pl.BlockSpec((None, CM_GROUPS, CM_GD), lambda i: (l, 0, 0)),
                  pl.BlockSpec((None, CM_GROUPS, ch, ch), lambda i: (l, 0, 0, 0)),
                  pl.BlockSpec((ch, CM_GROUPS), lambda i: (0, 0))],
        out_specs=pl.BlockSpec((rows, width), lambda i: (i, 0)),
        out_shape=jax.ShapeDtypeStruct((m_rows, width), BF),
        compiler_params=_cparams("parallel"),
        name="cm_prompt",
    )(p, p, g_v, w_s, b_s[l].T)


def _mlstm_step_kernel(qr_ref, kr_ref, vr_ref, or_ref, qkc_ref, g_ref, brow_ref, m0_ref, c0_ref, n0_ref,
                       xa_ref, c1_ref, n1_ref, m1_ref):
    scale = ML_DQK ** -0.5
    gates = g_ref[...] + brow_ref[...]
    m0 = m0_ref[...]
    for h in range(ML_HEADS):
        ig = gates[:, h:h + 1]
        lf = _log_sigmoid(gates[:, ML_HEADS + h:ML_HEADS + h + 1])
        m_prev = m0[:, h:h + 1]
        inter = lf + m_prev
        m_t = jnp.maximum(inter, ig)
        d_w = jnp.exp(ig - m_t)
        w_inter = jnp.exp(inter - m_t)
        q_row = qr_ref[h:h + 1, :]
        k_row = kr_ref[h:h + 1, :]
        v_row = vr_ref[h:h + 1, :]
        q_col = qkc_ref[:, h:h + 1]
        k_col = qkc_ref[:, ML_HEADS + h:ML_HEADS + h + 1]
        c_old = c0_ref[h]
        n_old = n0_ref[h:h + 1, :]
        s = jnp.sum(q_row * k_row, axis=1, keepdims=True) * scale * d_w
        qc = jnp.sum(q_col * c_old, axis=0, keepdims=True)
        num = s * v_row + w_inter * qc
        den = s + w_inter * jnp.sum(q_row * n_old, axis=1, keepdims=True)
        hh = num / jnp.maximum(jnp.abs(den), jnp.exp(-m_t))
        xa_ref[:, h * ML_DV:(h + 1) * ML_DV] = _sigmoid(or_ref[h:h + 1, :]) * hh
        w_last = d_w * scale
        decay = w_inter
        c1_ref[h] = decay * c_old + (k_col * w_last) * v_row
        n1_ref[h:h + 1, :] = decay * n_old + w_last * k_row
        m1_ref[h:h + 1, :] = jnp.broadcast_to(m_t, (1, ML_DQK))


def _mlstm_sample(ps, gates, b_i, b_f, c0, n0, m0, l):
    nb = c0.shape[1]
    q = ps[:nb, C_QA:C_QA + 1024].reshape(nb, ML_HEADS, ML_DQK)
    k = ps[:nb, C_KA:C_KA + 1024].reshape(nb, ML_HEADS, ML_DQK)
    v = ps[:nb, C_VA:C_VA + 2048].reshape(nb, ML_HEADS, ML_DV)
    o = ps[:nb, C_OA:C_OA + 2048].reshape(nb, ML_HEADS, ML_DV)
    qk_col = jnp.concatenate([q, k], axis=1).transpose(0, 2, 1)
    bias = jnp.concatenate([b_i, b_f]).astype(F32)
    bias_row = jnp.zeros((1, N_GATE), F32).at[0, :2 * ML_HEADS].set(bias)
    g3 = gates[:nb].reshape(nb, 1, N_GATE)
    m03 = m0[l].reshape(nb, 1, ML_HEADS)
    per_b3 = lambda shape: pl.BlockSpec((None,) + shape, lambda b: (b, 0, 0))
    return pl.pallas_call(
        _mlstm_step_kernel,
        grid=(nb,),
        in_specs=[per_b3((ML_HEADS, ML_DQK)), per_b3((ML_HEADS, ML_DQK)),
                  per_b3((ML_HEADS, ML_DV)), per_b3((ML_HEADS, ML_DV)),
                  per_b3((ML_DQK, 2 * ML_HEADS)), per_b3((1, N_GATE)),
                  pl.BlockSpec((1, N_GATE), lambda b: (0, 0)),
                  per_b3((1, ML_HEADS)),
                  pl.BlockSpec((None, None, ML_HEADS, ML_DQK, ML_DV), lambda b: (l, b, 0, 0, 0)),
                  pl.BlockSpec((None, None, ML_HEADS, ML_DQK), lambda b: (l, b, 0, 0))],
        out_specs=[per_b3((1, ML_HEADS * ML_DV)),
                   pl.BlockSpec((None, ML_HEADS, ML_DQK, ML_DV), lambda b: (b, 0, 0, 0)),
                   per_b3((ML_HEADS, ML_DQK)), per_b3((ML_HEADS, ML_DQK))],
        out_shape=[jax.ShapeDtypeStruct((nb, 1, ML_HEADS * ML_DV), F32),
                   jax.ShapeDtypeStruct((nb, ML_HEADS, ML_DQK, ML_DV), F32),
                   jax.ShapeDtypeStruct((nb, ML_HEADS, ML_DQK), F32),
                   jax.ShapeDtypeStruct((nb, ML_HEADS, ML_DQK), F32)],
        compiler_params=_cparams("parallel"),
        name="mlstm_sample",
    )(q, k, v, o, qk_col, g3, bias_row, m03, c0, n0)


def _swa_step_kernel(g0_ref, g1_ref, g2_ref, c0_ref, c1_ref, c2_ref, xb_ref):
    hw = SW_HPG * SW_HD
    per_pos = 2 * SW_HPG
    news = (g0_ref, g1_ref, g2_ref)
    caches = (c0_ref, c1_ref, c2_ref)
    for hh in range(SW_HPG):
        m_run = l_run = acc = None
        for gi in range(len(SW_GROUPS)):
            lo = hh * SW_HD
            q = news[gi][:, lo:lo + SW_HD]
            k_new = news[gi][:, hw + lo:hw + lo + SW_HD]
            v_new = news[gi][:, 2 * hw + lo:2 * hw + lo + SW_HD]
            kc = caches[gi][:, hh, :]
            vc = caches[gi][:, SW_HPG + hh, :]
            s_c = jnp.sum(kc * q, axis=1, keepdims=True)
            s_n = jnp.sum(k_new * q, axis=1, keepdims=True)
            mx = jnp.maximum(jnp.max(s_c, axis=0, keepdims=True), s_n)
            p_c = jnp.exp(s_c - mx)
            p_n = jnp.exp(s_n - mx)
            den = jnp.sum(p_c, axis=0, keepdims=True) + p_n
            part = jnp.sum(p_c * vc, axis=0, keepdims=True) + p_n * v_new
            if gi == 0:
                m_run, l_run, acc = mx, den, part
            else:
                top = jnp.maximum(m_run, mx)
                a, b = jnp.exp(m_run - top), jnp.exp(mx - top)
                m_run, l_run, acc = top, a * l_run + b * den, a * acc + b * part
        xb_ref[:, hh * SW_HD:(hh + 1) * SW_HD] = acc / l_run


def _swa_sample(qkvs, caches, l):
    nb = caches[0].shape[1]
    hw = SW_HPG * SW_HD
    per_pos = 2 * SW_HPG
    news, views, specs = [], [], []
    for (win, dil), cache, qkv in zip(SW_GROUPS, caches, qkvs):
        assert cache.shape[2] == win and win // dil == SW_BLK
        news.append(qkv[0, 0, :nb].reshape(nb, 1, 3 * hw))
        views.append(cache.reshape(cache.shape[0], nb, win // dil, dil * per_pos, SW_HD))
        specs.append(pl.BlockSpec((None, None, SW_BLK, per_pos, SW_HD), lambda b: (l, b, 0, 0, 0)))
    row3 = pl.BlockSpec((None, 1, 3 * hw), lambda b: (b, 0, 0))
    xb = pl.pallas_call(
        _swa_step_kernel,
        grid=(nb,),
        in_specs=[row3, row3, row3] + specs,
        out_specs=pl.BlockSpec((None, 1, hw), lambda b: (b, 0, 0)),
        out_shape=jax.ShapeDtypeStruct((nb, 1, hw), F32),
        compiler_params=_cparams("parallel"),
        name="swa_sample",
    )(*news, *views)
    kvs = [jnp.stack([n[:, :, hw:2 * hw].reshape(nb, 1, SW_HPG, SW_HD),
                      n[:, :, 2 * hw:].reshape(nb, 1, SW_HPG, SW_HD)], axis=2) for n in news]
    return xb.reshape(nb, hw), kvs


def _cm_step_kernel(u_ref, v_ref, gv_ref, ws_ref, bs_ref, o_ref, vn_ref):
    for g in range(CM_GROUPS):
        sl = slice(g * CM_GD, (g + 1) * CM_GD)
        vg = v_ref[:, sl]
        vn = vg * lax.rsqrt(jnp.mean(vg * vg, axis=1, keepdims=True) + RMS_EPS) * gv_ref[g:g + 1, :]
        vn_ref[:, sl] = vn
        w00 = ws_ref[g][0:1, 0:1]
        o_ref[:, sl] = u_ref[:, sl] * (w00 * vn + bs_ref[g:g + 1, 0:1])


def _cm_sample(ps, g_v, w_s, b_s, l):
    rows = ps.shape[0]
    width = CM_GROUPS * CM_GD
    ch = CM_CHUNK
    return pl.pallas_call(
        _cm_step_kernel,
        grid=(1,),
        in_specs=[pl.BlockSpec((rows, width), lambda i: (0, C_UC // width)),
                  pl.BlockSpec((rows, width), lambda i: (0, C_VC // width)),
                  pl.BlockSpec((None, CM_GROUPS, CM_GD), lambda i: (l, 0, 0)),
                  pl.BlockSpec((None, CM_GROUPS, ch, ch), lambda i: (l, 0, 0, 0)),
                  pl.BlockSpec((None, CM_GROUPS, ch), lambda i: (l, 0, 0))],
        out_specs=[pl.BlockSpec((rows, width), lambda i: (0, 0))] * 2,
        out_shape=[jax.ShapeDtypeStruct((rows, width), F32)] * 2,
        compiler_params=_cparams("arbitrary"),
        name="cm_sample",
    )(ps, ps, g_v, w_s, b_s)


def _in_proj(h, w_in, l, tabs, batch, out_dtype, dils, tag):
    wt = jnp.swapaxes(w_in, 1, 2)
    p_a = _mm_t(h, wt, l, 1024, 0, O_GATES // 1024, out_dtype, "in_proj_a" + tag)
    gates = _mm_t(h, wt, l, N_GATE, O_GATES, 1, F32, "gate_proj" + tag)
    qkvs = [_qkv_proj(h, wt, l, gi, dil, tabs, batch, out_dtype, f"qkv_proj{gi}" + tag)
            for gi, dil in enumerate(dils)]
    p_r = _mm_t(h, wt, l, 1024, O_UC, N_REST // 1024, out_dtype, "in_proj_r" + tag)
    return p_a, gates, qkvs, p_r


def _qk_tables(pos):
    tab = _rope_table(pos)
    return jnp.stack([tab * (SW_HD ** -0.5), tab])


def _dense_tail(x, mixers, p, l, mod, rows_per_batch, wts, w_down, next_norm):
    xa, xb, xc = mixers
    merged = _merge(xa, xb, xc, p, wts["w_br_a"], wts["w_br_b"], wts["w_br_c"], l)
    x, h2 = _mm_res_norm(merged, wts["w_out"], l, x, mod, 2, rows_per_batch, wts["g_norm2"][l], mod, 4, 3, "out_proj")
    act = _ffn1(h2, wts["w_ffn_gate_up"], l)
    gain, mod_next = next_norm
    return _mm_res_norm(act, w_down, l, x, mod, 5, rows_per_batch, gain, mod_next, 1, 0, "ffn_down")


def kernel(x_prompt, x_sample, state_mlstm_C, state_mlstm_n, state_mlstm_m, cache_swa_kv_w128, cache_swa_kv_w512, cache_swa_kv_w2048, c_prompt, c_sample, w_ada, b_ada, g_norm1, g_norm2, g_final, w_in, b_ml_i, b_ml_f, g_cm_v, w_s, b_s, w_br_a, w_br_b, w_br_c, w_out, w_ffn_gate_up, w_ffn_down):
    bp, tp, d = x_prompt.shape
    ns, ts, _ = x_sample.shape
    depth = w_in.shape[0]
    past_len = 16384
    assert d == D_MODEL and ts == 1 and ns == 8 and tp % max(ML_L, SW_GROUPS[-1][0]) == 0
    s_rows = 16
    wts = dict(w_br_a=w_br_a, w_br_b=w_br_b, w_br_c=w_br_c, w_out=w_out, g_norm2=g_norm2,
               w_ffn_gate_up=w_ffn_gate_up, w_ffn_down=w_ffn_down)

    c_all = jnp.zeros((s_rows, d), F32).at[:ns].set(c_sample).at[ns:ns + bp].set(c_prompt)
    mod_all = _ada(c_all, w_ada, b_ada)
    tabs_p = _qk_tables(jnp.arange(tp, dtype=jnp.int32))
    tabs_s = _qk_tables(jnp.full((s_rows,), past_len, jnp.int32))
    caches = (cache_swa_kv_w128, cache_swa_kv_w512, cache_swa_kv_w2048)
    dils = [dil for _, dil in SW_GROUPS]

    xp = x_prompt.reshape(bp * tp, d)
    xs = jnp.zeros((s_rows, d), F32).at[:ns].set(x_sample.reshape(ns, d))
    p_c, p_n, p_m, p_kv = [], [], [], [[], [], []]
    s_c, s_n, s_m, s_kv, s_v = [], [], [], [[], [], []], []
    mods_p = [mod_all[l, ns:ns + bp].reshape(bp, 1, 6 * d) for l in range(depth)]
    mods_s = [mod_all[l] for l in range(depth)]
    hp = _norm_mod(xp, g_norm1[0], mods_p[0], 1, 0, tp)
    hs = _norm_mod(xs, g_norm1[0], mods_s[0], 1, 0, None)
    for l in range(depth):
        mod_p, mod_s = mods_p[l], mods_s[l]
        last = l == depth - 1
        w_down = _cast_bf16(w_ffn_down, l)
        p_a, gates, qkvs, p_r = _in_proj(hp, w_in, l, tabs_p, bp, BF, dils, "")
        xa, c1, n1, m1 = _mlstm_prompt(p_a, gates, b_ml_i[l], b_ml_f[l], bp, tp)
        xb = _swa_prompt(qkvs, bp, tp)
        kvs = [_kv_rows(qkvs[gi], gi, bp, tp) for gi in range(len(SW_GROUPS))]
        xc = _cm_prompt(p_r, g_cm_v, w_s, b_s, l)
        nxt = (g_final, None) if last else (g_norm1[l + 1], mods_p[l + 1])
        out = _dense_tail(xp, (xa, xb, xc), p_r, l, mod_p, tp, wts, w_down, nxt)
        if last:
            y_p = out.reshape(bp, tp, d)
        else:
            xp, hp = out
        p_c.append(c1)
        p_n.append(n1)
        p_m.append(m1[:, 0, :ML_HEADS])
        for gi in range(3):
            p_kv[gi].append(kvs[gi])
        p_a, gates, qkvs, p_r = _in_proj(hs, w_in, l, tabs_s, 1, F32, [1] * len(SW_GROUPS), "_s")
        xa, c1, n1, m1 = _mlstm_sample(p_a, gates, b_ml_i[l], b_ml_f[l], state_mlstm_C, state_mlstm_n,
                                       state_mlstm_m, l)
        xb, kvs = _swa_sample(qkvs, caches, l)
        xc, vn = _cm_sample(p_r, g_cm_v, w_s, b_s, l)
        pad = lambda a: jnp.zeros((s_rows, a.shape[-1]), F32).at[:ns].set(a.reshape(ns, -1))
        nxt = (g_final, None) if last else (g_norm1[l + 1], mods_s[l + 1])
        out = _dense_tail(xs, (pad(xa), pad(xb), xc), p_r, l, mod_s, None, wts, w_down, nxt)
        if last:
            y_s = out[:ns].reshape(ns, ts, d)
        else:
            xs, hs = out
        s_c.append(c1)
        s_n.append(n1)
        s_m.append(m1[:, :, 0])
        for gi in range(3):
            s_kv[gi].append(kvs[gi])
        s_v.append(vn[:ns].reshape(ns, 1, CM_GROUPS, CM_GD))
    st = jnp.stack
    return (y_p, y_s, st(p_c), st(p_n), st(p_m), st(p_kv[0]), st(p_kv[1]), st(p_kv[2]),
            st(s_c), st(s_n), st(s_m), st(s_kv[0]), st(s_kv[1]), st(s_kv[2]), st(s_v))
```

```python
import functools

import jax
import jax.numpy as jnp
from jax import lax
from jax.experimental import pallas as pl
from jax.experimental.pallas import tpu as pltpu

BF = jnp.bfloat16
F32 = jnp.float32

D_MODEL = 2048
ML_HEADS = 8
ML_DQK = 128
ML_DV = 256
SW_GROUPS = ((128, 1), (512, 4), (2048, 16))
SW_HPG = 4
SW_HEADS = 12
SW_HD = 128
SW_ROT = SW_HD // 4
SW_BLK = 128
ROPE_THETA = 500000.0
CM_CHUNK = 128
CM_GROUPS = 8
CM_GD = 256
D_FF = 5632
RMS_EPS = 1e-6
NEG = -1e30

LANE = 128
O_GATES = 6144
O_QB = O_GATES + 2 * ML_HEADS
O_UC = O_QB + 3 * SW_HEADS * SW_HD
N_REST = 2 * CM_GROUPS * CM_GD + 3 * D_MODEL
N_GATE = LANE
C_QA, C_KA, C_VA, C_OA = 0, 1024, 2048, 4096
C_UC, C_VC, C_G = 0, 2048, 4096

TM = 512
TM_WIDE = 1024
QKV_SUB = 256
SWA_UNITS = 4
ML_L = 128
VMEM_LIMIT = 48 * 1024 * 1024
VMEM_LIMIT_BIG = 56 * 1024 * 1024


def _cparams(*sem, vmem=VMEM_LIMIT):
    return pltpu.CompilerParams(dimension_semantics=sem, vmem_limit_bytes=vmem)


def _sigmoid(x):
    return 0.5 * (jnp.tanh(0.5 * x) + 1.0)


def _log_sigmoid(x):
    return jnp.minimum(x, 0.0) - jnp.log1p(jnp.exp(-jnp.abs(x)))


def _ada_kernel(c_ref, w_ref, b_ref, o_ref):
    c = c_ref[...]
    a = (c * _sigmoid(c)).astype(BF)
    o_ref[...] = jnp.dot(a, w_ref[...].astype(BF), preferred_element_type=F32) + b_ref[...]


def _ada(c_all, w_ada, b_ada):
    depth, d, n = w_ada.shape
    rows = c_all.shape[0]
    tn = 1024
    return pl.pallas_call(
        _ada_kernel,
        grid=(depth, n // tn),
        in_specs=[pl.BlockSpec((rows, d), lambda l, j: (0, 0)),
                  pl.BlockSpec((None, d, tn), lambda l, j: (l, 0, j)),
                  pl.BlockSpec((None, 1, tn), lambda l, j: (l, 0, j))],
        out_specs=pl.BlockSpec((None, rows, tn), lambda l, j: (l, 0, j)),
        out_shape=jax.ShapeDtypeStruct((depth, rows, n), F32),
        compiler_params=_cparams("parallel", "parallel"),
        name="ada",
    )(c_all, w_ada, b_ada.reshape(depth, 1, n))


def _mod_spec(mod, chunk, width, tm, rows_per_batch, row_of, col_of):
    per = D_MODEL // width
    if mod.ndim == 3:
        tiles_per_batch = rows_per_batch // tm
        return pl.BlockSpec((None, 1, width),
                            lambda *g: (row_of(*g) // tiles_per_batch, 0, chunk * per + col_of(*g)))
    return pl.BlockSpec((tm, width), lambda *g: (row_of(*g), chunk * per + col_of(*g)))


NORM_ROWS = 16


def _row_groups(n_rows, body):
    def step(i, c):
        body(pl.ds(pl.multiple_of(i * NORM_ROWS, NORM_ROWS), NORM_ROWS))
        return c
    groups = n_rows // NORM_ROWS
    lax.fori_loop(0, groups, step, 0, unroll=min(groups, 4))


def _norm_mod_kernel(x_ref, g_ref, sc_ref, sh_ref, o_ref):
    per_row = sc_ref.shape[0] > 1

    def body(rows):
        x = x_ref[rows, :]
        y = x * lax.rsqrt(jnp.mean(x * x, axis=1, keepdims=True) + RMS_EPS) * g_ref[...]
        sc = sc_ref[rows, :] if per_row else sc_ref[...]
        sh = sh_ref[rows, :] if per_row else sh_ref[...]
        o_ref[rows, :] = (y * (1.0 + sc) + sh).astype(o_ref.dtype)

    _row_groups(x_ref.shape[0], body)


def _norm_mod(x, g, mod, sc_chunk, sh_chunk, rows_per_batch):
    m, d = x.shape
    tm = min(TM, m)
    row_of = lambda i: i
    col_of = lambda i: 0
    return pl.pallas_call(
        _norm_mod_kernel,
        grid=(m // tm,),
        in_specs=[pl.BlockSpec((tm, d), lambda i: (i, 0)),
                  pl.BlockSpec((1, d), lambda i: (0, 0)),
                  _mod_spec(mod, sc_chunk, d, tm, rows_per_batch, row_of, col_of),
                  _mod_spec(mod, sh_chunk, d, tm, rows_per_batch, row_of, col_of)],
        out_specs=pl.BlockSpec((tm, d), lambda i: (i, 0)),
        out_shape=jax.ShapeDtypeStruct((m, d), BF),
        compiler_params=_cparams("parallel"),
        name="norm_mod",
    )(x, g.reshape(1, d), mod, mod)


def _cast_weights_once(pairs):
    @pl.when(pl.program_id(1) == 0)
    def _():
        for src, dst in pairs:
            dst[...] = src[...].astype(BF)


_NT =(((1,), (1,)), ((), ()))


def _wt_spec(k, l, tn, start_of):
    return pl.BlockSpec((pl.Element(1), pl.Element(tn), pl.Element(k)),
                        lambda j, i: (l, pl.multiple_of(start_of(j), 8), 0))


def _with_rider(x_ref, xs_ref, compute, store):
    tm = x_ref.shape[0]

    @pl.when(pl.program_id(1) == 0)
    def _():
        res = compute(jnp.concatenate([x_ref[...], xs_ref[...]], axis=0))
        store(tuple(r[:tm] for r in res), tuple(r[tm:] for r in res))

    @pl.when(pl.program_id(1) != 0)
    def _():
        store(compute(x_ref[...]), None)


def _mm_t_kernel(x_ref, xs_ref, w_ref, o_ref, os_ref, wbf):
    _cast_weights_once([(w_ref.at[0], wbf)])

    def compute(rows):
        return (lax.dot_general(rows, wbf[...], _NT, preferred_element_type=F32),)

    def store(main, rider):
        o_ref[...] = main[0].astype(o_ref.dtype)
        if rider is not None:
            os_ref[...] = rider[0].astype(os_ref.dtype)

    _with_rider(x_ref, xs_ref, compute, store)


def _mm_t(x, xs, wt, l, tn, col0, n_tiles, out_dtype, name):
    m, k = x.shape
    ms = xs.shape[0]
    tm = min(TM_WIDE, m)
    return pl.pallas_call(
        _mm_t_kernel,
        grid=(n_tiles, m // tm),
        in_specs=[pl.BlockSpec((tm, k), lambda j, i: (i, 0)),
                  pl.BlockSpec((ms, k), lambda j, i: (0, 0)),
                  _wt_spec(k, l, tn, lambda j: col0 + j * tn)],
        out_specs=[pl.BlockSpec((tm, tn), lambda j, i: (i, j)),
                   pl.BlockSpec((ms, tn), lambda j, i: (0, j))],
        out_shape=[jax.ShapeDtypeStruct((m, n_tiles * tn), out_dtype),
                   jax.ShapeDtypeStruct((ms, n_tiles * tn), F32)],
        scratch_shapes=[pltpu.VMEM((tn, k), BF)],
        compiler_params=_cparams("arbitrary", "arbitrary"),
        name=name,
    )(x, xs, wt)


def _qkv_kernel(x_ref, w_ref, tab_ref, o_ref, wbf, acc_ref, *, dil):
    _cast_weights_once([(w_ref.at[0], wbf)])
    heads = [slice(hh * SW_HD, (hh + 1) * SW_HD) for hh in range(SW_HPG)]
    tm = x_ref.shape[0]
    sub = min(tm, QKV_SUB)

    def tile(rotary):
        for s in range(tm // sub):
            rows_in = slice(s * sub, (s + 1) * sub)
            acc = lax.dot_general(x_ref[rows_in, :], wbf[...], _NT, preferred_element_type=F32)
            tab = tab_ref[rows_in, :]
            vals = [_rope(acc[:, sl], tab) if rotary else acc[:, sl] for sl in heads]
            if dil == 1:
                for sl, val in zip(heads, vals):
                    o_ref[0, rows_in, sl] = val.astype(o_ref.dtype)
                continue
            for hh, val in enumerate(vals):
                acc_ref[hh, rows_in, :] = val
            n = sub // dil
            for r in range(dil):
                for hh, sl in enumerate(heads):
                    o_ref[r, s * n:(s + 1) * n, sl] = acc_ref[hh, pl.ds(s * sub + r, n, stride=dil), :].astype(
                        o_ref.dtype)

    is_v = pl.program_id(0) == 2
    pl.when(jnp.logical_not(is_v))(lambda: tile(True))
    pl.when(is_v)(lambda: tile(False))


def _qkv_proj(x, wt, l, gi, dil, tabs, batch, out_dtype, name):
    m, k = x.shape
    seq = m // batch
    tm = min(TM_WIDE, seq)
    tpb = seq // tm
    hw = SW_HPG * SW_HD
    wide = len(SW_GROUPS) * hw
    return pl.pallas_call(
        functools.partial(_qkv_kernel, dil=dil),
        grid=(3, m // tm),
        in_specs=[pl.BlockSpec((tm, k), lambda j, i: (i, 0)),
                  _wt_spec(k, l, hw, lambda j: O_QB + j * wide + gi * hw),
                  pl.BlockSpec((None, tm, 3 * SW_HD), lambda j, i: (jnp.minimum(j, 1), i % tpb, 0))],
        out_specs=pl.BlockSpec((None, dil, tm // dil, hw), lambda j, i: (i // tpb, 0, i % tpb, j)),
        out_shape=jax.ShapeDtypeStruct((batch, dil, seq // dil, 3 * hw), out_dtype),
        scratch_shapes=[pltpu.VMEM((hw, k), BF), pltpu.VMEM((SW_HPG, tm, SW_HD), F32)],
        compiler_params=_cparams("arbitrary", "arbitrary"),
        name=name,
    )(x, wt, tabs)


def _cast_kernel(w_ref, o_ref):
    o_ref[...] = w_ref[...].astype(o_ref.dtype)


def _cast_bf16(w3, l):
    _, k, n = w3.shape
    rows = 512
    return pl.pallas_call(
        _cast_kernel,
        grid=(k // rows,),
        in_specs=[pl.BlockSpec((None, rows, n), lambda i: (l, i, 0))],
        out_specs=pl.BlockSpec((rows, n), lambda i: (i, 0)),
        out_shape=jax.ShapeDtypeStruct((k, n), BF),
        compiler_params=_cparams("parallel"),
        name="cast_bf16",
    )(w3)


def _mm_res_norm_kernel(*refs, cast, final):
    x_ref, w_ref, r_ref, g_ref, gn_ref = refs[:5]
    if final:
        y_ref = refs[5]
        scratch = refs[6:]
        o_ref = scratch[-1]
    else:
        sc_ref, sh_ref, o_ref, y_ref = refs[5:9]
        scratch = refs[9:]
    if cast:
        wbf = scratch[0]

        @pl.when(pl.program_id(0) == 0)
        def _():
            wbf[...] = w_ref[...].astype(BF)
        w = wbf[...]
    else:
        w = w_ref[...]
    o_ref[...] = r_ref[...] + g_ref[...] * jnp.dot(x_ref[...], w, preferred_element_type=F32)
    per_row = (not final) and sc_ref.shape[0] > 1

    def body(rows):
        x = o_ref[rows, :]
        y = x * lax.rsqrt(jnp.mean(x * x, axis=1, keepdims=True) + RMS_EPS) * gn_ref[...]
        if not final:
            sc = sc_ref[rows, :] if per_row else sc_ref[...]
            sh = sh_ref[rows, :] if per_row else sh_ref[...]
            y = y * (1.0 + sc) + sh
        y_ref[rows, :] = y.astype(y_ref.dtype)

    _row_groups(o_ref.shape[0], body)


def _mm_res_norm(x, w, l, res, mod, gate_chunk, rows_per_batch, gn, mod_next, sc_chunk, sh_chunk, name):
    m, k = x.shape
    n = D_MODEL
    tm = min(256, m)
    cast = w.ndim == 3
    final = mod_next is None
    once = dict(pipeline_mode=pl.Buffered(1))
    w_spec = (pl.BlockSpec((None, k, n), lambda i: (l, 0, 0), **once) if cast
              else pl.BlockSpec((k, n), lambda i: (0, 0), **once))
    row_of, col_of = (lambda i: i), (lambda i: 0)
    tile = pl.BlockSpec((tm, n), lambda i: (i, 0))
    in_specs = [pl.BlockSpec((tm, k), lambda i: (i, 0)), w_spec, tile,
                _mod_spec(mod, gate_chunk, n, tm, rows_per_batch, row_of, col_of),
                pl.BlockSpec((1, n), lambda i: (0, 0))]
    args = [x, w, res, mod, gn.reshape(1, n)]
    if final:
        out_specs, out_shape = tile, jax.ShapeDtypeStruct((m, n), F32)
    else:
        in_specs += [_mod_spec(mod_next, sc_chunk, n, tm, rows_per_batch, row_of, col_of),
                     _mod_spec(mod_next, sh_chunk, n, tm, rows_per_batch, row_of, col_of)]
        args += [mod_next, mod_next]
        out_specs = [tile, tile]
        out_shape = [jax.ShapeDtypeStruct((m, n), F32), jax.ShapeDtypeStruct((m, n), BF)]
    return pl.pallas_call(
        functools.partial(_mm_res_norm_kernel, cast=cast, final=final),
        grid=(m // tm,),
        in_specs=in_specs,
        out_specs=out_specs,
        out_shape=out_shape,
        scratch_shapes=([pltpu.VMEM((k, n), BF)] if cast else []) + ([pltpu.VMEM((tm, n), F32)] if final else []),
        compiler_params=_cparams("arbitrary", vmem=VMEM_LIMIT_BIG),
        name=name,
    )(*args)


def _ffn1_kernel(x_ref, xs_ref, wg_ref, wu_ref, o_ref, os_ref, wgbf, wubf):
    _cast_weights_once([(wg_ref, wgbf), (wu_ref, wubf)])

    def compute(rows):
        g = jnp.dot(rows, wgbf[...], preferred_element_type=F32)
        u = jnp.dot(rows, wubf[...], preferred_element_type=F32)
        return (g * _sigmoid(g) * u,)

    def store(main, rider):
        o_ref[...] = main[0].astype(o_ref.dtype)
        if rider is not None:
            os_ref[...] = rider[0].astype(os_ref.dtype)

    _with_rider(x_ref, xs_ref, compute, store)


def _ffn1(x, xs, w_gu, l):
    m, k = x.shape
    ms = xs.shape[0]
    tn = 512
    nj = D_FF // tn
    tm = min(TM_WIDE, m)
    return pl.pallas_call(
        _ffn1_kernel,
        grid=(nj, m // tm),
        in_specs=[pl.BlockSpec((tm, k), lambda j, i: (i, 0)),
                  pl.BlockSpec((ms, k), lambda j, i: (0, 0)),
                  pl.BlockSpec((None, k, tn), lambda j, i: (l, 0, j)),
                  pl.BlockSpec((None, k, tn), lambda j, i: (l, 0, nj + j))],
        out_specs=[pl.BlockSpec((tm, tn), lambda j, i: (i, j)),
                   pl.BlockSpec((ms, tn), lambda j, i: (0, j))],
        out_shape=[jax.ShapeDtypeStruct((m, D_FF), BF), jax.ShapeDtypeStruct((ms, D_FF), BF)],
        scratch_shapes=[pltpu.VMEM((k, tn), BF), pltpu.VMEM((k, tn), BF)],
        compiler_params=_cparams("arbitrary", "arbitrary"),
        name="ffn_gate_up",
    )(x, xs, w_gu, w_gu)


def _merge_kernel(xa_ref, xb_ref, xc_ref, g0_ref, g1_ref, g2_ref, wa_ref, wb_ref, wc_ref, o_ref,
                  wabf, wbbf, wcbf):
    _cast_weights_once([(wa_ref, wabf), (wb_ref, wbbf), (wc_ref, wcbf)])
    da = jnp.dot(xa_ref[...].astype(BF), wabf[...], preferred_element_type=F32)
    db = jnp.dot(xb_ref[...].astype(BF), wbbf[...], preferred_element_type=F32)
    dc = jnp.dot(xc_ref[...].astype(BF), wcbf[...], preferred_element_type=F32)
    o_ref[...] = (_sigmoid(g0_ref[...].astype(F32)) * da + _sigmoid(g1_ref[...].astype(F32)) * db
                  + _sigmoid(g2_ref[...].astype(F32)) * dc).astype(o_ref.dtype)


def _merge(xa, xb, xc, p, w_a, w_b, w_c, l):
    m = xa.shape[0]
    tn = 1024
    tm = min(256, m)
    ka, kb, kc = xa.shape[1], xb.shape[1], xc.shape[1]
    g_blk = C_G // tn
    per = D_MODEL // tn
    once = dict(pipeline_mode=pl.Buffered(1))
    return pl.pallas_call(
        _merge_kernel,
        grid=(D_MODEL // tn, m // tm),
        in_specs=[pl.BlockSpec((tm, ka), lambda j, i: (i, 0)),
                  pl.BlockSpec((tm, kb), lambda j, i: (i, 0)),
                  pl.BlockSpec((tm, kc), lambda j, i: (i, 0)),
                  pl.BlockSpec((tm, tn), lambda j, i: (i, g_blk + j)),
                  pl.BlockSpec((tm, tn), lambda j, i: (i, g_blk + per + j)),
                  pl.BlockSpec((tm, tn), lambda j, i: (i, g_blk + 2 * per + j)),
                  pl.BlockSpec((None, ka, tn), lambda j, i: (l, 0, j), **once),
                  pl.BlockSpec((None, kb, tn), lambda j, i: (l, 0, j), **once),
                  pl.BlockSpec((None, kc, tn), lambda j, i: (l, 0, j), **once)],
        out_specs=pl.BlockSpec((tm, tn), lambda j, i: (i, j)),
        out_shape=jax.ShapeDtypeStruct((m, D_MODEL), BF),
        scratch_shapes=[pltpu.VMEM((ka, tn), BF), pltpu.VMEM((kb, tn), BF), pltpu.VMEM((kc, tn), BF)],
        compiler_params=_cparams("arbitrary", "arbitrary", vmem=VMEM_LIMIT_BIG),
        name="branch_merge",
    )(xa, xb, xc, p, p, p, w_a, w_b, w_c)


def _scan(x, axis, op, fill):
    idx = lax.broadcasted_iota(jnp.int32, x.shape, axis)
    shift = 1
    while shift < x.shape[axis]:
        x = op(x, jnp.where(idx >= shift, pltpu.roll(x, shift, axis), fill))
        shift *= 2
    return x


def _mlstm_kernel(q_ref, k_ref, v_ref, o_ref, gc_ref, gr_ref, brow_ref, bcol_ref,
                  xa_ref, c_ref, n_ref, m_ref, nn_ref):
    chunk = pl.program_id(1)

    @pl.when(chunk == 0)
    def _():
        c_ref[...] = jnp.zeros_like(c_ref)
        m_ref[...] = jnp.zeros_like(m_ref)
        nn_ref[...] = jnp.zeros_like(nn_ref)

    L = q_ref.shape[0]
    nh = ML_HEADS
    scale = ML_DQK ** -0.5
    gc = gc_ref[...] + brow_ref[...]
    gr = gr_ref[...] + bcol_ref[...]
    f_c = pltpu.roll(_scan(_log_sigmoid(gc), 0, jnp.add, 0.0), LANE - nh, 1)
    m_prev = m_ref[...]
    m_t = f_c + jnp.maximum(m_prev, _scan(gc - f_c, 0, jnp.maximum, NEG))
    w_inter = jnp.exp(f_c + m_prev - m_t)
    e_neg_m = jnp.exp(-m_t)
    f_minus_m = f_c - m_t
    m_new = m_t[L - 1:L, :]
    f_last = f_c[L - 1:L, :]
    w_last = jnp.exp(f_last - f_c + gc - m_new) * scale
    decay = jnp.exp(f_last + m_prev - m_new)
    m_ref[...] = m_new
    f_r = _scan(_log_sigmoid(gr), 1, jnp.add, 0.0)
    a_r = gr[0:nh, :] - f_r[nh:2 * nh, :]
    row = lax.broadcasted_iota(jnp.int32, (L, L), 0)
    col = lax.broadcasted_iota(jnp.int32, (L, L), 1)
    tri = col <= row
    ones = jnp.ones((L, ML_DQK), BF)
    nt = (((1,), (1,)), ((), ()))
    tn = (((0,), (0,)), ((), ()))
    for h in range(nh):
        lane = lambda x: x[:, h:h + 1]
        d_w = jnp.exp(jnp.where(tri, lane(f_minus_m) + a_r[h:h + 1, :], NEG))
        q = q_ref[:, h * ML_DQK:(h + 1) * ML_DQK]
        k = k_ref[:, h * ML_DQK:(h + 1) * ML_DQK]
        v = v_ref[:, h * ML_DV:(h + 1) * ML_DV]
        s = (lax.dot_general(q, k, nt, preferred_element_type=F32) * (scale * d_w)).astype(BF)
        c_old = c_ref[h]
        nn_old = nn_ref[h]
        wi = lane(w_inter)
        num = (jnp.dot(s, v, preferred_element_type=F32)
               + wi * jnp.dot(q, c_old.astype(BF), preferred_element_type=F32))
        den = (jnp.dot(s, ones, preferred_element_type=F32)
               + wi * jnp.dot(q, nn_old.astype(BF), preferred_element_type=F32))
        inv = 1.0 / jnp.maximum(jnp.abs(den[:, 0:1]), lane(e_neg_m))
        og = o_ref[:, h * ML_DV:(h + 1) * ML_DV].astype(F32)
        xa_ref[:, h * ML_DV:(h + 1) * ML_DV] = (_sigmoid(og) * (num * inv)).astype(xa_ref.dtype)
        kw = (k.astype(F32) * lane(w_last)).astype(BF)
        dk = decay[:, h:h + 1]
        c_ref[h] = dk * c_old + lax.dot_general(kw, v, tn, preferred_element_type=F32)
        nn_ref[h] = dk * nn_old + lax.dot_general(kw, ones, tn, preferred_element_type=F32)

    @pl.when(chunk == pl.num_programs(1) - 1)
    def _():
        for h in range(nh):
            n_ref[h:h + 1, :] = nn_ref[h].T[0:1, :]


def _mlstm_prompt(p, gates, b_i, b_f, batch, seq):
    m_rows = p.shape[0]
    L = ML_L
    nc = seq // L
    bias = jnp.concatenate([b_i, b_f]).astype(F32)
    bias_row = jnp.zeros((1, N_GATE), F32).at[0, :2 * ML_HEADS].set(bias)
    bias_col = bias.reshape(2 * ML_HEADS, 1)
    gates_t = gates[:, :2 * ML_HEADS].T
    hq, hv = ML_HEADS * ML_DQK, ML_HEADS * ML_DV
    return pl.pallas_call(
        _mlstm_kernel,
        grid=(batch, nc),
        in_specs=[pl.BlockSpec((L, hq), lambda b, c: (b * nc + c, C_QA // hq)),
                  pl.BlockSpec((L, hq), lambda b, c: (b * nc + c, C_KA // hq)),
                  pl.BlockSpec((L, hv), lambda b, c: (b * nc + c, C_VA // hv)),
                  pl.BlockSpec((L, hv), lambda b, c: (b * nc + c, C_OA // hv)),
                  pl.BlockSpec((L, N_GATE), lambda b, c: (b * nc + c, 0)),
                  pl.BlockSpec((2 * ML_HEADS, L), lambda b, c: (0, b * nc + c)),
                  pl.BlockSpec((1, N_GATE), lambda b, c: (0, 0)),
                  pl.BlockSpec((2 * ML_HEADS, 1), lambda b, c: (0, 0))],
        out_specs=[pl.BlockSpec((L, hv), lambda b, c: (b * nc + c, 0)),
                   pl.BlockSpec((None, ML_HEADS, ML_DQK, ML_DV), lambda b, c: (b, 0, 0, 0)),
                   pl.BlockSpec((None, ML_HEADS, ML_DQK), lambda b, c: (b, 0, 0)),
                   pl.BlockSpec((None, 1, LANE), lambda b, c: (b, 0, 0))],
        out_shape=[jax.ShapeDtypeStruct((m_rows, hv), BF),
                   jax.ShapeDtypeStruct((batch, ML_HEADS, ML_DQK, ML_DV), F32),
                   jax.ShapeDtypeStruct((batch, ML_HEADS, ML_DQK), F32),
                   jax.ShapeDtypeStruct((batch, 1, LANE), F32)],
        scratch_shapes=[pltpu.VMEM((ML_HEADS, ML_DQK, ML_DQK), F32)],
        compiler_params=_cparams("parallel", "arbitrary"),
        name="mlstm_prompt",
    )(p, p, p, p, gates, gates_t, bias_row, bias_col)


def _rope_table(pos):
    half = SW_ROT // 2
    freq = ROPE_THETA ** (-jnp.arange(half, dtype=F32) * 2.0 / SW_ROT)
    ang = pos.astype(F32)[:, None] * freq[None, :]
    cos, sin = jnp.cos(ang), jnp.sin(ang)
    t = pos.shape[0]
    z = lambda w: jnp.zeros((t, w), F32)
    return jnp.concatenate([cos, cos, jnp.ones((t, SW_HD - SW_ROT), F32),
                            -sin, z(SW_HD - half),
                            z(half), sin, z(SW_HD - SW_ROT)], axis=1)


def _rope(x, tab):
    half = SW_ROT // 2
    return (x * tab[:, 0:SW_HD]
            + pltpu.roll(x, SW_HD - half, 1) * tab[:, SW_HD:2 * SW_HD]
            + pltpu.roll(x, half, 1) * tab[:, 2 * SW_HD:3 * SW_HD])


def _attend(q, k_own, v_own, k_prev, v_prev, prev_shift):
    blk = SW_BLK
    row = lax.broadcasted_iota(jnp.int32, (1, blk, blk), 1)
    col = lax.broadcasted_iota(jnp.int32, (1, blk, blk), 2)
    qk = (((2,), (2,)), ((0,), (0,)))
    pv = (((2,), (1,)), ((0,), (0,)))
    s_o = jnp.where(col <= row, lax.dot_general(q, k_own, qk, preferred_element_type=F32), NEG)
    s_p = jnp.where(col >= row + prev_shift, lax.dot_general(q, k_prev, qk, preferred_element_type=F32), NEG)
    mx = jnp.maximum(jnp.max(s_o, axis=2, keepdims=True), jnp.max(s_p, axis=2, keepdims=True))
    p_o = jnp.exp(s_o - mx)
    p_p = jnp.exp(s_p - mx)
    den = jnp.sum(p_o, axis=2, keepdims=True) + jnp.sum(p_p, axis=2, keepdims=True)
    acc = (lax.dot_general(p_o.astype(BF), v_own, pv, preferred_element_type=F32)
           + lax.dot_general(p_p.astype(BF), v_prev, pv, preferred_element_type=F32))
    return acc, mx, den


def _swa_kernel(q1, k1, v1, k1p, v1p, q2, k2, v2, k2p, v2p, q3, k3, v3, k3p, v3p, xb_ref, o_acc, m_acc, l_acc):
    blk = SW_BLK
    u = SWA_UNITS
    first_shift = jnp.where(pl.program_id(1) > 0, 0, blk)
    wide = lambda x: jnp.broadcast_to(x, x.shape[:-1] + (SW_HD,))

    def merge(rows, acc, mx, den):
        m_old = m_acc[rows, :]
        m_new = jnp.maximum(m_old, mx)
        a = jnp.exp(m_old - m_new)
        b = jnp.exp(mx - m_new)
        o_acc[rows, :] = a * o_acc[rows, :] + b * acc
        l_acc[rows, :] = a * l_acc[rows, :] + b * den
        m_acc[rows, :] = m_new

    def g1_batch(rows, k_p, v_p, shift):
        split = lambda x: x.reshape(u, blk, SW_HD)
        acc, mx, den = _attend(split(q1[rows, :]), split(k1[rows, :]), split(v1[rows, :]),
                               split(k_p), split(v_p), shift)
        o_acc[rows, :] = acc.reshape(u * blk, SW_HD)
        m_acc[rows, :] = wide(mx).reshape(u * blk, SW_HD)
        l_acc[rows, :] = wide(den).reshape(u * blk, SW_HD)

    head = pl.ds(0, (u - 1) * blk)
    unit = lax.broadcasted_iota(jnp.int32, (u, 1, 1), 0)
    g1_batch(pl.ds(0, u * blk),
             jnp.concatenate([k1p[...], k1[head, :]], axis=0), jnp.concatenate([v1p[...], v1[head, :]], axis=0),
             jnp.where(unit == 0, first_shift, 0))

    def g1_body(i, c):
        rows = pl.ds(pl.multiple_of(i * u * blk, blk), u * blk)
        prev = pl.ds(pl.multiple_of(i * u * blk - blk, blk), u * blk)
        g1_batch(rows, k1[prev, :], v1[prev, :], 0)
        return c

    lax.fori_loop(1, q1.shape[0] // (u * blk), g1_body, 0)

    def dilated(q, k, v, kp, vp):
        dil = q.shape[0]
        n_sub = q.shape[1] // blk

        def batch(sub, r0, k_p, v_p, shift):
            rs = pl.ds(r0, u)
            rows = pl.ds(pl.multiple_of(sub * blk, blk), blk)
            acc, mx, den = _attend(q[rs, rows, :], k[rs, rows, :], v[rs, rows, :], k_p, v_p, shift)
            for i in range(u):
                merge(pl.ds(sub * blk * dil + r0 + i, blk, stride=dil), acc[i], wide(mx[i]), wide(den[i]))

        def first(i, c):
            rs = pl.ds(i * u, u)
            batch(0, i * u, kp[rs], vp[rs], first_shift)
            return c

        lax.fori_loop(0, dil // u, first, 0)

        def rest(i, c):
            sub = 1 + i // (dil // u)
            r0 = (i % (dil // u)) * u
            prev = pl.ds(pl.multiple_of((sub - 1) * blk, blk), blk)
            batch(sub, r0, k[pl.ds(r0, u), prev, :], v[pl.ds(r0, u), prev, :], 0)
            return c

        lax.fori_loop(0, (n_sub - 1) * (dil // u), rest, 0)

    dilated(q2, k2, v2, k2p, v2p)
    dilated(q3, k3, v3, k3p, v3p)
    xb_ref[...] = (o_acc[...] * (1.0 / l_acc[...])).astype(xb_ref.dtype)


def _swa_prompt(qkvs, batch, seq):
    blk = SW_BLK
    span = SW_GROUPS[-1][0]
    assert seq % span == 0 and all(w // d == blk for w, d in SW_GROUPS)
    nspan = seq // span
    in_specs = []
    args = []
    for (win, dil), qkv in zip(SW_GROUPS, qkvs):
        rows = span // dil
        per = rows // blk
        own = lambda c: pl.BlockSpec((None, dil, rows, SW_HD), lambda b, s, h, c=c: (b, 0, s, c * SW_HPG + h))
        prev = lambda c: pl.BlockSpec(
            (None, dil, blk, SW_HD), lambda b, s, h, c=c, per=per: (b, 0, jnp.maximum(s * per - 1, 0), c * SW_HPG + h))
        if dil == 1:
            own = lambda c: pl.BlockSpec((None, None, span, SW_HD), lambda b, s, h, c=c: (b, 0, s, c * SW_HPG + h))
            prev = lambda c: pl.BlockSpec(
                (None, None, blk, SW_HD),
                lambda b, s, h, c=c, per=per: (b, 0, jnp.maximum(s * per - 1, 0), c * SW_HPG + h))
        in_specs += [own(0), own(1), own(2), prev(1), prev(2)]
        args += [qkv] * 5
    return pl.pallas_call(
        _swa_kernel,
        grid=(batch, nspan, SW_HPG),
        in_specs=in_specs,
        out_specs=pl.BlockSpec((span, SW_HD), lambda b, s, h: (b * nspan + s, h)),
        out_shape=jax.ShapeDtypeStruct((batch * seq, SW_HPG * SW_HD), BF),
        scratch_shapes=[pltpu.VMEM((span, SW_HD), F32)] * 3,
        compiler_params=_cparams("parallel", "parallel", "parallel"),
        name="swa_prompt",
    )(*args)


def _kv_rows_kernel(k_ref, v_ref, o_ref, *, dil):
    for r in range(dil):
        rows = pl.ds(r, SW_BLK, stride=dil) if dil > 1 else pl.ds(0, SW_BLK)
        for hh in range(SW_HPG):
            sl = slice(hh * SW_HD, (hh + 1) * SW_HD)
            o_ref[hh, rows, :] = k_ref[r, :, sl].astype(F32)
            o_ref[SW_HPG + hh, rows, :] = v_ref[r, :, sl].astype(F32)


def _kv_rows(qkv, gi, batch, seq):
    win, dil = SW_GROUPS[gi]
    hw = SW_HPG * SW_HD
    last = seq // dil // SW_BLK - 1
    out = pl.pallas_call(
        functools.partial(_kv_rows_kernel, dil=dil),
        grid=(batch,),
        in_specs=[pl.BlockSpec((None, dil, SW_BLK, hw), lambda b: (b, 0, last, 1)),
                  pl.BlockSpec((None, dil, SW_BLK, hw), lambda b: (b, 0, last, 2))],
        out_specs=pl.BlockSpec((None, 2 * SW_HPG, win, SW_HD), lambda b: (b, 0, 0, 0)),
        out_shape=jax.ShapeDtypeStruct((batch, 2 * SW_HPG, win, SW_HD), F32),
        compiler_params=_cparams("parallel"),
        name=f"kv_rows_w{win}",
    )(qkv, qkv)
    return out.reshape(batch, 2, SW_HPG, win, SW_HD).transpose(0, 3, 1, 2, 4)


def _cm_kernel(u_ref, v_ref, gv_ref, ws_ref, bs_ref, o_ref):
    ch = CM_CHUNK
    row = lax.broadcasted_iota(jnp.int32, (ch, ch), 0)
    col = lax.broadcasted_iota(jnp.int32, (ch, ch), 1)
    tri = col <= row
    for g in range(CM_GROUPS):
        sl = slice(g * CM_GD, (g + 1) * CM_GD)
        w = jnp.where(tri, ws_ref[g], 0.0).astype(BF)
        for c in range(u_ref.shape[0] // ch):
            rows = slice(c * ch, (c + 1) * ch)
            vg = v_ref[rows, sl].astype(F32)
            vn = vg * lax.rsqrt(jnp.mean(vg * vg, axis=1, keepdims=True) + RMS_EPS) * gv_ref[g:g + 1, :]
            mixed = jnp.dot(w, vn.astype(BF), preferred_element_type=F32) + bs_ref[:, g:g + 1]
            o_ref[rows, sl] = (u_ref[rows, sl].astype(F32) * mixed).astype(o_ref.dtype)


def _cm_prompt(p, g_v, w_s, b_s, l):
    m_rows = p.shape[0]
    ch = CM_CHUNK
    width = CM_GROUPS * CM_GD
    rows = 2 * ch
    return pl.pallas_call(
        _cm_kernel,
        grid=(m_rows // rows,),
        in_specs=[pl.BlockSpec((rows, width), lambda i: (i, C_UC // width)),
                  pl.BlockSpec((rows, width), lambda i: (i, C_VC // width)),
                  pl.BlockSpec((None, CM_GROUPS, CM_GD), lambda i: (l, 0, 0)),
                  pl.BlockSpec((None, CM_GROUPS, ch, ch), lambda i: (l, 0, 0, 0)),
                  pl.BlockSpec((ch, CM_GROUPS), lambda i: (0, 0))],
        out_specs=pl.BlockSpec((rows, width), lambda i: (i, 0)),
        out_shape=jax.ShapeDtypeStruct((m_rows, width), BF),
        compiler_params=_cparams("parallel"),
        name="cm_prompt",
    )(p, p, g_v, w_s, b_s[l].T)


def _mlstm_step_kernel(qr_ref, kr_ref, vr_ref, or_ref, qkc_ref, g_ref, brow_ref, m0_ref, c0_ref, n0_ref,
                       xa_ref, c1_ref, n1_ref, m1_ref):
    scale = ML_DQK ** -0.5
    gates = g_ref[...] + brow_ref[...]
    m0 = m0_ref[...]
    for h in range(ML_HEADS):
        ig = gates[:, h:h + 1]
        lf = _log_sigmoid(gates[:, ML_HEADS + h:ML_HEADS + h + 1])
        m_prev = m0[:, h:h + 1]
        inter = lf + m_prev
        m_t = jnp.maximum(inter, ig)
        d_w = jnp.exp(ig - m_t)
        w_inter = jnp.exp(inter - m_t)
        q_row = qr_ref[h:h + 1, :]
        k_row = kr_ref[h:h + 1, :]
        v_row = vr_ref[h:h + 1, :]
        q_col = qkc_ref[:, h:h + 1]
        k_col = qkc_ref[:, ML_HEADS + h:ML_HEADS + h + 1]
        c_old = c0_ref[h]
        n_old = n0_ref[h:h + 1, :]
        s = jnp.sum(q_row * k_row, axis=1, keepdims=True) * scale * d_w
        qc = jnp.sum(q_col * c_old, axis=0, keepdims=True)
        num = s * v_row + w_inter * qc
        den = s + w_inter * jnp.sum(q_row * n_old, axis=1, keepdims=True)
        hh = num / jnp.maximum(jnp.abs(den), jnp.exp(-m_t))
        xa_ref[:, h * ML_DV:(h + 1) * ML_DV] = _sigmoid(or_ref[h:h + 1, :]) * hh
        w_last = d_w * scale
        decay = w_inter
        c1_ref[h] = decay * c_old + (k_col * w_last) * v_row
        n1_ref[h:h + 1, :] = decay * n_old + w_last * k_row
        m1_ref[h:h + 1, :] = jnp.broadcast_to(m_t, (1, ML_DQK))


def _mlstm_sample(ps, gates, b_i, b_f, c0, n0, m0, l):
    nb = c0.shape[1]
    q = ps[:nb, C_QA:C_QA + 1024].reshape(nb, ML_HEADS, ML_DQK)
    k = ps[:nb, C_KA:C_KA + 1024].reshape(nb, ML_HEADS, ML_DQK)
    v = ps[:nb, C_VA:C_VA + 2048].reshape(nb, ML_HEADS, ML_DV)
    o = ps[:nb, C_OA:C_OA + 2048].reshape(nb, ML_HEADS, ML_DV)
    qk_col = jnp.concatenate([q, k], axis=1).transpose(0, 2, 1)
    bias = jnp.concatenate([b_i, b_f]).astype(F32)
    bias_row = jnp.zeros((1, N_GATE), F32).at[0, :2 * ML_HEADS].set(bias)
    g3 = gates[:nb].reshape(nb, 1, N_GATE)
    m03 = m0[l].reshape(nb, 1, ML_HEADS)
    per_b3 = lambda shape: pl.BlockSpec((None,) + shape, lambda b: (b, 0, 0))
    return pl.pallas_call(
        _mlstm_step_kernel,
        grid=(nb,),
        in_specs=[per_b3((ML_HEADS, ML_DQK)), per_b3((ML_HEADS, ML_DQK)),
                  per_b3((ML_HEADS, ML_DV)), per_b3((ML_HEADS, ML_DV)),
                  per_b3((ML_DQK, 2 * ML_HEADS)), per_b3((1, N_GATE)),
                  pl.BlockSpec((1, N_GATE), lambda b: (0, 0)),
                  per_b3((1, ML_HEADS)),
                  pl.BlockSpec((None, None, ML_HEADS, ML_DQK, ML_DV), lambda b: (l, b, 0, 0, 0)),
                  pl.BlockSpec((None, None, ML_HEADS, ML_DQK), lambda b: (l, b, 0, 0))],
        out_specs=[per_b3((1, ML_HEADS * ML_DV)),
                   pl.BlockSpec((None, ML_HEADS, ML_DQK, ML_DV), lambda b: (b, 0, 0, 0)),
                   per_b3((ML_HEADS, ML_DQK)), per_b3((ML_HEADS, ML_DQK))],
        out_shape=[jax.ShapeDtypeStruct((nb, 1, ML_HEADS * ML_DV), F32),
                   jax.ShapeDtypeStruct((nb, ML_HEADS, ML_DQK, ML_DV), F32),
                   jax.ShapeDtypeStruct((nb, ML_HEADS, ML_DQK), F32),
                   jax.ShapeDtypeStruct((nb, ML_HEADS, ML_DQK), F32)],
        compiler_params=_cparams("parallel"),
        name="mlstm_sample",
    )(q, k, v, o, qk_col, g3, bias_row, m03, c0, n0)


def _swa_step_kernel(g0_ref, g1_ref, g2_ref, c0_ref, c1_ref, c2_ref, xb_ref):
    hw = SW_HPG * SW_HD
    per_pos = 2 * SW_HPG
    news = (g0_ref, g1_ref, g2_ref)
    caches = (c0_ref, c1_ref, c2_ref)
    for hh in range(SW_HPG):
        m_run = l_run = acc = None
        for gi in range(len(SW_GROUPS)):
            lo = hh * SW_HD
            q = news[gi][:, lo:lo + SW_HD]
            k_new = news[gi][:, hw + lo:hw + lo + SW_HD]
            v_new = news[gi][:, 2 * hw + lo:2 * hw + lo + SW_HD]
            kc = caches[gi][:, hh, :]
            vc = caches[gi][:, SW_HPG + hh, :]
            s_c = jnp.sum(kc * q, axis=1, keepdims=True)
            s_n = jnp.sum(k_new * q, axis=1, keepdims=True)
            mx = jnp.maximum(jnp.max(s_c, axis=0, keepdims=True), s_n)
            p_c = jnp.exp(s_c - mx)
            p_n = jnp.exp(s_n - mx)
            den = jnp.sum(p_c, axis=0, keepdims=True) + p_n
            part = jnp.sum(p_c * vc, axis=0, keepdims=True) + p_n * v_new
            if gi == 0:
                m_run, l_run, acc = mx, den, part
            else:
                top = jnp.maximum(m_run, mx)
                a, b = jnp.exp(m_run - top), jnp.exp(mx - top)
                m_run, l_run, acc = top, a * l_run + b * den, a * acc + b * part
        xb_ref[:, hh * SW_HD:(hh + 1) * SW_HD] = acc / l_run


def _swa_sample(qkvs, caches, l):
    nb = caches[0].shape[1]
    hw = SW_HPG * SW_HD
    per_pos = 2 * SW_HPG
    news, views, specs = [], [], []
    for (win, dil), cache, qkv in zip(SW_GROUPS, caches, qkvs):
        assert cache.shape[2] == win and win // dil == SW_BLK
        news.append(qkv[0, 0, :nb].reshape(nb, 1, 3 * hw))
        views.append(cache.reshape(cache.shape[0], nb, win // dil, dil * per_pos, SW_HD))
        specs.append(pl.BlockSpec((None, None, SW_BLK, per_pos, SW_HD), lambda b: (l, b, 0, 0, 0)))
    row3 = pl.BlockSpec((None, 1, 3 * hw), lambda b: (b, 0, 0))
    xb = pl.pallas_call(
        _swa_step_kernel,
        grid=(nb,),
        in_specs=[row3, row3, row3] + specs,
        out_specs=pl.BlockSpec((None, 1, hw), lambda b: (b, 0, 0)),
        out_shape=jax.ShapeDtypeStruct((nb, 1, hw), F32),
        compiler_params=_cparams("parallel"),
        name="swa_sample",
    )(*news, *views)
    kvs = [jnp.stack([n[:, :, hw:2 * hw].reshape(nb, 1, SW_HPG, SW_HD),
                      n[:, :, 2 * hw:].reshape(nb, 1, SW_HPG, SW_HD)], axis=2) for n in news]
    return xb.reshape(nb, hw), kvs


def _cm_step_kernel(u_ref, v_ref, gv_ref, ws_ref, bs_ref, o_ref, vn_ref):
    for g in range(CM_GROUPS):
        sl = slice(g * CM_GD, (g + 1) * CM_GD)
        vg = v_ref[:, sl]
        vn = vg * lax.rsqrt(jnp.mean(vg * vg, axis=1, keepdims=True) + RMS_EPS) * gv_ref[g:g + 1, :]
        vn_ref[:, sl] = vn
        w00 = ws_ref[g][0:1, 0:1]
        o_ref[:, sl] = u_ref[:, sl] * (w00 * vn + bs_ref[g:g + 1, 0:1])


def _cm_sample(ps, g_v, w_s, b_s, l):
    rows = ps.shape[0]
    width = CM_GROUPS * CM_GD
    ch = CM_CHUNK
    return pl.pallas_call(
        _cm_step_kernel,
        grid=(1,),
        in_specs=[pl.BlockSpec((rows, width), lambda i: (0, C_UC // width)),
                  pl.BlockSpec((rows, width), lambda i: (0, C_VC // width)),
                  pl.BlockSpec((None, CM_GROUPS, CM_GD), lambda i: (l, 0, 0)),
                  pl.BlockSpec((None, CM_GROUPS, ch, ch), lambda i: (l, 0, 0, 0)),
                  pl.BlockSpec((None, CM_GROUPS, ch), lambda i: (l, 0, 0))],
        out_specs=[pl.BlockSpec((rows, width), lambda i: (0, 0))] * 2,
        out_shape=[jax.ShapeDtypeStruct((rows, width), F32)] * 2,
        compiler_params=_cparams("arbitrary"),
        name="cm_sample",
    )(ps, ps, g_v, w_s, b_s)


def _in_proj(hp, hs, w_in, l, tabs_p, tabs_s, batch, dils):
    wt = jnp.swapaxes(w_in, 1, 2)
    p_a, s_a = _mm_t(hp, hs, wt, l, 1024, 0, O_GATES // 1024, BF, "in_proj_a")
    p_g, s_g = _mm_t(hp, hs, wt, l, N_GATE, O_GATES, 1, F32, "gate_proj")
    p_qkv = [_qkv_proj(hp, wt, l, gi, dil, tabs_p, batch, BF, f"qkv_proj{gi}") for gi, dil in enumerate(dils)]
    s_qkv = [_qkv_proj(hs, wt, l, gi, 1, tabs_s, 1, F32, f"qkv_proj{gi}_s") for gi in range(len(dils))]
    p_r, s_r = _mm_t(hp, hs, wt, l, 1024, O_UC, N_REST // 1024, BF, "in_proj_r")
    return (p_a, p_g, p_qkv, p_r), (s_a, s_g, s_qkv, s_r)


def _qk_tables(pos):
    tab = _rope_table(pos)
    return jnp.stack([tab * (SW_HD ** -0.5), tab])


def _dense_tail(xs, mixers, ps, l, mods, rows_per_batch, wts, gain, mods_next):
    w_down = _cast_bf16(wts["w_ffn_down"], l)
    xs, h2 = list(xs), [None, None]
    for g in range(2):
        xa, xb, xc = mixers[g]
        merged = _merge(xa, xb, xc, ps[g], wts["w_br_a"], wts["w_br_b"], wts["w_br_c"], l)
        xs[g], h2[g] = _mm_res_norm(merged, wts["w_out"], l, xs[g], mods[g], 2, rows_per_batch[g],
                                    wts["g_norm2"][l], mods[g], 4, 3, "out_proj")
    acts = _ffn1(h2[0], h2[1], wts["w_ffn_gate_up"], l)
    return [_mm_res_norm(acts[g], w_down, l, xs[g], mods[g], 5, rows_per_batch[g], gain,
                         None if mods_next is None else mods_next[g], 1, 0, "ffn_down") for g in range(2)]


def kernel(x_prompt, x_sample, state_mlstm_C, state_mlstm_n, state_mlstm_m, cache_swa_kv_w128, cache_swa_kv_w512, cache_swa_kv_w2048, c_prompt, c_sample, w_ada, b_ada, g_norm1, g_norm2, g_final, w_in, b_ml_i, b_ml_f, g_cm_v, w_s, b_s, w_br_a, w_br_b, w_br_c, w_out, w_ffn_gate_up, w_ffn_down):
    bp, tp, d = x_prompt.shape
    ns, ts, _ = x_sample.shape
    depth = w_in.shape[0]
    past_len = 16384
    assert d == D_MODEL and ts == 1 and ns == 8 and tp % max(ML_L, SW_GROUPS[-1][0]) == 0
    s_rows = 16
    wts = dict(w_br_a=w_br_a, w_br_b=w_br_b, w_br_c=w_br_c, w_out=w_out, g_norm2=g_norm2,
               w_ffn_gate_up=w_ffn_gate_up, w_ffn_down=w_ffn_down)

    c_all = jnp.zeros((s_rows, d), F32).at[:ns].set(c_sample).at[ns:ns + bp].set(c_prompt)
    mod_all = _ada(c_all, w_ada, b_ada)
    tabs_p = _qk_tables(jnp.arange(tp, dtype=jnp.int32))
    tabs_s = _qk_tables(jnp.full((s_rows,), past_len, jnp.int32))
    caches = (cache_swa_kv_w128, cache_swa_kv_w512, cache_swa_kv_w2048)
    dils = [dil for _, dil in SW_GROUPS]

    xp = x_prompt.reshape(bp * tp, d)
    xs = jnp.zeros((s_rows, d), F32).at[:ns].set(x_sample.reshape(ns, d))
    p_c, p_n, p_m, p_kv = [], [], [], [[], [], []]
    s_c, s_n, s_m, s_kv, s_v = [], [], [], [[], [], []], []
    mods_p = [mod_all[l, ns:ns + bp].reshape(bp, 1, 6 * d) for l in range(depth)]
    mods_s = [mod_all[l] for l in range(depth)]
    hp = _norm_mod(xp, g_norm1[0], mods_p[0], 1, 0, tp)
    hs = _norm_mod(xs, g_norm1[0], mods_s[0], 1, 0, None)
    for l in range(depth):
        last = l == depth - 1
        (p_a, gates, qkvs, p_r), (s_a, s_gates, s_qkvs, s_r) = _in_proj(hp, hs, w_in, l, tabs_p, tabs_s, bp, dils)
        xa, c1, n1, m1 = _mlstm_prompt(p_a, gates, b_ml_i[l], b_ml_f[l], bp, tp)
        xb = _swa_prompt(qkvs, bp, tp)
        kvs = [_kv_rows(qkvs[gi], gi, bp, tp) for gi in range(len(SW_GROUPS))]
        xc = _cm_prompt(p_r, g_cm_v, w_s, b_s, l)
        mix_p = (xa, xb, xc)
        p_c.append(c1)
        p_n.append(n1)
        p_m.append(m1[:, 0, :ML_HEADS])
        for gi in range(3):
            p_kv[gi].append(kvs[gi])
        xa, c1, n1, m1 = _mlstm_sample(s_a, s_gates, b_ml_i[l], b_ml_f[l], state_mlstm_C, state_mlstm_n,
                                       state_mlstm_m, l)
        xb, kvs = _swa_sample(s_qkvs, caches, l)
        xc, vn = _cm_sample(s_r, g_cm_v, w_s, b_s, l)
        pad = lambda a: jnp.zeros((s_rows, a.shape[-1]), F32).at[:ns].set(a.reshape(ns, -1))
        mix_s = (pad(xa), pad(xb), xc)
        outs = _dense_tail((xp, xs), (mix_p, mix_s), (p_r, s_r), l, (mods_p[l], mods_s[l]), (tp, None), wts,
                           g_final if last else g_norm1[l + 1],
                           None if last else (mods_p[l + 1], mods_s[l + 1]))
        if last:
            y_p = outs[0].reshape(bp, tp, d)
            y_s = outs[1][:ns].reshape(ns, ts, d)
        else:
            (xp, hp), (xs, hs) = outs
        s_c.append(c1)
        s_n.append(n1)
        s_m.append(m1[:, :, 0])
        for gi in range(3):
            s_kv[gi].append(kvs[gi])
        s_v.append(vn[:ns].reshape(ns, 1, CM_GROUPS, CM_GD))
    st = jnp.stack
    return (y_p, y_s, st(p_c), st(p_n), st(p_m), st(p_kv[0]), st(p_kv[1]), st(p_kv[2]),
            st(s_c), st(s_n), st(s_m), st(s_kv[0]), st(s_kv[1]), st(s_kv[2]), st(s_v))
```

```python
import functools

import jax
import jax.numpy as jnp
from jax import lax
from jax.experimental import pallas as pl
from jax.experimental.pallas import tpu as pltpu

BF = jnp.bfloat16
F32 = jnp.float32

D_MODEL = 2048
ML_HEADS = 8
ML_DQK = 128
ML_DV = 256
SW_GROUPS = ((128, 1), (512, 4), (2048, 16))
SW_HPG = 4
SW_HEADS = 12
SW_HD = 128
SW_ROT = SW_HD // 4
SW_BLK = 128
ROPE_THETA = 500000.0
CM_CHUNK = 128
CM_GROUPS = 8
CM_GD = 256
D_FF = 5632
RMS_EPS = 1e-6
NEG = -1e30

LANE = 128
O_GATES = 6144
O_QB = O_GATES + 2 * ML_HEADS
O_UC = O_QB + 3 * SW_HEADS * SW_HD
N_REST = 2 * CM_GROUPS * CM_GD + 3 * D_MODEL
N_GATE = LANE
C_QA, C_KA, C_VA, C_OA = 0, 1024, 2048, 4096
C_UC, C_VC, C_G = 0, 2048, 4096

TM = 512
TM_WIDE = 1024
QKV_SUB = 256
SWA_UNITS = 4
ML_L = 128
VMEM_LIMIT = 48 * 1024 * 1024
VMEM_LIMIT_BIG = 52 * 1024 * 1024


def _cparams(*sem, vmem=VMEM_LIMIT):
    return pltpu.CompilerParams(dimension_semantics=sem, vmem_limit_bytes=vmem)


def _sigmoid(x):
    return 0.5 * (jnp.tanh(0.5 * x) + 1.0)


def _log_sigmoid(x):
    return jnp.minimum(x, 0.0) - jnp.log1p(jnp.exp(-jnp.abs(x)))


def _ada_kernel(c_ref, w_ref, b_ref, o_ref):
    c = c_ref[...]
    a = (c * _sigmoid(c)).astype(BF)
    o_ref[...] = jnp.dot(a, w_ref[...].astype(BF), preferred_element_type=F32) + b_ref[...]


def _ada(c_all, w_ada, b_ada):
    depth, d, n = w_ada.shape
    rows = c_all.shape[0]
    tn = 1024
    return pl.pallas_call(
        _ada_kernel,
        grid=(depth, n // tn),
        in_specs=[pl.BlockSpec((rows, d), lambda l, j: (0, 0)),
                  pl.BlockSpec((None, d, tn), lambda l, j: (l, 0, j)),
                  pl.BlockSpec((None, 1, tn), lambda l, j: (l, 0, j))],
        out_specs=pl.BlockSpec((None, rows, tn), lambda l, j: (l, 0, j)),
        out_shape=jax.ShapeDtypeStruct((depth, rows, n), F32),
        compiler_params=_cparams("parallel", "parallel"),
        name="ada",
    )(c_all, w_ada, b_ada.reshape(depth, 1, n))


def _mod_spec(mod, chunk, width, tm, rows_per_batch, row_of, col_of):
    per = D_MODEL // width
    if mod.ndim == 3:
        tiles_per_batch = rows_per_batch // tm
        return pl.BlockSpec((None, 1, width),
                            lambda *g: (row_of(*g) // tiles_per_batch, 0, chunk * per + col_of(*g)))
    return pl.BlockSpec((tm, width), lambda *g: (row_of(*g), chunk * per + col_of(*g)))


NORM_ROWS = 16


def _row_groups(n_rows, body):
    def step(i, c):
        body(pl.ds(pl.multiple_of(i * NORM_ROWS, NORM_ROWS), NORM_ROWS))
        return c
    groups = n_rows // NORM_ROWS
    lax.fori_loop(0, groups, step, 0, unroll=min(groups, 4))


def _norm_mod_kernel(x_ref, g_ref, sc_ref, sh_ref, o_ref):
    per_row = sc_ref.shape[0] > 1

    def body(rows):
        x = x_ref[rows, :]
        y = x * lax.rsqrt(jnp.mean(x * x, axis=1, keepdims=True) + RMS_EPS) * g_ref[...]
        sc = sc_ref[rows, :] if per_row else sc_ref[...]
        sh = sh_ref[rows, :] if per_row else sh_ref[...]
        o_ref[rows, :] = (y * (1.0 + sc) + sh).astype(o_ref.dtype)

    _row_groups(x_ref.shape[0], body)


def _norm_mod(x, g, mod, sc_chunk, sh_chunk, rows_per_batch):
    m, d = x.shape
    tm = min(TM, m)
    row_of = lambda i: i
    col_of = lambda i: 0
    return pl.pallas_call(
        _norm_mod_kernel,
        grid=(m // tm,),
        in_specs=[pl.BlockSpec((tm, d), lambda i: (i, 0)),
                  pl.BlockSpec((1, d), lambda i: (0, 0)),
                  _mod_spec(mod, sc_chunk, d, tm, rows_per_batch, row_of, col_of),
                  _mod_spec(mod, sh_chunk, d, tm, rows_per_batch, row_of, col_of)],
        out_specs=pl.BlockSpec((tm, d), lambda i: (i, 0)),
        out_shape=jax.ShapeDtypeStruct((m, d), BF),
        compiler_params=_cparams("parallel"),
        name="norm_mod",
    )(x, g.reshape(1, d), mod, mod)


def _cast_weights_once(pairs):
    @pl.when(pl.program_id(1) == 0)
    def _():
        for src, dst in pairs:
            dst[...] = src[...].astype(BF)


_NT =(((1,), (1,)), ((), ()))


def _wt_spec(k, l, tn, start_of):
    return pl.BlockSpec((pl.Element(1), pl.Element(tn), pl.Element(k)),
                        lambda j, i: (l, pl.multiple_of(start_of(j), 8), 0))


def _with_rider(x_ref, xs_ref, compute, store):
    tm = x_ref.shape[0]

    @pl.when(pl.program_id(1) == 0)
    def _():
        res = compute(jnp.concatenate([x_ref[...], xs_ref[...]], axis=0))
        store(tuple(r[:tm] for r in res), tuple(r[tm:] for r in res))

    @pl.when(pl.program_id(1) != 0)
    def _():
        store(compute(x_ref[...]), None)


def _mm_t_kernel(x_ref, xs_ref, w_ref, o_ref, os_ref, wbf):
    _cast_weights_once([(w_ref.at[0], wbf)])

    def compute(rows):
        return (lax.dot_general(rows, wbf[...], _NT, preferred_element_type=F32),)

    def store(main, rider):
        o_ref[...] = main[0].astype(o_ref.dtype)
        if rider is not None:
            os_ref[...] = rider[0].astype(os_ref.dtype)

    _with_rider(x_ref, xs_ref, compute, store)


def _mm_t(x, xs, wt, l, tn, col0, n_tiles, out_dtype, name):
    m, k = x.shape
    ms = xs.shape[0]
    tm = min(TM_WIDE, m)
    return pl.pallas_call(
        _mm_t_kernel,
        grid=(n_tiles, m // tm),
        in_specs=[pl.BlockSpec((tm, k), lambda j, i: (i, 0)),
                  pl.BlockSpec((ms, k), lambda j, i: (0, 0)),
                  _wt_spec(k, l, tn, lambda j: col0 + j * tn)],
        out_specs=[pl.BlockSpec((tm, tn), lambda j, i: (i, j)),
                   pl.BlockSpec((ms, tn), lambda j, i: (0, j))],
        out_shape=[jax.ShapeDtypeStruct((m, n_tiles * tn), out_dtype),
                   jax.ShapeDtypeStruct((ms, n_tiles * tn), F32)],
        scratch_shapes=[pltpu.VMEM((tn, k), BF)],
        compiler_params=_cparams("arbitrary", "arbitrary"),
        name=name,
    )(x, xs, wt)


def _qkv_kernel(x_ref, w_ref, tab_ref, o_ref, wbf, acc_ref, *, dil):
    _cast_weights_once([(w_ref.at[0], wbf)])
    heads = [slice(hh * SW_HD, (hh + 1) * SW_HD) for hh in range(SW_HPG)]
    tm = x_ref.shape[0]
    sub = min(tm, QKV_SUB)

    def tile(rotary):
        for s in range(tm // sub):
            rows_in = slice(s * sub, (s + 1) * sub)
            acc = lax.dot_general(x_ref[rows_in, :], wbf[...], _NT, preferred_element_type=F32)
            tab = tab_ref[rows_in, :]
            vals = [_rope(acc[:, sl], tab) if rotary else acc[:, sl] for sl in heads]
            if dil == 1:
                for sl, val in zip(heads, vals):
                    o_ref[0, rows_in, sl] = val.astype(o_ref.dtype)
                continue
            for hh, val in enumerate(vals):
                acc_ref[hh, rows_in, :] = val
            n = sub // dil
            for r in range(dil):
                for hh, sl in enumerate(heads):
                    o_ref[r, s * n:(s + 1) * n, sl] = acc_ref[hh, pl.ds(s * sub + r, n, stride=dil), :].astype(
                        o_ref.dtype)

    is_v = pl.program_id(0) == 2
    pl.when(jnp.logical_not(is_v))(lambda: tile(True))
    pl.when(is_v)(lambda: tile(False))


def _qkv_proj(x, wt, l, gi, dil, tabs, batch, out_dtype, name):
    m, k = x.shape
    seq = m // batch
    tm = min(TM_WIDE, seq)
    tpb = seq // tm
    hw = SW_HPG * SW_HD
    wide = len(SW_GROUPS) * hw
    return pl.pallas_call(
        functools.partial(_qkv_kernel, dil=dil),
        grid=(3, m // tm),
        in_specs=[pl.BlockSpec((tm, k), lambda j, i: (i, 0)),
                  _wt_spec(k, l, hw, lambda j: O_QB + j * wide + gi * hw),
                  pl.BlockSpec((None, tm, 2 * SW_HD), lambda j, i: (jnp.minimum(j, 1), i % tpb, 0))],
        out_specs=pl.BlockSpec((None, dil, tm // dil, hw), lambda j, i: (i // tpb, 0, i % tpb, j)),
        out_shape=jax.ShapeDtypeStruct((batch, dil, seq // dil, 3 * hw), out_dtype),
        scratch_shapes=[pltpu.VMEM((hw, k), BF), pltpu.VMEM((SW_HPG, tm, SW_HD), F32)],
        compiler_params=_cparams("arbitrary", "arbitrary"),
        name=name,
    )(x, wt, tabs)


def _cast_kernel(w_ref, o_ref):
    o_ref[...] = w_ref[...].astype(o_ref.dtype)


def _cast_bf16(w3, l):
    _, k, n = w3.shape
    rows = 512
    return pl.pallas_call(
        _cast_kernel,
        grid=(k // rows,),
        in_specs=[pl.BlockSpec((None, rows, n), lambda i: (l, i, 0))],
        out_specs=pl.BlockSpec((rows, n), lambda i: (i, 0)),
        out_shape=jax.ShapeDtypeStruct((k, n), BF),
        compiler_params=_cparams("parallel"),
        name="cast_bf16",
    )(w3)


def _mm_res_norm_kernel(*refs, cast, final, ride):
    refs = list(refs)
    take = lambda n: [refs.pop(0) for _ in range(n)]
    x_ref, w_ref, r_ref, g_ref, gn_ref = take(5)
    sc_ref, sh_ref = (None, None) if final else take(2)
    if ride:
        xs_ref, rs_ref, gs_ref = take(3)
        scs_ref, shs_ref = (None, None) if final else take(2)
    if final:
        (y_ref,) = take(1)
    else:
        o_ref, y_ref = take(2)
    if ride:
        if not final:
            (os_ref,) = take(1)
        (ys_ref,) = take(1)
    scratch = refs
    if final:
        o_ref = scratch[-1]
    if cast:
        wbf = scratch[0]

        @pl.when(pl.program_id(0) == 0)
        def _():
            wbf[...] = w_ref[...].astype(BF)
        w = wbf[...]
    else:
        w = w_ref[...]
    tm = x_ref.shape[0]

    def norm(x, sc, sh):
        y = x * lax.rsqrt(jnp.mean(x * x, axis=1, keepdims=True) + RMS_EPS) * gn_ref[...]
        return y if final else y * (1.0 + sc) + sh

    def plain():
        o_ref[...] = r_ref[...] + g_ref[...] * jnp.dot(x_ref[...], w, preferred_element_type=F32)

    if ride:
        @pl.when(pl.program_id(0) == 0)
        def _():
            acc = jnp.dot(jnp.concatenate([x_ref[...], xs_ref[...]], axis=0), w, preferred_element_type=F32)
            o_ref[...] = r_ref[...] + g_ref[...] * acc[:tm]
            rows_s = rs_ref[...] + gs_ref[...] * acc[tm:]
            if final:
                ys_ref[...] = norm(rows_s, None, None)
            else:
                os_ref[...] = rows_s
                ys_ref[...] = norm(rows_s, scs_ref[...], shs_ref[...]).astype(ys_ref.dtype)

        pl.when(pl.program_id(0) != 0)(plain)
    else:
        plain()
    per_row = (not final) and sc_ref.shape[0] > 1

    def body(rows):
        sc = sh = None
        if not final:
            sc = sc_ref[rows, :] if per_row else sc_ref[...]
            sh = sh_ref[rows, :] if per_row else sh_ref[...]
        y_ref[rows, :] = norm(o_ref[rows, :], sc, sh).astype(y_ref.dtype)

    _row_groups(tm, body)


def _mm_res_norm(xs, w, l, ress, mods, gate_chunk, rows_per_batch, gn, mods_next, sc_chunk, sh_chunk, name, ride):
    nxt = (None, None) if mods_next is None else mods_next
    groups = [(xs[g], ress[g], mods[g], nxt[g]) for g in range(2)]
    call = functools.partial(_mm_res_norm_call, w=w, l=l, gate_chunk=gate_chunk, gn=gn, sc_chunk=sc_chunk,
                             sh_chunk=sh_chunk, name=name)
    if ride:
        return call(groups[0], groups[1], rows_per_batch)
    return call(groups[0], None, rows_per_batch), call(groups[1], None, None)


def _mm_res_norm_call(main, rider, rows_per_batch, *, w, l, gate_chunk, gn, sc_chunk, sh_chunk, name):
    x, res, mod, mod_next = main
    m, k = x.shape
    n = D_MODEL
    tm = min(256, m)
    cast = w.ndim == 3
    final = mod_next is None
    ride = rider is not None
    if ride:
        x_s, res_s, mod_s, modn_s = rider
        ms = x_s.shape[0]
    mods_next = (mod_next, modn_s if ride else None)
    once = dict(pipeline_mode=pl.Buffered(1))
    w_spec = (pl.BlockSpec((None, k, n), lambda i: (l, 0, 0), **once) if cast
              else pl.BlockSpec((k, n), lambda i: (0, 0), **once))
    row_of, col_of = (lambda i: i), (lambda i: 0)
    tile = pl.BlockSpec((tm, n), lambda i: (i, 0))
    in_specs = [pl.BlockSpec((tm, k), lambda i: (i, 0)), w_spec, tile,
                _mod_spec(mod, gate_chunk, n, tm, rows_per_batch, row_of, col_of),
                pl.BlockSpec((1, n), lambda i: (0, 0))]
    args = [x, w, res, mod, gn.reshape(1, n)]
    if not final:
        in_specs += [_mod_spec(mods_next[0], sc_chunk, n, tm, rows_per_batch, row_of, col_of),
                     _mod_spec(mods_next[0], sh_chunk, n, tm, rows_per_batch, row_of, col_of)]
        args += [mods_next[0], mods_next[0]]
    out_specs = [tile] if final else [tile, tile]
    out_shape = ([jax.ShapeDtypeStruct((m, n), F32)] if final
                 else [jax.ShapeDtypeStruct((m, n), F32), jax.ShapeDtypeStruct((m, n), BF)])
    if ride:
        whole = pl.BlockSpec((ms, n), lambda i: (0, 0))
        chunk = lambda c: pl.BlockSpec((ms, n), lambda i: (0, c))
        in_specs += [pl.BlockSpec((ms, k), lambda i: (0, 0)), whole, chunk(gate_chunk)]
        args += [x_s, res_s, mod_s]
        if final:
            out_specs += [whole]
            out_shape += [jax.ShapeDtypeStruct((ms, n), F32)]
        else:
            in_specs += [chunk(sc_chunk), chunk(sh_chunk)]
            args += [mods_next[1], mods_next[1]]
            out_specs += [whole, whole]
            out_shape += [jax.ShapeDtypeStruct((ms, n), F32), jax.ShapeDtypeStruct((ms, n), BF)]
    outs = pl.pallas_call(
        functools.partial(_mm_res_norm_kernel, cast=cast, final=final, ride=ride),
        grid=(m // tm,),
        in_specs=in_specs,
        out_specs=out_specs,
        out_shape=out_shape,
        scratch_shapes=([pltpu.VMEM((k, n), BF)] if cast else []) + ([pltpu.VMEM((tm, n), F32)] if final else []),
        compiler_params=_cparams("arbitrary", vmem=VMEM_LIMIT_BIG),
        name=name,
    )(*args)
    if not ride:
        return outs[0] if final else (outs[0], outs[1])
    return tuple(outs) if final else ((outs[0], outs[1]), (outs[2], outs[3]))


def _ffn1_kernel(x_ref, xs_ref, wg_ref, wu_ref, o_ref, os_ref, wgbf, wubf):
    _cast_weights_once([(wg_ref, wgbf), (wu_ref, wubf)])

    def compute(rows):
        g = jnp.dot(rows, wgbf[...], preferred_element_type=F32)
        u = jnp.dot(rows, wubf[...], preferred_element_type=F32)
        return (g * _sigmoid(g) * u,)

    def store(main, rider):
        o_ref[...] = main[0].astype(o_ref.dtype)
        if rider is not None:
            os_ref[...] = rider[0].astype(os_ref.dtype)

    _with_rider(x_ref, xs_ref, compute, store)


def _ffn1(x, xs, w_gu, l):
    m, k = x.shape
    ms = xs.shape[0]
    tn = 512
    nj = D_FF // tn
    tm = min(TM_WIDE, m)
    return pl.pallas_call(
        _ffn1_kernel,
        grid=(nj, m // tm),
        in_specs=[pl.BlockSpec((tm, k), lambda j, i: (i, 0)),
                  pl.BlockSpec((ms, k), lambda j, i: (0, 0)),
                  pl.BlockSpec((None, k, tn), lambda j, i: (l, 0, j)),
                  pl.BlockSpec((None, k, tn), lambda j, i: (l, 0, nj + j))],
        out_specs=[pl.BlockSpec((tm, tn), lambda j, i: (i, j)),
                   pl.BlockSpec((ms, tn), lambda j, i: (0, j))],
        out_shape=[jax.ShapeDtypeStruct((m, D_FF), BF), jax.ShapeDtypeStruct((ms, D_FF), BF)],
        scratch_shapes=[pltpu.VMEM((k, tn), BF), pltpu.VMEM((k, tn), BF)],
        compiler_params=_cparams("arbitrary", "arbitrary"),
        name="ffn_gate_up",
    )(x, xs, w_gu, w_gu)


def _merge_kernel(xa_ref, xb_ref, xc_ref, g0_ref, g1_ref, g2_ref, sa_ref, sb_ref, sc_ref, t0_ref, t1_ref, t2_ref,
                  wa_ref, wb_ref, wc_ref, o_ref, os_ref, wabf, wbbf, wcbf):
    _cast_weights_once([(wa_ref, wabf), (wb_ref, wbbf), (wc_ref, wcbf)])
    tm = xa_ref.shape[0]

    def dots(xa, xb, xc):
        return (jnp.dot(xa, wabf[...], preferred_element_type=F32),
                jnp.dot(xb, wbbf[...], preferred_element_type=F32),
                jnp.dot(xc, wcbf[...], preferred_element_type=F32))

    def gated(d, gates):
        return sum(_sigmoid(g[...].astype(F32)) * x for g, x in zip(gates, d))

    @pl.when(pl.program_id(1) == 0)
    def _():
        cat = lambda a, b: jnp.concatenate([a[...].astype(BF), b[...].astype(BF)], axis=0)
        d = dots(cat(xa_ref, sa_ref), cat(xb_ref, sb_ref), cat(xc_ref, sc_ref))
        o_ref[...] = gated([x[:tm] for x in d], (g0_ref, g1_ref, g2_ref)).astype(o_ref.dtype)
        os_ref[...] = gated([x[tm:] for x in d], (t0_ref, t1_ref, t2_ref)).astype(os_ref.dtype)

    @pl.when(pl.program_id(1) != 0)
    def _():
        d = dots(xa_ref[...].astype(BF), xb_ref[...].astype(BF), xc_ref[...].astype(BF))
        o_ref[...] = gated(d, (g0_ref, g1_ref, g2_ref)).astype(o_ref.dtype)


def _merge(prompt, sample, w_a, w_b, w_c, l):
    xa, xb, xc, p = prompt
    sa, sb, sc, ps = sample
    ms = sa.shape[0]
    m = xa.shape[0]
    tn = 1024
    tm = min(TM, m)
    ka, kb, kc = xa.shape[1], xb.shape[1], xc.shape[1]
    g_blk = C_G // tn
    per = D_MODEL // tn
    once = dict(pipeline_mode=pl.Buffered(1))
    return pl.pallas_call(
        _merge_kernel,
        grid=(D_MODEL // tn, m // tm),
        in_specs=[pl.BlockSpec((tm, ka), lambda j, i: (i, 0)),
                  pl.BlockSpec((tm, kb), lambda j, i: (i, 0)),
                  pl.BlockSpec((tm, kc), lambda j, i: (i, 0)),
                  pl.BlockSpec((tm, tn), lambda j, i: (i, g_blk + j)),
                  pl.BlockSpec((tm, tn), lambda j, i: (i, g_blk + per + j)),
                  pl.BlockSpec((tm, tn), lambda j, i: (i, g_blk + 2 * per + j)),
                  pl.BlockSpec((ms, ka), lambda j, i: (0, 0)),
                  pl.BlockSpec((ms, kb), lambda j, i: (0, 0)),
                  pl.BlockSpec((ms, kc), lambda j, i: (0, 0)),
                  pl.BlockSpec((ms, tn), lambda j, i: (0, g_blk + j)),
                  pl.BlockSpec((ms, tn), lambda j, i: (0, g_blk + per + j)),
                  pl.BlockSpec((ms, tn), lambda j, i: (0, g_blk + 2 * per + j)),
                  pl.BlockSpec((None, ka, tn), lambda j, i: (l, 0, j), **once),
                  pl.BlockSpec((None, kb, tn), lambda j, i: (l, 0, j), **once),
                  pl.BlockSpec((None, kc, tn), lambda j, i: (l, 0, j), **once)],
        out_specs=[pl.BlockSpec((tm, tn), lambda j, i: (i, j)),
                   pl.BlockSpec((ms, tn), lambda j, i: (0, j))],
        out_shape=[jax.ShapeDtypeStruct((m, D_MODEL), BF), jax.ShapeDtypeStruct((ms, D_MODEL), BF)],
        scratch_shapes=[pltpu.VMEM((ka, tn), BF), pltpu.VMEM((kb, tn), BF), pltpu.VMEM((kc, tn), BF)],
        compiler_params=_cparams("arbitrary", "arbitrary", vmem=VMEM_LIMIT_BIG),
        name="branch_merge",
    )(xa, xb, xc, p, p, p, sa, sb, sc, ps, ps, ps, w_a, w_b, w_c)


def _scan(x, axis, op, fill):
    idx = lax.broadcasted_iota(jnp.int32, x.shape, axis)
    shift = 1
    while shift < x.shape[axis]:
        x = op(x, jnp.where(idx >= shift, pltpu.roll(x, shift, axis), fill))
        shift *= 2
    return x


def _mlstm_kernel(q_ref, k_ref, v_ref, o_ref, gc_ref, gr_ref, brow_ref, bcol_ref,
                  xa_ref, c_ref, n_ref, m_ref, nn_ref):
    chunk = pl.program_id(1)

    @pl.when(chunk == 0)
    def _():
        c_ref[...] = jnp.zeros_like(c_ref)
        m_ref[...] = jnp.zeros_like(m_ref)
        nn_ref[...] = jnp.zeros_like(nn_ref)

    L = q_ref.shape[0]
    nh = ML_HEADS
    scale = ML_DQK ** -0.5
    gc = gc_ref[...] + brow_ref[...]
    gr = gr_ref[...] + bcol_ref[...]
    f_c = pltpu.roll(_scan(_log_sigmoid(gc), 0, jnp.add, 0.0), LANE - nh, 1)
    m_prev = m_ref[...]
    m_t = f_c + jnp.maximum(m_prev, _scan(gc - f_c, 0, jnp.maximum, NEG))
    w_inter = jnp.exp(f_c + m_prev - m_t)
    e_neg_m = jnp.exp(-m_t)
    f_minus_m = f_c - m_t
    m_new = m_t[L - 1:L, :]
    f_last = f_c[L - 1:L, :]
    w_last = jnp.exp(f_last - f_c + gc - m_new) * scale
    decay = jnp.exp(f_last + m_prev - m_new)
    m_ref[...] = m_new
    f_r = _scan(_log_sigmoid(gr), 1, jnp.add, 0.0)
    a_r = gr[0:nh, :] - f_r[nh:2 * nh, :]
    row = lax.broadcasted_iota(jnp.int32, (L, L), 0)
    col = lax.broadcasted_iota(jnp.int32, (L, L), 1)
    tri = col <= row
    ones = jnp.ones((L, ML_DQK), BF)
    nt = (((1,), (1,)), ((), ()))
    tn = (((0,), (0,)), ((), ()))
    for h in range(nh):
        lane = lambda x: x[:, h:h + 1]
        d_w = jnp.exp(jnp.where(tri, lane(f_minus_m) + a_r[h:h + 1, :], NEG))
        q = q_ref[:, h * ML_DQK:(h + 1) * ML_DQK]
        k = k_ref[:, h * ML_DQK:(h + 1) * ML_DQK]
        v = v_ref[:, h * ML_DV:(h + 1) * ML_DV]
        s = (lax.dot_general(q, k, nt, preferred_element_type=F32) * (scale * d_w)).astype(BF)
        c_old = c_ref[h]
        nn_old = nn_ref[h]
        wi = lane(w_inter)
        num = (jnp.dot(s, v, preferred_element_type=F32)
               + wi * jnp.dot(q, c_old.astype(BF), preferred_element_type=F32))
        den = (jnp.dot(s, ones, preferred_element_type=F32)
               + wi * jnp.dot(q, nn_old.astype(BF), preferred_element_type=F32))
        inv = 1.0 / jnp.maximum(jnp.abs(den[:, 0:1]), lane(e_neg_m))
        og = o_ref[:, h * ML_DV:(h + 1) * ML_DV].astype(F32)
        xa_ref[:, h * ML_DV:(h + 1) * ML_DV] = (_sigmoid(og) * (num * inv)).astype(xa_ref.dtype)
        kw = (k.astype(F32) * lane(w_last)).astype(BF)
        dk = decay[:, h:h + 1]
        c_ref[h] = dk * c_old + lax.dot_general(kw, v, tn, preferred_element_type=F32)
        nn_ref[h] = dk * nn_old + lax.dot_general(kw, ones, tn, preferred_element_type=F32)

    @pl.when(chunk == pl.num_programs(1) - 1)
    def _():
        for h in range(nh):
            n_ref[h:h + 1, :] = nn_ref[h].T[0:1, :]


def _mlstm_prompt(p, gates, b_i, b_f, batch, seq):
    m_rows = p.shape[0]
    L = ML_L
    nc = seq // L
    bias = jnp.concatenate([b_i, b_f]).astype(F32)
    bias_row = jnp.zeros((1, N_GATE), F32).at[0, :2 * ML_HEADS].set(bias)
    bias_col = bias.reshape(2 * ML_HEADS, 1)
    gates_t = gates[:, :2 * ML_HEADS].T
    hq, hv = ML_HEADS * ML_DQK, ML_HEADS * ML_DV
    return pl.pallas_call(
        _mlstm_kernel,
        grid=(batch, nc),
        in_specs=[pl.BlockSpec((L, hq), lambda b, c: (b * nc + c, C_QA // hq)),
                  pl.BlockSpec((L, hq), lambda b, c: (b * nc + c, C_KA // hq)),
                  pl.BlockSpec((L, hv), lambda b, c: (b * nc + c, C_VA // hv)),
                  pl.BlockSpec((L, hv), lambda b, c: (b * nc + c, C_OA // hv)),
                  pl.BlockSpec((L, N_GATE), lambda b, c: (b * nc + c, 0)),
                  pl.BlockSpec((2 * ML_HEADS, L), lambda b, c: (0, b * nc + c)),
                  pl.BlockSpec((1, N_GATE), lambda b, c: (0, 0)),
                  pl.BlockSpec((2 * ML_HEADS, 1), lambda b, c: (0, 0))],
        out_specs=[pl.BlockSpec((L, hv), lambda b, c: (b * nc + c, 0)),
                   pl.BlockSpec((None, ML_HEADS, ML_DQK, ML_DV), lambda b, c: (b, 0, 0, 0)),
                   pl.BlockSpec((None, ML_HEADS, ML_DQK), lambda b, c: (b, 0, 0)),
                   pl.BlockSpec((None, 1, LANE), lambda b, c: (b, 0, 0))],
        out_shape=[jax.ShapeDtypeStruct((m_rows, hv), BF),
                   jax.ShapeDtypeStruct((batch, ML_HEADS, ML_DQK, ML_DV), F32),
                   jax.ShapeDtypeStruct((batch, ML_HEADS, ML_DQK), F32),
                   jax.ShapeDtypeStruct((batch, 1, LANE), F32)],
        scratch_shapes=[pltpu.VMEM((ML_HEADS, ML_DQK, ML_DQK), F32)],
        compiler_params=_cparams("parallel", "arbitrary"),
        name="mlstm_prompt",
    )(p, p, p, p, gates, gates_t, bias_row, bias_col)


def _rope_table(pos):
    half = SW_ROT // 2
    freq = ROPE_THETA ** (-jnp.arange(half, dtype=F32) * 2.0 / SW_ROT)
    ang = pos.astype(F32)[:, None] * freq[None, :]
    cos, sin = jnp.cos(ang), jnp.sin(ang)
    t = pos.shape[0]
    return jnp.concatenate([cos, cos, jnp.ones((t, SW_HD - SW_ROT), F32),
                            -sin, sin, jnp.zeros((t, SW_HD - SW_ROT), F32)], axis=1)


def _rope(x, tab):
    half = SW_ROT // 2
    lane = lax.broadcasted_iota(jnp.int32, x.shape, 1)
    partner = jnp.where(lane < SW_ROT, lane ^ half, lane)
    return x * tab[:, 0:SW_HD] + jnp.take_along_axis(x, partner, axis=1) * tab[:, SW_HD:2 * SW_HD]


def _attend(q, k_own, v_own, k_prev, v_prev, prev_shift):
    blk = SW_BLK
    row = lax.broadcasted_iota(jnp.int32, (1, blk, blk), 1)
    col = lax.broadcasted_iota(jnp.int32, (1, blk, blk), 2)
    qk = (((2,), (2,)), ((0,), (0,)))
    pv = (((2,), (1,)), ((0,), (0,)))
    s_o = jnp.where(col <= row, lax.dot_general(q, k_own, qk, preferred_element_type=F32), NEG)
    s_p = jnp.where(col >= row + prev_shift, lax.dot_general(q, k_prev, qk, preferred_element_type=F32), NEG)
    mx = jnp.maximum(jnp.max(s_o, axis=2, keepdims=True), jnp.max(s_p, axis=2, keepdims=True))
    p_o = jnp.exp(s_o - mx)
    p_p = jnp.exp(s_p - mx)
    den = jnp.sum(p_o, axis=2, keepdims=True) + jnp.sum(p_p, axis=2, keepdims=True)
    acc = (lax.dot_general(p_o.astype(BF), v_own, pv, preferred_element_type=F32)
           + lax.dot_general(p_p.astype(BF), v_prev, pv, preferred_element_type=F32))
    return acc, mx, den


def _swa_kernel(q1, k1, v1, k1p, v1p, q2, k2, v2, k2p, v2p, q3, k3, v3, k3p, v3p, xb_ref, o_acc, m_acc, l_acc):
    blk = SW_BLK
    u = SWA_UNITS
    first_shift = jnp.where(pl.program_id(1) > 0, 0, blk)
    wide = lambda x: jnp.broadcast_to(x, x.shape[:-1] + (SW_HD,))

    def merge(rows, acc, mx, den):
        m_old = m_acc[rows, :]
        m_new = jnp.maximum(m_old, mx)
        a = jnp.exp(m_old - m_new)
        b = jnp.exp(mx - m_new)
        o_acc[rows, :] = a * o_acc[rows, :] + b * acc
        l_acc[rows, :] = a * l_acc[rows, :] + b * den
        m_acc[rows, :] = m_new

    def g1_batch(rows, k_p, v_p, shift):
        split = lambda x: x.reshape(u, blk, SW_HD)
        acc, mx, den = _attend(split(q1[rows, :]), split(k1[rows, :]), split(v1[rows, :]),
                               split(k_p), split(v_p), shift)
        o_acc[rows, :] = acc.reshape(u * blk, SW_HD)
        m_acc[rows, :] = wide(mx).reshape(u * blk, SW_HD)
        l_acc[rows, :] = wide(den).reshape(u * blk, SW_HD)

    head = pl.ds(0, (u - 1) * blk)
    unit = lax.broadcasted_iota(jnp.int32, (u, 1, 1), 0)
    g1_batch(pl.ds(0, u * blk),
             jnp.concatenate([k1p[...], k1[head, :]], axis=0), jnp.concatenate([v1p[...], v1[head, :]], axis=0),
             jnp.where(unit == 0, first_shift, 0))

    def g1_body(i, c):
        rows = pl.ds(pl.multiple_of(i * u * blk, blk), u * blk)
        prev = pl.ds(pl.multiple_of(i * u * blk - blk, blk), u * blk)
        g1_batch(rows, k1[prev, :], v1[prev, :], 0)
        return c

    lax.fori_loop(1, q1.shape[0] // (u * blk), g1_body, 0)

    def dilated(q, k, v, kp, vp):
        dil = q.shape[0]
        n_sub = q.shape[1] // blk

        def batch(sub, r0, k_p, v_p, shift):
            rs = pl.ds(r0, u)
            rows = pl.ds(pl.multiple_of(sub * blk, blk), blk)
            acc, mx, den = _attend(q[rs, rows, :], k[rs, rows, :], v[rs, rows, :], k_p, v_p, shift)
            for i in range(u):
                merge(pl.ds(sub * blk * dil + r0 + i, blk, stride=dil), acc[i], wide(mx[i]), wide(den[i]))

        def first(i, c):
            rs = pl.ds(i * u, u)
            batch(0, i * u, kp[rs], vp[rs], first_shift)
            return c

        lax.fori_loop(0, dil // u, first, 0)

        def rest(i, c):
            sub = 1 + i // (dil // u)
            r0 = (i % (dil // u)) * u
            prev = pl.ds(pl.multiple_of((sub - 1) * blk, blk), blk)
            batch(sub, r0, k[pl.ds(r0, u), prev, :], v[pl.ds(r0, u), prev, :], 0)
            return c

        lax.fori_loop(0, (n_sub - 1) * (dil // u), rest, 0)

    dilated(q2, k2, v2, k2p, v2p)
    dilated(q3, k3, v3, k3p, v3p)
    xb_ref[...] = (o_acc[...] * (1.0 / l_acc[...])).astype(xb_ref.dtype)


def _swa_prompt(qkvs, batch, seq):
    blk = SW_BLK
    span = SW_GROUPS[-1][0]
    assert seq % span == 0 and all(w // d == blk for w, d in SW_GROUPS)
    nspan = seq // span
    in_specs = []
    args = []
    for (win, dil), qkv in zip(SW_GROUPS, qkvs):
        rows = span // dil
        per = rows // blk
        own = lambda c: pl.BlockSpec((None, dil, rows, SW_HD), lambda b, s, h, c=c: (b, 0, s, c * SW_HPG + h))
        prev = lambda c: pl.BlockSpec(
            (None, dil, blk, SW_HD), lambda b, s, h, c=c, per=per: (b, 0, jnp.maximum(s * per - 1, 0), c * SW_HPG + h))
        if dil == 1:
            own = lambda c: pl.BlockSpec((None, None, span, SW_HD), lambda b, s, h, c=c: (b, 0, s, c * SW_HPG + h))
            prev = lambda c: pl.BlockSpec(
                (None, None, blk, SW_HD),
                lambda b, s, h, c=c, per=per: (b, 0, jnp.maximum(s * per - 1, 0), c * SW_HPG + h))
        in_specs += [own(0), own(1), own(2), prev(1), prev(2)]
        args += [qkv] * 5
    return pl.pallas_call(
        _swa_kernel,
        grid=(batch, nspan, SW_HPG),
        in_specs=in_specs,
        out_specs=pl.BlockSpec((span, SW_HD), lambda b, s, h: (b * nspan + s, h)),
        out_shape=jax.ShapeDtypeStruct((batch * seq, SW_HPG * SW_HD), BF),
        scratch_shapes=[pltpu.VMEM((span, SW_HD), F32)] * 3,
        compiler_params=_cparams("parallel", "parallel", "parallel"),
        name="swa_prompt",
    )(*args)


def _kv_rows_kernel(k_ref, v_ref, o_ref, *, dil):
    for r in range(dil):
        rows = pl.ds(r, SW_BLK, stride=dil) if dil > 1 else pl.ds(0, SW_BLK)
        for hh in range(SW_HPG):
            sl = slice(hh * SW_HD, (hh + 1) * SW_HD)
            o_ref[hh, rows, :] = k_ref[r, :, sl].astype(F32)
            o_ref[SW_HPG + hh, rows, :] = v_ref[r, :, sl].astype(F32)


def _kv_rows(qkv, gi, batch, seq):
    win, dil = SW_GROUPS[gi]
    hw = SW_HPG * SW_HD
    last = seq // dil // SW_BLK - 1
    out = pl.pallas_call(
        functools.partial(_kv_rows_kernel, dil=dil),
        grid=(batch,),
        in_specs=[pl.BlockSpec((None, dil, SW_BLK, hw), lambda b: (b, 0, last, 1)),
                  pl.BlockSpec((None, dil, SW_BLK, hw), lambda b: (b, 0, last, 2))],
        out_specs=pl.BlockSpec((None, 2 * SW_HPG, win, SW_HD), lambda b: (b, 0, 0, 0)),
        out_shape=jax.ShapeDtypeStruct((batch, 2 * SW_HPG, win, SW_HD), F32),
        compiler_params=_cparams("parallel"),
        name=f"kv_rows_w{win}",
    )(qkv, qkv)
    return out.reshape(batch, 2, SW_HPG, win, SW_HD).transpose(0, 3, 1, 2, 4)


def _cm_kernel(u_ref, v_ref, gv_ref, ws_ref, bs_ref, o_ref):
    ch = CM_CHUNK
    row = lax.broadcasted_iota(jnp.int32, (ch, ch), 0)
    col = lax.broadcasted_iota(jnp.int32, (ch, ch), 1)
    tri = col <= row
    for g in range(CM_GROUPS):
        sl = slice(g * CM_GD, (g + 1) * CM_GD)
        w = jnp.where(tri, ws_ref[g], 0.0).astype(BF)
        for c in range(u_ref.shape[0] // ch):
            rows = slice(c * ch, (c + 1) * ch)
            vg = v_ref[rows, sl].astype(F32)
            vn = vg * lax.rsqrt(jnp.mean(vg * vg, axis=1, keepdims=True) + RMS_EPS) * gv_ref[g:g + 1, :]
            mixed = jnp.dot(w, vn.astype(BF), preferred_element_type=F32) + bs_ref[:, g:g + 1]
            o_ref[rows, sl] = (u_ref[rows, sl].astype(F32) * mixed).astype(o_ref.dtype)


def _cm_prompt(p, g_v, w_s, b_s, l):
    m_rows = p.shape[0]
    ch = CM_CHUNK
    width = CM_GROUPS * CM_GD
    rows = 2 * ch
    return pl.pallas_call(
        _cm_kernel,
        grid=(m_rows // rows,),
        in_specs=[pl.BlockSpec((rows, width), lambda i: (i, C_UC // width)),
                  pl.BlockSpec((rows, width), lambda i: (i, C_VC // width)),
                  pl.BlockSpec((None, CM_GROUPS, CM_GD), lambda i: (l, 0, 0)),
                  pl.BlockSpec((None, CM_GROUPS, ch, ch), lambda i: (l, 0, 0, 0)),
                  pl.BlockSpec((ch, CM_GROUPS), lambda i: (0, 0))],
        out_specs=pl.BlockSpec((rows, width), lambda i: (i, 0)),
        out_shape=jax.ShapeDtypeStruct((m_rows, width), BF),
        compiler_params=_cparams("parallel"),
        name="cm_prompt",
    )(p, p, g_v, w_s, b_s[l].T)


def _mlstm_step_kernel(qr_ref, kr_ref, vr_ref, or_ref, qkc_ref, g_ref, brow_ref, m0_ref, c0_ref, n0_ref,
                       xa_ref, c1_ref, n1_ref, m1_ref):
    scale = ML_DQK ** -0.5
    gates = g_ref[...] + brow_ref[...]
    m0 = m0_ref[...]
    for h in range(ML_HEADS):
        ig = gates[:, h:h + 1]
        lf = _log_sigmoid(gates[:, ML_HEADS + h:ML_HEADS + h + 1])
        m_prev = m0[:, h:h + 1]
        inter = lf + m_prev
        m_t = jnp.maximum(inter, ig)
        d_w = jnp.exp(ig - m_t)
        w_inter = jnp.exp(inter - m_t)
        q_row = qr_ref[h:h + 1, :]
        k_row = kr_ref[h:h + 1, :]
        v_row = vr_ref[h:h + 1, :]
        q_col = qkc_ref[:, h:h + 1]
        k_col = qkc_ref[:, ML_HEADS + h:ML_HEADS + h + 1]
        c_old = c0_ref[h]
        n_old = n0_ref[h:h + 1, :]
        s = jnp.sum(q_row * k_row, axis=1, keepdims=True) * scale * d_w
        qc = jnp.sum(q_col * c_old, axis=0, keepdims=True)
        num = s * v_row + w_inter * qc
        den = s + w_inter * jnp.sum(q_row * n_old, axis=1, keepdims=True)
        hh = num / jnp.maximum(jnp.abs(den), jnp.exp(-m_t))
        xa_ref[:, h * ML_DV:(h + 1) * ML_DV] = _sigmoid(or_ref[h:h + 1, :]) * hh
        w_last = d_w * scale
        decay = w_inter
        c1_ref[h] = decay * c_old + (k_col * w_last) * v_row
        n1_ref[h:h + 1, :] = decay * n_old + w_last * k_row
        m1_ref[h:h + 1, :] = jnp.broadcast_to(m_t, (1, ML_DQK))


def _mlstm_sample(ps, gates, b_i, b_f, c0, n0, m0, l):
    nb = c0.shape[1]
    q = ps[:nb, C_QA:C_QA + 1024].reshape(nb, ML_HEADS, ML_DQK)
    k = ps[:nb, C_KA:C_KA + 1024].reshape(nb, ML_HEADS, ML_DQK)
    v = ps[:nb, C_VA:C_VA + 2048].reshape(nb, ML_HEADS, ML_DV)
    o = ps[:nb, C_OA:C_OA + 2048].reshape(nb, ML_HEADS, ML_DV)
    qk_col = jnp.concatenate([q, k], axis=1).transpose(0, 2, 1)
    bias = jnp.concatenate([b_i, b_f]).astype(F32)
    bias_row = jnp.zeros((1, N_GATE), F32).at[0, :2 * ML_HEADS].set(bias)
    g3 = gates[:nb].reshape(nb, 1, N_GATE)
    m03 = m0[l].reshape(nb, 1, ML_HEADS)
    per_b3 = lambda shape: pl.BlockSpec((None,) + shape, lambda b: (b, 0, 0))
    return pl.pallas_call(
        _mlstm_step_kernel,
        grid=(nb,),
        in_specs=[per_b3((ML_HEADS, ML_DQK)), per_b3((ML_HEADS, ML_DQK)),
                  per_b3((ML_HEADS, ML_DV)), per_b3((ML_HEADS, ML_DV)),
                  per_b3((ML_DQK, 2 * ML_HEADS)), per_b3((1, N_GATE)),
                  pl.BlockSpec((1, N_GATE), lambda b: (0, 0)),
                  per_b3((1, ML_HEADS)),
                  pl.BlockSpec((None, None, ML_HEADS, ML_DQK, ML_DV), lambda b: (l, b, 0, 0, 0)),
                  pl.BlockSpec((None, None, ML_HEADS, ML_DQK), lambda b: (l, b, 0, 0))],
        out_specs=[per_b3((1, ML_HEADS * ML_DV)),
                   pl.BlockSpec((None, ML_HEADS, ML_DQK, ML_DV), lambda b: (b, 0, 0, 0)),
                   per_b3((ML_HEADS, ML_DQK)), per_b3((ML_HEADS, ML_DQK))],
        out_shape=[jax.ShapeDtypeStruct((nb, 1, ML_HEADS * ML_DV), F32),
                   jax.ShapeDtypeStruct((nb, ML_HEADS, ML_DQK, ML_DV), F32),
                   jax.ShapeDtypeStruct((nb, ML_HEADS, ML_DQK), F32),
                   jax.ShapeDtypeStruct((nb, ML_HEADS, ML_DQK), F32)],
        compiler_params=_cparams("parallel"),
        name="mlstm_sample",
    )(q, k, v, o, qk_col, g3, bias_row, m03, c0, n0)


def _swa_step_kernel(g0_ref, g1_ref, g2_ref, c0_ref, c1_ref, c2_ref, xb_ref):
    hw = SW_HPG * SW_HD
    per_pos = 2 * SW_HPG
    news = (g0_ref, g1_ref, g2_ref)
    caches = (c0_ref, c1_ref, c2_ref)
    for hh in range(SW_HPG):
        m_run = l_run = acc = None
        for gi in range(len(SW_GROUPS)):
            lo = hh * SW_HD
            q = news[gi][:, lo:lo + SW_HD]
            k_new = news[gi][:, hw + lo:hw + lo + SW_HD]
            v_new = news[gi][:, 2 * hw + lo:2 * hw + lo + SW_HD]
            kc = caches[gi][:, hh, :]
            vc = caches[gi][:, SW_HPG + hh, :]
            s_c = jnp.sum(kc * q, axis=1, keepdims=True)
            s_n = jnp.sum(k_new * q, axis=1, keepdims=True)
            mx = jnp.maximum(jnp.max(s_c, axis=0, keepdims=True), s_n)
            p_c = jnp.exp(s_c - mx)
            p_n = jnp.exp(s_n - mx)
            den = jnp.sum(p_c, axis=0, keepdims=True) + p_n
            part = jnp.sum(p_c * vc, axis=0, keepdims=True) + p_n * v_new
            if gi == 0:
                m_run, l_run, acc = mx, den, part
            else:
                top = jnp.maximum(m_run, mx)
                a, b = jnp.exp(m_run - top), jnp.exp(mx - top)
                m_run, l_run, acc = top, a * l_run + b * den, a * acc + b * part
        xb_ref[:, hh * SW_HD:(hh + 1) * SW_HD] = acc / l_run


def _swa_sample(qkvs, caches, l):
    nb = caches[0].shape[1]
    hw = SW_HPG * SW_HD
    per_pos = 2 * SW_HPG
    news, views, specs = [], [], []
    for (win, dil), cache, qkv in zip(SW_GROUPS, caches, qkvs):
        assert cache.shape[2] == win and win // dil == SW_BLK
        news.append(qkv[0, 0, :nb].reshape(nb, 1, 3 * hw))
        views.append(cache.reshape(cache.shape[0], nb, win // dil, dil * per_pos, SW_HD))
        specs.append(pl.BlockSpec((None, None, SW_BLK, per_pos, SW_HD), lambda b: (l, b, 0, 0, 0)))
    row3 = pl.BlockSpec((None, 1, 3 * hw), lambda b: (b, 0, 0))
    xb = pl.pallas_call(
        _swa_step_kernel,
        grid=(nb,),
        in_specs=[row3, row3, row3] + specs,
        out_specs=pl.BlockSpec((None, 1, hw), lambda b: (b, 0, 0)),
        out_shape=jax.ShapeDtypeStruct((nb, 1, hw), F32),
        compiler_params=_cparams("parallel"),
        name="swa_sample",
    )(*news, *views)
    kvs = [jnp.stack([n[:, :, hw:2 * hw].reshape(nb, 1, SW_HPG, SW_HD),
                      n[:, :, 2 * hw:].reshape(nb, 1, SW_HPG, SW_HD)], axis=2) for n in news]
    return xb.reshape(nb, hw), kvs


def _cm_step_kernel(u_ref, v_ref, gv_ref, ws_ref, bs_ref, o_ref, vn_ref):
    for g in range(CM_GROUPS):
        sl = slice(g * CM_GD, (g + 1) * CM_GD)
        vg = v_ref[:, sl]
        vn = vg * lax.rsqrt(jnp.mean(vg * vg, axis=1, keepdims=True) + RMS_EPS) * gv_ref[g:g + 1, :]
        vn_ref[:, sl] = vn
        w00 = ws_ref[g][0:1, 0:1]
        o_ref[:, sl] = u_ref[:, sl] * (w00 * vn + bs_ref[g:g + 1, 0:1])


def _cm_sample(ps, g_v, w_s, b_s, l):
    rows = ps.shape[0]
    width = CM_GROUPS * CM_GD
    ch = CM_CHUNK
    return pl.pallas_call(
        _cm_step_kernel,
        grid=(1,),
        in_specs=[pl.BlockSpec((rows, width), lambda i: (0, C_UC // width)),
                  pl.BlockSpec((rows, width), lambda i: (0, C_VC // width)),
                  pl.BlockSpec((None, CM_GROUPS, CM_GD), lambda i: (l, 0, 0)),
                  pl.BlockSpec((None, CM_GROUPS, ch, ch), lambda i: (l, 0, 0, 0)),
                  pl.BlockSpec((None, CM_GROUPS, ch), lambda i: (l, 0, 0))],
        out_specs=[pl.BlockSpec((rows, width), lambda i: (0, 0))] * 2,
        out_shape=[jax.ShapeDtypeStruct((rows, width), F32)] * 2,
        compiler_params=_cparams("arbitrary"),
        name="cm_sample",
    )(ps, ps, g_v, w_s, b_s)


def _in_proj(hp, hs, w_in, l, tabs_p, tabs_s, batch, dils):
    wt = jnp.swapaxes(w_in, 1, 2)
    p_a, s_a = _mm_t(hp, hs, wt, l, 1024, 0, O_GATES // 1024, BF, "in_proj_a")
    p_g, s_g = _mm_t(hp, hs, wt, l, N_GATE, O_GATES, 1, F32, "gate_proj")
    p_qkv = [_qkv_proj(hp, wt, l, gi, dil, tabs_p, batch, BF, f"qkv_proj{gi}") for gi, dil in enumerate(dils)]
    s_qkv = [_qkv_proj(hs, wt, l, gi, 1, tabs_s, 1, F32, f"qkv_proj{gi}_s") for gi in range(len(dils))]
    p_r, s_r = _mm_t(hp, hs, wt, l, 1024, O_UC, N_REST // 1024, BF, "in_proj_r")
    return (p_a, p_g, p_qkv, p_r), (s_a, s_g, s_qkv, s_r)


def _qk_tables(pos):
    tab = _rope_table(pos)
    return jnp.stack([tab * (SW_HD ** -0.5), tab])


def _dense_tail(xs, mixers, ps, l, mods, rows_per_batch, wts, gain, mods_next):
    w_down = _cast_bf16(wts["w_ffn_down"], l)
    merged = _merge(mixers[0] + (ps[0],), mixers[1] + (ps[1],), wts["w_br_a"], wts["w_br_b"], wts["w_br_c"], l)
    (x_p, h2_p), (x_s, h2_s) = _mm_res_norm(merged, wts["w_out"], l, xs, mods, 2, rows_per_batch,
                                            wts["g_norm2"][l], mods, 4, 3, "out_proj", ride=True)
    acts = _ffn1(h2_p, h2_s, wts["w_ffn_gate_up"], l)
    return _mm_res_norm(acts, w_down, l, (x_p, x_s), mods, 5, rows_per_batch, gain, mods_next, 1, 0, "ffn_down",
                        ride=False)


def kernel(x_prompt, x_sample, state_mlstm_C, state_mlstm_n, state_mlstm_m, cache_swa_kv_w128, cache_swa_kv_w512, cache_swa_kv_w2048, c_prompt, c_sample, w_ada, b_ada, g_norm1, g_norm2, g_final, w_in, b_ml_i, b_ml_f, g_cm_v, w_s, b_s, w_br_a, w_br_b, w_br_c, w_out, w_ffn_gate_up, w_ffn_down):
    bp, tp, d = x_prompt.shape
    ns, ts, _ = x_sample.shape
    depth = w_in.shape[0]
    past_len = 16384
    assert d == D_MODEL and ts == 1 and ns == 8 and tp % max(ML_L, SW_GROUPS[-1][0]) == 0
    s_rows = 16
    wts = dict(w_br_a=w_br_a, w_br_b=w_br_b, w_br_c=w_br_c, w_out=w_out, g_norm2=g_norm2,
               w_ffn_gate_up=w_ffn_gate_up, w_ffn_down=w_ffn_down)

    c_all = jnp.zeros((s_rows, d), F32).at[:ns].set(c_sample).at[ns:ns + bp].set(c_prompt)
    mod_all = _ada(c_all, w_ada, b_ada)
    tabs_p = _qk_tables(jnp.arange(tp, dtype=jnp.int32))
    tabs_s = _qk_tables(jnp.full((s_rows,), past_len, jnp.int32))
    caches = (cache_swa_kv_w128, cache_swa_kv_w512, cache_swa_kv_w2048)
    dils = [dil for _, dil in SW_GROUPS]

    xp = x_prompt.reshape(bp * tp, d)
    xs = jnp.zeros((s_rows, d), F32).at[:ns].set(x_sample.reshape(ns, d))
    p_c, p_n, p_m, p_kv = [], [], [], [[], [], []]
    s_c, s_n, s_m, s_kv, s_v = [], [], [], [[], [], []], []
    mods_p = [mod_all[l, ns:ns + bp].reshape(bp, 1, 6 * d) for l in range(depth)]
    mods_s = [mod_all[l] for l in range(depth)]
    hp = _norm_mod(xp, g_norm1[0], mods_p[0], 1, 0, tp)
    hs = _norm_mod(xs, g_norm1[0], mods_s[0], 1, 0, None)
    for l in range(depth):
        last = l == depth - 1
        (p_a, gates, qkvs, p_r), (s_a, s_gates, s_qkvs, s_r) = _in_proj(hp, hs, w_in, l, tabs_p, tabs_s, bp, dils)
        xa, c1, n1, m1 = _mlstm_prompt(p_a, gates, b_ml_i[l], b_ml_f[l], bp, tp)
        xb = _swa_prompt(qkvs, bp, tp)
        kvs = [_kv_rows(qkvs[gi], gi, bp, tp) for gi in range(len(SW_GROUPS))]
        xc = _cm_prompt(p_r, g_cm_v, w_s, b_s, l)
        mix_p = (xa, xb, xc)
        p_c.append(c1)
        p_n.append(n1)
        p_m.append(m1[:, 0, :ML_HEADS])
        for gi in range(3):
            p_kv[gi].append(kvs[gi])
        xa, c1, n1, m1 = _mlstm_sample(s_a, s_gates, b_ml_i[l], b_ml_f[l], state_mlstm_C, state_mlstm_n,
                                       state_mlstm_m, l)
        xb, kvs = _swa_sample(s_qkvs, caches, l)
        xc, vn = _cm_sample(s_r, g_cm_v, w_s, b_s, l)
        pad = lambda a: jnp.zeros((s_rows, a.shape[-1]), F32).at[:ns].set(a.reshape(ns, -1))
        mix_s = (pad(xa), pad(xb), xc)
        outs = _dense_tail((xp, xs), (mix_p, mix_s), (p_r, s_r), l, (mods_p[l], mods_s[l]), tp, wts,
                           g_final if last else g_norm1[l + 1],
                           None if last else (mods_p[l + 1], mods_s[l + 1]))
        if last:
            y_p = outs[0].reshape(bp, tp, d)
            y_s = outs[1][:ns].reshape(ns, ts, d)
        else:
            (xp, hp), (xs, hs) = outs
        s_c.append(c1)
        s_n.append(n1)
        s_m.append(m1[:, :, 0])
        for gi in range(3):
            s_kv[gi].append(kvs[gi])
        s_v.append(vn[:ns].reshape(ns, 1, CM_GROUPS, CM_GD))
    st = jnp.stack
    return (y_p, y_s, st(p_c), st(p_n), st(p_m), st(p_kv[0]), st(p_kv[1]), st(p_kv[2]),
            st(s_c), st(s_n), st(s_m), st(s_kv[0]), st(s_kv[1]), st(s_kv[2]), st(s_v))
```

```python
import functools

import jax
import jax.numpy as jnp
from jax import lax
from jax.experimental import pallas as pl
from jax.experimental.pallas import tpu as pltpu

BF = jnp.bfloat16
F32 = jnp.float32

D_MODEL = 2048
ML_HEADS = 8
ML_DQK = 128
ML_DV = 256
SW_GROUPS = ((128, 1), (512, 4), (2048, 16))
SW_HPG = 4
SW_HEADS = 12
SW_HD = 128
SW_ROT = SW_HD // 4
SW_BLK = 128
ROPE_THETA = 500000.0
CM_CHUNK = 128
CM_GROUPS = 8
CM_GD = 256
D_FF = 5632
RMS_EPS = 1e-6
NEG = -1e30

LANE = 128
O_GATES = 6144
O_QB = O_GATES + 2 * ML_HEADS
O_UC = O_QB + 3 * SW_HEADS * SW_HD
N_REST = 2 * CM_GROUPS * CM_GD + 3 * D_MODEL
N_GATE = LANE
C_QA, C_KA, C_VA, C_OA = 0, 1024, 2048, 4096
C_UC, C_VC, C_G = 0, 2048, 4096

TM = 512
TM_WIDE = 1024
QKV_SUB = 256
SWA_UNITS = 4
ML_L = 128
VMEM_LIMIT = 48 * 1024 * 1024
VMEM_LIMIT_BIG = 52 * 1024 * 1024


def _cparams(*sem, vmem=VMEM_LIMIT):
    return pltpu.CompilerParams(dimension_semantics=sem, vmem_limit_bytes=vmem)


def _sigmoid(x):
    return 0.5 * (jnp.tanh(0.5 * x) + 1.0)


def _log_sigmoid(x):
    return jnp.minimum(x, 0.0) - jnp.log1p(jnp.exp(-jnp.abs(x)))


def _ada_kernel(c_ref, w_ref, b_ref, o_ref):
    c = c_ref[...]
    a = (c * _sigmoid(c)).astype(BF)
    o_ref[...] = jnp.dot(a, w_ref[...].astype(BF), preferred_element_type=F32) + b_ref[...]


def _ada(c_all, w_ada, b_ada):
    depth, d, n = w_ada.shape
    rows = c_all.shape[0]
    tn = 1024
    return pl.pallas_call(
        _ada_kernel,
        grid=(depth, n // tn),
        in_specs=[pl.BlockSpec((rows, d), lambda l, j: (0, 0)),
                  pl.BlockSpec((None, d, tn), lambda l, j: (l, 0, j)),
                  pl.BlockSpec((None, 1, tn), lambda l, j: (l, 0, j))],
        out_specs=pl.BlockSpec((None, rows, tn), lambda l, j: (l, 0, j)),
        out_shape=jax.ShapeDtypeStruct((depth, rows, n), F32),
        compiler_params=_cparams("parallel", "parallel"),
        name="ada",
    )(c_all, w_ada, b_ada.reshape(depth, 1, n))


def _mod_spec(mod, chunk, width, tm, rows_per_batch, row_of, col_of):
    per = D_MODEL // width
    if mod.ndim == 3:
        tiles_per_batch = rows_per_batch // tm
        return pl.BlockSpec((None, 1, width),
                            lambda *g: (row_of(*g) // tiles_per_batch, 0, chunk * per + col_of(*g)))
    return pl.BlockSpec((tm, width), lambda *g: (row_of(*g), chunk * per + col_of(*g)))


NORM_ROWS = 16


def _row_groups(n_rows, body):
    def step(i, c):
        body(pl.ds(pl.multiple_of(i * NORM_ROWS, NORM_ROWS), NORM_ROWS))
        return c
    groups = n_rows // NORM_ROWS
    lax.fori_loop(0, groups, step, 0, unroll=min(groups, 4))


def _norm_mod_kernel(x_ref, g_ref, sc_ref, sh_ref, o_ref):
    per_row = sc_ref.shape[0] > 1

    def body(rows):
        x = x_ref[rows, :]
        y = x * lax.rsqrt(jnp.mean(x * x, axis=1, keepdims=True) + RMS_EPS) * g_ref[...]
        sc = sc_ref[rows, :] if per_row else sc_ref[...]
        sh = sh_ref[rows, :] if per_row else sh_ref[...]
        o_ref[rows, :] = (y * (1.0 + sc) + sh).astype(o_ref.dtype)

    _row_groups(x_ref.shape[0], body)


def _norm_mod(x, g, mod, sc_chunk, sh_chunk, rows_per_batch):
    m, d = x.shape
    tm = min(TM, m)
    row_of = lambda i: i
    col_of = lambda i: 0
    return pl.pallas_call(
        _norm_mod_kernel,
        grid=(m // tm,),
        in_specs=[pl.BlockSpec((tm, d), lambda i: (i, 0)),
                  pl.BlockSpec((1, d), lambda i: (0, 0)),
                  _mod_spec(mod, sc_chunk, d, tm, rows_per_batch, row_of, col_of),
                  _mod_spec(mod, sh_chunk, d, tm, rows_per_batch, row_of, col_of)],
        out_specs=pl.BlockSpec((tm, d), lambda i: (i, 0)),
        out_shape=jax.ShapeDtypeStruct((m, d), BF),
        compiler_params=_cparams("parallel"),
        name="norm_mod",
    )(x, g.reshape(1, d), mod, mod)


def _cast_weights_once(pairs):
    @pl.when(pl.program_id(1) == 0)
    def _():
        for src, dst in pairs:
            dst[...] = src[...].astype(BF)


_NT =(((1,), (1,)), ((), ()))


def _wt_spec(k, l, tn, start_of):
    return pl.BlockSpec((pl.Element(1), pl.Element(tn), pl.Element(k)),
                        lambda j, i: (l, pl.multiple_of(start_of(j), 8), 0))


def _with_rider(x_ref, xs_ref, compute, store):
    tm = x_ref.shape[0]

    @pl.when(pl.program_id(1) == 0)
    def _():
        res = compute(jnp.concatenate([x_ref[...], xs_ref[...]], axis=0))
        store(tuple(r[:tm] for r in res), tuple(r[tm:] for r in res))

    @pl.when(pl.program_id(1) != 0)
    def _():
        store(compute(x_ref[...]), None)


def _mm_t_kernel(x_ref, xs_ref, w_ref, o_ref, os_ref, wbf):
    _cast_weights_once([(w_ref.at[0], wbf)])

    def compute(rows):
        return (lax.dot_general(rows, wbf[...], _NT, preferred_element_type=F32),)

    def store(main, rider):
        o_ref[...] = main[0].astype(o_ref.dtype)
        if rider is not None:
            os_ref[...] = rider[0].astype(os_ref.dtype)

    _with_rider(x_ref, xs_ref, compute, store)


def _mm_t(x, xs, wt, l, tn, col0, n_tiles, out_dtype, name):
    m, k = x.shape
    ms = xs.shape[0]
    tm = min(TM_WIDE, m)
    return pl.pallas_call(
        _mm_t_kernel,
        grid=(n_tiles, m // tm),
        in_specs=[pl.BlockSpec((tm, k), lambda j, i: (i, 0)),
                  pl.BlockSpec((ms, k), lambda j, i: (0, 0)),
                  _wt_spec(k, l, tn, lambda j: col0 + j * tn)],
        out_specs=[pl.BlockSpec((tm, tn), lambda j, i: (i, j)),
                   pl.BlockSpec((ms, tn), lambda j, i: (0, j))],
        out_shape=[jax.ShapeDtypeStruct((m, n_tiles * tn), out_dtype),
                   jax.ShapeDtypeStruct((ms, n_tiles * tn), F32)],
        scratch_shapes=[pltpu.VMEM((tn, k), BF)],
        compiler_params=_cparams("arbitrary", "arbitrary"),
        name=name,
    )(x, xs, wt)


def _qkv_kernel(x_ref, wq_ref, wk_ref, wv_ref, tab_ref, o_ref, wbf, acc_ref, *, dil):
    @pl.when(pl.program_id(0) == 0)
    def _():
        for part, w_ref in enumerate((wq_ref, wk_ref, wv_ref)):
            wbf[part] = w_ref[0].astype(BF)

    hw = SW_HPG * SW_HD
    heads = [slice(hh * SW_HD, (hh + 1) * SW_HD) for hh in range(SW_HPG)]
    tm = x_ref.shape[0]
    sub = min(tm, QKV_SUB)
    for s in range(tm // sub):
        rows_in = slice(s * sub, (s + 1) * sub)
        x = x_ref[rows_in, :]
        for part in range(3):
            acc = lax.dot_general(x, wbf[part], _NT, preferred_element_type=F32)
            if part < 2:
                tab = tab_ref[part, rows_in, :]
                vals = [_rope(acc[:, sl], tab) for sl in heads]
            else:
                vals = [acc[:, sl] for sl in heads]
            cols = [slice(part * hw + sl.start, part * hw + sl.stop) for sl in heads]
            if dil == 1:
                for col, val in zip(cols, vals):
                    o_ref[0, rows_in, col] = val.astype(o_ref.dtype)
                continue
            for hh, val in enumerate(vals):
                acc_ref[part, hh, rows_in, :] = val
            n = sub // dil
            for r in range(dil):
                for hh, col in enumerate(cols):
                    o_ref[r, s * n:(s + 1) * n, col] = acc_ref[
                        part, hh, pl.ds(s * sub + r, n, stride=dil), :].astype(o_ref.dtype)


def _qkv_proj(x, wt, l, gi, dil, tabs, batch, out_dtype, name):
    m, k = x.shape
    seq = m // batch
    tm = min(TM_WIDE, seq)
    tpb = seq // tm
    hw = SW_HPG * SW_HD
    wide = len(SW_GROUPS) * hw
    once = dict(pipeline_mode=pl.Buffered(1))
    w_spec = lambda part: pl.BlockSpec((pl.Element(1), pl.Element(hw), pl.Element(k)),
                                       lambda i: (l, O_QB + part * wide + gi * hw, 0), **once)
    return pl.pallas_call(
        functools.partial(_qkv_kernel, dil=dil),
        grid=(m // tm,),
        in_specs=[pl.BlockSpec((tm, k), lambda i: (i, 0)), w_spec(0), w_spec(1), w_spec(2),
                  pl.BlockSpec((2, tm, 2 * SW_HD), lambda i: (0, i % tpb, 0))],
        out_specs=pl.BlockSpec((None, dil, tm // dil, 3 * hw), lambda i: (i // tpb, 0, i % tpb, 0)),
        out_shape=jax.ShapeDtypeStruct((batch, dil, seq // dil, 3 * hw), out_dtype),
        scratch_shapes=[pltpu.VMEM((3, hw, k), BF), pltpu.VMEM((3, SW_HPG, tm, SW_HD), F32)],
        compiler_params=_cparams("arbitrary"),
        name=name,
    )(x, wt, wt, wt, tabs)


def _cast_kernel(w_ref, o_ref):
    o_ref[...] = w_ref[...].astype(o_ref.dtype)


def _cast_bf16(w3, l):
    _, k, n = w3.shape
    rows = 512
    return pl.pallas_call(
        _cast_kernel,
        grid=(k // rows,),
        in_specs=[pl.BlockSpec((None, rows, n), lambda i: (l, i, 0))],
        out_specs=pl.BlockSpec((rows, n), lambda i: (i, 0)),
        out_shape=jax.ShapeDtypeStruct((k, n), BF),
        compiler_params=_cparams("parallel"),
        name="cast_bf16",
    )(w3)


def _mm_res_norm_kernel(*refs, cast, final, ride):
    refs = list(refs)
    take = lambda n: [refs.pop(0) for _ in range(n)]
    x_ref, w_ref, r_ref, g_ref, gn_ref = take(5)
    sc_ref, sh_ref = (None, None) if final else take(2)
    if ride:
        xs_ref, rs_ref, gs_ref = take(3)
        scs_ref, shs_ref = (None, None) if final else take(2)
    if final:
        (y_ref,) = take(1)
    else:
        o_ref, y_ref = take(2)
    if ride:
        if not final:
            (os_ref,) = take(1)
        (ys_ref,) = take(1)
    scratch = refs
    if final:
        o_ref = scratch[-1]
    if cast:
        wbf = scratch[0]

        @pl.when(pl.program_id(0) == 0)
        def _():
            wbf[...] = w_ref[...].astype(BF)
        w = wbf[...]
    else:
        w = w_ref[...]
    tm = x_ref.shape[0]

    def norm(x, sc, sh):
        y = x * lax.rsqrt(jnp.mean(x * x, axis=1, keepdims=True) + RMS_EPS) * gn_ref[...]
        return y if final else y * (1.0 + sc) + sh

    def plain():
        o_ref[...] = r_ref[...] + g_ref[...] * jnp.dot(x_ref[...], w, preferred_element_type=F32)

    if ride:
        @pl.when(pl.program_id(0) == 0)
        def _():
            acc = jnp.dot(jnp.concatenate([x_ref[...], xs_ref[...]], axis=0), w, preferred_element_type=F32)
            o_ref[...] = r_ref[...] + g_ref[...] * acc[:tm]
            rows_s = rs_ref[...] + gs_ref[...] * acc[tm:]
            if final:
                ys_ref[...] = norm(rows_s, None, None)
            else:
                os_ref[...] = rows_s
                ys_ref[...] = norm(rows_s, scs_ref[...], shs_ref[...]).astype(ys_ref.dtype)

        pl.when(pl.program_id(0) != 0)(plain)
    else:
        plain()
    per_row = (not final) and sc_ref.shape[0] > 1

    def body(rows):
        sc = sh = None
        if not final:
            sc = sc_ref[rows, :] if per_row else sc_ref[...]
            sh = sh_ref[rows, :] if per_row else sh_ref[...]
        y_ref[rows, :] = norm(o_ref[rows, :], sc, sh).astype(y_ref.dtype)

    _row_groups(tm, body)


def _mm_res_norm(xs, w, l, ress, mods, gate_chunk, rows_per_batch, gn, mods_next, sc_chunk, sh_chunk, name, ride):
    nxt = (None, None) if mods_next is None else mods_next
    groups = [(xs[g], ress[g], mods[g], nxt[g]) for g in range(2)]
    call = functools.partial(_mm_res_norm_call, w=w, l=l, gate_chunk=gate_chunk, gn=gn, sc_chunk=sc_chunk,
                             sh_chunk=sh_chunk, name=name)
    if ride:
        return call(groups[0], groups[1], rows_per_batch)
    return call(groups[0], None, rows_per_batch), call(groups[1], None, None)


def _mm_res_norm_call(main, rider, rows_per_batch, *, w, l, gate_chunk, gn, sc_chunk, sh_chunk, name):
    x, res, mod, mod_next = main
    m, k = x.shape
    n = D_MODEL
    tm = min(256, m)
    cast = w.ndim == 3
    final = mod_next is None
    ride = rider is not None
    if ride:
        x_s, res_s, mod_s, modn_s = rider
        ms = x_s.shape[0]
    mods_next = (mod_next, modn_s if ride else None)
    once = dict(pipeline_mode=pl.Buffered(1))
    w_spec = (pl.BlockSpec((None, k, n), lambda i: (l, 0, 0), **once) if cast
              else pl.BlockSpec((k, n), lambda i: (0, 0), **once))
    row_of, col_of = (lambda i: i), (lambda i: 0)
    tile = pl.BlockSpec((tm, n), lambda i: (i, 0))
    in_specs = [pl.BlockSpec((tm, k), lambda i: (i, 0)), w_spec, tile,
                _mod_spec(mod, gate_chunk, n, tm, rows_per_batch, row_of, col_of),
                pl.BlockSpec((1, n), lambda i: (0, 0))]
    args = [x, w, res, mod, gn.reshape(1, n)]
    if not final:
        in_specs += [_mod_spec(mods_next[0], sc_chunk, n, tm, rows_per_batch, row_of, col_of),
                     _mod_spec(mods_next[0], sh_chunk, n, tm, rows_per_batch, row_of, col_of)]
        args += [mods_next[0], mods_next[0]]
    out_specs = [tile] if final else [tile, tile]
    out_shape = ([jax.ShapeDtypeStruct((m, n), F32)] if final
                 else [jax.ShapeDtypeStruct((m, n), F32), jax.ShapeDtypeStruct((m, n), BF)])
    if ride:
        whole = pl.BlockSpec((ms, n), lambda i: (0, 0))
        chunk = lambda c: pl.BlockSpec((ms, n), lambda i: (0, c))
        in_specs += [pl.BlockSpec((ms, k), lambda i: (0, 0)), whole, chunk(gate_chunk)]
        args += [x_s, res_s, mod_s]
        if final:
            out_specs += [whole]
            out_shape += [jax.ShapeDtypeStruct((ms, n), F32)]
        else:
            in_specs += [chunk(sc_chunk), chunk(sh_chunk)]
            args += [mods_next[1], mods_next[1]]
            out_specs += [whole, whole]
            out_shape += [jax.ShapeDtypeStruct((ms, n), F32), jax.ShapeDtypeStruct((ms, n), BF)]
    outs = pl.pallas_call(
        functools.partial(_mm_res_norm_kernel, cast=cast, final=final, ride=ride),
        grid=(m // tm,),
        in_specs=in_specs,
        out_specs=out_specs,
        out_shape=out_shape,
        scratch_shapes=([pltpu.VMEM((k, n), BF)] if cast else []) + ([pltpu.VMEM((tm, n), F32)] if final else []),
        compiler_params=_cparams("arbitrary", vmem=VMEM_LIMIT_BIG),
        name=name,
    )(*args)
    if not ride:
        return outs[0] if final else (outs[0], outs[1])
    return tuple(outs) if final else ((outs[0], outs[1]), (outs[2], outs[3]))


def _ffn1_kernel(x_ref, xs_ref, wg_ref, wu_ref, o_ref, os_ref, wgbf, wubf):
    _cast_weights_once([(wg_ref, wgbf), (wu_ref, wubf)])

    def compute(rows):
        g = jnp.dot(rows, wgbf[...], preferred_element_type=F32)
        u = jnp.dot(rows, wubf[...], preferred_element_type=F32)
        return (g * _sigmoid(g) * u,)

    def store(main, rider):
        o_ref[...] = main[0].astype(o_ref.dtype)
        if rider is not None:
            os_ref[...] = rider[0].astype(os_ref.dtype)

    _with_rider(x_ref, xs_ref, compute, store)


def _ffn1(x, xs, w_gu, l):
    m, k = x.shape
    ms = xs.shape[0]
    tn = 512
    nj = D_FF // tn
    tm = min(TM_WIDE, m)
    return pl.pallas_call(
        _ffn1_kernel,
        grid=(nj, m // tm),
        in_specs=[pl.BlockSpec((tm, k), lambda j, i: (i, 0)),
                  pl.BlockSpec((ms, k), lambda j, i: (0, 0)),
                  pl.BlockSpec((None, k, tn), lambda j, i: (l, 0, j)),
                  pl.BlockSpec((None, k, tn), lambda j, i: (l, 0, nj + j))],
        out_specs=[pl.BlockSpec((tm, tn), lambda j, i: (i, j)),
                   pl.BlockSpec((ms, tn), lambda j, i: (0, j))],
        out_shape=[jax.ShapeDtypeStruct((m, D_FF), BF), jax.ShapeDtypeStruct((ms, D_FF), BF)],
        scratch_shapes=[pltpu.VMEM((k, tn), BF), pltpu.VMEM((k, tn), BF)],
        compiler_params=_cparams("arbitrary", "arbitrary"),
        name="ffn_gate_up",
    )(x, xs, w_gu, w_gu)


def _merge_kernel(xa_ref, xb_ref, xc_ref, g0_ref, g1_ref, g2_ref, sa_ref, sb_ref, sc_ref, t0_ref, t1_ref, t2_ref,
                  wa_ref, wb_ref, wc_ref, o_ref, os_ref, wabf, wbbf, wcbf):
    _cast_weights_once([(wa_ref, wabf), (wb_ref, wbbf), (wc_ref, wcbf)])
    tm = xa_ref.shape[0]

    def dots(xa, xb, xc):
        return (jnp.dot(xa, wabf[...], preferred_element_type=F32),
                jnp.dot(xb, wbbf[...], preferred_element_type=F32),
                jnp.dot(xc, wcbf[...], preferred_element_type=F32))

    def gated(d, gates):
        return sum(_sigmoid(g[...].astype(F32)) * x for g, x in zip(gates, d))

    @pl.when(pl.program_id(1) == 0)
    def _():
        cat = lambda a, b: jnp.concatenate([a[...].astype(BF), b[...].astype(BF)], axis=0)
        d = dots(cat(xa_ref, sa_ref), cat(xb_ref, sb_ref), cat(xc_ref, sc_ref))
        o_ref[...] = gated([x[:tm] for x in d], (g0_ref, g1_ref, g2_ref)).astype(o_ref.dtype)
        os_ref[...] = gated([x[tm:] for x in d], (t0_ref, t1_ref, t2_ref)).astype(os_ref.dtype)

    @pl.when(pl.program_id(1) != 0)
    def _():
        d = dots(xa_ref[...].astype(BF), xb_ref[...].astype(BF), xc_ref[...].astype(BF))
        o_ref[...] = gated(d, (g0_ref, g1_ref, g2_ref)).astype(o_ref.dtype)


def _merge(prompt, sample, w_a, w_b, w_c, l):
    xa, xb, xc, p = prompt
    sa, sb, sc, ps = sample
    ms = sa.shape[0]
    m = xa.shape[0]
    tn = 1024
    tm = min(TM, m)
    ka, kb, kc = xa.shape[1], xb.shape[1], xc.shape[1]
    g_blk = C_G // tn
    per = D_MODEL // tn
    once = dict(pipeline_mode=pl.Buffered(1))
    return pl.pallas_call(
        _merge_kernel,
        grid=(D_MODEL // tn, m // tm),
        in_specs=[pl.BlockSpec((tm, ka), lambda j, i: (i, 0)),
                  pl.BlockSpec((tm, kb), lambda j, i: (i, 0)),
                  pl.BlockSpec((tm, kc), lambda j, i: (i, 0)),
                  pl.BlockSpec((tm, tn), lambda j, i: (i, g_blk + j)),
                  pl.BlockSpec((tm, tn), lambda j, i: (i, g_blk + per + j)),
                  pl.BlockSpec((tm, tn), lambda j, i: (i, g_blk + 2 * per + j)),
                  pl.BlockSpec((ms, ka), lambda j, i: (0, 0)),
                  pl.BlockSpec((ms, kb), lambda j, i: (0, 0)),
                  pl.BlockSpec((ms, kc), lambda j, i: (0, 0)),
                  pl.BlockSpec((ms, tn), lambda j, i: (0, g_blk + j)),
                  pl.BlockSpec((ms, tn), lambda j, i: (0, g_blk + per + j)),
                  pl.BlockSpec((ms, tn), lambda j, i: (0, g_blk + 2 * per + j)),
                  pl.BlockSpec((None, ka, tn), lambda j, i: (l, 0, j), **once),
                  pl.BlockSpec((None, kb, tn), lambda j, i: (l, 0, j), **once),
                  pl.BlockSpec((None, kc, tn), lambda j, i: (l, 0, j), **once)],
        out_specs=[pl.BlockSpec((tm, tn), lambda j, i: (i, j)),
                   pl.BlockSpec((ms, tn), lambda j, i: (0, j))],
        out_shape=[jax.ShapeDtypeStruct((m, D_MODEL), BF), jax.ShapeDtypeStruct((ms, D_MODEL), BF)],
        scratch_shapes=[pltpu.VMEM((ka, tn), BF), pltpu.VMEM((kb, tn), BF), pltpu.VMEM((kc, tn), BF)],
        compiler_params=_cparams("arbitrary", "arbitrary", vmem=VMEM_LIMIT_BIG),
        name="branch_merge",
    )(xa, xb, xc, p, p, p, sa, sb, sc, ps, ps, ps, w_a, w_b, w_c)


def _scan(x, axis, op, fill):
    idx = lax.broadcasted_iota(jnp.int32, x.shape, axis)
    shift = 1
    while shift < x.shape[axis]:
        x = op(x, jnp.where(idx >= shift, pltpu.roll(x, shift, axis), fill))
        shift *= 2
    return x


def _mlstm_kernel(q_ref, k_ref, v_ref, o_ref, gc_ref, gr_ref, brow_ref, bcol_ref,
                  xa_ref, c_ref, n_ref, m_ref, nn_ref):
    chunk = pl.program_id(1)

    @pl.when(chunk == 0)
    def _():
        c_ref[...] = jnp.zeros_like(c_ref)
        m_ref[...] = jnp.zeros_like(m_ref)
        nn_ref[...] = jnp.zeros_like(nn_ref)

    L = q_ref.shape[0]
    nh = ML_HEADS
    scale = ML_DQK ** -0.5
    gc = gc_ref[...] + brow_ref[...]
    gr = gr_ref[...] + bcol_ref[...]
    f_c = pltpu.roll(_scan(_log_sigmoid(gc), 0, jnp.add, 0.0), LANE - nh, 1)
    m_prev = m_ref[...]
    m_t = f_c + jnp.maximum(m_prev, _scan(gc - f_c, 0, jnp.maximum, NEG))
    w_inter = jnp.exp(f_c + m_prev - m_t)
    e_neg_m = jnp.exp(-m_t)
    f_minus_m = f_c - m_t
    m_new = m_t[L - 1:L, :]
    f_last = f_c[L - 1:L, :]
    w_last = jnp.exp(f_last - f_c + gc - m_new) * scale
    decay = jnp.exp(f_last + m_prev - m_new)
    m_ref[...] = m_new
    f_r = _scan(_log_sigmoid(gr), 1, jnp.add, 0.0)
    a_r = gr[0:nh, :] - f_r[nh:2 * nh, :]
    row = lax.broadcasted_iota(jnp.int32, (L, L), 0)
    col = lax.broadcasted_iota(jnp.int32, (L, L), 1)
    tri = col <= row
    ones = jnp.ones((L, ML_DQK), BF)
    nt = (((1,), (1,)), ((), ()))
    tn = (((0,), (0,)), ((), ()))
    for h in range(nh):
        lane = lambda x: x[:, h:h + 1]
        d_w = jnp.exp(jnp.where(tri, lane(f_minus_m) + a_r[h:h + 1, :], NEG))
        q = q_ref[:, h * ML_DQK:(h + 1) * ML_DQK]
        k = k_ref[:, h * ML_DQK:(h + 1) * ML_DQK]
        v = v_ref[:, h * ML_DV:(h + 1) * ML_DV]
        s = (lax.dot_general(q, k, nt, preferred_element_type=F32) * (scale * d_w)).astype(BF)
        c_old = c_ref[h]
        nn_old = nn_ref[h]
        wi = lane(w_inter)
        num = (jnp.dot(s, v, preferred_element_type=F32)
               + wi * jnp.dot(q, c_old.astype(BF), preferred_element_type=F32))
        den = (jnp.dot(s, ones, preferred_element_type=F32)
               + wi * jnp.dot(q, nn_old.astype(BF), preferred_element_type=F32))
        inv = 1.0 / jnp.maximum(jnp.abs(den[:, 0:1]), lane(e_neg_m))
        og = o_ref[:, h * ML_DV:(h + 1) * ML_DV].astype(F32)
        xa_ref[:, h * ML_DV:(h + 1) * ML_DV] = (_sigmoid(og) * (num * inv)).astype(xa_ref.dtype)
        kw = (k.astype(F32) * lane(w_last)).astype(BF)
        dk = decay[:, h:h + 1]
        c_ref[h] = dk * c_old + lax.dot_general(kw, v, tn, preferred_element_type=F32)
        nn_ref[h] = dk * nn_old + lax.dot_general(kw, ones, tn, preferred_element_type=F32)

    @pl.when(chunk == pl.num_programs(1) - 1)
    def _():
        for h in range(nh):
            n_ref[h:h + 1, :] = nn_ref[h].T[0:1, :]


def _mlstm_prompt(p, gates, b_i, b_f, batch, seq):
    m_rows = p.shape[0]
    L = ML_L
    nc = seq // L
    bias = jnp.concatenate([b_i, b_f]).astype(F32)
    bias_row = jnp.zeros((1, N_GATE), F32).at[0, :2 * ML_HEADS].set(bias)
    bias_col = bias.reshape(2 * ML_HEADS, 1)
    gates_t = gates[:, :2 * ML_HEADS].T
    hq, hv = ML_HEADS * ML_DQK, ML_HEADS * ML_DV
    return pl.pallas_call(
        _mlstm_kernel,
        grid=(batch, nc),
        in_specs=[pl.BlockSpec((L, hq), lambda b, c: (b * nc + c, C_QA // hq)),
                  pl.BlockSpec((L, hq), lambda b, c: (b * nc + c, C_KA // hq)),
                  pl.BlockSpec((L, hv), lambda b, c: (b * nc + c, C_VA // hv)),
                  pl.BlockSpec((L, hv), lambda b, c: (b * nc + c, C_OA // hv)),
                  pl.BlockSpec((L, N_GATE), lambda b, c: (b * nc + c, 0)),
                  pl.BlockSpec((2 * ML_HEADS, L), lambda b, c: (0, b * nc + c)),
                  pl.BlockSpec((1, N_GATE), lambda b, c: (0, 0)),
                  pl.BlockSpec((2 * ML_HEADS, 1), lambda b, c: (0, 0))],
        out_specs=[pl.BlockSpec((L, hv), lambda b, c: (b * nc + c, 0)),
                   pl.BlockSpec((None, ML_HEADS, ML_DQK, ML_DV), lambda b, c: (b, 0, 0, 0)),
                   pl.BlockSpec((None, ML_HEADS, ML_DQK), lambda b, c: (b, 0, 0)),
                   pl.BlockSpec((None, 1, LANE), lambda b, c: (b, 0, 0))],
        out_shape=[jax.ShapeDtypeStruct((m_rows, hv), BF),
                   jax.ShapeDtypeStruct((batch, ML_HEADS, ML_DQK, ML_DV), F32),
                   jax.ShapeDtypeStruct((batch, ML_HEADS, ML_DQK), F32),
                   jax.ShapeDtypeStruct((batch, 1, LANE), F32)],
        scratch_shapes=[pltpu.VMEM((ML_HEADS, ML_DQK, ML_DQK), F32)],
        compiler_params=_cparams("parallel", "arbitrary"),
        name="mlstm_prompt",
    )(p, p, p, p, gates, gates_t, bias_row, bias_col)


def _rope_table(pos):
    half = SW_ROT // 2
    freq = ROPE_THETA ** (-jnp.arange(half, dtype=F32) * 2.0 / SW_ROT)
    ang = pos.astype(F32)[:, None] * freq[None, :]
    cos, sin = jnp.cos(ang), jnp.sin(ang)
    t = pos.shape[0]
    return jnp.concatenate([cos, cos, jnp.ones((t, SW_HD - SW_ROT), F32),
                            -sin, sin, jnp.zeros((t, SW_HD - SW_ROT), F32)], axis=1)


def _rope(x, tab):
    half = SW_ROT // 2
    lane = lax.broadcasted_iota(jnp.int32, x.shape, 1)
    partner = jnp.where(lane < SW_ROT, lane ^ half, lane)
    return x * tab[:, 0:SW_HD] + jnp.take_along_axis(x, partner, axis=1) * tab[:, SW_HD:2 * SW_HD]


def _attend(q, k_own, v_own, k_prev, v_prev, prev_shift):
    blk = SW_BLK
    row = lax.broadcasted_iota(jnp.int32, (1, blk, blk), 1)
    col = lax.broadcasted_iota(jnp.int32, (1, blk, blk), 2)
    qk = (((2,), (2,)), ((0,), (0,)))
    pv = (((2,), (1,)), ((0,), (0,)))
    s_o = jnp.where(col <= row, lax.dot_general(q, k_own, qk, preferred_element_type=F32), NEG)
    s_p = jnp.where(col >= row + prev_shift, lax.dot_general(q, k_prev, qk, preferred_element_type=F32), NEG)
    mx = jnp.maximum(jnp.max(s_o, axis=2, keepdims=True), jnp.max(s_p, axis=2, keepdims=True))
    p_o = jnp.exp(s_o - mx)
    p_p = jnp.exp(s_p - mx)
    den = jnp.sum(p_o, axis=2, keepdims=True) + jnp.sum(p_p, axis=2, keepdims=True)
    acc = (lax.dot_general(p_o.astype(BF), v_own, pv, preferred_element_type=F32)
           + lax.dot_general(p_p.astype(BF), v_prev, pv, preferred_element_type=F32))
    return acc, mx, den


def _swa_kernel(q1, k1, v1, k1p, v1p, q2, k2, v2, k2p, v2p, q3, k3, v3, k3p, v3p, xb_ref, o_acc, m_acc, l_acc):
    blk = SW_BLK
    u = SWA_UNITS
    first_shift = jnp.where(pl.program_id(1) > 0, 0, blk)
    wide = lambda x: jnp.broadcast_to(x, x.shape[:-1] + (SW_HD,))

    def merge(rows, acc, mx, den):
        m_old = m_acc[rows, :]
        m_new = jnp.maximum(m_old, mx)
        a = jnp.exp(m_old - m_new)
        b = jnp.exp(mx - m_new)
        o_acc[rows, :] = a * o_acc[rows, :] + b * acc
        l_acc[rows, :] = a * l_acc[rows, :] + b * den
        m_acc[rows, :] = m_new

    def g1_batch(rows, k_p, v_p, shift):
        split = lambda x: x.reshape(u, blk, SW_HD)
        acc, mx, den = _attend(split(q1[rows, :]), split(k1[rows, :]), split(v1[rows, :]),
                               split(k_p), split(v_p), shift)
        o_acc[rows, :] = acc.reshape(u * blk, SW_HD)
        m_acc[rows, :] = wide(mx).reshape(u * blk, SW_HD)
        l_acc[rows, :] = wide(den).reshape(u * blk, SW_HD)

    head = pl.ds(0, (u - 1) * blk)
    unit = lax.broadcasted_iota(jnp.int32, (u, 1, 1), 0)
    g1_batch(pl.ds(0, u * blk),
             jnp.concatenate([k1p[...], k1[head, :]], axis=0), jnp.concatenate([v1p[...], v1[head, :]], axis=0),
             jnp.where(unit == 0, first_shift, 0))

    def g1_body(i, c):
        rows = pl.ds(pl.multiple_of(i * u * blk, blk), u * blk)
        prev = pl.ds(pl.multiple_of(i * u * blk - blk, blk), u * blk)
        g1_batch(rows, k1[prev, :], v1[prev, :], 0)
        return c

    lax.fori_loop(1, q1.shape[0] // (u * blk), g1_body, 0)

    def dilated(q, k, v, kp, vp):
        dil = q.shape[0]
        n_sub = q.shape[1] // blk

        def batch(sub, r0, k_p, v_p, shift):
            rs = pl.ds(r0, u)
            rows = pl.ds(pl.multiple_of(sub * blk, blk), blk)
            acc, mx, den = _attend(q[rs, rows, :], k[rs, rows, :], v[rs, rows, :], k_p, v_p, shift)
            for i in range(u):
                merge(pl.ds(sub * blk * dil + r0 + i, blk, stride=dil), acc[i], wide(mx[i]), wide(den[i]))

        def first(i, c):
            rs = pl.ds(i * u, u)
            batch(0, i * u, kp[rs], vp[rs], first_shift)
            return c

        lax.fori_loop(0, dil // u, first, 0)

        def rest(i, c):
            sub = 1 + i // (dil // u)
            r0 = (i % (dil // u)) * u
            prev = pl.ds(pl.multiple_of((sub - 1) * blk, blk), blk)
            batch(sub, r0, k[pl.ds(r0, u), prev, :], v[pl.ds(r0, u), prev, :], 0)
            return c

        lax.fori_loop(0, (n_sub - 1) * (dil // u), rest, 0)

    dilated(q2, k2, v2, k2p, v2p)
    dilated(q3, k3, v3, k3p, v3p)
    xb_ref[...] = (o_acc[...] * (1.0 / l_acc[...])).astype(xb_ref.dtype)


def _swa_prompt(qkvs, batch, seq):
    blk = SW_BLK
    span = SW_GROUPS[-1][0]
    assert seq % span == 0 and all(w // d == blk for w, d in SW_GROUPS)
    nspan = seq // span
    in_specs = []
    args = []
    for (win, dil), qkv in zip(SW_GROUPS, qkvs):
        rows = span // dil
        per = rows // blk
        own = lambda c: pl.BlockSpec((None, dil, rows, SW_HD), lambda b, s, h, c=c: (b, 0, s, c * SW_HPG + h))
        prev = lambda c: pl.BlockSpec(
            (None, dil, blk, SW_HD), lambda b, s, h, c=c, per=per: (b, 0, jnp.maximum(s * per - 1, 0), c * SW_HPG + h))
        if dil == 1:
            own = lambda c: pl.BlockSpec((None, None, span, SW_HD), lambda b, s, h, c=c: (b, 0, s, c * SW_HPG + h))
            prev = lambda c: pl.BlockSpec(
                (None, None, blk, SW_HD),
                lambda b, s, h, c=c, per=per: (b, 0, jnp.maximum(s * per - 1, 0), c * SW_HPG + h))
        in_specs += [own(0), own(1), own(2), prev(1), prev(2)]
        args += [qkv] * 5
    return pl.pallas_call(
        _swa_kernel,
        grid=(batch, nspan, SW_HPG),
        in_specs=in_specs,
        out_specs=pl.BlockSpec((span, SW_HD), lambda b, s, h: (b * nspan + s, h)),
        out_shape=jax.ShapeDtypeStruct((batch * seq, SW_HPG * SW_HD), BF),
        scratch_shapes=[pltpu.VMEM((span, SW_HD), F32)] * 3,
        compiler_params=_cparams("parallel", "parallel", "parallel"),
        name="swa_prompt",
    )(*args)


def _kv_rows_kernel(k_ref, v_ref, o_ref, *, dil):
    for r in range(dil):
        rows = pl.ds(r, SW_BLK, stride=dil) if dil > 1 else pl.ds(0, SW_BLK)
        for hh in range(SW_HPG):
            sl = slice(hh * SW_HD, (hh + 1) * SW_HD)
            o_ref[hh, rows, :] = k_ref[r, :, sl].astype(F32)
            o_ref[SW_HPG + hh, rows, :] = v_ref[r, :, sl].astype(F32)


def _kv_rows(qkv, gi, batch, seq):
    win, dil = SW_GROUPS[gi]
    hw = SW_HPG * SW_HD
    last = seq // dil // SW_BLK - 1
    out = pl.pallas_call(
        functools.partial(_kv_rows_kernel, dil=dil),
        grid=(batch,),
        in_specs=[pl.BlockSpec((None, dil, SW_BLK, hw), lambda b: (b, 0, last, 1)),
                  pl.BlockSpec((None, dil, SW_BLK, hw), lambda b: (b, 0, last, 2))],
        out_specs=pl.BlockSpec((None, 2 * SW_HPG, win, SW_HD), lambda b: (b, 0, 0, 0)),
        out_shape=jax.ShapeDtypeStruct((batch, 2 * SW_HPG, win, SW_HD), F32),
        compiler_params=_cparams("parallel"),
        name=f"kv_rows_w{win}",
    )(qkv, qkv)
    return out.reshape(batch, 2, SW_HPG, win, SW_HD).transpose(0, 3, 1, 2, 4)


def _cm_kernel(u_ref, v_ref, gv_ref, ws_ref, bs_ref, o_ref):
    ch = CM_CHUNK
    row = lax.broadcasted_iota(jnp.int32, (ch, ch), 0)
    col = lax.broadcasted_iota(jnp.int32, (ch, ch), 1)
    tri = col <= row
    for g in range(CM_GROUPS):
        sl = slice(g * CM_GD, (g + 1) * CM_GD)
        w = jnp.where(tri, ws_ref[g], 0.0).astype(BF)
        for c in range(u_ref.shape[0] // ch):
            rows = slice(c * ch, (c + 1) * ch)
            vg = v_ref[rows, sl].astype(F32)
            vn = vg * lax.rsqrt(jnp.mean(vg * vg, axis=1, keepdims=True) + RMS_EPS) * gv_ref[g:g + 1, :]
            mixed = jnp.dot(w, vn.astype(BF), preferred_element_type=F32) + bs_ref[:, g:g + 1]
            o_ref[rows, sl] = (u_ref[rows, sl].astype(F32) * mixed).astype(o_ref.dtype)


def _cm_prompt(p, g_v, w_s, b_s, l):
    m_rows = p.shape[0]
    ch = CM_CHUNK
    width = CM_GROUPS * CM_GD
    rows = 2 * ch
    return pl.pallas_call(
        _cm_kernel,
        grid=(m_rows // rows,),
        in_specs=[pl.BlockSpec((rows, width), lambda i: (i, C_UC // width)),
                  pl.BlockSpec((rows, width), lambda i: (i, C_VC // width)),
                  pl.BlockSpec((None, CM_GROUPS, CM_GD), lambda i: (l, 0, 0)),
                  pl.BlockSpec((None, CM_GROUPS, ch, ch), lambda i: (l, 0, 0, 0)),
                  pl.BlockSpec((ch, CM_GROUPS), lambda i: (0, 0))],
        out_specs=pl.BlockSpec((rows, width), lambda i: (i, 0)),
        out_shape=jax.ShapeDtypeStruct((m_rows, width), BF),
        compiler_params=_cparams("parallel"),
        name="cm_prompt",
    )(p, p, g_v, w_s, b_s[l].T)


def _mlstm_step_kernel(qr_ref, kr_ref, vr_ref, or_ref, qkc_ref, g_ref, brow_ref, m0_ref, c0_ref, n0_ref,
                       xa_ref, c1_ref, n1_ref, m1_ref):
    scale = ML_DQK ** -0.5
    gates = g_ref[...] + brow_ref[...]
    m0 = m0_ref[...]
    for h in range(ML_HEADS):
        ig = gates[:, h:h + 1]
        lf = _log_sigmoid(gates[:, ML_HEADS + h:ML_HEADS + h + 1])
        m_prev = m0[:, h:h + 1]
        inter = lf + m_prev
        m_t = jnp.maximum(inter, ig)
        d_w = jnp.exp(ig - m_t)
        w_inter = jnp.exp(inter - m_t)
        q_row = qr_ref[h:h + 1, :]
        k_row = kr_ref[h:h + 1, :]
        v_row = vr_ref[h:h + 1, :]
        q_col = qkc_ref[:, h:h + 1]
        k_col = qkc_ref[:, ML_HEADS + h:ML_HEADS + h + 1]
        c_old = c0_ref[h]
        n_old = n0_ref[h:h + 1, :]
        s = jnp.sum(q_row * k_row, axis=1, keepdims=True) * scale * d_w
        qc = jnp.sum(q_col * c_old, axis=0, keepdims=True)
        num = s * v_row + w_inter * qc
        den = s + w_inter * jnp.sum(q_row * n_old, axis=1, keepdims=True)
        hh = num / jnp.maximum(jnp.abs(den), jnp.exp(-m_t))
        xa_ref[:, h * ML_DV:(h + 1) * ML_DV] = _sigmoid(or_ref[h:h + 1, :]) * hh
        w_last = d_w * scale
        decay = w_inter
        c1_ref[h] = decay * c_old + (k_col * w_last) * v_row
        n1_ref[h:h + 1, :] = decay * n_old + w_last * k_row
        m1_ref[h:h + 1, :] = jnp.broadcast_to(m_t, (1, ML_DQK))


def _mlstm_sample(ps, gates, b_i, b_f, c0, n0, m0, l):
    nb = c0.shape[1]
    q = ps[:nb, C_QA:C_QA + 1024].reshape(nb, ML_HEADS, ML_DQK)
    k = ps[:nb, C_KA:C_KA + 1024].reshape(nb, ML_HEADS, ML_DQK)
    v = ps[:nb, C_VA:C_VA + 2048].reshape(nb, ML_HEADS, ML_DV)
    o = ps[:nb, C_OA:C_OA + 2048].reshape(nb, ML_HEADS, ML_DV)
    qk_col = jnp.concatenate([q, k], axis=1).transpose(0, 2, 1)
    bias = jnp.concatenate([b_i, b_f]).astype(F32)
    bias_row = jnp.zeros((1, N_GATE), F32).at[0, :2 * ML_HEADS].set(bias)
    g3 = gates[:nb].reshape(nb, 1, N_GATE)
    m03 = m0[l].reshape(nb, 1, ML_HEADS)
    per_b3 = lambda shape: pl.BlockSpec((None,) + shape, lambda b: (b, 0, 0))
    return pl.pallas_call(
        _mlstm_step_kernel,
        grid=(nb,),
        in_specs=[per_b3((ML_HEADS, ML_DQK)), per_b3((ML_HEADS, ML_DQK)),
                  per_b3((ML_HEADS, ML_DV)), per_b3((ML_HEADS, ML_DV)),
                  per_b3((ML_DQK, 2 * ML_HEADS)), per_b3((1, N_GATE)),
                  pl.BlockSpec((1, N_GATE), lambda b: (0, 0)),
                  per_b3((1, ML_HEADS)),
                  pl.BlockSpec((None, None, ML_HEADS, ML_DQK, ML_DV), lambda b: (l, b, 0, 0, 0)),
                  pl.BlockSpec((None, None, ML_HEADS, ML_DQK), lambda b: (l, b, 0, 0))],
        out_specs=[per_b3((1, ML_HEADS * ML_DV)),
                   pl.BlockSpec((None, ML_HEADS, ML_DQK, ML_DV), lambda b: (b, 0, 0, 0)),
                   per_b3((ML_HEADS, ML_DQK)), per_b3((ML_HEADS, ML_DQK))],
        out_shape=[jax.ShapeDtypeStruct((nb, 1, ML_HEADS * ML_DV), F32),
                   jax.ShapeDtypeStruct((nb, ML_HEADS, ML_DQK, ML_DV), F32),
                   jax.ShapeDtypeStruct((nb, ML_HEADS, ML_DQK), F32),
                   jax.ShapeDtypeStruct((nb, ML_HEADS, ML_DQK), F32)],
        compiler_params=_cparams("parallel"),
        name="mlstm_sample",
    )(q, k, v, o, qk_col, g3, bias_row, m03, c0, n0)


def _swa_step_kernel(g0_ref, g1_ref, g2_ref, c0_ref, c1_ref, c2_ref, xb_ref):
    hw = SW_HPG * SW_HD
    per_pos = 2 * SW_HPG
    news = (g0_ref, g1_ref, g2_ref)
    caches = (c0_ref, c1_ref, c2_ref)
    for hh in range(SW_HPG):
        m_run = l_run = acc = None
        for gi in range(len(SW_GROUPS)):
            lo = hh * SW_HD
            q = news[gi][:, lo:lo + SW_HD]
            k_new = news[gi][:, hw + lo:hw + lo + SW_HD]
            v_new = news[gi][:, 2 * hw + lo:2 * hw + lo + SW_HD]
            kc = caches[gi][:, hh, :]
            vc = caches[gi][:, SW_HPG + hh, :]
            s_c = jnp.sum(kc * q, axis=1, keepdims=True)
            s_n = jnp.sum(k_new * q, axis=1, keepdims=True)
            mx = jnp.maximum(jnp.max(s_c, axis=0, keepdims=True), s_n)
            p_c = jnp.exp(s_c - mx)
            p_n = jnp.exp(s_n - mx)
            den = jnp.sum(p_c, axis=0, keepdims=True) + p_n
            part = jnp.sum(p_c * vc, axis=0, keepdims=True) + p_n * v_new
            if gi == 0:
                m_run, l_run, acc = mx, den, part
            else:
                top = jnp.maximum(m_run, mx)
                a, b = jnp.exp(m_run - top), jnp.exp(mx - top)
                m_run, l_run, acc = top, a * l_run + b * den, a * acc + b * part
        xb_ref[:, hh * SW_HD:(hh + 1) * SW_HD] = acc / l_run


def _swa_sample(qkvs, caches, l):
    nb = caches[0].shape[1]
    hw = SW_HPG * SW_HD
    per_pos = 2 * SW_HPG
    news, views, specs = [], [], []
    for (win, dil), cache, qkv in zip(SW_GROUPS, caches, qkvs):
        assert cache.shape[2] == win and win // dil == SW_BLK
        news.append(qkv[0, 0, :nb].reshape(nb, 1, 3 * hw))
        views.append(cache.reshape(cache.shape[0], nb, win // dil, dil * per_pos, SW_HD))
        specs.append(pl.BlockSpec((None, None, SW_BLK, per_pos, SW_HD), lambda b: (l, b, 0, 0, 0)))
    row3 = pl.BlockSpec((None, 1, 3 * hw), lambda b: (b, 0, 0))
    xb = pl.pallas_call(
        _swa_step_kernel,
        grid=(nb,),
        in_specs=[row3, row3, row3] + specs,
        out_specs=pl.BlockSpec((None, 1, hw), lambda b: (b, 0, 0)),
        out_shape=jax.ShapeDtypeStruct((nb, 1, hw), F32),
        compiler_params=_cparams("parallel"),
        name="swa_sample",
    )(*news, *views)
    kvs = [jnp.stack([n[:, :, hw:2 * hw].reshape(nb, 1, SW_HPG, SW_HD),
                      n[:, :, 2 * hw:].reshape(nb, 1, SW_HPG, SW_HD)], axis=2) for n in news]
    return xb.reshape(nb, hw), kvs


def _cm_step_kernel(u_ref, v_ref, gv_ref, ws_ref, bs_ref, o_ref, vn_ref):
    for g in range(CM_GROUPS):
        sl = slice(g * CM_GD, (g + 1) * CM_GD)
        vg = v_ref[:, sl]
        vn = vg * lax.rsqrt(jnp.mean(vg * vg, axis=1, keepdims=True) + RMS_EPS) * gv_ref[g:g + 1, :]
        vn_ref[:, sl] = vn
        w00 = ws_ref[g][0:1, 0:1]
        o_ref[:, sl] = u_ref[:, sl] * (w00 * vn + bs_ref[g:g + 1, 0:1])


def _cm_sample(ps, g_v, w_s, b_s, l):
    rows = ps.shape[0]
    width = CM_GROUPS * CM_GD
    ch = CM_CHUNK
    return pl.pallas_call(
        _cm_step_kernel,
        grid=(1,),
        in_specs=[pl.BlockSpec((rows, width), lambda i: (0, C_UC // width)),
                  pl.BlockSpec((rows, width), lambda i: (0, C_VC // width)),
                  pl.BlockSpec((None, CM_GROUPS, CM_GD), lambda i: (l, 0, 0)),
                  pl.BlockSpec((None, CM_GROUPS, ch, ch), lambda i: (l, 0, 0, 0)),
                  pl.BlockSpec((None, CM_GROUPS, ch), lambda i: (l, 0, 0))],
        out_specs=[pl.BlockSpec((rows, width), lambda i: (0, 0))] * 2,
        out_shape=[jax.ShapeDtypeStruct((rows, width), F32)] * 2,
        compiler_params=_cparams("arbitrary"),
        name="cm_sample",
    )(ps, ps, g_v, w_s, b_s)


def _in_proj(hp, hs, w_in, l, tabs_p, tabs_s, batch, dils):
    wt = jnp.swapaxes(w_in, 1, 2)
    p_a, s_a = _mm_t(hp, hs, wt, l, 1024, 0, O_GATES // 1024, BF, "in_proj_a")
    p_g, s_g = _mm_t(hp, hs, wt, l, N_GATE, O_GATES, 1, F32, "gate_proj")
    p_qkv = [_qkv_proj(hp, wt, l, gi, dil, tabs_p, batch, BF, f"qkv_proj{gi}") for gi, dil in enumerate(dils)]
    s_qkv = [_qkv_proj(hs, wt, l, gi, 1, tabs_s, 1, F32, f"qkv_proj{gi}_s") for gi in range(len(dils))]
    p_r, s_r = _mm_t(hp, hs, wt, l, 1024, O_UC, N_REST // 1024, BF, "in_proj_r")
    return (p_a, p_g, p_qkv, p_r), (s_a, s_g, s_qkv, s_r)


def _qk_tables(pos):
    tab = _rope_table(pos)
    return jnp.stack([tab * (SW_HD ** -0.5), tab])


def _dense_tail(xs, mixers, ps, l, mods, rows_per_batch, wts, gain, mods_next):
    w_down = _cast_bf16(wts["w_ffn_down"], l)
    merged = _merge(mixers[0] + (ps[0],), mixers[1] + (ps[1],), wts["w_br_a"], wts["w_br_b"], wts["w_br_c"], l)
    (x_p, h2_p), (x_s, h2_s) = _mm_res_norm(merged, wts["w_out"], l, xs, mods, 2, rows_per_batch,
                                            wts["g_norm2"][l], mods, 4, 3, "out_proj", ride=False)
    acts = _ffn1(h2_p, h2_s, wts["w_ffn_gate_up"], l)
    return _mm_res_norm(acts, w_down, l, (x_p, x_s), mods, 5, rows_per_batch, gain, mods_next, 1, 0, "ffn_down",
                        ride=False)


def kernel(x_prompt, x_sample, state_mlstm_C, state_mlstm_n, state_mlstm_m, cache_swa_kv_w128, cache_swa_kv_w512, cache_swa_kv_w2048, c_prompt, c_sample, w_ada, b_ada, g_norm1, g_norm2, g_final, w_in, b_ml_i, b_ml_f, g_cm_v, w_s, b_s, w_br_a, w_br_b, w_br_c, w_out, w_ffn_gate_up, w_ffn_down):
    bp, tp, d = x_prompt.shape
    ns, ts, _ = x_sample.shape
    depth = w_in.shape[0]
    past_len = 16384
    assert d == D_MODEL and ts == 1 and ns == 8 and tp % max(ML_L, SW_GROUPS[-1][0]) == 0
    s_rows = 16
    wts = dict(w_br_a=w_br_a, w_br_b=w_br_b, w_br_c=w_br_c, w_out=w_out, g_norm2=g_norm2,
               w_ffn_gate_up=w_ffn_gate_up, w_ffn_down=w_ffn_down)

    c_all = jnp.zeros((s_rows, d), F32).at[:ns].set(c_sample).at[ns:ns + bp].set(c_prompt)
    mod_all = _ada(c_all, w_ada, b_ada)
    tabs_p = _qk_tables(jnp.arange(tp, dtype=jnp.int32))
    tabs_s = _qk_tables(jnp.full((s_rows,), past_len, jnp.int32))
    caches = (cache_swa_kv_w128, cache_swa_kv_w512, cache_swa_kv_w2048)
    dils = [dil for _, dil in SW_GROUPS]

    xp = x_prompt.reshape(bp * tp, d)
    xs = jnp.zeros((s_rows, d), F32).at[:ns].set(x_sample.reshape(ns, d))
    p_c, p_n, p_m, p_kv = [], [], [], [[], [], []]
    s_c, s_n, s_m, s_kv, s_v = [], [], [], [[], [], []], []
    mods_p = [mod_all[l, ns:ns + bp].reshape(bp, 1, 6 * d) for l in range(depth)]
    mods_s = [mod_all[l] for l in range(depth)]
    hp = _norm_mod(xp, g_norm1[0], mods_p[0], 1, 0, tp)
    hs = _norm_mod(xs, g_norm1[0], mods_s[0], 1, 0, None)
    for l in range(depth):
        last = l == depth - 1
        (p_a, gates, qkvs, p_r), (s_a, s_gates, s_qkvs, s_r) = _in_proj(hp, hs, w_in, l, tabs_p, tabs_s, bp, dils)
        xa, c1, n1, m1 = _mlstm_prompt(p_a, gates, b_ml_i[l], b_ml_f[l], bp, tp)
        xb = _swa_prompt(qkvs, bp, tp)
        kvs = [_kv_rows(qkvs[gi], gi, bp, tp) for gi in range(len(SW_GROUPS))]
        xc = _cm_prompt(p_r, g_cm_v, w_s, b_s, l)
        mix_p = (xa, xb, xc)
        p_c.append(c1)
        p_n.append(n1)
        p_m.append(m1[:, 0, :ML_HEADS])
        for gi in range(3):
            p_kv[gi].append(kvs[gi])
        xa, c1, n1, m1 = _mlstm_sample(s_a, s_gates, b_ml_i[l], b_ml_f[l], state_mlstm_C, state_mlstm_n,
                                       state_mlstm_m, l)
        xb, kvs = _swa_sample(s_qkvs, caches, l)
        xc, vn = _cm_sample(s_r, g_cm_v, w_s, b_s, l)
        pad = lambda a: jnp.zeros((s_rows, a.shape[-1]), F32).at[:ns].set(a.reshape(ns, -1))
        mix_s = (pad(xa), pad(xb), xc)
        outs = _dense_tail((xp, xs), (mix_p, mix_s), (p_r, s_r), l, (mods_p[l], mods_s[l]), tp, wts,
                           g_final if last else g_norm1[l + 1],
                           None if last else (mods_p[l + 1], mods_s[l + 1]))
        if last:
            y_p = outs[0].reshape(bp, tp, d)
            y_s = outs[1][:ns].reshape(ns, ts, d)
        else:
            (xp, hp), (xs, hs) = outs
        s_c.append(c1)
        s_n.append(n1)
        s_m.append(m1[:, :, 0])
        for gi in range(3):
            s_kv[gi].append(kvs[gi])
        s_v.append(vn[:ns].reshape(ns, 1, CM_GROUPS, CM_GD))
    st = jnp.stack
    return (y_p, y_s, st(p_c), st(p_n), st(p_m), st(p_kv[0]), st(p_kv[1]), st(p_kv[2]),
            st(s_c), st(s_n), st(s_m), st(s_kv[0]), st(s_kv[1]), st(s_kv[2]), st(s_v))
```

```python
import functools

import jax
import jax.numpy as jnp
from jax import lax
from jax.experimental import pallas as pl
from jax.experimental.pallas import tpu as pltpu

BF = jnp.bfloat16
F32 = jnp.float32

D_MODEL = 2048
ML_HEADS = 8
ML_DQK = 128
ML_DV = 256
SW_GROUPS = ((128, 1), (512, 4), (2048, 16))
SW_HPG = 4
SW_HEADS = 12
SW_HD = 128
SW_ROT = SW_HD // 4
SW_BLK = 128
ROPE_THETA = 500000.0
CM_CHUNK = 128
CM_GROUPS = 8
CM_GD = 256
D_FF = 5632
RMS_EPS = 1e-6
NEG = -1e30

LANE = 128
O_GATES = 6144
O_QB = O_GATES + 2 * ML_HEADS
O_UC = O_QB + 3 * SW_HEADS * SW_HD
N_REST = 2 * CM_GROUPS * CM_GD + 3 * D_MODEL
N_GATE = LANE
C_QA, C_KA, C_VA, C_OA = 0, 1024, 2048, 4096
C_UC, C_VC, C_G = 0, 2048, 4096

TM = 512
TM_WIDE = 1024
QKV_SUB = 256
SWA_UNITS = 4
ML_L = 256
VMEM_LIMIT = 48 * 1024 * 1024
VMEM_LIMIT_BIG = 52 * 1024 * 1024


def _cparams(*sem, vmem=VMEM_LIMIT):
    return pltpu.CompilerParams(dimension_semantics=sem, vmem_limit_bytes=vmem)


def _sigmoid(x):
    return 0.5 * (jnp.tanh(0.5 * x) + 1.0)


def _log_sigmoid(x):
    return jnp.minimum(x, 0.0) - jnp.log1p(jnp.exp(-jnp.abs(x)))


def _ada_kernel(c_ref, w_ref, b_ref, o_ref):
    c = c_ref[...]
    a = (c * _sigmoid(c)).astype(BF)
    o_ref[...] = jnp.dot(a, w_ref[...].astype(BF), preferred_element_type=F32) + b_ref[...]


def _ada(c_all, w_ada, b_ada):
    depth, d, n = w_ada.shape
    rows = c_all.shape[0]
    tn = 1024
    return pl.pallas_call(
        _ada_kernel,
        grid=(depth, n // tn),
        in_specs=[pl.BlockSpec((rows, d), lambda l, j: (0, 0)),
                  pl.BlockSpec((None, d, tn), lambda l, j: (l, 0, j)),
                  pl.BlockSpec((None, 1, tn), lambda l, j: (l, 0, j))],
        out_specs=pl.BlockSpec((None, rows, tn), lambda l, j: (l, 0, j)),
        out_shape=jax.ShapeDtypeStruct((depth, rows, n), F32),
        compiler_params=_cparams("parallel", "parallel"),
        name="ada",
    )(c_all, w_ada, b_ada.reshape(depth, 1, n))


def _mod_spec(mod, chunk, width, tm, rows_per_batch, row_of, col_of):
    per = D_MODEL // width
    if mod.ndim == 3:
        tiles_per_batch = rows_per_batch // tm
        return pl.BlockSpec((None, 1, width),
                            lambda *g: (row_of(*g) // tiles_per_batch, 0, chunk * per + col_of(*g)))
    return pl.BlockSpec((tm, width), lambda *g: (row_of(*g), chunk * per + col_of(*g)))


NORM_ROWS = 16


def _row_groups(n_rows, body):
    def step(i, c):
        body(pl.ds(pl.multiple_of(i * NORM_ROWS, NORM_ROWS), NORM_ROWS))
        return c
    groups = n_rows // NORM_ROWS
    lax.fori_loop(0, groups, step, 0, unroll=min(groups, 4))


def _norm_mod_kernel(x_ref, g_ref, sc_ref, sh_ref, o_ref):
    per_row = sc_ref.shape[0] > 1

    def body(rows):
        x = x_ref[rows, :]
        y = x * lax.rsqrt(jnp.mean(x * x, axis=1, keepdims=True) + RMS_EPS) * g_ref[...]
        sc = sc_ref[rows, :] if per_row else sc_ref[...]
        sh = sh_ref[rows, :] if per_row else sh_ref[...]
        o_ref[rows, :] = (y * (1.0 + sc) + sh).astype(o_ref.dtype)

    _row_groups(x_ref.shape[0], body)


def _norm_mod(x, g, mod, sc_chunk, sh_chunk, rows_per_batch):
    m, d = x.shape
    tm = min(TM, m)
    row_of = lambda i: i
    col_of = lambda i: 0
    return pl.pallas_call(
        _norm_mod_kernel,
        grid=(m // tm,),
        in_specs=[pl.BlockSpec((tm, d), lambda i: (i, 0)),
                  pl.BlockSpec((1, d), lambda i: (0, 0)),
                  _mod_spec(mod, sc_chunk, d, tm, rows_per_batch, row_of, col_of),
                  _mod_spec(mod, sh_chunk, d, tm, rows_per_batch, row_of, col_of)],
        out_specs=pl.BlockSpec((tm, d), lambda i: (i, 0)),
        out_shape=jax.ShapeDtypeStruct((m, d), BF),
        compiler_params=_cparams("parallel"),
        name="norm_mod",
    )(x, g.reshape(1, d), mod, mod)


def _cast_weights_once(pairs):
    @pl.when(pl.program_id(1) == 0)
    def _():
        for src, dst in pairs:
            dst[...] = src[...].astype(BF)


_NT =(((1,), (1,)), ((), ()))


def _wt_spec(k, l, tn, start_of):
    return pl.BlockSpec((pl.Element(1), pl.Element(tn), pl.Element(k)),
                        lambda j, i: (l, pl.multiple_of(start_of(j), 8), 0))


def _with_rider(x_ref, xs_ref, compute, store):
    tm = x_ref.shape[0]

    @pl.when(pl.program_id(1) == 0)
    def _():
        res = compute(jnp.concatenate([x_ref[...], xs_ref[...]], axis=0))
        store(tuple(r[:tm] for r in res), tuple(r[tm:] for r in res))

    @pl.when(pl.program_id(1) != 0)
    def _():
        store(compute(x_ref[...]), None)


def _mm_t_kernel(x_ref, xs_ref, w_ref, o_ref, os_ref, wbf):
    _cast_weights_once([(w_ref.at[0], wbf)])

    def compute(rows):
        return (lax.dot_general(rows, wbf[...], _NT, preferred_element_type=F32),)

    def store(main, rider):
        o_ref[...] = main[0].astype(o_ref.dtype)
        if rider is not None:
            os_ref[...] = rider[0].astype(os_ref.dtype)

    _with_rider(x_ref, xs_ref, compute, store)


def _mm_t(x, xs, wt, l, tn, col0, n_tiles, out_dtype, name):
    m, k = x.shape
    ms = xs.shape[0]
    tm = min(TM_WIDE, m)
    return pl.pallas_call(
        _mm_t_kernel,
        grid=(n_tiles, m // tm),
        in_specs=[pl.BlockSpec((tm, k), lambda j, i: (i, 0)),
                  pl.BlockSpec((ms, k), lambda j, i: (0, 0)),
                  _wt_spec(k, l, tn, lambda j: col0 + j * tn)],
        out_specs=[pl.BlockSpec((tm, tn), lambda j, i: (i, j)),
                   pl.BlockSpec((ms, tn), lambda j, i: (0, j))],
        out_shape=[jax.ShapeDtypeStruct((m, n_tiles * tn), out_dtype),
                   jax.ShapeDtypeStruct((ms, n_tiles * tn), F32)],
        scratch_shapes=[pltpu.VMEM((tn, k), BF)],
        compiler_params=_cparams("arbitrary", "arbitrary"),
        name=name,
    )(x, xs, wt)


def _qkv_kernel(x_ref, wq_ref, wk_ref, wv_ref, tab_ref, o_ref, wbf, acc_ref, *, dil):
    @pl.when(pl.program_id(0) == 0)
    def _():
        for part, w_ref in enumerate((wq_ref, wk_ref, wv_ref)):
            wbf[part] = w_ref[0].astype(BF)

    hw = SW_HPG * SW_HD
    heads = [slice(hh * SW_HD, (hh + 1) * SW_HD) for hh in range(SW_HPG)]
    tm = x_ref.shape[0]
    sub = min(tm, QKV_SUB)
    for s in range(tm // sub):
        rows_in = slice(s * sub, (s + 1) * sub)
        x = x_ref[rows_in, :]
        for part in range(3):
            acc = lax.dot_general(x, wbf[part], _NT, preferred_element_type=F32)
            if part < 2:
                tab = tab_ref[rows_in, :]
                if part == 0:
                    tab = tab * (SW_HD ** -0.5)
                vals = [_rope(acc[:, sl], tab) for sl in heads]
            else:
                vals = [acc[:, sl] for sl in heads]
            cols = [slice(part * hw + sl.start, part * hw + sl.stop) for sl in heads]
            if dil == 1:
                for col, val in zip(cols, vals):
                    o_ref[0, rows_in, col] = val.astype(o_ref.dtype)
                continue
            for hh, val in enumerate(vals):
                acc_ref[part, hh, rows_in, :] = val
            n = sub // dil
            for r in range(dil):
                for hh, col in enumerate(cols):
                    o_ref[r, s * n:(s + 1) * n, col] = acc_ref[
                        part, hh, pl.ds(s * sub + r, n, stride=dil), :].astype(o_ref.dtype)


def _qkv_proj(x, wt, l, gi, dil, tabs, batch, out_dtype, name):
    m, k = x.shape
    seq = m // batch
    tm = min(TM_WIDE, seq)
    tpb = seq // tm
    hw = SW_HPG * SW_HD
    wide = len(SW_GROUPS) * hw
    once = dict(pipeline_mode=pl.Buffered(1))
    w_spec = lambda part: pl.BlockSpec((pl.Element(1), pl.Element(hw), pl.Element(k)),
                                       lambda i: (l, O_QB + part * wide + gi * hw, 0), **once)
    return pl.pallas_call(
        functools.partial(_qkv_kernel, dil=dil),
        grid=(m // tm,),
        in_specs=[pl.BlockSpec((tm, k), lambda i: (i, 0)), w_spec(0), w_spec(1), w_spec(2),
                  pl.BlockSpec((tm, 2 * SW_HD), lambda i: (i % tpb, 0))],
        out_specs=pl.BlockSpec((None, dil, tm // dil, 3 * hw), lambda i: (i // tpb, 0, i % tpb, 0)),
        out_shape=jax.ShapeDtypeStruct((batch, dil, seq // dil, 3 * hw), out_dtype),
        scratch_shapes=[pltpu.VMEM((3, hw, k), BF), pltpu.VMEM((3, SW_HPG, tm, SW_HD), F32)],
        compiler_params=_cparams("arbitrary"),
        name=name,
    )(x, wt, wt, wt, tabs)


def _cast_kernel(w_ref, o_ref):
    o_ref[...] = w_ref[...].astype(o_ref.dtype)


def _cast_bf16(w3, l):
    _, k, n = w3.shape
    rows = 512
    return pl.pallas_call(
        _cast_kernel,
        grid=(k // rows,),
        in_specs=[pl.BlockSpec((None, rows, n), lambda i: (l, i, 0))],
        out_specs=pl.BlockSpec((rows, n), lambda i: (i, 0)),
        out_shape=jax.ShapeDtypeStruct((k, n), BF),
        compiler_params=_cparams("parallel"),
        name="cast_bf16",
    )(w3)


def _mm_res_norm_kernel(*refs, cast, final, ride):
    refs = list(refs)
    take = lambda n: [refs.pop(0) for _ in range(n)]
    x_ref, w_ref, r_ref, g_ref, gn_ref = take(5)
    sc_ref, sh_ref = (None, None) if final else take(2)
    if ride:
        xs_ref, rs_ref, gs_ref = take(3)
        scs_ref, shs_ref = (None, None) if final else take(2)
    if final:
        (y_ref,) = take(1)
    else:
        o_ref, y_ref = take(2)
    if ride:
        if not final:
            (os_ref,) = take(1)
        (ys_ref,) = take(1)
    scratch = refs
    if final:
        o_ref = scratch[-1]
    if cast:
        wbf = scratch[0]

        @pl.when(pl.program_id(0) == 0)
        def _():
            wbf[...] = w_ref[...].astype(BF)
        w = wbf[...]
    else:
        w = w_ref[...]
    tm = x_ref.shape[0]

    def norm(x, sc, sh):
        y = x * lax.rsqrt(jnp.mean(x * x, axis=1, keepdims=True) + RMS_EPS) * gn_ref[...]
        return y if final else y * (1.0 + sc) + sh

    def plain():
        o_ref[...] = r_ref[...] + g_ref[...] * jnp.dot(x_ref[...], w, preferred_element_type=F32)

    if ride:
        @pl.when(pl.program_id(0) == 0)
        def _():
            acc = jnp.dot(jnp.concatenate([x_ref[...], xs_ref[...]], axis=0), w, preferred_element_type=F32)
            o_ref[...] = r_ref[...] + g_ref[...] * acc[:tm]
            rows_s = rs_ref[...] + gs_ref[...] * acc[tm:]
            if final:
                ys_ref[...] = norm(rows_s, None, None)
            else:
                os_ref[...] = rows_s
                ys_ref[...] = norm(rows_s, scs_ref[...], shs_ref[...]).astype(ys_ref.dtype)

        pl.when(pl.program_id(0) != 0)(plain)
    else:
        plain()
    per_row = (not final) and sc_ref.shape[0] > 1

    def body(rows):
        sc = sh = None
        if not final:
            sc = sc_ref[rows, :] if per_row else sc_ref[...]
            sh = sh_ref[rows, :] if per_row else sh_ref[...]
        y_ref[rows, :] = norm(o_ref[rows, :], sc, sh).astype(y_ref.dtype)

    _row_groups(tm, body)


def _mm_res_norm(xs, w, l, ress, mods, gate_chunk, rows_per_batch, gn, mods_next, sc_chunk, sh_chunk, name, ride):
    nxt = (None, None) if mods_next is None else mods_next
    groups = [(xs[g], ress[g], mods[g], nxt[g]) for g in range(2)]
    call = functools.partial(_mm_res_norm_call, w=w, l=l, gate_chunk=gate_chunk, gn=gn, sc_chunk=sc_chunk,
                             sh_chunk=sh_chunk, name=name)
    if ride:
        return call(groups[0], groups[1], rows_per_batch)
    return call(groups[0], None, rows_per_batch), call(groups[1], None, None)


def _mm_res_norm_call(main, rider, rows_per_batch, *, w, l, gate_chunk, gn, sc_chunk, sh_chunk, name):
    x, res, mod, mod_next = main
    m, k = x.shape
    n = D_MODEL
    tm = min(256, m)
    cast = w.ndim == 3
    final = mod_next is None
    ride = rider is not None
    if ride:
        x_s, res_s, mod_s, modn_s = rider
        ms = x_s.shape[0]
    mods_next = (mod_next, modn_s if ride else None)
    once = dict(pipeline_mode=pl.Buffered(1))
    w_spec = (pl.BlockSpec((None, k, n), lambda i: (l, 0, 0), **once) if cast
              else pl.BlockSpec((k, n), lambda i: (0, 0), **once))
    row_of, col_of = (lambda i: i), (lambda i: 0)
    tile = pl.BlockSpec((tm, n), lambda i: (i, 0))
    in_specs = [pl.BlockSpec((tm, k), lambda i: (i, 0)), w_spec, tile,
                _mod_spec(mod, gate_chunk, n, tm, rows_per_batch, row_of, col_of),
                pl.BlockSpec((1, n), lambda i: (0, 0))]
    args = [x, w, res, mod, gn.reshape(1, n)]
    if not final:
        in_specs += [_mod_spec(mods_next[0], sc_chunk, n, tm, rows_per_batch, row_of, col_of),
                     _mod_spec(mods_next[0], sh_chunk, n, tm, rows_per_batch, row_of, col_of)]
        args += [mods_next[0], mods_next[0]]
    out_specs = [tile] if final else [tile, tile]
    out_shape = ([jax.ShapeDtypeStruct((m, n), F32)] if final
                 else [jax.ShapeDtypeStruct((m, n), F32), jax.ShapeDtypeStruct((m, n), BF)])
    if ride:
        whole = pl.BlockSpec((ms, n), lambda i: (0, 0))
        chunk = lambda c: pl.BlockSpec((ms, n), lambda i: (0, c))
        in_specs += [pl.BlockSpec((ms, k), lambda i: (0, 0)), whole, chunk(gate_chunk)]
        args += [x_s, res_s, mod_s]
        if final:
            out_specs += [whole]
            out_shape += [jax.ShapeDtypeStruct((ms, n), F32)]
        else:
            in_specs += [chunk(sc_chunk), chunk(sh_chunk)]
            args += [mods_next[1], mods_next[1]]
            out_specs += [whole, whole]
            out_shape += [jax.ShapeDtypeStruct((ms, n), F32), jax.ShapeDtypeStruct((ms, n), BF)]
    outs = pl.pallas_call(
        functools.partial(_mm_res_norm_kernel, cast=cast, final=final, ride=ride),
        grid=(m // tm,),
        in_specs=in_specs,
        out_specs=out_specs,
        out_shape=out_shape,
        scratch_shapes=([pltpu.VMEM((k, n), BF)] if cast else []) + ([pltpu.VMEM((tm, n), F32)] if final else []),
        compiler_params=_cparams("arbitrary", vmem=VMEM_LIMIT_BIG),
        name=name,
    )(*args)
    if not ride:
        return outs[0] if final else (outs[0], outs[1])
    return tuple(outs) if final else ((outs[0], outs[1]), (outs[2], outs[3]))


def _ffn1_kernel(x_ref, xs_ref, wg_ref, wu_ref, o_ref, os_ref, wgbf, wubf):
    _cast_weights_once([(wg_ref, wgbf), (wu_ref, wubf)])

    def compute(rows):
        g = jnp.dot(rows, wgbf[...], preferred_element_type=F32)
        u = jnp.dot(rows, wubf[...], preferred_element_type=F32)
        return (g * _sigmoid(g) * u,)

    def store(main, rider):
        o_ref[...] = main[0].astype(o_ref.dtype)
        if rider is not None:
            os_ref[...] = rider[0].astype(os_ref.dtype)

    _with_rider(x_ref, xs_ref, compute, store)


def _ffn1(x, xs, w_gu, l):
    m, k = x.shape
    ms = xs.shape[0]
    tn = 512
    nj = D_FF // tn
    tm = min(TM_WIDE, m)
    return pl.pallas_call(
        _ffn1_kernel,
        grid=(nj, m // tm),
        in_specs=[pl.BlockSpec((tm, k), lambda j, i: (i, 0)),
                  pl.BlockSpec((ms, k), lambda j, i: (0, 0)),
                  pl.BlockSpec((None, k, tn), lambda j, i: (l, 0, j)),
                  pl.BlockSpec((None, k, tn), lambda j, i: (l, 0, nj + j))],
        out_specs=[pl.BlockSpec((tm, tn), lambda j, i: (i, j)),
                   pl.BlockSpec((ms, tn), lambda j, i: (0, j))],
        out_shape=[jax.ShapeDtypeStruct((m, D_FF), BF), jax.ShapeDtypeStruct((ms, D_FF), BF)],
        scratch_shapes=[pltpu.VMEM((k, tn), BF), pltpu.VMEM((k, tn), BF)],
        compiler_params=_cparams("arbitrary", "arbitrary"),
        name="ffn_gate_up",
    )(x, xs, w_gu, w_gu)


def _merge_kernel(xa_ref, xb_ref, xc_ref, g0_ref, g1_ref, g2_ref, sa_ref, sb_ref, sc_ref, t0_ref, t1_ref, t2_ref,
                  wa_ref, wb_ref, wc_ref, o_ref, os_ref, wabf, wbbf, wcbf):
    _cast_weights_once([(wa_ref, wabf), (wb_ref, wbbf), (wc_ref, wcbf)])
    tm = xa_ref.shape[0]

    def dots(xa, xb, xc):
        return (jnp.dot(xa, wabf[...], preferred_element_type=F32),
                jnp.dot(xb, wbbf[...], preferred_element_type=F32),
                jnp.dot(xc, wcbf[...], preferred_element_type=F32))

    def gated(d, gates):
        return sum(_sigmoid(g[...].astype(F32)) * x for g, x in zip(gates, d))

    @pl.when(pl.program_id(1) == 0)
    def _():
        cat = lambda a, b: jnp.concatenate([a[...].astype(BF), b[...].astype(BF)], axis=0)
        d = dots(cat(xa_ref, sa_ref), cat(xb_ref, sb_ref), cat(xc_ref, sc_ref))
        o_ref[...] = gated([x[:tm] for x in d], (g0_ref, g1_ref, g2_ref)).astype(o_ref.dtype)
        os_ref[...] = gated([x[tm:] for x in d], (t0_ref, t1_ref, t2_ref)).astype(os_ref.dtype)

    @pl.when(pl.program_id(1) != 0)
    def _():
        d = dots(xa_ref[...].astype(BF), xb_ref[...].astype(BF), xc_ref[...].astype(BF))
        o_ref[...] = gated(d, (g0_ref, g1_ref, g2_ref)).astype(o_ref.dtype)


def _merge(prompt, sample, w_a, w_b, w_c, l):
    xa, xb, xc, p = prompt
    sa, sb, sc, ps = sample
    ms = sa.shape[0]
    m = xa.shape[0]
    tn = 1024
    tm = min(TM, m)
    ka, kb, kc = xa.shape[1], xb.shape[1], xc.shape[1]
    g_blk = C_G // tn
    per = D_MODEL // tn
    once = dict(pipeline_mode=pl.Buffered(1))
    return pl.pallas_call(
        _merge_kernel,
        grid=(D_MODEL // tn, m // tm),
        in_specs=[pl.BlockSpec((tm, ka), lambda j, i: (i, 0)),
                  pl.BlockSpec((tm, kb), lambda j, i: (i, 0)),
                  pl.BlockSpec((tm, kc), lambda j, i: (i, 0)),
                  pl.BlockSpec((tm, tn), lambda j, i: (i, g_blk + j)),
                  pl.BlockSpec((tm, tn), lambda j, i: (i, g_blk + per + j)),
                  pl.BlockSpec((tm, tn), lambda j, i: (i, g_blk + 2 * per + j)),
                  pl.BlockSpec((ms, ka), lambda j, i: (0, 0)),
                  pl.BlockSpec((ms, kb), lambda j, i: (0, 0)),
                  pl.BlockSpec((ms, kc), lambda j, i: (0, 0)),
                  pl.BlockSpec((ms, tn), lambda j, i: (0, g_blk + j)),
                  pl.BlockSpec((ms, tn), lambda j, i: (0, g_blk + per + j)),
                  pl.BlockSpec((ms, tn), lambda j, i: (0, g_blk + 2 * per + j)),
                  pl.BlockSpec((None, ka, tn), lambda j, i: (l, 0, j), **once),
                  pl.BlockSpec((None, kb, tn), lambda j, i: (l, 0, j), **once),
                  pl.BlockSpec((None, kc, tn), lambda j, i: (l, 0, j), **once)],
        out_specs=[pl.BlockSpec((tm, tn), lambda j, i: (i, j)),
                   pl.BlockSpec((ms, tn), lambda j, i: (0, j))],
        out_shape=[jax.ShapeDtypeStruct((m, D_MODEL), BF), jax.ShapeDtypeStruct((ms, D_MODEL), BF)],
        scratch_shapes=[pltpu.VMEM((ka, tn), BF), pltpu.VMEM((kb, tn), BF), pltpu.VMEM((kc, tn), BF)],
        compiler_params=_cparams("arbitrary", "arbitrary", vmem=VMEM_LIMIT_BIG),
        name="branch_merge",
    )(xa, xb, xc, p, p, p, sa, sb, sc, ps, ps, ps, w_a, w_b, w_c)


def _scan(x, axis, op, fill):
    idx = lax.broadcasted_iota(jnp.int32, x.shape, axis)
    shift = 1
    while shift < x.shape[axis]:
        x = op(x, jnp.where(idx >= shift, pltpu.roll(x, shift, axis), fill))
        shift *= 2
    return x


def _mlstm_kernel(q_ref, k_ref, v_ref, o_ref, gc_ref, gr_ref, brow_ref, bcol_ref,
                  xa_ref, c_ref, n_ref, m_ref, nn_ref):
    chunk = pl.program_id(1)

    @pl.when(chunk == 0)
    def _():
        c_ref[...] = jnp.zeros_like(c_ref)
        m_ref[...] = jnp.zeros_like(m_ref)
        nn_ref[...] = jnp.zeros_like(nn_ref)

    L = q_ref.shape[0]
    nh = ML_HEADS
    scale = ML_DQK ** -0.5
    gc = gc_ref[...] + brow_ref[...]
    gr = gr_ref[...] + bcol_ref[...]
    f_c = pltpu.roll(_scan(_log_sigmoid(gc), 0, jnp.add, 0.0), LANE - nh, 1)
    m_prev = m_ref[...]
    m_t = f_c + jnp.maximum(m_prev, _scan(gc - f_c, 0, jnp.maximum, NEG))
    w_inter = jnp.exp(f_c + m_prev - m_t)
    e_neg_m = jnp.exp(-m_t)
    f_minus_m = f_c - m_t
    m_new = m_t[L - 1:L, :]
    f_last = f_c[L - 1:L, :]
    w_last = jnp.exp(f_last - f_c + gc - m_new) * scale
    decay = jnp.exp(f_last + m_prev - m_new)
    m_ref[...] = m_new
    f_r = _scan(_log_sigmoid(gr), 1, jnp.add, 0.0)
    a_r = gr[0:nh, :] - f_r[nh:2 * nh, :]
    row = lax.broadcasted_iota(jnp.int32, (L, L), 0)
    col = lax.broadcasted_iota(jnp.int32, (L, L), 1)
    tri = col <= row
    ones = jnp.ones((L, ML_DQK), BF)
    nt = (((1,), (1,)), ((), ()))
    tn = (((0,), (0,)), ((), ()))
    for h in range(nh):
        lane = lambda x: x[:, h:h + 1]
        d_w = jnp.exp(jnp.where(tri, lane(f_minus_m) + a_r[h:h + 1, :], NEG))
        q = q_ref[:, h * ML_DQK:(h + 1) * ML_DQK]
        k = k_ref[:, h * ML_DQK:(h + 1) * ML_DQK]
        v = v_ref[:, h * ML_DV:(h + 1) * ML_DV]
        s = (lax.dot_general(q, k, nt, preferred_element_type=F32) * (scale * d_w)).astype(BF)
        c_old = c_ref[h]
        nn_old = nn_ref[h]
        wi = lane(w_inter)
        num = (jnp.dot(s, v, preferred_element_type=F32)
               + wi * jnp.dot(q, c_old.astype(BF), preferred_element_type=F32))
        den = (jnp.dot(s, ones, preferred_element_type=F32)
               + wi * jnp.dot(q, nn_old.astype(BF), preferred_element_type=F32))
        inv = 1.0 / jnp.maximum(jnp.abs(den[:, 0:1]), lane(e_neg_m))
        og = o_ref[:, h * ML_DV:(h + 1) * ML_DV].astype(F32)
        xa_ref[:, h * ML_DV:(h + 1) * ML_DV] = (_sigmoid(og) * (num * inv)).astype(xa_ref.dtype)
        kw = (k.astype(F32) * lane(w_last)).astype(BF)
        dk = decay[:, h:h + 1]
        c_ref[h] = dk * c_old + lax.dot_general(kw, v, tn, preferred_element_type=F32)
        nn_ref[h] = dk * nn_old + lax.dot_general(kw, ones, tn, preferred_element_type=F32)

    @pl.when(chunk == pl.num_programs(1) - 1)
    def _():
        for h in range(nh):
            n_ref[h:h + 1, :] = nn_ref[h].T[0:1, :]


def _mlstm_prompt(p, gates, b_i, b_f, batch, seq):
    m_rows = p.shape[0]
    L = ML_L
    nc = seq // L
    bias = jnp.concatenate([b_i, b_f]).astype(F32)
    bias_row = jnp.zeros((1, N_GATE), F32).at[0, :2 * ML_HEADS].set(bias)
    bias_col = bias.reshape(2 * ML_HEADS, 1)
    gates_t = gates[:, :2 * ML_HEADS].T
    hq, hv = ML_HEADS * ML_DQK, ML_HEADS * ML_DV
    return pl.pallas_call(
        _mlstm_kernel,
        grid=(batch, nc),
        in_specs=[pl.BlockSpec((L, hq), lambda b, c: (b * nc + c, C_QA // hq)),
                  pl.BlockSpec((L, hq), lambda b, c: (b * nc + c, C_KA // hq)),
                  pl.BlockSpec((L, hv), lambda b, c: (b * nc + c, C_VA // hv)),
                  pl.BlockSpec((L, hv), lambda b, c: (b * nc + c, C_OA // hv)),
                  pl.BlockSpec((L, N_GATE), lambda b, c: (b * nc + c, 0)),
                  pl.BlockSpec((2 * ML_HEADS, L), lambda b, c: (0, b * nc + c)),
                  pl.BlockSpec((1, N_GATE), lambda b, c: (0, 0)),
                  pl.BlockSpec((2 * ML_HEADS, 1), lambda b, c: (0, 0))],
        out_specs=[pl.BlockSpec((L, hv), lambda b, c: (b * nc + c, 0)),
                   pl.BlockSpec((None, ML_HEADS, ML_DQK, ML_DV), lambda b, c: (b, 0, 0, 0)),
                   pl.BlockSpec((None, ML_HEADS, ML_DQK), lambda b, c: (b, 0, 0)),
                   pl.BlockSpec((None, 1, LANE), lambda b, c: (b, 0, 0))],
        out_shape=[jax.ShapeDtypeStruct((m_rows, hv), BF),
                   jax.ShapeDtypeStruct((batch, ML_HEADS, ML_DQK, ML_DV), F32),
                   jax.ShapeDtypeStruct((batch, ML_HEADS, ML_DQK), F32),
                   jax.ShapeDtypeStruct((batch, 1, LANE), F32)],
        scratch_shapes=[pltpu.VMEM((ML_HEADS, ML_DQK, ML_DQK), F32)],
        compiler_params=_cparams("parallel", "arbitrary"),
        name="mlstm_prompt",
    )(p, p, p, p, gates, gates_t, bias_row, bias_col)


def _rope_table(pos):
    half = SW_ROT // 2
    freq = ROPE_THETA ** (-jnp.arange(half, dtype=F32) * 2.0 / SW_ROT)
    ang = pos.astype(F32)[:, None] * freq[None, :]
    cos, sin = jnp.cos(ang), jnp.sin(ang)
    t = pos.shape[0]
    return jnp.concatenate([cos, cos, jnp.ones((t, SW_HD - SW_ROT), F32),
                            -sin, sin, jnp.zeros((t, SW_HD - SW_ROT), F32)], axis=1)


def _rope(x, tab):
    half = SW_ROT // 2
    lane = lax.broadcasted_iota(jnp.int32, x.shape, 1)
    partner = jnp.where(lane < SW_ROT, lane ^ half, lane)
    return x * tab[:, 0:SW_HD] + jnp.take_along_axis(x, partner, axis=1) * tab[:, SW_HD:2 * SW_HD]


def _attend(q, k_own, v_own, k_prev, v_prev, prev_shift):
    blk = SW_BLK
    row = lax.broadcasted_iota(jnp.int32, (1, blk, blk), 1)
    col = lax.broadcasted_iota(jnp.int32, (1, blk, blk), 2)
    qk = (((2,), (2,)), ((0,), (0,)))
    pv = (((2,), (1,)), ((0,), (0,)))
    s_o = jnp.where(col <= row, lax.dot_general(q, k_own, qk, preferred_element_type=F32), NEG)
    s_p = jnp.where(col >= row + prev_shift, lax.dot_general(q, k_prev, qk, preferred_element_type=F32), NEG)
    mx = jnp.maximum(jnp.max(s_o, axis=2, keepdims=True), jnp.max(s_p, axis=2, keepdims=True))
    p_o = jnp.exp(s_o - mx)
    p_p = jnp.exp(s_p - mx)
    den = jnp.sum(p_o, axis=2, keepdims=True) + jnp.sum(p_p, axis=2, keepdims=True)
    acc = (lax.dot_general(p_o.astype(BF), v_own, pv, preferred_element_type=F32)
           + lax.dot_general(p_p.astype(BF), v_prev, pv, preferred_element_type=F32))
    return acc, mx, den


def _swa_kernel(q1, k1, v1, k1p, v1p, q2, k2, v2, k2p, v2p, q3, k3, v3, k3p, v3p, xb_ref, o_acc, m_acc, l_acc):
    blk = SW_BLK
    u = SWA_UNITS
    first_shift = jnp.where(pl.program_id(1) > 0, 0, blk)
    wide = lambda x: jnp.broadcast_to(x, x.shape[:-1] + (SW_HD,))

    def merge(rows, acc, mx, den):
        m_old = m_acc[rows, :]
        m_new = jnp.maximum(m_old, mx)
        a = jnp.exp(m_old - m_new)
        b = jnp.exp(mx - m_new)
        o_acc[rows, :] = a * o_acc[rows, :] + b * acc
        l_acc[rows, :] = a * l_acc[rows, :] + b * den
        m_acc[rows, :] = m_new

    def g1_batch(rows, k_p, v_p, shift):
        split = lambda x: x.reshape(u, blk, SW_HD)
        acc, mx, den = _attend(split(q1[rows, :]), split(k1[rows, :]), split(v1[rows, :]),
                               split(k_p), split(v_p), shift)
        o_acc[rows, :] = acc.reshape(u * blk, SW_HD)
        m_acc[rows, :] = wide(mx).reshape(u * blk, SW_HD)
        l_acc[rows, :] = wide(den).reshape(u * blk, SW_HD)

    head = pl.ds(0, (u - 1) * blk)
    unit = lax.broadcasted_iota(jnp.int32, (u, 1, 1), 0)
    g1_batch(pl.ds(0, u * blk),
             jnp.concatenate([k1p[...], k1[head, :]], axis=0), jnp.concatenate([v1p[...], v1[head, :]], axis=0),
             jnp.where(unit == 0, first_shift, 0))

    def g1_body(i, c):
        rows = pl.ds(pl.multiple_of(i * u * blk, blk), u * blk)
        prev = pl.ds(pl.multiple_of(i * u * blk - blk, blk), u * blk)
        g1_batch(rows, k1[prev, :], v1[prev, :], 0)
        return c

    lax.fori_loop(1, q1.shape[0] // (u * blk), g1_body, 0)

    def dilated(q, k, v, kp, vp):
        dil = q.shape[0]
        n_sub = q.shape[1] // blk

        def batch(sub, r0, k_p, v_p, shift):
            rs = pl.ds(r0, u)
            rows = pl.ds(pl.multiple_of(sub * blk, blk), blk)
            acc, mx, den = _attend(q[rs, rows, :], k[rs, rows, :], v[rs, rows, :], k_p, v_p, shift)
            for i in range(u):
                merge(pl.ds(sub * blk * dil + r0 + i, blk, stride=dil), acc[i], wide(mx[i]), wide(den[i]))

        def first(i, c):
            rs = pl.ds(i * u, u)
            batch(0, i * u, kp[rs], vp[rs], first_shift)
            return c

        lax.fori_loop(0, dil // u, first, 0)

        def rest(i, c):
            sub = 1 + i // (dil // u)
            r0 = (i % (dil // u)) * u
            prev = pl.ds(pl.multiple_of((sub - 1) * blk, blk), blk)
            batch(sub, r0, k[pl.ds(r0, u), prev, :], v[pl.ds(r0, u), prev, :], 0)
            return c

        lax.fori_loop(0, (n_sub - 1) * (dil // u), rest, 0)

    dilated(q2, k2, v2, k2p, v2p)
    dilated(q3, k3, v3, k3p, v3p)
    xb_ref[...] = (o_acc[...] * (1.0 / l_acc[...])).astype(xb_ref.dtype)


def _swa_prompt(qkvs, batch, seq):
    blk = SW_BLK
    span = SW_GROUPS[-1][0]
    assert seq % span == 0 and all(w // d == blk for w, d in SW_GROUPS)
    nspan = seq // span
    in_specs = []
    args = []
    for (win, dil), qkv in zip(SW_GROUPS, qkvs):
        rows = span // dil
        per = rows // blk
        own = lambda c: pl.BlockSpec((None, dil, rows, SW_HD), lambda b, s, h, c=c: (b, 0, s, c * SW_HPG + h))
        prev = lambda c: pl.BlockSpec(
            (None, dil, blk, SW_HD), lambda b, s, h, c=c, per=per: (b, 0, jnp.maximum(s * per - 1, 0), c * SW_HPG + h))
        if dil == 1:
            own = lambda c: pl.BlockSpec((None, None, span, SW_HD), lambda b, s, h, c=c: (b, 0, s, c * SW_HPG + h))
            prev = lambda c: pl.BlockSpec(
                (None, None, blk, SW_HD),
                lambda b, s, h, c=c, per=per: (b, 0, jnp.maximum(s * per - 1, 0), c * SW_HPG + h))
        in_specs += [own(0), own(1), own(2), prev(1), prev(2)]
        args += [qkv] * 5
    return pl.pallas_call(
        _swa_kernel,
        grid=(batch, nspan, SW_HPG),
        in_specs=in_specs,
        out_specs=pl.BlockSpec((span, SW_HD), lambda b, s, h: (b * nspan + s, h)),
        out_shape=jax.ShapeDtypeStruct((batch * seq, SW_HPG * SW_HD), BF),
        scratch_shapes=[pltpu.VMEM((span, SW_HD), F32)] * 3,
        compiler_params=_cparams("parallel", "parallel", "parallel"),
        name="swa_prompt",
    )(*args)


def _kv_rows_kernel(k_ref, v_ref, o_ref, *, dil):
    for r in range(dil):
        rows = pl.ds(r, SW_BLK, stride=dil) if dil > 1 else pl.ds(0, SW_BLK)
        for hh in range(SW_HPG):
            sl = slice(hh * SW_HD, (hh + 1) * SW_HD)
            o_ref[hh, rows, :] = k_ref[r, :, sl].astype(F32)
            o_ref[SW_HPG + hh, rows, :] = v_ref[r, :, sl].astype(F32)


def _kv_rows(qkv, gi, batch, seq):
    win, dil = SW_GROUPS[gi]
    hw = SW_HPG * SW_HD
    last = seq // dil // SW_BLK - 1
    out = pl.pallas_call(
        functools.partial(_kv_rows_kernel, dil=dil),
        grid=(batch,),
        in_specs=[pl.BlockSpec((None, dil, SW_BLK, hw), lambda b: (b, 0, last, 1)),
                  pl.BlockSpec((None, dil, SW_BLK, hw), lambda b: (b, 0, last, 2))],
        out_specs=pl.BlockSpec((None, 2 * SW_HPG, win, SW_HD), lambda b: (b, 0, 0, 0)),
        out_shape=jax.ShapeDtypeStruct((batch, 2 * SW_HPG, win, SW_HD), F32),
        compiler_params=_cparams("parallel"),
        name=f"kv_rows_w{win}",
    )(qkv, qkv)
    return out.reshape(batch, 2, SW_HPG, win, SW_HD).transpose(0, 3, 1, 2, 4)


def _cm_kernel(u_ref, v_ref, gv_ref, ws_ref, bs_ref, o_ref):
    ch = CM_CHUNK
    row = lax.broadcasted_iota(jnp.int32, (ch, ch), 0)
    col = lax.broadcasted_iota(jnp.int32, (ch, ch), 1)
    tri = col <= row
    for g in range(CM_GROUPS):
        sl = slice(g * CM_GD, (g + 1) * CM_GD)
        w = jnp.where(tri, ws_ref[g], 0.0).astype(BF)
        for c in range(u_ref.shape[0] // ch):
            rows = slice(c * ch, (c + 1) * ch)
            vg = v_ref[rows, sl].astype(F32)
            vn = vg * lax.rsqrt(jnp.mean(vg * vg, axis=1, keepdims=True) + RMS_EPS) * gv_ref[g:g + 1, :]
            mixed = jnp.dot(w, vn.astype(BF), preferred_element_type=F32) + bs_ref[:, g:g + 1]
            o_ref[rows, sl] = (u_ref[rows, sl].astype(F32) * mixed).astype(o_ref.dtype)


def _cm_prompt(p, g_v, w_s, b_s, l):
    m_rows = p.shape[0]
    ch = CM_CHUNK
    width = CM_GROUPS * CM_GD
    rows = 2 * ch
    return pl.pallas_call(
        _cm_kernel,
        grid=(m_rows // rows,),
        in_specs=[pl.BlockSpec((rows, width), lambda i: (i, C_UC // width)),
                  pl.BlockSpec((rows, width), lambda i: (i, C_VC // width)),
                  pl.BlockSpec((None, CM_GROUPS, CM_GD), lambda i: (l, 0, 0)),
                  pl.BlockSpec((None, CM_GROUPS, ch, ch), lambda i: (l, 0, 0, 0)),
                  pl.BlockSpec((ch, CM_GROUPS), lambda i: (0, 0))],
        out_specs=pl.BlockSpec((rows, width), lambda i: (i, 0)),
        out_shape=jax.ShapeDtypeStruct((m_rows, width), BF),
        compiler_params=_cparams("parallel"),
        name="cm_prompt",
    )(p, p, g_v, w_s, b_s[l].T)


def _mlstm_step_kernel(qr_ref, kr_ref, vr_ref, or_ref, qkc_ref, g_ref, brow_ref, m0_ref, c0_ref, n0_ref,
                       xa_ref, c1_ref, n1_ref, m1_ref):
    scale = ML_DQK ** -0.5
    gates = g_ref[...] + brow_ref[...]
    m0 = m0_ref[...]
    for h in range(ML_HEADS):
        ig = gates[:, h:h + 1]
        lf = _log_sigmoid(gates[:, ML_HEADS + h:ML_HEADS + h + 1])
        m_prev = m0[:, h:h + 1]
        inter = lf + m_prev
        m_t = jnp.maximum(inter, ig)
        d_w = jnp.exp(ig - m_t)
        w_inter = jnp.exp(inter - m_t)
        q_row = qr_ref[h:h + 1, :]
        k_row = kr_ref[h:h + 1, :]
        v_row = vr_ref[h:h + 1, :]
        q_col = qkc_ref[:, h:h + 1]
        k_col = qkc_ref[:, ML_HEADS + h:ML_HEADS + h + 1]
        c_old = c0_ref[h]
        n_old = n0_ref[h:h + 1, :]
        s = jnp.sum(q_row * k_row, axis=1, keepdims=True) * scale * d_w
        qc = jnp.sum(q_col * c_old, axis=0, keepdims=True)
        num = s * v_row + w_inter * qc
        den = s + w_inter * jnp.sum(q_row * n_old, axis=1, keepdims=True)
        hh = num / jnp.maximum(jnp.abs(den), jnp.exp(-m_t))
        xa_ref[:, h * ML_DV:(h + 1) * ML_DV] = _sigmoid(or_ref[h:h + 1, :]) * hh
        w_last = d_w * scale
        decay = w_inter
        c1_ref[h] = decay * c_old + (k_col * w_last) * v_row
        n1_ref[h:h + 1, :] = decay * n_old + w_last * k_row
        m1_ref[h:h + 1, :] = jnp.broadcast_to(m_t, (1, ML_DQK))


def _mlstm_sample(ps, gates, b_i, b_f, c0, n0, m0, l):
    nb = c0.shape[1]
    q = ps[:nb, C_QA:C_QA + 1024].reshape(nb, ML_HEADS, ML_DQK)
    k = ps[:nb, C_KA:C_KA + 1024].reshape(nb, ML_HEADS, ML_DQK)
    v = ps[:nb, C_VA:C_VA + 2048].reshape(nb, ML_HEADS, ML_DV)
    o = ps[:nb, C_OA:C_OA + 2048].reshape(nb, ML_HEADS, ML_DV)
    qk_col = jnp.concatenate([q, k], axis=1).transpose(0, 2, 1)
    bias = jnp.concatenate([b_i, b_f]).astype(F32)
    bias_row = jnp.zeros((1, N_GATE), F32).at[0, :2 * ML_HEADS].set(bias)
    g3 = gates[:nb].reshape(nb, 1, N_GATE)
    m03 = m0[l].reshape(nb, 1, ML_HEADS)
    per_b3 = lambda shape: pl.BlockSpec((None,) + shape, lambda b: (b, 0, 0))
    return pl.pallas_call(
        _mlstm_step_kernel,
        grid=(nb,),
        in_specs=[per_b3((ML_HEADS, ML_DQK)), per_b3((ML_HEADS, ML_DQK)),
                  per_b3((ML_HEADS, ML_DV)), per_b3((ML_HEADS, ML_DV)),
                  per_b3((ML_DQK, 2 * ML_HEADS)), per_b3((1, N_GATE)),
                  pl.BlockSpec((1, N_GATE), lambda b: (0, 0)),
                  per_b3((1, ML_HEADS)),
                  pl.BlockSpec((None, None, ML_HEADS, ML_DQK, ML_DV), lambda b: (l, b, 0, 0, 0)),
                  pl.BlockSpec((None, None, ML_HEADS, ML_DQK), lambda b: (l, b, 0, 0))],
        out_specs=[per_b3((1, ML_HEADS * ML_DV)),
                   pl.BlockSpec((None, ML_HEADS, ML_DQK, ML_DV), lambda b: (b, 0, 0, 0)),
                   per_b3((ML_HEADS, ML_DQK)), per_b3((ML_HEADS, ML_DQK))],
        out_shape=[jax.ShapeDtypeStruct((nb, 1, ML_HEADS * ML_DV), F32),
                   jax.ShapeDtypeStruct((nb, ML_HEADS, ML_DQK, ML_DV), F32),
                   jax.ShapeDtypeStruct((nb, ML_HEADS, ML_DQK), F32),
                   jax.ShapeDtypeStruct((nb, ML_HEADS, ML_DQK), F32)],
        compiler_params=_cparams("parallel"),
        name="mlstm_sample",
    )(q, k, v, o, qk_col, g3, bias_row, m03, c0, n0)


def _swa_step_kernel(g0_ref, g1_ref, g2_ref, c0_ref, c1_ref, c2_ref, xb_ref):
    hw = SW_HPG * SW_HD
    per_pos = 2 * SW_HPG
    news = (g0_ref, g1_ref, g2_ref)
    caches = (c0_ref, c1_ref, c2_ref)
    for hh in range(SW_HPG):
        m_run = l_run = acc = None
        for gi in range(len(SW_GROUPS)):
            lo = hh * SW_HD
            q = news[gi][:, lo:lo + SW_HD]
            k_new = news[gi][:, hw + lo:hw + lo + SW_HD]
            v_new = news[gi][:, 2 * hw + lo:2 * hw + lo + SW_HD]
            kc = caches[gi][:, hh, :]
            vc = caches[gi][:, SW_HPG + hh, :]
            s_c = jnp.sum(kc * q, axis=1, keepdims=True)
            s_n = jnp.sum(k_new * q, axis=1, keepdims=True)
            mx = jnp.maximum(jnp.max(s_c, axis=0, keepdims=True), s_n)
            p_c = jnp.exp(s_c - mx)
            p_n = jnp.exp(s_n - mx)
            den = jnp.sum(p_c, axis=0, keepdims=True) + p_n
            part = jnp.sum(p_c * vc, axis=0, keepdims=True) + p_n * v_new
            if gi == 0:
                m_run, l_run, acc = mx, den, part
            else:
                top = jnp.maximum(m_run, mx)
                a, b = jnp.exp(m_run - top), jnp.exp(mx - top)
                m_run, l_run, acc = top, a * l_run + b * den, a * acc + b * part
        xb_ref[:, hh * SW_HD:(hh + 1) * SW_HD] = acc / l_run


def _swa_sample(qkvs, caches, l):
    nb = caches[0].shape[1]
    hw = SW_HPG * SW_HD
    per_pos = 2 * SW_HPG
    news, views, specs = [], [], []
    for (win, dil), cache, qkv in zip(SW_GROUPS, caches, qkvs):
        assert cache.shape[2] == win and win // dil == SW_BLK
        news.append(qkv[0, 0, :nb].reshape(nb, 1, 3 * hw))
        views.append(cache.reshape(cache.shape[0], nb, win // dil, dil * per_pos, SW_HD))
        specs.append(pl.BlockSpec((None, None, SW_BLK, per_pos, SW_HD), lambda b: (l, b, 0, 0, 0)))
    row3 = pl.BlockSpec((None, 1, 3 * hw), lambda b: (b, 0, 0))
    xb = pl.pallas_call(
        _swa_step_kernel,
        grid=(nb,),
        in_specs=[row3, row3, row3] + specs,
        out_specs=pl.BlockSpec((None, 1, hw), lambda b: (b, 0, 0)),
        out_shape=jax.ShapeDtypeStruct((nb, 1, hw), F32),
        compiler_params=_cparams("parallel"),
        name="swa_sample",
    )(*news, *views)
    kvs = [jnp.stack([n[:, :, hw:2 * hw].reshape(nb, 1, SW_HPG, SW_HD),
                      n[:, :, 2 * hw:].reshape(nb, 1, SW_HPG, SW_HD)], axis=2) for n in news]
    return xb.reshape(nb, hw), kvs


def _cm_step_kernel(u_ref, v_ref, gv_ref, ws_ref, bs_ref, o_ref, vn_ref):
    for g in range(CM_GROUPS):
        sl = slice(g * CM_GD, (g + 1) * CM_GD)
        vg = v_ref[:, sl]
        vn = vg * lax.rsqrt(jnp.mean(vg * vg, axis=1, keepdims=True) + RMS_EPS) * gv_ref[g:g + 1, :]
        vn_ref[:, sl] = vn
        w00 = ws_ref[g][0:1, 0:1]
        o_ref[:, sl] = u_ref[:, sl] * (w00 * vn + bs_ref[g:g + 1, 0:1])


def _cm_sample(ps, g_v, w_s, b_s, l):
    rows = ps.shape[0]
    width = CM_GROUPS * CM_GD
    ch = CM_CHUNK
    return pl.pallas_call(
        _cm_step_kernel,
        grid=(1,),
        in_specs=[pl.BlockSpec((rows, width), lambda i: (0, C_UC // width)),
                  pl.BlockSpec((rows, width), lambda i: (0, C_VC // width)),
                  pl.BlockSpec((None, CM_GROUPS, CM_GD), lambda i: (l, 0, 0)),
                  pl.BlockSpec((None, CM_GROUPS, ch, ch), lambda i: (l, 0, 0, 0)),
                  pl.BlockSpec((None, CM_GROUPS, ch), lambda i: (l, 0, 0))],
        out_specs=[pl.BlockSpec((rows, width), lambda i: (0, 0))] * 2,
        out_shape=[jax.ShapeDtypeStruct((rows, width), F32)] * 2,
        compiler_params=_cparams("arbitrary"),
        name="cm_sample",
    )(ps, ps, g_v, w_s, b_s)


def _in_proj(hp, hs, w_in, l, tabs_p, tabs_s, batch, dils):
    wt = jnp.swapaxes(w_in, 1, 2)
    p_a, s_a = _mm_t(hp, hs, wt, l, 1024, 0, O_GATES // 1024, BF, "in_proj_a")
    p_g, s_g = _mm_t(hp, hs, wt, l, N_GATE, O_GATES, 1, F32, "gate_proj")
    p_qkv = [_qkv_proj(hp, wt, l, gi, dil, tabs_p, batch, BF, f"qkv_proj{gi}") for gi, dil in enumerate(dils)]
    s_qkv = [_qkv_proj(hs, wt, l, gi, 1, tabs_s, 1, F32, f"qkv_proj{gi}_s") for gi in range(len(dils))]
    p_r, s_r = _mm_t(hp, hs, wt, l, 1024, O_UC, N_REST // 1024, BF, "in_proj_r")
    return (p_a, p_g, p_qkv, p_r), (s_a, s_g, s_qkv, s_r)


def _dense_tail(xs, mixers, ps, l, mods, rows_per_batch, wts, gain, mods_next):
    w_down = _cast_bf16(wts["w_ffn_down"], l)
    merged = _merge(mixers[0] + (ps[0],), mixers[1] + (ps[1],), wts["w_br_a"], wts["w_br_b"], wts["w_br_c"], l)
    (x_p, h2_p), (x_s, h2_s) = _mm_res_norm(merged, wts["w_out"], l, xs, mods, 2, rows_per_batch,
                                            wts["g_norm2"][l], mods, 4, 3, "out_proj", ride=False)
    acts = _ffn1(h2_p, h2_s, wts["w_ffn_gate_up"], l)
    return _mm_res_norm(acts, w_down, l, (x_p, x_s), mods, 5, rows_per_batch, gain, mods_next, 1, 0, "ffn_down",
                        ride=False)


def kernel(x_prompt, x_sample, state_mlstm_C, state_mlstm_n, state_mlstm_m, cache_swa_kv_w128, cache_swa_kv_w512, cache_swa_kv_w2048, c_prompt, c_sample, w_ada, b_ada, g_norm1, g_norm2, g_final, w_in, b_ml_i, b_ml_f, g_cm_v, w_s, b_s, w_br_a, w_br_b, w_br_c, w_out, w_ffn_gate_up, w_ffn_down):
    bp, tp, d = x_prompt.shape
    ns, ts, _ = x_sample.shape
    depth = w_in.shape[0]
    past_len = 16384
    assert d == D_MODEL and ts == 1 and ns == 8 and tp % max(ML_L, SW_GROUPS[-1][0]) == 0
    s_rows = 16
    wts = dict(w_br_a=w_br_a, w_br_b=w_br_b, w_br_c=w_br_c, w_out=w_out, g_norm2=g_norm2,
               w_ffn_gate_up=w_ffn_gate_up, w_ffn_down=w_ffn_down)

    c_all = jnp.zeros((s_rows, d), F32).at[:ns].set(c_sample).at[ns:ns + bp].set(c_prompt)
    mod_all = _ada(c_all, w_ada, b_ada)
    tabs_p = _rope_table(jnp.arange(tp, dtype=jnp.int32))
    tabs_s = _rope_table(jnp.full((s_rows,), past_len, jnp.int32))
    caches = (cache_swa_kv_w128, cache_swa_kv_w512, cache_swa_kv_w2048)
    dils = [dil for _, dil in SW_GROUPS]

    xp = x_prompt.reshape(bp * tp, d)
    xs = jnp.zeros((s_rows, d), F32).at[:ns].set(x_sample.reshape(ns, d))
    p_c, p_n, p_m, p_kv = [], [], [], [[], [], []]
    s_c, s_n, s_m, s_kv, s_v = [], [], [], [[], [], []], []
    mods_p = [mod_all[l, ns:ns + bp].reshape(bp, 1, 6 * d) for l in range(depth)]
    mods_s = [mod_all[l] for l in range(depth)]
    hp = _norm_mod(xp, g_norm1[0], mods_p[0], 1, 0, tp)
    hs = _norm_mod(xs, g_norm1[0], mods_s[0], 1, 0, None)
    for l in range(depth):
        last = l == depth - 1
        (p_a, gates, qkvs, p_r), (s_a, s_gates, s_qkvs, s_r) = _in_proj(hp, hs, w_in, l, tabs_p, tabs_s, bp, dils)
        xa, c1, n1, m1 = _mlstm_prompt(p_a, gates, b_ml_i[l], b_ml_f[l], bp, tp)
        xb = _swa_prompt(qkvs, bp, tp)
        kvs = [_kv_rows(qkvs[gi], gi, bp, tp) for gi in range(len(SW_GROUPS))]
        xc = _cm_prompt(p_r, g_cm_v, w_s, b_s, l)
        mix_p = (xa, xb, xc)
        p_c.append(c1)
        p_n.append(n1)
        p_m.append(m1[:, 0, :ML_HEADS])
        for gi in range(3):
            p_kv[gi].append(kvs[gi])
        xa, c1, n1, m1 = _mlstm_sample(s_a, s_gates, b_ml_i[l], b_ml_f[l], state_mlstm_C, state_mlstm_n,
                                       state_mlstm_m, l)
        xb, kvs = _swa_sample(s_qkvs, caches, l)
        xc, vn = _cm_sample(s_r, g_cm_v, w_s, b_s, l)
        pad = lambda a: jnp.zeros((s_rows, a.shape[-1]), F32).at[:ns].set(a.reshape(ns, -1))
        mix_s = (pad(xa), pad(xb), xc)
        outs = _dense_tail((xp, xs), (mix_p, mix_s), (p_r, s_r), l, (mods_p[l], mods_s[l]), tp, wts,
                           g_final if last else g_norm1[l + 1],
                           None if last else (mods_p[l + 1], mods_s[l + 1]))
        if last:
            y_p = outs[0].reshape(bp, tp, d)
            y_s = outs[1][:ns].reshape(ns, ts, d)
        else:
            (xp, hp), (xs, hs) = outs
        s_c.append(c1)
        s_n.append(n1)
        s_m.append(m1[:, :, 0])
        for gi in range(3):
            s_kv[gi].append(kvs[gi])
        s_v.append(vn[:ns].reshape(ns, 1, CM_GROUPS, CM_GD))
    st = jnp.stack
    return (y_p, y_s, st(p_c), st(p_n), st(p_m), st(p_kv[0]), st(p_kv[1]), st(p_kv[2]),
            st(s_c), st(s_n), st(s_m), st(s_kv[0]), st(s_kv[1]), st(s_kv[2]), st(s_v))
```

```python
import functools

import jax
import jax.numpy as jnp
from jax import lax
from jax.experimental import pallas as pl
from jax.experimental.pallas import tpu as pltpu

BF = jnp.bfloat16
F32 = jnp.float32

D_MODEL = 2048
ML_HEADS = 8
ML_DQK = 128
ML_DV = 256
SW_GROUPS = ((128, 1), (512, 4), (2048, 16))
SW_HPG = 4
SW_HEADS = 12
SW_HD = 128
SW_ROT = SW_HD // 4
SW_BLK = 128
ROPE_THETA = 500000.0
CM_CHUNK = 128
CM_GROUPS = 8
CM_GD = 256
D_FF = 5632
RMS_EPS = 1e-6
NEG = -1e30

LANE = 128
O_GATES = 6144
O_QB = O_GATES + 2 * ML_HEADS
O_UC = O_QB + 3 * SW_HEADS * SW_HD
N_REST = 2 * CM_GROUPS * CM_GD + 3 * D_MODEL
N_GATE = LANE
C_QA, C_KA, C_VA, C_OA = 0, 1024, 2048, 4096
C_UC, C_VC, C_G = 0, 2048, 4096

TM = 512
TM_WIDE = 1024
QKV_SUB = 256
SWA_UNITS = 8
ML_L = 256
VMEM_LIMIT = 48 * 1024 * 1024
VMEM_LIMIT_BIG = 52 * 1024 * 1024


def _cparams(*sem, vmem=VMEM_LIMIT):
    return pltpu.CompilerParams(dimension_semantics=sem, vmem_limit_bytes=vmem)


def _sigmoid(x):
    return 0.5 * (jnp.tanh(0.5 * x) + 1.0)


def _log_sigmoid(x):
    return jnp.minimum(x, 0.0) - jnp.log1p(jnp.exp(-jnp.abs(x)))


def _ada_kernel(c_ref, w_ref, b_ref, o_ref):
    c = c_ref[...]
    a = (c * _sigmoid(c)).astype(BF)
    o_ref[...] = jnp.dot(a, w_ref[...].astype(BF), preferred_element_type=F32) + b_ref[...]


def _ada(c_all, w_ada, b_ada):
    depth, d, n = w_ada.shape
    rows = c_all.shape[0]
    tn = 1024
    return pl.pallas_call(
        _ada_kernel,
        grid=(depth, n // tn),
        in_specs=[pl.BlockSpec((rows, d), lambda l, j: (0, 0)),
                  pl.BlockSpec((None, d, tn), lambda l, j: (l, 0, j)),
                  pl.BlockSpec((None, 1, tn), lambda l, j: (l, 0, j))],
        out_specs=pl.BlockSpec((None, rows, tn), lambda l, j: (l, 0, j)),
        out_shape=jax.ShapeDtypeStruct((depth, rows, n), F32),
        compiler_params=_cparams("parallel", "parallel"),
        name="ada",
    )(c_all, w_ada, b_ada.reshape(depth, 1, n))


def _mod_spec(mod, chunk, width, tm, rows_per_batch, row_of, col_of):
    per = D_MODEL // width
    if mod.ndim == 3:
        tiles_per_batch = rows_per_batch // tm
        return pl.BlockSpec((None, 1, width),
                            lambda *g: (row_of(*g) // tiles_per_batch, 0, chunk * per + col_of(*g)))
    return pl.BlockSpec((tm, width), lambda *g: (row_of(*g), chunk * per + col_of(*g)))


NORM_ROWS = 16


def _row_groups(n_rows, body):
    def step(i, c):
        body(pl.ds(pl.multiple_of(i * NORM_ROWS, NORM_ROWS), NORM_ROWS))
        return c
    groups = n_rows // NORM_ROWS
    lax.fori_loop(0, groups, step, 0, unroll=min(groups, 4))


def _norm_mod_kernel(x_ref, g_ref, sc_ref, sh_ref, o_ref):
    per_row = sc_ref.shape[0] > 1

    def body(rows):
        x = x_ref[rows, :]
        y = x * lax.rsqrt(jnp.mean(x * x, axis=1, keepdims=True) + RMS_EPS) * g_ref[...]
        sc = sc_ref[rows, :] if per_row else sc_ref[...]
        sh = sh_ref[rows, :] if per_row else sh_ref[...]
        o_ref[rows, :] = (y * (1.0 + sc) + sh).astype(o_ref.dtype)

    _row_groups(x_ref.shape[0], body)


def _norm_mod(x, g, mod, sc_chunk, sh_chunk, rows_per_batch):
    m, d = x.shape
    tm = min(TM, m)
    row_of = lambda i: i
    col_of = lambda i: 0
    return pl.pallas_call(
        _norm_mod_kernel,
        grid=(m // tm,),
        in_specs=[pl.BlockSpec((tm, d), lambda i: (i, 0)),
                  pl.BlockSpec((1, d), lambda i: (0, 0)),
                  _mod_spec(mod, sc_chunk, d, tm, rows_per_batch, row_of, col_of),
                  _mod_spec(mod, sh_chunk, d, tm, rows_per_batch, row_of, col_of)],
        out_specs=pl.BlockSpec((tm, d), lambda i: (i, 0)),
        out_shape=jax.ShapeDtypeStruct((m, d), BF),
        compiler_params=_cparams("parallel"),
        name="norm_mod",
    )(x, g.reshape(1, d), mod, mod)


def _cast_weights_once(pairs):
    @pl.when(pl.program_id(1) == 0)
    def _():
        for src, dst in pairs:
            dst[...] = src[...].astype(BF)


_NT =(((1,), (1,)), ((), ()))


def _wt_spec(k, l, tn, start_of):
    return pl.BlockSpec((pl.Element(1), pl.Element(tn), pl.Element(k)),
                        lambda j, i: (l, pl.multiple_of(start_of(j), 8), 0))


def _with_rider(x_ref, xs_ref, compute, store):
    tm = x_ref.shape[0]

    @pl.when(pl.program_id(1) == 0)
    def _():
        res = compute(jnp.concatenate([x_ref[...], xs_ref[...]], axis=0))
        store(tuple(r[:tm] for r in res), tuple(r[tm:] for r in res))

    @pl.when(pl.program_id(1) != 0)
    def _():
        store(compute(x_ref[...]), None)


def _mm_t_kernel(x_ref, xs_ref, w_ref, o_ref, os_ref, wbf):
    _cast_weights_once([(w_ref.at[0], wbf)])

    def compute(rows):
        return (lax.dot_general(rows, wbf[...], _NT, preferred_element_type=F32),)

    def store(main, rider):
        o_ref[...] = main[0].astype(o_ref.dtype)
        if rider is not None:
            os_ref[...] = rider[0].astype(os_ref.dtype)

    _with_rider(x_ref, xs_ref, compute, store)


def _mm_t(x, xs, wt, l, tn, col0, n_tiles, out_dtype, name):
    m, k = x.shape
    ms = xs.shape[0]
    tm = min(TM_WIDE, m)
    return pl.pallas_call(
        _mm_t_kernel,
        grid=(n_tiles, m // tm),
        in_specs=[pl.BlockSpec((tm, k), lambda j, i: (i, 0)),
                  pl.BlockSpec((ms, k), lambda j, i: (0, 0)),
                  _wt_spec(k, l, tn, lambda j: col0 + j * tn)],
        out_specs=[pl.BlockSpec((tm, tn), lambda j, i: (i, j)),
                   pl.BlockSpec((ms, tn), lambda j, i: (0, j))],
        out_shape=[jax.ShapeDtypeStruct((m, n_tiles * tn), out_dtype),
                   jax.ShapeDtypeStruct((ms, n_tiles * tn), F32)],
        scratch_shapes=[pltpu.VMEM((tn, k), BF)],
        compiler_params=_cparams("arbitrary", "arbitrary"),
        name=name,
    )(x, xs, wt)


def _qkv_kernel(x_ref, wq_ref, wk_ref, wv_ref, tab_ref, o_ref, wbf, acc_ref, *, dil):
    @pl.when(pl.program_id(0) == 0)
    def _():
        for part, w_ref in enumerate((wq_ref, wk_ref, wv_ref)):
            wbf[part] = w_ref[0].astype(BF)

    hw = SW_HPG * SW_HD
    heads = [slice(hh * SW_HD, (hh + 1) * SW_HD) for hh in range(SW_HPG)]
    tm = x_ref.shape[0]
    sub = min(tm, QKV_SUB)
    for s in range(tm // sub):
        rows_in = slice(s * sub, (s + 1) * sub)
        x = x_ref[rows_in, :]
        for part in range(3):
            acc = lax.dot_general(x, wbf[part], _NT, preferred_element_type=F32)
            if part < 2:
                tab = tab_ref[rows_in, :]
                if part == 0:
                    tab = tab * (SW_HD ** -0.5)
                vals = [_rope(acc[:, sl], tab) for sl in heads]
            else:
                vals = [acc[:, sl] for sl in heads]
            cols = [slice(part * hw + sl.start, part * hw + sl.stop) for sl in heads]
            if dil == 1:
                for col, val in zip(cols, vals):
                    o_ref[0, rows_in, col] = val.astype(o_ref.dtype)
                continue
            for hh, val in enumerate(vals):
                acc_ref[part, hh, rows_in, :] = val
            n = sub // dil
            for r in range(dil):
                for hh, col in enumerate(cols):
                    o_ref[r, s * n:(s + 1) * n, col] = acc_ref[
                        part, hh, pl.ds(s * sub + r, n, stride=dil), :].astype(o_ref.dtype)


def _qkv_proj(x, wt, l, gi, dil, tabs, batch, out_dtype, name):
    m, k = x.shape
    seq = m // batch
    tm = min(TM_WIDE, seq)
    tpb = seq // tm
    hw = SW_HPG * SW_HD
    wide = len(SW_GROUPS) * hw
    once = dict(pipeline_mode=pl.Buffered(1))
    w_spec = lambda part: pl.BlockSpec((pl.Element(1), pl.Element(hw), pl.Element(k)),
                                       lambda i: (l, O_QB + part * wide + gi * hw, 0), **once)
    return pl.pallas_call(
        functools.partial(_qkv_kernel, dil=dil),
        grid=(m // tm,),
        in_specs=[pl.BlockSpec((tm, k), lambda i: (i, 0)), w_spec(0), w_spec(1), w_spec(2),
                  pl.BlockSpec((tm, 2 * SW_HD), lambda i: (i % tpb, 0))],
        out_specs=pl.BlockSpec((None, dil, tm // dil, 3 * hw), lambda i: (i // tpb, 0, i % tpb, 0)),
        out_shape=jax.ShapeDtypeStruct((batch, dil, seq // dil, 3 * hw), out_dtype),
        scratch_shapes=[pltpu.VMEM((3, hw, k), BF), pltpu.VMEM((3, SW_HPG, tm, SW_HD), F32)],
        compiler_params=_cparams("arbitrary"),
        name=name,
    )(x, wt, wt, wt, tabs)


def _cast_kernel(w_ref, o_ref):
    o_ref[...] = w_ref[...].astype(o_ref.dtype)


def _cast_bf16(w3, l):
    _, k, n = w3.shape
    rows = 512
    return pl.pallas_call(
        _cast_kernel,
        grid=(k // rows,),
        in_specs=[pl.BlockSpec((None, rows, n), lambda i: (l, i, 0))],
        out_specs=pl.BlockSpec((rows, n), lambda i: (i, 0)),
        out_shape=jax.ShapeDtypeStruct((k, n), BF),
        compiler_params=_cparams("parallel"),
        name="cast_bf16",
    )(w3)


def _mm_res_norm_kernel(*refs, cast, final, ride):
    refs = list(refs)
    take = lambda n: [refs.pop(0) for _ in range(n)]
    x_ref, w_ref, r_ref, g_ref, gn_ref = take(5)
    sc_ref, sh_ref = (None, None) if final else take(2)
    if ride:
        xs_ref, rs_ref, gs_ref = take(3)
        scs_ref, shs_ref = (None, None) if final else take(2)
    if final:
        (y_ref,) = take(1)
    else:
        o_ref, y_ref = take(2)
    if ride:
        if not final:
            (os_ref,) = take(1)
        (ys_ref,) = take(1)
    scratch = refs
    if final:
        o_ref = scratch[-1]
    if cast:
        wbf = scratch[0]

        @pl.when(pl.program_id(0) == 0)
        def _():
            wbf[...] = w_ref[...].astype(BF)
        w = wbf[...]
    else:
        w = w_ref[...]
    tm = x_ref.shape[0]

    def norm(x, sc, sh):
        y = x * lax.rsqrt(jnp.mean(x * x, axis=1, keepdims=True) + RMS_EPS) * gn_ref[...]
        return y if final else y * (1.0 + sc) + sh

    def plain():
        o_ref[...] = r_ref[...] + g_ref[...] * jnp.dot(x_ref[...], w, preferred_element_type=F32)

    if ride:
        @pl.when(pl.program_id(0) == 0)
        def _():
            acc = jnp.dot(jnp.concatenate([x_ref[...], xs_ref[...]], axis=0), w, preferred_element_type=F32)
            o_ref[...] = r_ref[...] + g_ref[...] * acc[:tm]
            rows_s = rs_ref[...] + gs_ref[...] * acc[tm:]
            if final:
                ys_ref[...] = norm(rows_s, None, None)
            else:
                os_ref[...] = rows_s
                ys_ref[...] = norm(rows_s, scs_ref[...], shs_ref[...]).astype(ys_ref.dtype)

        pl.when(pl.program_id(0) != 0)(plain)
    else:
        plain()
    per_row = (not final) and sc_ref.shape[0] > 1

    def body(rows):
        sc = sh = None
        if not final:
            sc = sc_ref[rows, :] if per_row else sc_ref[...]
            sh = sh_ref[rows, :] if per_row else sh_ref[...]
        y_ref[rows, :] = norm(o_ref[rows, :], sc, sh).astype(y_ref.dtype)

    _row_groups(tm, body)


def _mm_res_norm(xs, w, l, ress, mods, gate_chunk, rows_per_batch, gn, mods_next, sc_chunk, sh_chunk, name, ride):
    nxt = (None, None) if mods_next is None else mods_next
    groups = [(xs[g], ress[g], mods[g], nxt[g]) for g in range(2)]
    call = functools.partial(_mm_res_norm_call, w=w, l=l, gate_chunk=gate_chunk, gn=gn, sc_chunk=sc_chunk,
                             sh_chunk=sh_chunk, name=name)
    if ride:
        return call(groups[0], groups[1], rows_per_batch)
    return call(groups[0], None, rows_per_batch), call(groups[1], None, None)


def _mm_res_norm_call(main, rider, rows_per_batch, *, w, l, gate_chunk, gn, sc_chunk, sh_chunk, name):
    x, res, mod, mod_next = main
    m, k = x.shape
    n = D_MODEL
    tm = min(256, m)
    cast = w.ndim == 3
    final = mod_next is None
    ride = rider is not None
    if ride:
        x_s, res_s, mod_s, modn_s = rider
        ms = x_s.shape[0]
    mods_next = (mod_next, modn_s if ride else None)
    once = dict(pipeline_mode=pl.Buffered(1))
    w_spec = (pl.BlockSpec((None, k, n), lambda i: (l, 0, 0), **once) if cast
              else pl.BlockSpec((k, n), lambda i: (0, 0), **once))
    row_of, col_of = (lambda i: i), (lambda i: 0)
    tile = pl.BlockSpec((tm, n), lambda i: (i, 0))
    in_specs = [pl.BlockSpec((tm, k), lambda i: (i, 0)), w_spec, tile,
                _mod_spec(mod, gate_chunk, n, tm, rows_per_batch, row_of, col_of),
                pl.BlockSpec((1, n), lambda i: (0, 0))]
    args = [x, w, res, mod, gn.reshape(1, n)]
    if not final:
        in_specs += [_mod_spec(mods_next[0], sc_chunk, n, tm, rows_per_batch, row_of, col_of),
                     _mod_spec(mods_next[0], sh_chunk, n, tm, rows_per_batch, row_of, col_of)]
        args += [mods_next[0], mods_next[0]]
    out_specs = [tile] if final else [tile, tile]
    out_shape = ([jax.ShapeDtypeStruct((m, n), F32)] if final
                 else [jax.ShapeDtypeStruct((m, n), F32), jax.ShapeDtypeStruct((m, n), BF)])
    if ride:
        whole = pl.BlockSpec((ms, n), lambda i: (0, 0))
        chunk = lambda c: pl.BlockSpec((ms, n), lambda i: (0, c))
        in_specs += [pl.BlockSpec((ms, k), lambda i: (0, 0)), whole, chunk(gate_chunk)]
        args += [x_s, res_s, mod_s]
        if final:
            out_specs += [whole]
            out_shape += [jax.ShapeDtypeStruct((ms, n), F32)]
        else:
            in_specs += [chunk(sc_chunk), chunk(sh_chunk)]
            args += [mods_next[1], mods_next[1]]
            out_specs += [whole, whole]
            out_shape += [jax.ShapeDtypeStruct((ms, n), F32), jax.ShapeDtypeStruct((ms, n), BF)]
    outs = pl.pallas_call(
        functools.partial(_mm_res_norm_kernel, cast=cast, final=final, ride=ride),
        grid=(m // tm,),
        in_specs=in_specs,
        out_specs=out_specs,
        out_shape=out_shape,
        scratch_shapes=([pltpu.VMEM((k, n), BF)] if cast else []) + ([pltpu.VMEM((tm, n), F32)] if final else []),
        compiler_params=_cparams("arbitrary", vmem=VMEM_LIMIT_BIG),
        name=name,
    )(*args)
    if not ride:
        return outs[0] if final else (outs[0], outs[1])
    return tuple(outs) if final else ((outs[0], outs[1]), (outs[2], outs[3]))


def _ffn1_kernel(x_ref, xs_ref, wg_ref, wu_ref, o_ref, os_ref, wgbf, wubf):
    _cast_weights_once([(wg_ref, wgbf), (wu_ref, wubf)])

    def compute(rows):
        g = jnp.dot(rows, wgbf[...], preferred_element_type=F32)
        u = jnp.dot(rows, wubf[...], preferred_element_type=F32)
        return (g * _sigmoid(g) * u,)

    def store(main, rider):
        o_ref[...] = main[0].astype(o_ref.dtype)
        if rider is not None:
            os_ref[...] = rider[0].astype(os_ref.dtype)

    _with_rider(x_ref, xs_ref, compute, store)


def _ffn1(x, xs, w_gu, l):
    m, k = x.shape
    ms = xs.shape[0]
    tn = 512
    nj = D_FF // tn
    tm = min(TM_WIDE, m)
    return pl.pallas_call(
        _ffn1_kernel,
        grid=(nj, m // tm),
        in_specs=[pl.BlockSpec((tm, k), lambda j, i: (i, 0)),
                  pl.BlockSpec((ms, k), lambda j, i: (0, 0)),
                  pl.BlockSpec((None, k, tn), lambda j, i: (l, 0, j)),
                  pl.BlockSpec((None, k, tn), lambda j, i: (l, 0, nj + j))],
        out_specs=[pl.BlockSpec((tm, tn), lambda j, i: (i, j)),
                   pl.BlockSpec((ms, tn), lambda j, i: (0, j))],
        out_shape=[jax.ShapeDtypeStruct((m, D_FF), BF), jax.ShapeDtypeStruct((ms, D_FF), BF)],
        scratch_shapes=[pltpu.VMEM((k, tn), BF), pltpu.VMEM((k, tn), BF)],
        compiler_params=_cparams("arbitrary", "arbitrary"),
        name="ffn_gate_up",
    )(x, xs, w_gu, w_gu)


def _merge_kernel(xa_ref, xb_ref, xc_ref, g0_ref, g1_ref, g2_ref, sa_ref, sb_ref, sc_ref, t0_ref, t1_ref, t2_ref,
                  wa_ref, wb_ref, wc_ref, o_ref, os_ref, wabf, wbbf, wcbf):
    _cast_weights_once([(wa_ref, wabf), (wb_ref, wbbf), (wc_ref, wcbf)])
    tm = xa_ref.shape[0]

    def dots(xa, xb, xc):
        return (jnp.dot(xa, wabf[...], preferred_element_type=F32),
                jnp.dot(xb, wbbf[...], preferred_element_type=F32),
                jnp.dot(xc, wcbf[...], preferred_element_type=F32))

    def gated(d, gates):
        return sum(_sigmoid(g[...].astype(F32)) * x for g, x in zip(gates, d))

    @pl.when(pl.program_id(1) == 0)
    def _():
        cat = lambda a, b: jnp.concatenate([a[...].astype(BF), b[...].astype(BF)], axis=0)
        d = dots(cat(xa_ref, sa_ref), cat(xb_ref, sb_ref), cat(xc_ref, sc_ref))
        o_ref[...] = gated([x[:tm] for x in d], (g0_ref, g1_ref, g2_ref)).astype(o_ref.dtype)
        os_ref[...] = gated([x[tm:] for x in d], (t0_ref, t1_ref, t2_ref)).astype(os_ref.dtype)

    @pl.when(pl.program_id(1) != 0)
    def _():
        d = dots(xa_ref[...].astype(BF), xb_ref[...].astype(BF), xc_ref[...].astype(BF))
        o_ref[...] = gated(d, (g0_ref, g1_ref, g2_ref)).astype(o_ref.dtype)


def _merge(prompt, sample, w_a, w_b, w_c, l):
    xa, xb, xc, p = prompt
    sa, sb, sc, ps = sample
    ms = sa.shape[0]
    m = xa.shape[0]
    tn = 1024
    tm = min(TM, m)
    ka, kb, kc = xa.shape[1], xb.shape[1], xc.shape[1]
    g_blk = C_G // tn
    per = D_MODEL // tn
    once = dict(pipeline_mode=pl.Buffered(1))
    return pl.pallas_call(
        _merge_kernel,
        grid=(D_MODEL // tn, m // tm),
        in_specs=[pl.BlockSpec((tm, ka), lambda j, i: (i, 0)),
                  pl.BlockSpec((tm, kb), lambda j, i: (i, 0)),
                  pl.BlockSpec((tm, kc), lambda j, i: (i, 0)),
                  pl.BlockSpec((tm, tn), lambda j, i: (i, g_blk + j)),
                  pl.BlockSpec((tm, tn), lambda j, i: (i, g_blk + per + j)),
                  pl.BlockSpec((tm, tn), lambda j, i: (i, g_blk + 2 * per + j)),
                  pl.BlockSpec((ms, ka), lambda j, i: (0, 0)),
                  pl.BlockSpec((ms, kb), lambda j, i: (0, 0)),
                  pl.BlockSpec((ms, kc), lambda j, i: (0, 0)),
                  pl.BlockSpec((ms, tn), lambda j, i: (0, g_blk + j)),
                  pl.BlockSpec((ms, tn), lambda j, i: (0, g_blk + per + j)),
                  pl.BlockSpec((ms, tn), lambda j, i: (0, g_blk + 2 * per + j)),
                  pl.BlockSpec((None, ka, tn), lambda j, i: (l, 0, j), **once),
                  pl.BlockSpec((None, kb, tn), lambda j, i: (l, 0, j), **once),
                  pl.BlockSpec((None, kc, tn), lambda j, i: (l, 0, j), **once)],
        out_specs=[pl.BlockSpec((tm, tn), lambda j, i: (i, j)),
                   pl.BlockSpec((ms, tn), lambda j, i: (0, j))],
        out_shape=[jax.ShapeDtypeStruct((m, D_MODEL), BF), jax.ShapeDtypeStruct((ms, D_MODEL), BF)],
        scratch_shapes=[pltpu.VMEM((ka, tn), BF), pltpu.VMEM((kb, tn), BF), pltpu.VMEM((kc, tn), BF)],
        compiler_params=_cparams("arbitrary", "arbitrary", vmem=VMEM_LIMIT_BIG),
        name="branch_merge",
    )(xa, xb, xc, p, p, p, sa, sb, sc, ps, ps, ps, w_a, w_b, w_c)


def _scan(x, axis, op, fill):
    idx = lax.broadcasted_iota(jnp.int32, x.shape, axis)
    shift = 1
    while shift < x.shape[axis]:
        x = op(x, jnp.where(idx >= shift, pltpu.roll(x, shift, axis), fill))
        shift *= 2
    return x


def _mlstm_kernel(q_ref, k_ref, v_ref, o_ref, gc_ref, gr_ref, brow_ref, bcol_ref,
                  xa_ref, c_ref, n_ref, m_ref, nn_ref):
    chunk = pl.program_id(1)

    @pl.when(chunk == 0)
    def _():
        c_ref[...] = jnp.zeros_like(c_ref)
        m_ref[...] = jnp.zeros_like(m_ref)
        nn_ref[...] = jnp.zeros_like(nn_ref)

    L = q_ref.shape[0]
    nh = ML_HEADS
    scale = ML_DQK ** -0.5
    gc = gc_ref[...] + brow_ref[...]
    gr = gr_ref[...] + bcol_ref[...]
    f_c = pltpu.roll(_scan(_log_sigmoid(gc), 0, jnp.add, 0.0), LANE - nh, 1)
    m_prev = m_ref[...]
    m_t = f_c + jnp.maximum(m_prev, _scan(gc - f_c, 0, jnp.maximum, NEG))
    w_inter = jnp.exp(f_c + m_prev - m_t)
    e_neg_m = jnp.exp(-m_t)
    f_minus_m = f_c - m_t
    m_new = m_t[L - 1:L, :]
    f_last = f_c[L - 1:L, :]
    w_last = jnp.exp(f_last - f_c + gc - m_new) * scale
    decay = jnp.exp(f_last + m_prev - m_new)
    m_ref[...] = m_new
    f_r = _scan(_log_sigmoid(gr), 1, jnp.add, 0.0)
    a_r = gr[0:nh, :] - f_r[nh:2 * nh, :]
    row = lax.broadcasted_iota(jnp.int32, (L, L), 0)
    col = lax.broadcasted_iota(jnp.int32, (L, L), 1)
    tri = col <= row
    ones = jnp.ones((L, ML_DQK), BF)
    nt = (((1,), (1,)), ((), ()))
    tn = (((0,), (0,)), ((), ()))
    for h in range(nh):
        lane = lambda x: x[:, h:h + 1]
        d_w = jnp.exp(jnp.where(tri, lane(f_minus_m) + a_r[h:h + 1, :], NEG))
        q = q_ref[:, h * ML_DQK:(h + 1) * ML_DQK]
        k = k_ref[:, h * ML_DQK:(h + 1) * ML_DQK]
        v = v_ref[:, h * ML_DV:(h + 1) * ML_DV]
        s = (lax.dot_general(q, k, nt, preferred_element_type=F32) * (scale * d_w)).astype(BF)
        c_old = c_ref[h]
        nn_old = nn_ref[h]
        wi = lane(w_inter)
        num = (jnp.dot(s, v, preferred_element_type=F32)
               + wi * jnp.dot(q, c_old.astype(BF), preferred_element_type=F32))
        den = (jnp.dot(s, ones, preferred_element_type=F32)
               + wi * jnp.dot(q, nn_old.astype(BF), preferred_element_type=F32))
        inv = 1.0 / jnp.maximum(jnp.abs(den[:, 0:1]), lane(e_neg_m))
        og = o_ref[:, h * ML_DV:(h + 1) * ML_DV].astype(F32)
        xa_ref[:, h * ML_DV:(h + 1) * ML_DV] = (_sigmoid(og) * (num * inv)).astype(xa_ref.dtype)
        kw = (k.astype(F32) * lane(w_last)).astype(BF)
        dk = decay[:, h:h + 1]
        c_ref[h] = dk * c_old + lax.dot_general(kw, v, tn, preferred_element_type=F32)
        nn_ref[h] = dk * nn_old + lax.dot_general(kw, ones, tn, preferred_element_type=F32)

    @pl.when(chunk == pl.num_programs(1) - 1)
    def _():
        for h in range(nh):
            n_ref[h:h + 1, :] = nn_ref[h].T[0:1, :]


def _mlstm_prompt(p, gates, b_i, b_f, batch, seq):
    m_rows = p.shape[0]
    L = ML_L
    nc = seq // L
    bias = jnp.concatenate([b_i, b_f]).astype(F32)
    bias_row = jnp.zeros((1, N_GATE), F32).at[0, :2 * ML_HEADS].set(bias)
    bias_col = bias.reshape(2 * ML_HEADS, 1)
    gates_t = gates[:, :2 * ML_HEADS].T
    hq, hv = ML_HEADS * ML_DQK, ML_HEADS * ML_DV
    return pl.pallas_call(
        _mlstm_kernel,
        grid=(batch, nc),
        in_specs=[pl.BlockSpec((L, hq), lambda b, c: (b * nc + c, C_QA // hq)),
                  pl.BlockSpec((L, hq), lambda b, c: (b * nc + c, C_KA // hq)),
                  pl.BlockSpec((L, hv), lambda b, c: (b * nc + c, C_VA // hv)),
                  pl.BlockSpec((L, hv), lambda b, c: (b * nc + c, C_OA // hv)),
                  pl.BlockSpec((L, N_GATE), lambda b, c: (b * nc + c, 0)),
                  pl.BlockSpec((2 * ML_HEADS, L), lambda b, c: (0, b * nc + c)),
                  pl.BlockSpec((1, N_GATE), lambda b, c: (0, 0)),
                  pl.BlockSpec((2 * ML_HEADS, 1), lambda b, c: (0, 0))],
        out_specs=[pl.BlockSpec((L, hv), lambda b, c: (b * nc + c, 0)),
                   pl.BlockSpec((None, ML_HEADS, ML_DQK, ML_DV), lambda b, c: (b, 0, 0, 0)),
                   pl.BlockSpec((None, ML_HEADS, ML_DQK), lambda b, c: (b, 0, 0)),
                   pl.BlockSpec((None, 1, LANE), lambda b, c: (b, 0, 0))],
        out_shape=[jax.ShapeDtypeStruct((m_rows, hv), BF),
                   jax.ShapeDtypeStruct((batch, ML_HEADS, ML_DQK, ML_DV), F32),
                   jax.ShapeDtypeStruct((batch, ML_HEADS, ML_DQK), F32),
                   jax.ShapeDtypeStruct((batch, 1, LANE), F32)],
        scratch_shapes=[pltpu.VMEM((ML_HEADS, ML_DQK, ML_DQK), F32)],
        compiler_params=_cparams("parallel", "arbitrary"),
        name="mlstm_prompt",
    )(p, p, p, p, gates, gates_t, bias_row, bias_col)


def _rope_table(pos):
    half = SW_ROT // 2
    freq = ROPE_THETA ** (-jnp.arange(half, dtype=F32) * 2.0 / SW_ROT)
    ang = pos.astype(F32)[:, None] * freq[None, :]
    cos, sin = jnp.cos(ang), jnp.sin(ang)
    t = pos.shape[0]
    return jnp.concatenate([cos, cos, jnp.ones((t, SW_HD - SW_ROT), F32),
                            -sin, sin, jnp.zeros((t, SW_HD - SW_ROT), F32)], axis=1)


def _rope(x, tab):
    half = SW_ROT // 2
    lane = lax.broadcasted_iota(jnp.int32, x.shape, 1)
    partner = jnp.where(lane < SW_ROT, lane ^ half, lane)
    return x * tab[:, 0:SW_HD] + jnp.take_along_axis(x, partner, axis=1) * tab[:, SW_HD:2 * SW_HD]


def _attend(q, k_own, v_own, k_prev, v_prev, prev_shift):
    blk = SW_BLK
    row = lax.broadcasted_iota(jnp.int32, (1, blk, blk), 1)
    col = lax.broadcasted_iota(jnp.int32, (1, blk, blk), 2)
    qk = (((2,), (2,)), ((0,), (0,)))
    pv = (((2,), (1,)), ((0,), (0,)))
    s_o = jnp.where(col <= row, lax.dot_general(q, k_own, qk, preferred_element_type=F32), NEG)
    s_p = jnp.where(col >= row + prev_shift, lax.dot_general(q, k_prev, qk, preferred_element_type=F32), NEG)
    mx = jnp.maximum(jnp.max(s_o, axis=2, keepdims=True), jnp.max(s_p, axis=2, keepdims=True))
    p_o = jnp.exp(s_o - mx)
    p_p = jnp.exp(s_p - mx)
    den = jnp.sum(p_o, axis=2, keepdims=True) + jnp.sum(p_p, axis=2, keepdims=True)
    acc = (lax.dot_general(p_o.astype(BF), v_own, pv, preferred_element_type=F32)
           + lax.dot_general(p_p.astype(BF), v_prev, pv, preferred_element_type=F32))
    return acc, mx, den


def _swa_kernel(q1, k1, v1, k1p, v1p, q2, k2, v2, k2p, v2p, q3, k3, v3, k3p, v3p, xb_ref, o_acc, m_acc, l_acc):
    blk = SW_BLK
    u = SWA_UNITS
    first_shift = jnp.where(pl.program_id(1) > 0, 0, blk)
    wide = lambda x: jnp.broadcast_to(x, x.shape[:-1] + (SW_HD,))

    def merge(rows, acc, mx, den):
        m_old = m_acc[rows, :]
        m_new = jnp.maximum(m_old, mx)
        a = jnp.exp(m_old - m_new)
        b = jnp.exp(mx - m_new)
        o_acc[rows, :] = a * o_acc[rows, :] + b * acc
        l_acc[rows, :] = a * l_acc[rows, :] + b * den
        m_acc[rows, :] = m_new

    def g1_batch(rows, k_p, v_p, shift):
        split = lambda x: x.reshape(u, blk, SW_HD)
        acc, mx, den = _attend(split(q1[rows, :]), split(k1[rows, :]), split(v1[rows, :]),
                               split(k_p), split(v_p), shift)
        o_acc[rows, :] = acc.reshape(u * blk, SW_HD)
        m_acc[rows, :] = wide(mx).reshape(u * blk, SW_HD)
        l_acc[rows, :] = wide(den).reshape(u * blk, SW_HD)

    head = pl.ds(0, (u - 1) * blk)
    unit = lax.broadcasted_iota(jnp.int32, (u, 1, 1), 0)
    g1_batch(pl.ds(0, u * blk),
             jnp.concatenate([k1p[...], k1[head, :]], axis=0), jnp.concatenate([v1p[...], v1[head, :]], axis=0),
             jnp.where(unit == 0, first_shift, 0))

    def g1_body(i, c):
        rows = pl.ds(pl.multiple_of(i * u * blk, blk), u * blk)
        prev = pl.ds(pl.multiple_of(i * u * blk - blk, blk), u * blk)
        g1_batch(rows, k1[prev, :], v1[prev, :], 0)
        return c

    lax.fori_loop(1, q1.shape[0] // (u * blk), g1_body, 0)

    def dilated(q, k, v, kp, vp):
        dil = q.shape[0]
        u = min(SWA_UNITS, dil)
        n_sub = q.shape[1] // blk

        def batch(sub, r0, k_p, v_p, shift):
            rs = pl.ds(r0, u)
            rows = pl.ds(pl.multiple_of(sub * blk, blk), blk)
            acc, mx, den = _attend(q[rs, rows, :], k[rs, rows, :], v[rs, rows, :], k_p, v_p, shift)
            for i in range(u):
                merge(pl.ds(sub * blk * dil + r0 + i, blk, stride=dil), acc[i], wide(mx[i]), wide(den[i]))

        def first(i, c):
            rs = pl.ds(i * u, u)
            batch(0, i * u, kp[rs], vp[rs], first_shift)
            return c

        lax.fori_loop(0, dil // u, first, 0)

        def rest(i, c):
            sub = 1 + i // (dil // u)
            r0 = (i % (dil // u)) * u
            prev = pl.ds(pl.multiple_of((sub - 1) * blk, blk), blk)
            batch(sub, r0, k[pl.ds(r0, u), prev, :], v[pl.ds(r0, u), prev, :], 0)
            return c

        lax.fori_loop(0, (n_sub - 1) * (dil // u), rest, 0)

    dilated(q2, k2, v2, k2p, v2p)
    dilated(q3, k3, v3, k3p, v3p)
    xb_ref[...] = (o_acc[...] * (1.0 / l_acc[...])).astype(xb_ref.dtype)


def _swa_prompt(qkvs, batch, seq):
    blk = SW_BLK
    span = SW_GROUPS[-1][0]
    assert seq % span == 0 and all(w // d == blk for w, d in SW_GROUPS)
    nspan = seq // span
    in_specs = []
    args = []
    for (win, dil), qkv in zip(SW_GROUPS, qkvs):
        rows = span // dil
        per = rows // blk
        own = lambda c: pl.BlockSpec((None, dil, rows, SW_HD), lambda b, s, h, c=c: (b, 0, s, c * SW_HPG + h))
        prev = lambda c: pl.BlockSpec(
            (None, dil, blk, SW_HD), lambda b, s, h, c=c, per=per: (b, 0, jnp.maximum(s * per - 1, 0), c * SW_HPG + h))
        if dil == 1:
            own = lambda c: pl.BlockSpec((None, None, span, SW_HD), lambda b, s, h, c=c: (b, 0, s, c * SW_HPG + h))
            prev = lambda c: pl.BlockSpec(
                (None, None, blk, SW_HD),
                lambda b, s, h, c=c, per=per: (b, 0, jnp.maximum(s * per - 1, 0), c * SW_HPG + h))
        in_specs += [own(0), own(1), own(2), prev(1), prev(2)]
        args += [qkv] * 5
    return pl.pallas_call(
        _swa_kernel,
        grid=(batch, nspan, SW_HPG),
        in_specs=in_specs,
        out_specs=pl.BlockSpec((span, SW_HD), lambda b, s, h: (b * nspan + s, h)),
        out_shape=jax.ShapeDtypeStruct((batch * seq, SW_HPG * SW_HD), BF),
        scratch_shapes=[pltpu.VMEM((span, SW_HD), F32)] * 3,
        compiler_params=_cparams("parallel", "parallel", "parallel"),
        name="swa_prompt",
    )(*args)


def _kv_rows_kernel(k_ref, v_ref, o_ref, *, dil):
    for r in range(dil):
        rows = pl.ds(r, SW_BLK, stride=dil) if dil > 1 else pl.ds(0, SW_BLK)
        for hh in range(SW_HPG):
            sl = slice(hh * SW_HD, (hh + 1) * SW_HD)
            o_ref[hh, rows, :] = k_ref[r, :, sl].astype(F32)
            o_ref[SW_HPG + hh, rows, :] = v_ref[r, :, sl].astype(F32)


def _kv_rows(qkv, gi, batch, seq):
    win, dil = SW_GROUPS[gi]
    hw = SW_HPG * SW_HD
    last = seq // dil // SW_BLK - 1
    out = pl.pallas_call(
        functools.partial(_kv_rows_kernel, dil=dil),
        grid=(batch,),
        in_specs=[pl.BlockSpec((None, dil, SW_BLK, hw), lambda b: (b, 0, last, 1)),
                  pl.BlockSpec((None, dil, SW_BLK, hw), lambda b: (b, 0, last, 2))],
        out_specs=pl.BlockSpec((None, 2 * SW_HPG, win, SW_HD), lambda b: (b, 0, 0, 0)),
        out_shape=jax.ShapeDtypeStruct((batch, 2 * SW_HPG, win, SW_HD), F32),
        compiler_params=_cparams("parallel"),
        name=f"kv_rows_w{win}",
    )(qkv, qkv)
    return out.reshape(batch, 2, SW_HPG, win, SW_HD).transpose(0, 3, 1, 2, 4)


def _cm_kernel(u_ref, v_ref, gv_ref, ws_ref, bs_ref, o_ref):
    ch = CM_CHUNK
    row = lax.broadcasted_iota(jnp.int32, (ch, ch), 0)
    col = lax.broadcasted_iota(jnp.int32, (ch, ch), 1)
    tri = col <= row
    for g in range(CM_GROUPS):
        sl = slice(g * CM_GD, (g + 1) * CM_GD)
        w = jnp.where(tri, ws_ref[g], 0.0).astype(BF)
        for c in range(u_ref.shape[0] // ch):
            rows = slice(c * ch, (c + 1) * ch)
            vg = v_ref[rows, sl].astype(F32)
            vn = vg * lax.rsqrt(jnp.mean(vg * vg, axis=1, keepdims=True) + RMS_EPS) * gv_ref[g:g + 1, :]
            mixed = jnp.dot(w, vn.astype(BF), preferred_element_type=F32) + bs_ref[:, g:g + 1]
            o_ref[rows, sl] = (u_ref[rows, sl].astype(F32) * mixed).astype(o_ref.dtype)


def _cm_prompt(p, g_v, w_s, b_s, l):
    m_rows = p.shape[0]
    ch = CM_CHUNK
    width = CM_GROUPS * CM_GD
    rows = 4 * ch
    return pl.pallas_call(
        _cm_kernel,
        grid=(m_rows // rows,),
        in_specs=[pl.BlockSpec((rows, width), lambda i: (i, C_UC // width)),
                  pl.BlockSpec((rows, width), lambda i: (i, C_VC // width)),
                  pl.BlockSpec((None, CM_GROUPS, CM_GD), lambda i: (l, 0, 0)),
                  pl.BlockSpec((None, CM_GROUPS, ch, ch), lambda i: (l, 0, 0, 0)),
                  pl.BlockSpec((ch, CM_GROUPS), lambda i: (0, 0))],
        out_specs=pl.BlockSpec((rows, width), lambda i: (i, 0)),
        out_shape=jax.ShapeDtypeStruct((m_rows, width), BF),
        compiler_params=_cparams("parallel"),
        name="cm_prompt",
    )(p, p, g_v, w_s, b_s[l].T)


def _mlstm_step_kernel(qr_ref, kr_ref, vr_ref, or_ref, qkc_ref, g_ref, brow_ref, m0_ref, c0_ref, n0_ref,
                       xa_ref, c1_ref, n1_ref, m1_ref):
    scale = ML_DQK ** -0.5
    gates = g_ref[...] + brow_ref[...]
    m0 = m0_ref[...]
    for h in range(ML_HEADS):
        ig = gates[:, h:h + 1]
        lf = _log_sigmoid(gates[:, ML_HEADS + h:ML_HEADS + h + 1])
        m_prev = m0[:, h:h + 1]
        inter = lf + m_prev
        m_t = jnp.maximum(inter, ig)
        d_w = jnp.exp(ig - m_t)
        w_inter = jnp.exp(inter - m_t)
        q_row = qr_ref[h:h + 1, :]
        k_row = kr_ref[h:h + 1, :]
        v_row = vr_ref[h:h + 1, :]
        q_col = qkc_ref[:, h:h + 1]
        k_col = qkc_ref[:, ML_HEADS + h:ML_HEADS + h + 1]
        c_old = c0_ref[h]
        n_old = n0_ref[h:h + 1, :]
        s = jnp.sum(q_row * k_row, axis=1, keepdims=True) * scale * d_w
        qc = jnp.sum(q_col * c_old, axis=0, keepdims=True)
        num = s * v_row + w_inter * qc
        den = s + w_inter * jnp.sum(q_row * n_old, axis=1, keepdims=True)
        hh = num / jnp.maximum(jnp.abs(den), jnp.exp(-m_t))
        xa_ref[:, h * ML_DV:(h + 1) * ML_DV] = _sigmoid(or_ref[h:h + 1, :]) * hh
        w_last = d_w * scale
        decay = w_inter
        c1_ref[h] = decay * c_old + (k_col * w_last) * v_row
        n1_ref[h:h + 1, :] = decay * n_old + w_last * k_row
        m1_ref[h:h + 1, :] = jnp.broadcast_to(m_t, (1, ML_DQK))


def _mlstm_sample(ps, gates, b_i, b_f, c0, n0, m0, l):
    nb = c0.shape[1]
    q = ps[:nb, C_QA:C_QA + 1024].reshape(nb, ML_HEADS, ML_DQK)
    k = ps[:nb, C_KA:C_KA + 1024].reshape(nb, ML_HEADS, ML_DQK)
    v = ps[:nb, C_VA:C_VA + 2048].reshape(nb, ML_HEADS, ML_DV)
    o = ps[:nb, C_OA:C_OA + 2048].reshape(nb, ML_HEADS, ML_DV)
    qk_col = jnp.concatenate([q, k], axis=1).transpose(0, 2, 1)
    bias = jnp.concatenate([b_i, b_f]).astype(F32)
    bias_row = jnp.zeros((1, N_GATE), F32).at[0, :2 * ML_HEADS].set(bias)
    g3 = gates[:nb].reshape(nb, 1, N_GATE)
    m03 = m0[l].reshape(nb, 1, ML_HEADS)
    per_b3 = lambda shape: pl.BlockSpec((None,) + shape, lambda b: (b, 0, 0))
    return pl.pallas_call(
        _mlstm_step_kernel,
        grid=(nb,),
        in_specs=[per_b3((ML_HEADS, ML_DQK)), per_b3((ML_HEADS, ML_DQK)),
                  per_b3((ML_HEADS, ML_DV)), per_b3((ML_HEADS, ML_DV)),
                  per_b3((ML_DQK, 2 * ML_HEADS)), per_b3((1, N_GATE)),
                  pl.BlockSpec((1, N_GATE), lambda b: (0, 0)),
                  per_b3((1, ML_HEADS)),
                  pl.BlockSpec((None, None, ML_HEADS, ML_DQK, ML_DV), lambda b: (l, b, 0, 0, 0)),
                  pl.BlockSpec((None, None, ML_HEADS, ML_DQK), lambda b: (l, b, 0, 0))],
        out_specs=[per_b3((1, ML_HEADS * ML_DV)),
                   pl.BlockSpec((None, ML_HEADS, ML_DQK, ML_DV), lambda b: (b, 0, 0, 0)),
                   per_b3((ML_HEADS, ML_DQK)), per_b3((ML_HEADS, ML_DQK))],
        out_shape=[jax.ShapeDtypeStruct((nb, 1, ML_HEADS * ML_DV), F32),
                   jax.ShapeDtypeStruct((nb, ML_HEADS, ML_DQK, ML_DV), F32),
                   jax.ShapeDtypeStruct((nb, ML_HEADS, ML_DQK), F32),
                   jax.ShapeDtypeStruct((nb, ML_HEADS, ML_DQK), F32)],
        compiler_params=_cparams("parallel"),
        name="mlstm_sample",
    )(q, k, v, o, qk_col, g3, bias_row, m03, c0, n0)


def _swa_step_kernel(g0_ref, g1_ref, g2_ref, c0_ref, c1_ref, c2_ref, xb_ref):
    hw = SW_HPG * SW_HD
    per_pos = 2 * SW_HPG
    news = (g0_ref, g1_ref, g2_ref)
    caches = (c0_ref, c1_ref, c2_ref)
    for hh in range(SW_HPG):
        m_run = l_run = acc = None
        for gi in range(len(SW_GROUPS)):
            lo = hh * SW_HD
            q = news[gi][:, lo:lo + SW_HD]
            k_new = news[gi][:, hw + lo:hw + lo + SW_HD]
            v_new = news[gi][:, 2 * hw + lo:2 * hw + lo + SW_HD]
            kc = caches[gi][:, hh, :]
            vc = caches[gi][:, SW_HPG + hh, :]
            s_c = jnp.sum(kc * q, axis=1, keepdims=True)
            s_n = jnp.sum(k_new * q, axis=1, keepdims=True)
            mx = jnp.maximum(jnp.max(s_c, axis=0, keepdims=True), s_n)
            p_c = jnp.exp(s_c - mx)
            p_n = jnp.exp(s_n - mx)
            den = jnp.sum(p_c, axis=0, keepdims=True) + p_n
            part = jnp.sum(p_c * vc, axis=0, keepdims=True) + p_n * v_new
            if gi == 0:
                m_run, l_run, acc = mx, den, part
            else:
                top = jnp.maximum(m_run, mx)
                a, b = jnp.exp(m_run - top), jnp.exp(mx - top)
                m_run, l_run, acc = top, a * l_run + b * den, a * acc + b * part
        xb_ref[:, hh * SW_HD:(hh + 1) * SW_HD] = acc / l_run


def _swa_sample(qkvs, caches, l):
    nb = caches[0].shape[1]
    hw = SW_HPG * SW_HD
    per_pos = 2 * SW_HPG
    news, views, specs = [], [], []
    for (win, dil), cache, qkv in zip(SW_GROUPS, caches, qkvs):
        assert cache.shape[2] == win and win // dil == SW_BLK
        news.append(qkv[0, 0, :nb].reshape(nb, 1, 3 * hw))
        views.append(cache.reshape(cache.shape[0], nb, win // dil, dil * per_pos, SW_HD))
        specs.append(pl.BlockSpec((None, None, SW_BLK, per_pos, SW_HD), lambda b: (l, b, 0, 0, 0)))
    row3 = pl.BlockSpec((None, 1, 3 * hw), lambda b: (b, 0, 0))
    xb = pl.pallas_call(
        _swa_step_kernel,
        grid=(nb,),
        in_specs=[row3, row3, row3] + specs,
        out_specs=pl.BlockSpec((None, 1, hw), lambda b: (b, 0, 0)),
        out_shape=jax.ShapeDtypeStruct((nb, 1, hw), F32),
        compiler_params=_cparams("parallel"),
        name="swa_sample",
    )(*news, *views)
    kvs = [jnp.stack([n[:, :, hw:2 * hw].reshape(nb, 1, SW_HPG, SW_HD),
                      n[:, :, 2 * hw:].reshape(nb, 1, SW_HPG, SW_HD)], axis=2) for n in news]
    return xb.reshape(nb, hw), kvs


def _cm_step_kernel(u_ref, v_ref, gv_ref, ws_ref, bs_ref, o_ref, vn_ref):
    for g in range(CM_GROUPS):
        sl = slice(g * CM_GD, (g + 1) * CM_GD)
        vg = v_ref[:, sl]
        vn = vg * lax.rsqrt(jnp.mean(vg * vg, axis=1, keepdims=True) + RMS_EPS) * gv_ref[g:g + 1, :]
        vn_ref[:, sl] = vn
        w00 = ws_ref[g][0:1, 0:1]
        o_ref[:, sl] = u_ref[:, sl] * (w00 * vn + bs_ref[g:g + 1, 0:1])


def _cm_sample(ps, g_v, w_s, b_s, l):
    rows = ps.shape[0]
    width = CM_GROUPS * CM_GD
    ch = CM_CHUNK
    return pl.pallas_call(
        _cm_step_kernel,
        grid=(1,),
        in_specs=[pl.BlockSpec((rows, width), lambda i: (0, C_UC // width)),
                  pl.BlockSpec((rows, width), lambda i: (0, C_VC // width)),
                  pl.BlockSpec((None, CM_GROUPS, CM_GD), lambda i: (l, 0, 0)),
                  pl.BlockSpec((None, CM_GROUPS, ch, ch), lambda i: (l, 0, 0, 0)),
                  pl.BlockSpec((None, CM_GROUPS, ch), lambda i: (l, 0, 0))],
        out_specs=[pl.BlockSpec((rows, width), lambda i: (0, 0))] * 2,
        out_shape=[jax.ShapeDtypeStruct((rows, width), F32)] * 2,
        compiler_params=_cparams("arbitrary"),
        name="cm_sample",
    )(ps, ps, g_v, w_s, b_s)


def _in_proj(hp, hs, w_in, l, tabs_p, tabs_s, batch, dils):
    wt = jnp.swapaxes(w_in, 1, 2)
    p_a, s_a = _mm_t(hp, hs, wt, l, 1024, 0, O_GATES // 1024, BF, "in_proj_a")
    p_g, s_g = _mm_t(hp, hs, wt, l, N_GATE, O_GATES, 1, F32, "gate_proj")
    p_qkv = [_qkv_proj(hp, wt, l, gi, dil, tabs_p, batch, BF, f"qkv_proj{gi}") for gi, dil in enumerate(dils)]
    s_qkv = [_qkv_proj(hs, wt, l, gi, 1, tabs_s, 1, F32, f"qkv_proj{gi}_s") for gi in range(len(dils))]
    p_r, s_r = _mm_t(hp, hs, wt, l, 1024, O_UC, N_REST // 1024, BF, "in_proj_r")
    return (p_a, p_g, p_qkv, p_r), (s_a, s_g, s_qkv, s_r)


def _dense_tail(xs, mixers, ps, l, mods, rows_per_batch, wts, gain, mods_next):
    w_down = _cast_bf16(wts["w_ffn_down"], l)
    merged = _merge(mixers[0] + (ps[0],), mixers[1] + (ps[1],), wts["w_br_a"], wts["w_br_b"], wts["w_br_c"], l)
    (x_p, h2_p), (x_s, h2_s) = _mm_res_norm(merged, wts["w_out"], l, xs, mods, 2, rows_per_batch,
                                            wts["g_norm2"][l], mods, 4, 3, "out_proj", ride=False)
    acts = _ffn1(h2_p, h2_s, wts["w_ffn_gate_up"], l)
    return _mm_res_norm(acts, w_down, l, (x_p, x_s), mods, 5, rows_per_batch, gain, mods_next, 1, 0, "ffn_down",
                        ride=False)


def kernel(x_prompt, x_sample, state_mlstm_C, state_mlstm_n, state_mlstm_m, cache_swa_kv_w128, cache_swa_kv_w512, cache_swa_kv_w2048, c_prompt, c_sample, w_ada, b_ada, g_norm1, g_norm2, g_final, w_in, b_ml_i, b_ml_f, g_cm_v, w_s, b_s, w_br_a, w_br_b, w_br_c, w_out, w_ffn_gate_up, w_ffn_down):
    bp, tp, d = x_prompt.shape
    ns, ts, _ = x_sample.shape
    depth = w_in.shape[0]
    past_len = 16384
    assert d == D_MODEL and ts == 1 and ns == 8 and tp % max(ML_L, SW_GROUPS[-1][0]) == 0
    s_rows = 16
    wts = dict(w_br_a=w_br_a, w_br_b=w_br_b, w_br_c=w_br_c, w_out=w_out, g_norm2=g_norm2,
               w_ffn_gate_up=w_ffn_gate_up, w_ffn_down=w_ffn_down)

    c_all = jnp.zeros((s_rows, d), F32).at[:ns].set(c_sample).at[ns:ns + bp].set(c_prompt)
    mod_all = _ada(c_all, w_ada, b_ada)
    tabs_p = _rope_table(jnp.arange(tp, dtype=jnp.int32))
    tabs_s = _rope_table(jnp.full((s_rows,), past_len, jnp.int32))
    caches = (cache_swa_kv_w128, cache_swa_kv_w512, cache_swa_kv_w2048)
    dils = [dil for _, dil in SW_GROUPS]

    xp = x_prompt.reshape(bp * tp, d)
    xs = jnp.zeros((s_rows, d), F32).at[:ns].set(x_sample.reshape(ns, d))
    p_c, p_n, p_m, p_kv = [], [], [], [[], [], []]
    s_c, s_n, s_m, s_kv, s_v = [], [], [], [[], [], []], []
    mods_p = [mod_all[l, ns:ns + bp].reshape(bp, 1, 6 * d) for l in range(depth)]
    mods_s = [mod_all[l] for l in range(depth)]
    hp = _norm_mod(xp, g_norm1[0], mods_p[0], 1, 0, tp)
    hs = _norm_mod(xs, g_norm1[0], mods_s[0], 1, 0, None)
    for l in range(depth):
        last = l == depth - 1
        (p_a, gates, qkvs, p_r), (s_a, s_gates, s_qkvs, s_r) = _in_proj(hp, hs, w_in, l, tabs_p, tabs_s, bp, dils)
        xa, c1, n1, m1 = _mlstm_prompt(p_a, gates, b_ml_i[l], b_ml_f[l], bp, tp)
        xb = _swa_prompt(qkvs, bp, tp)
        kvs = [_kv_rows(qkvs[gi], gi, bp, tp) for gi in range(len(SW_GROUPS))]
        xc = _cm_prompt(p_r, g_cm_v, w_s, b_s, l)
        mix_p = (xa, xb, xc)
        p_c.append(c1)
        p_n.append(n1)
        p_m.append(m1[:, 0, :ML_HEADS])
        for gi in range(3):
            p_kv[gi].append(kvs[gi])
        xa, c1, n1, m1 = _mlstm_sample(s_a, s_gates, b_ml_i[l], b_ml_f[l], state_mlstm_C, state_mlstm_n,
                                       state_mlstm_m, l)
        xb, kvs = _swa_sample(s_qkvs, caches, l)
        xc, vn = _cm_sample(s_r, g_cm_v, w_s, b_s, l)
        pad = lambda a: jnp.zeros((s_rows, a.shape[-1]), F32).at[:ns].set(a.reshape(ns, -1))
        mix_s = (pad(xa), pad(xb), xc)
        outs = _dense_tail((xp, xs), (mix_p, mix_s), (p_r, s_r), l, (mods_p[l], mods_s[l]), tp, wts,
                           g_final if last else g_norm1[l + 1],
                           None if last else (mods_p[l + 1], mods_s[l + 1]))
        if last:
            y_p = outs[0].reshape(bp, tp, d)
            y_s = outs[1][:ns].reshape(ns, ts, d)
        else:
            (xp, hp), (xs, hs) = outs
        s_c.append(c1)
        s_n.append(n1)
        s_m.append(m1[:, :, 0])
        for gi in range(3):
            s_kv[gi].append(kvs[gi])
        s_v.append(vn[:ns].reshape(ns, 1, CM_GROUPS, CM_GD))
    st = jnp.stack
    return (y_p, y_s, st(p_c), st(p_n), st(p_m), st(p_kv[0]), st(p_kv[1]), st(p_kv[2]),
            st(s_c), st(s_n), st(s_m), st(s_kv[0]), st(s_kv[1]), st(s_kv[2]), st(s_v))
```

```python
import functools

import jax
import jax.numpy as jnp
from jax import lax
from jax.experimental import pallas as pl
from jax.experimental.pallas import tpu as pltpu

BF = jnp.bfloat16
F32 = jnp.float32

D_MODEL = 2048
ML_HEADS = 8
ML_DQK = 128
ML_DV = 256
SW_GROUPS = ((128, 1), (512, 4), (2048, 16))
SW_HPG = 4
SW_HEADS = 12
SW_HD = 128
SW_ROT = SW_HD // 4
SW_BLK = 128
ROPE_THETA = 500000.0
CM_CHUNK = 128
CM_GROUPS = 8
CM_GD = 256
D_FF = 5632
RMS_EPS = 1e-6
NEG = -1e30

LANE = 128
O_GATES = 6144
O_QB = O_GATES + 2 * ML_HEADS
O_UC = O_QB + 3 * SW_HEADS * SW_HD
N_REST = 2 * CM_GROUPS * CM_GD + 3 * D_MODEL
N_GATE = LANE
C_QA, C_KA, C_VA, C_OA = 0, 1024, 2048, 4096
C_UC, C_VC, C_G = 0, 2048, 4096

TM = 512
TM_WIDE = 1024
QKV_SUB = 256
SWA_UNITS = 8
ML_L = 256
VMEM_LIMIT = 48 * 1024 * 1024
VMEM_LIMIT_BIG = 52 * 1024 * 1024


def _cparams(*sem, vmem=VMEM_LIMIT):
    return pltpu.CompilerParams(dimension_semantics=sem, vmem_limit_bytes=vmem)


def _sigmoid(x):
    return 0.5 * (jnp.tanh(0.5 * x) + 1.0)


def _log_sigmoid(x):
    return jnp.minimum(x, 0.0) - jnp.log1p(jnp.exp(-jnp.abs(x)))


def _ada_kernel(c_ref, w_ref, b_ref, o_ref):
    c = c_ref[...]
    a = (c * _sigmoid(c)).astype(BF)
    o_ref[...] = jnp.dot(a, w_ref[...].astype(BF), preferred_element_type=F32) + b_ref[...]


def _ada(c_all, w_ada, b_ada):
    depth, d, n = w_ada.shape
    rows = c_all.shape[0]
    tn = 1024
    return pl.pallas_call(
        _ada_kernel,
        grid=(depth, n // tn),
        in_specs=[pl.BlockSpec((rows, d), lambda l, j: (0, 0)),
                  pl.BlockSpec((None, d, tn), lambda l, j: (l, 0, j)),
                  pl.BlockSpec((None, 1, tn), lambda l, j: (l, 0, j))],
        out_specs=pl.BlockSpec((None, rows, tn), lambda l, j: (l, 0, j)),
        out_shape=jax.ShapeDtypeStruct((depth, rows, n), F32),
        compiler_params=_cparams("parallel", "parallel"),
        name="ada",
    )(c_all, w_ada, b_ada.reshape(depth, 1, n))


def _mod_spec(mod, chunk, width, tm, rows_per_batch, row_of, col_of):
    per = D_MODEL // width
    if mod.ndim == 3:
        tiles_per_batch = rows_per_batch // tm
        return pl.BlockSpec((None, 1, width),
                            lambda *g: (row_of(*g) // tiles_per_batch, 0, chunk * per + col_of(*g)))
    return pl.BlockSpec((tm, width), lambda *g: (row_of(*g), chunk * per + col_of(*g)))


NORM_ROWS = 16


def _row_groups(n_rows, body):
    def step(i, c):
        body(pl.ds(pl.multiple_of(i * NORM_ROWS, NORM_ROWS), NORM_ROWS))
        return c
    groups = n_rows // NORM_ROWS
    lax.fori_loop(0, groups, step, 0, unroll=min(groups, 4))


def _norm_mod_kernel(x_ref, g_ref, sc_ref, sh_ref, o_ref):
    per_row = sc_ref.shape[0] > 1

    def body(rows):
        x = x_ref[rows, :]
        y = x * lax.rsqrt(jnp.mean(x * x, axis=1, keepdims=True) + RMS_EPS) * g_ref[...]
        sc = sc_ref[rows, :] if per_row else sc_ref[...]
        sh = sh_ref[rows, :] if per_row else sh_ref[...]
        o_ref[rows, :] = (y * (1.0 + sc) + sh).astype(o_ref.dtype)

    _row_groups(x_ref.shape[0], body)


def _norm_mod(x, g, mod, sc_chunk, sh_chunk, rows_per_batch):
    m, d = x.shape
    tm = min(TM, m)
    row_of = lambda i: i
    col_of = lambda i: 0
    return pl.pallas_call(
        _norm_mod_kernel,
        grid=(m // tm,),
        in_specs=[pl.BlockSpec((tm, d), lambda i: (i, 0)),
                  pl.BlockSpec((1, d), lambda i: (0, 0)),
                  _mod_spec(mod, sc_chunk, d, tm, rows_per_batch, row_of, col_of),
                  _mod_spec(mod, sh_chunk, d, tm, rows_per_batch, row_of, col_of)],
        out_specs=pl.BlockSpec((tm, d), lambda i: (i, 0)),
        out_shape=jax.ShapeDtypeStruct((m, d), BF),
        compiler_params=_cparams("parallel"),
        name="norm_mod",
    )(x, g.reshape(1, d), mod, mod)


def _cast_weights_once(pairs):
    @pl.when(pl.program_id(1) == 0)
    def _():
        for src, dst in pairs:
            dst[...] = src[...].astype(BF)


_NT =(((1,), (1,)), ((), ()))


def _wt_spec(k, l, tn, start_of):
    return pl.BlockSpec((pl.Element(1), pl.Element(tn), pl.Element(k)),
                        lambda j, i: (l, pl.multiple_of(start_of(j), 8), 0))


def _with_rider(x_ref, xs_ref, compute, store):
    tm = x_ref.shape[0]

    @pl.when(pl.program_id(1) == 0)
    def _():
        res = compute(jnp.concatenate([x_ref[...], xs_ref[...]], axis=0))
        store(tuple(r[:tm] for r in res), tuple(r[tm:] for r in res))

    @pl.when(pl.program_id(1) != 0)
    def _():
        store(compute(x_ref[...]), None)


def _mm_t_kernel(x_ref, xs_ref, w_ref, o_ref, os_ref, wbf):
    _cast_weights_once([(w_ref.at[0], wbf)])

    def compute(rows):
        return (lax.dot_general(rows, wbf[...], _NT, preferred_element_type=F32),)

    def store(main, rider):
        o_ref[...] = main[0].astype(o_ref.dtype)
        if rider is not None:
            os_ref[...] = rider[0].astype(os_ref.dtype)

    _with_rider(x_ref, xs_ref, compute, store)


def _mm_t(x, xs, wt, l, tn, col0, n_tiles, out_dtype, name):
    m, k = x.shape
    ms = xs.shape[0]
    tm = min(TM_WIDE, m)
    return pl.pallas_call(
        _mm_t_kernel,
        grid=(n_tiles, m // tm),
        in_specs=[pl.BlockSpec((tm, k), lambda j, i: (i, 0)),
                  pl.BlockSpec((ms, k), lambda j, i: (0, 0)),
                  _wt_spec(k, l, tn, lambda j: col0 + j * tn)],
        out_specs=[pl.BlockSpec((tm, tn), lambda j, i: (i, j)),
                   pl.BlockSpec((ms, tn), lambda j, i: (0, j))],
        out_shape=[jax.ShapeDtypeStruct((m, n_tiles * tn), out_dtype),
                   jax.ShapeDtypeStruct((ms, n_tiles * tn), F32)],
        scratch_shapes=[pltpu.VMEM((tn, k), BF)],
        compiler_params=_cparams("arbitrary", "arbitrary"),
        name=name,
    )(x, xs, wt)


def _qkv_kernel(x_ref, wq_ref, wk_ref, wv_ref, tab_ref, o_ref, wbf, acc_ref, *, dil):
    @pl.when(pl.program_id(0) == 0)
    def _():
        for part, w_ref in enumerate((wq_ref, wk_ref, wv_ref)):
            wbf[part] = w_ref[0].astype(BF)

    hw = SW_HPG * SW_HD
    heads = [slice(hh * SW_HD, (hh + 1) * SW_HD) for hh in range(SW_HPG)]
    tm = x_ref.shape[0]
    sub = min(tm, QKV_SUB)
    for s in range(tm // sub):
        rows_in = slice(s * sub, (s + 1) * sub)
        x = x_ref[rows_in, :]
        for part in range(3):
            acc = lax.dot_general(x, wbf[part], _NT, preferred_element_type=F32)
            if part < 2:
                tab = tab_ref[rows_in, :]
                if part == 0:
                    tab = tab * (SW_HD ** -0.5)
                vals = [_rope(acc[:, sl], tab) for sl in heads]
            else:
                vals = [acc[:, sl] for sl in heads]
            cols = [slice(part * hw + sl.start, part * hw + sl.stop) for sl in heads]
            if dil == 1:
                for col, val in zip(cols, vals):
                    o_ref[0, rows_in, col] = val.astype(o_ref.dtype)
                continue
            for hh, val in enumerate(vals):
                acc_ref[part, hh, rows_in, :] = val
            n = sub // dil
            for r in range(dil):
                for hh, col in enumerate(cols):
                    o_ref[r, s * n:(s + 1) * n, col] = acc_ref[
                        part, hh, pl.ds(s * sub + r, n, stride=dil), :].astype(o_ref.dtype)


def _qkv_proj(x, wt, l, gi, dil, tabs, batch, out_dtype, name):
    m, k = x.shape
    seq = m // batch
    tm = min(TM_WIDE, seq)
    tpb = seq // tm
    hw = SW_HPG * SW_HD
    wide = len(SW_GROUPS) * hw
    once = dict(pipeline_mode=pl.Buffered(1))
    w_spec = lambda part: pl.BlockSpec((pl.Element(1), pl.Element(hw), pl.Element(k)),
                                       lambda i: (l, O_QB + part * wide + gi * hw, 0), **once)
    return pl.pallas_call(
        functools.partial(_qkv_kernel, dil=dil),
        grid=(m // tm,),
        in_specs=[pl.BlockSpec((tm, k), lambda i: (i, 0)), w_spec(0), w_spec(1), w_spec(2),
                  pl.BlockSpec((tm, 2 * SW_HD), lambda i: (i % tpb, 0))],
        out_specs=pl.BlockSpec((None, dil, tm // dil, 3 * hw), lambda i: (i // tpb, 0, i % tpb, 0)),
        out_shape=jax.ShapeDtypeStruct((batch, dil, seq // dil, 3 * hw), out_dtype),
        scratch_shapes=[pltpu.VMEM((3, hw, k), BF), pltpu.VMEM((3, SW_HPG, tm, SW_HD), F32)],
        compiler_params=_cparams("arbitrary"),
        name=name,
    )(x, wt, wt, wt, tabs)


def _cast_kernel(w_ref, o_ref):
    o_ref[...] = w_ref[...].astype(o_ref.dtype)


def _cast_bf16(w3, l):
    _, k, n = w3.shape
    rows = 512
    return pl.pallas_call(
        _cast_kernel,
        grid=(k // rows,),
        in_specs=[pl.BlockSpec((None, rows, n), lambda i: (l, i, 0))],
        out_specs=pl.BlockSpec((rows, n), lambda i: (i, 0)),
        out_shape=jax.ShapeDtypeStruct((k, n), BF),
        compiler_params=_cparams("parallel"),
        name="cast_bf16",
    )(w3)


def _mm_res_norm_kernel(*refs, cast, final, ride):
    refs = list(refs)
    take = lambda n: [refs.pop(0) for _ in range(n)]
    x_ref, w_ref, r_ref, g_ref, gn_ref = take(5)
    sc_ref, sh_ref = (None, None) if final else take(2)
    if ride:
        xs_ref, rs_ref, gs_ref = take(3)
        scs_ref, shs_ref = (None, None) if final else take(2)
    if final:
        (y_ref,) = take(1)
    else:
        o_ref, y_ref = take(2)
    if ride:
        if not final:
            (os_ref,) = take(1)
        (ys_ref,) = take(1)
    scratch = refs
    if final:
        o_ref = scratch[-1]
    if cast:
        wbf = scratch[0]

        @pl.when(pl.program_id(0) == 0)
        def _():
            wbf[...] = w_ref[...].astype(BF)
        w = wbf[...]
    else:
        w = w_ref[...]
    tm = x_ref.shape[0]

    def norm(x, sc, sh):
        y = x * lax.rsqrt(jnp.mean(x * x, axis=1, keepdims=True) + RMS_EPS) * gn_ref[...]
        return y if final else y * (1.0 + sc) + sh

    def plain():
        o_ref[...] = r_ref[...] + g_ref[...] * jnp.dot(x_ref[...], w, preferred_element_type=F32)

    if ride:
        @pl.when(pl.program_id(0) == 0)
        def _():
            acc = jnp.dot(jnp.concatenate([x_ref[...], xs_ref[...]], axis=0), w, preferred_element_type=F32)
            o_ref[...] = r_ref[...] + g_ref[...] * acc[:tm]
            rows_s = rs_ref[...] + gs_ref[...] * acc[tm:]
            if final:
                ys_ref[...] = norm(rows_s, None, None)
            else:
                os_ref[...] = rows_s
                ys_ref[...] = norm(rows_s, scs_ref[...], shs_ref[...]).astype(ys_ref.dtype)

        pl.when(pl.program_id(0) != 0)(plain)
    else:
        plain()
    per_row = (not final) and sc_ref.shape[0] > 1

    def body(rows):
        sc = sh = None
        if not final:
            sc = sc_ref[rows, :] if per_row else sc_ref[...]
            sh = sh_ref[rows, :] if per_row else sh_ref[...]
        y_ref[rows, :] = norm(o_ref[rows, :], sc, sh).astype(y_ref.dtype)

    _row_groups(tm, body)


def _mm_res_norm(xs, w, l, ress, mods, gate_chunk, rows_per_batch, gn, mods_next, sc_chunk, sh_chunk, name, ride):
    nxt = (None, None) if mods_next is None else mods_next
    groups = [(xs[g], ress[g], mods[g], nxt[g]) for g in range(2)]
    call = functools.partial(_mm_res_norm_call, w=w, l=l, gate_chunk=gate_chunk, gn=gn, sc_chunk=sc_chunk,
                             sh_chunk=sh_chunk, name=name)
    if ride:
        return call(groups[0], groups[1], rows_per_batch)
    return call(groups[0], None, rows_per_batch), call(groups[1], None, None)


def _mm_res_norm_call(main, rider, rows_per_batch, *, w, l, gate_chunk, gn, sc_chunk, sh_chunk, name):
    x, res, mod, mod_next = main
    m, k = x.shape
    n = D_MODEL
    tm = min(256, m)
    cast = w.ndim == 3
    final = mod_next is None
    ride = rider is not None
    if ride:
        x_s, res_s, mod_s, modn_s = rider
        ms = x_s.shape[0]
    mods_next = (mod_next, modn_s if ride else None)
    once = dict(pipeline_mode=pl.Buffered(1))
    w_spec = (pl.BlockSpec((None, k, n), lambda i: (l, 0, 0), **once) if cast
              else pl.BlockSpec((k, n), lambda i: (0, 0), **once))
    row_of, col_of = (lambda i: i), (lambda i: 0)
    tile = pl.BlockSpec((tm, n), lambda i: (i, 0))
    in_specs = [pl.BlockSpec((tm, k), lambda i: (i, 0)), w_spec, tile,
                _mod_spec(mod, gate_chunk, n, tm, rows_per_batch, row_of, col_of),
                pl.BlockSpec((1, n), lambda i: (0, 0))]
    args = [x, w, res, mod, gn.reshape(1, n)]
    if not final:
        in_specs += [_mod_spec(mods_next[0], sc_chunk, n, tm, rows_per_batch, row_of, col_of),
                     _mod_spec(mods_next[0], sh_chunk, n, tm, rows_per_batch, row_of, col_of)]
        args += [mods_next[0], mods_next[0]]
    out_specs = [tile] if final else [tile, tile]
    out_shape = ([jax.ShapeDtypeStruct((m, n), F32)] if final
                 else [jax.ShapeDtypeStruct((m, n), F32), jax.ShapeDtypeStruct((m, n), BF)])
    if ride:
        whole = pl.BlockSpec((ms, n), lambda i: (0, 0))
        chunk = lambda c: pl.BlockSpec((ms, n), lambda i: (0, c))
        in_specs += [pl.BlockSpec((ms, k), lambda i: (0, 0)), whole, chunk(gate_chunk)]
        args += [x_s, res_s, mod_s]
        if final:
            out_specs += [whole]
            out_shape += [jax.ShapeDtypeStruct((ms, n), F32)]
        else:
            in_specs += [chunk(sc_chunk), chunk(sh_chunk)]
            args += [mods_next[1], mods_next[1]]
            out_specs += [whole, whole]
            out_shape += [jax.ShapeDtypeStruct((ms, n), F32), jax.ShapeDtypeStruct((ms, n), BF)]
    outs = pl.pallas_call(
        functools.partial(_mm_res_norm_kernel, cast=cast, final=final, ride=ride),
        grid=(m // tm,),
        in_specs=in_specs,
        out_specs=out_specs,
        out_shape=out_shape,
        scratch_shapes=([pltpu.VMEM((k, n), BF)] if cast else []) + ([pltpu.VMEM((tm, n), F32)] if final else []),
        compiler_params=_cparams("arbitrary", vmem=VMEM_LIMIT_BIG),
        name=name,
    )(*args)
    if not ride:
        return outs[0] if final else (outs[0], outs[1])
    return tuple(outs) if final else ((outs[0], outs[1]), (outs[2], outs[3]))


def _ffn1_kernel(x_ref, xs_ref, wg_ref, wu_ref, o_ref, os_ref, wgbf, wubf):
    _cast_weights_once([(wg_ref, wgbf), (wu_ref, wubf)])

    def compute(rows):
        g = jnp.dot(rows, wgbf[...], preferred_element_type=F32)
        u = jnp.dot(rows, wubf[...], preferred_element_type=F32)
        return (g * _sigmoid(g) * u,)

    def store(main, rider):
        o_ref[...] = main[0].astype(o_ref.dtype)
        if rider is not None:
            os_ref[...] = rider[0].astype(os_ref.dtype)

    _with_rider(x_ref, xs_ref, compute, store)


def _ffn1(x, xs, w_gu, l):
    m, k = x.shape
    ms = xs.shape[0]
    tn = 512
    nj = D_FF // tn
    tm = min(TM_WIDE, m)
    return pl.pallas_call(
        _ffn1_kernel,
        grid=(nj, m // tm),
        in_specs=[pl.BlockSpec((tm, k), lambda j, i: (i, 0)),
                  pl.BlockSpec((ms, k), lambda j, i: (0, 0)),
                  pl.BlockSpec((None, k, tn), lambda j, i: (l, 0, j)),
                  pl.BlockSpec((None, k, tn), lambda j, i: (l, 0, nj + j))],
        out_specs=[pl.BlockSpec((tm, tn), lambda j, i: (i, j)),
                   pl.BlockSpec((ms, tn), lambda j, i: (0, j))],
        out_shape=[jax.ShapeDtypeStruct((m, D_FF), BF), jax.ShapeDtypeStruct((ms, D_FF), BF)],
        scratch_shapes=[pltpu.VMEM((k, tn), BF), pltpu.VMEM((k, tn), BF)],
        compiler_params=_cparams("arbitrary", "arbitrary"),
        name="ffn_gate_up",
    )(x, xs, w_gu, w_gu)


def _merge_kernel(xa_ref, xb_ref, xc_ref, g0_ref, g1_ref, g2_ref, sa_ref, sb_ref, sc_ref, t0_ref, t1_ref, t2_ref,
                  wa_ref, wb_ref, wc_ref, o_ref, os_ref, wabf, wbbf, wcbf):
    _cast_weights_once([(wa_ref, wabf), (wb_ref, wbbf), (wc_ref, wcbf)])
    tm = xa_ref.shape[0]

    def dots(xa, xb, xc):
        return (jnp.dot(xa, wabf[...], preferred_element_type=F32),
                jnp.dot(xb, wbbf[...], preferred_element_type=F32),
                jnp.dot(xc, wcbf[...], preferred_element_type=F32))

    def gated(d, gates):
        return sum(_sigmoid(g[...].astype(F32)) * x for g, x in zip(gates, d))

    @pl.when(pl.program_id(1) == 0)
    def _():
        cat = lambda a, b: jnp.concatenate([a[...].astype(BF), b[...].astype(BF)], axis=0)
        d = dots(cat(xa_ref, sa_ref), cat(xb_ref, sb_ref), cat(xc_ref, sc_ref))
        o_ref[...] = gated([x[:tm] for x in d], (g0_ref, g1_ref, g2_ref)).astype(o_ref.dtype)
        os_ref[...] = gated([x[tm:] for x in d], (t0_ref, t1_ref, t2_ref)).astype(os_ref.dtype)

    @pl.when(pl.program_id(1) != 0)
    def _():
        d = dots(xa_ref[...].astype(BF), xb_ref[...].astype(BF), xc_ref[...].astype(BF))
        o_ref[...] = gated(d, (g0_ref, g1_ref, g2_ref)).astype(o_ref.dtype)


def _merge(prompt, sample, w_a, w_b, w_c, l):
    xa, xb, xc, p = prompt
    sa, sb, sc, ps = sample
    ms = sa.shape[0]
    m = xa.shape[0]
    tn = 1024
    tm = min(TM, m)
    ka, kb, kc = xa.shape[1], xb.shape[1], xc.shape[1]
    g_blk = C_G // tn
    per = D_MODEL // tn
    once = dict(pipeline_mode=pl.Buffered(1))
    return pl.pallas_call(
        _merge_kernel,
        grid=(D_MODEL // tn, m // tm),
        in_specs=[pl.BlockSpec((tm, ka), lambda j, i: (i, 0)),
                  pl.BlockSpec((tm, kb), lambda j, i: (i, 0)),
                  pl.BlockSpec((tm, kc), lambda j, i: (i, 0)),
                  pl.BlockSpec((tm, tn), lambda j, i: (i, g_blk + j)),
                  pl.BlockSpec((tm, tn), lambda j, i: (i, g_blk + per + j)),
                  pl.BlockSpec((tm, tn), lambda j, i: (i, g_blk + 2 * per + j)),
                  pl.BlockSpec((ms, ka), lambda j, i: (0, 0)),
                  pl.BlockSpec((ms, kb), lambda j, i: (0, 0)),
                  pl.BlockSpec((ms, kc), lambda j, i: (0, 0)),
                  pl.BlockSpec((ms, tn), lambda j, i: (0, g_blk + j)),
                  pl.BlockSpec((ms, tn), lambda j, i: (0, g_blk + per + j)),
                  pl.BlockSpec((ms, tn), lambda j, i: (0, g_blk + 2 * per + j)),
                  pl.BlockSpec((None, ka, tn), lambda j, i: (l, 0, j), **once),
                  pl.BlockSpec((None, kb, tn), lambda j, i: (l, 0, j), **once),
                  pl.BlockSpec((None, kc, tn), lambda j, i: (l, 0, j), **once)],
        out_specs=[pl.BlockSpec((tm, tn), lambda j, i: (i, j)),
                   pl.BlockSpec((ms, tn), lambda j, i: (0, j))],
        out_shape=[jax.ShapeDtypeStruct((m, D_MODEL), BF), jax.ShapeDtypeStruct((ms, D_MODEL), BF)],
        scratch_shapes=[pltpu.VMEM((ka, tn), BF), pltpu.VMEM((kb, tn), BF), pltpu.VMEM((kc, tn), BF)],
        compiler_params=_cparams("arbitrary", "arbitrary", vmem=VMEM_LIMIT_BIG),
        name="branch_merge",
    )(xa, xb, xc, p, p, p, sa, sb, sc, ps, ps, ps, w_a, w_b, w_c)


def _scan(x, axis, op, fill):
    idx = lax.broadcasted_iota(jnp.int32, x.shape, axis)
    shift = 1
    while shift < x.shape[axis]:
        x = op(x, jnp.where(idx >= shift, pltpu.roll(x, shift, axis), fill))
        shift *= 2
    return x


def _mlstm_kernel(q_ref, k_ref, v_ref, o_ref, gc_ref, gr_ref, brow_ref, bcol_ref,
                  xa_ref, c_ref, n_ref, m_ref, nn_ref):
    chunk = pl.program_id(1)

    @pl.when(chunk == 0)
    def _():
        c_ref[...] = jnp.zeros_like(c_ref)
        m_ref[...] = jnp.zeros_like(m_ref)
        nn_ref[...] = jnp.zeros_like(nn_ref)

    L = q_ref.shape[0]
    nh = ML_HEADS
    scale = ML_DQK ** -0.5
    gc = gc_ref[...] + brow_ref[...]
    gr = gr_ref[...] + bcol_ref[...]
    f_c = pltpu.roll(_scan(_log_sigmoid(gc), 0, jnp.add, 0.0), LANE - nh, 1)
    m_prev = m_ref[...]
    m_t = f_c + jnp.maximum(m_prev, _scan(gc - f_c, 0, jnp.maximum, NEG))
    w_inter = jnp.exp(f_c + m_prev - m_t)
    e_neg_m = jnp.exp(-m_t)
    f_minus_m = f_c - m_t
    m_new = m_t[L - 1:L, :]
    f_last = f_c[L - 1:L, :]
    w_last = jnp.exp(f_last - f_c + gc - m_new) * scale
    decay = jnp.exp(f_last + m_prev - m_new)
    m_ref[...] = m_new
    f_r = _scan(_log_sigmoid(gr), 1, jnp.add, 0.0)
    a_r = gr[0:nh, :] - f_r[nh:2 * nh, :]
    row = lax.broadcasted_iota(jnp.int32, (L, L), 0)
    col = lax.broadcasted_iota(jnp.int32, (L, L), 1)
    tri = col <= row
    ones = jnp.ones((L, ML_DQK), BF)
    nt = (((1,), (1,)), ((), ()))
    tn = (((0,), (0,)), ((), ()))
    for h in range(nh):
        lane = lambda x: x[:, h:h + 1]
        d_w = jnp.exp(jnp.where(tri, lane(f_minus_m) + a_r[h:h + 1, :], NEG))
        q = q_ref[:, h * ML_DQK:(h + 1) * ML_DQK]
        k = k_ref[:, h * ML_DQK:(h + 1) * ML_DQK]
        v = v_ref[:, h * ML_DV:(h + 1) * ML_DV]
        s = (lax.dot_general(q, k, nt, preferred_element_type=F32) * (scale * d_w)).astype(BF)
        c_old = c_ref[h]
        nn_old = nn_ref[h]
        wi = lane(w_inter)
        num = (jnp.dot(s, v, preferred_element_type=F32)
               + wi * jnp.dot(q, c_old.astype(BF), preferred_element_type=F32))
        den = (jnp.dot(s, ones, preferred_element_type=F32)
               + wi * jnp.dot(q, nn_old.astype(BF), preferred_element_type=F32))
        inv = 1.0 / jnp.maximum(jnp.abs(den[:, 0:1]), lane(e_neg_m))
        og = o_ref[:, h * ML_DV:(h + 1) * ML_DV].astype(F32)
        xa_ref[:, h * ML_DV:(h + 1) * ML_DV] = (_sigmoid(og) * (num * inv)).astype(xa_ref.dtype)
        kw = (k.astype(F32) * lane(w_last)).astype(BF)
        dk = decay[:, h:h + 1]
        c_ref[h] = dk * c_old + lax.dot_general(kw, v, tn, preferred_element_type=F32)
        nn_ref[h] = dk * nn_old + lax.dot_general(kw, ones, tn, preferred_element_type=F32)

    @pl.when(chunk == pl.num_programs(1) - 1)
    def _():
        for h in range(nh):
            n_ref[h:h + 1, :] = nn_ref[h].T[0:1, :]


def _mlstm_prompt(p, gates, b_i, b_f, batch, seq):
    m_rows = p.shape[0]
    L = ML_L
    nc = seq // L
    bias = jnp.concatenate([b_i, b_f]).astype(F32)
    bias_row = jnp.zeros((1, N_GATE), F32).at[0, :2 * ML_HEADS].set(bias)
    bias_col = bias.reshape(2 * ML_HEADS, 1)
    gates_t = gates[:, :2 * ML_HEADS].T
    hq, hv = ML_HEADS * ML_DQK, ML_HEADS * ML_DV
    return pl.pallas_call(
        _mlstm_kernel,
        grid=(batch, nc),
        in_specs=[pl.BlockSpec((L, hq), lambda b, c: (b * nc + c, C_QA // hq)),
                  pl.BlockSpec((L, hq), lambda b, c: (b * nc + c, C_KA // hq)),
                  pl.BlockSpec((L, hv), lambda b, c: (b * nc + c, C_VA // hv)),
                  pl.BlockSpec((L, hv), lambda b, c: (b * nc + c, C_OA // hv)),
                  pl.BlockSpec((L, N_GATE), lambda b, c: (b * nc + c, 0)),
                  pl.BlockSpec((2 * ML_HEADS, L), lambda b, c: (0, b * nc + c)),
                  pl.BlockSpec((1, N_GATE), lambda b, c: (0, 0)),
                  pl.BlockSpec((2 * ML_HEADS, 1), lambda b, c: (0, 0))],
        out_specs=[pl.BlockSpec((L, hv), lambda b, c: (b * nc + c, 0)),
                   pl.BlockSpec((None, ML_HEADS, ML_DQK, ML_DV), lambda b, c: (b, 0, 0, 0)),
                   pl.BlockSpec((None, ML_HEADS, ML_DQK), lambda b, c: (b, 0, 0)),
                   pl.BlockSpec((None, 1, LANE), lambda b, c: (b, 0, 0))],
        out_shape=[jax.ShapeDtypeStruct((m_rows, hv), BF),
                   jax.ShapeDtypeStruct((batch, ML_HEADS, ML_DQK, ML_DV), F32),
                   jax.ShapeDtypeStruct((batch, ML_HEADS, ML_DQK), F32),
                   jax.ShapeDtypeStruct((batch, 1, LANE), F32)],
        scratch_shapes=[pltpu.VMEM((ML_HEADS, ML_DQK, ML_DQK), F32)],
        compiler_params=_cparams("parallel", "arbitrary"),
        name="mlstm_prompt",
    )(p, p, p, p, gates, gates_t, bias_row, bias_col)


def _rope_table(pos):
    half = SW_ROT // 2
    freq = ROPE_THETA ** (-jnp.arange(half, dtype=F32) * 2.0 / SW_ROT)
    ang = pos.astype(F32)[:, None] * freq[None, :]
    cos, sin = jnp.cos(ang), jnp.sin(ang)
    t = pos.shape[0]
    return jnp.concatenate([cos, cos, jnp.ones((t, SW_HD - SW_ROT), F32),
                            -sin, sin, jnp.zeros((t, SW_HD - SW_ROT), F32)], axis=1)


def _rope(x, tab):
    half = SW_ROT // 2
    lane = lax.broadcasted_iota(jnp.int32, x.shape, 1)
    partner = jnp.where(lane < SW_ROT, lane ^ half, lane)
    return x * tab[:, 0:SW_HD] + jnp.take_along_axis(x, partner, axis=1) * tab[:, SW_HD:2 * SW_HD]


def _attend(q, k_own, v_own, k_prev, v_prev, prev_shift):
    blk = SW_BLK
    row = lax.broadcasted_iota(jnp.int32, (1, blk, blk), 1)
    col = lax.broadcasted_iota(jnp.int32, (1, blk, blk), 2)
    qk = (((2,), (2,)), ((0,), (0,)))
    pv = (((2,), (1,)), ((0,), (0,)))
    s_o = jnp.where(col <= row, lax.dot_general(q, k_own, qk, preferred_element_type=F32), NEG)
    s_p = jnp.where(col >= row + prev_shift, lax.dot_general(q, k_prev, qk, preferred_element_type=F32), NEG)
    mx = jnp.maximum(jnp.max(s_o, axis=2, keepdims=True), jnp.max(s_p, axis=2, keepdims=True))
    p_o = jnp.exp(s_o - mx)
    p_p = jnp.exp(s_p - mx)
    den = jnp.sum(p_o, axis=2, keepdims=True) + jnp.sum(p_p, axis=2, keepdims=True)
    acc = (lax.dot_general(p_o.astype(BF), v_own, pv, preferred_element_type=F32)
           + lax.dot_general(p_p.astype(BF), v_prev, pv, preferred_element_type=F32))
    return acc, mx, den


def _swa_kernel(q1, k1, v1, k1p, v1p, q2, k2, v2, k2p, v2p, q3, k3, v3, k3p, v3p, xb_ref, o_acc, m_acc, l_acc):
    blk = SW_BLK
    u = SWA_UNITS
    first_shift = jnp.where(pl.program_id(1) > 0, 0, blk)
    wide = lambda x: jnp.broadcast_to(x, x.shape[:-1] + (SW_HD,))

    def merge(rows, acc, mx, den):
        m_old = m_acc[rows, :]
        m_new = jnp.maximum(m_old, mx)
        a = jnp.exp(m_old - m_new)
        b = jnp.exp(mx - m_new)
        o_acc[rows, :] = a * o_acc[rows, :] + b * acc
        l_acc[rows, :] = a * l_acc[rows, :] + b * den
        m_acc[rows, :] = m_new

    def g1_batch(rows, k_p, v_p, shift):
        split = lambda x: x.reshape(u, blk, SW_HD)
        acc, mx, den = _attend(split(q1[rows, :]), split(k1[rows, :]), split(v1[rows, :]),
                               split(k_p), split(v_p), shift)
        o_acc[rows, :] = acc.reshape(u * blk, SW_HD)
        m_acc[rows, :] = wide(mx).reshape(u * blk, SW_HD)
        l_acc[rows, :] = wide(den).reshape(u * blk, SW_HD)

    head = pl.ds(0, (u - 1) * blk)
    unit = lax.broadcasted_iota(jnp.int32, (u, 1, 1), 0)
    g1_batch(pl.ds(0, u * blk),
             jnp.concatenate([k1p[...], k1[head, :]], axis=0), jnp.concatenate([v1p[...], v1[head, :]], axis=0),
             jnp.where(unit == 0, first_shift, 0))

    def g1_body(i, c):
        rows = pl.ds(pl.multiple_of(i * u * blk, blk), u * blk)
        prev = pl.ds(pl.multiple_of(i * u * blk - blk, blk), u * blk)
        g1_batch(rows, k1[prev, :], v1[prev, :], 0)
        return c

    lax.fori_loop(1, q1.shape[0] // (u * blk), g1_body, 0)

    def dilated(q, k, v, kp, vp):
        dil = q.shape[0]
        u = min(SWA_UNITS, dil)
        n_sub = q.shape[1] // blk

        def batch(sub, r0, k_p, v_p, shift):
            rs = pl.ds(r0, u)
            rows = pl.ds(pl.multiple_of(sub * blk, blk), blk)
            acc, mx, den = _attend(q[rs, rows, :], k[rs, rows, :], v[rs, rows, :], k_p, v_p, shift)
            for i in range(u):
                merge(pl.ds(sub * blk * dil + r0 + i, blk, stride=dil), acc[i], wide(mx[i]), wide(den[i]))

        def first(i, c):
            rs = pl.ds(i * u, u)
            batch(0, i * u, kp[rs], vp[rs], first_shift)
            return c

        lax.fori_loop(0, dil // u, first, 0)

        def rest(i, c):
            sub = 1 + i // (dil // u)
            r0 = (i % (dil // u)) * u
            prev = pl.ds(pl.multiple_of((sub - 1) * blk, blk), blk)
            batch(sub, r0, k[pl.ds(r0, u), prev, :], v[pl.ds(r0, u), prev, :], 0)
            return c

        lax.fori_loop(0, (n_sub - 1) * (dil // u), rest, 0)

    dilated(q2, k2, v2, k2p, v2p)
    dilated(q3, k3, v3, k3p, v3p)
    xb_ref[...] = (o_acc[...] * (1.0 / l_acc[...])).astype(xb_ref.dtype)


def _swa_prompt(qkvs, batch, seq):
    blk = SW_BLK
    span = SW_GROUPS[-1][0]
    assert seq % span == 0 and all(w // d == blk for w, d in SW_GROUPS)
    nspan = seq // span
    in_specs = []
    args = []
    for (win, dil), qkv in zip(SW_GROUPS, qkvs):
        rows = span // dil
        per = rows // blk
        own = lambda c: pl.BlockSpec((None, dil, rows, SW_HD), lambda b, s, h, c=c: (b, 0, s, c * SW_HPG + h))
        prev = lambda c: pl.BlockSpec(
            (None, dil, blk, SW_HD), lambda b, s, h, c=c, per=per: (b, 0, jnp.maximum(s * per - 1, 0), c * SW_HPG + h))
        if dil == 1:
            own = lambda c: pl.BlockSpec((None, None, span, SW_HD), lambda b, s, h, c=c: (b, 0, s, c * SW_HPG + h))
            prev = lambda c: pl.BlockSpec(
                (None, None, blk, SW_HD),
                lambda b, s, h, c=c, per=per: (b, 0, jnp.maximum(s * per - 1, 0), c * SW_HPG + h))
        in_specs += [own(0), own(1), own(2), prev(1), prev(2)]
        args += [qkv] * 5
    return pl.pallas_call(
        _swa_kernel,
        grid=(batch, nspan, SW_HPG),
        in_specs=in_specs,
        out_specs=pl.BlockSpec((span, SW_HD), lambda b, s, h: (b * nspan + s, h)),
        out_shape=jax.ShapeDtypeStruct((batch * seq, SW_HPG * SW_HD), BF),
        scratch_shapes=[pltpu.VMEM((span, SW_HD), F32)] * 3,
        compiler_params=_cparams("parallel", "parallel", "parallel"),
        name="swa_prompt",
    )(*args)


def _kv_rows_kernel(k_ref, v_ref, o_ref, *, dil):
    for r in range(dil):
        rows = pl.ds(r, SW_BLK, stride=dil) if dil > 1 else pl.ds(0, SW_BLK)
        for hh in range(SW_HPG):
            sl = slice(hh * SW_HD, (hh + 1) * SW_HD)
            o_ref[hh, rows, :] = k_ref[r, :, sl].astype(F32)
            o_ref[SW_HPG + hh, rows, :] = v_ref[r, :, sl].astype(F32)


def _kv_rows(qkv, gi, batch, seq):
    win, dil = SW_GROUPS[gi]
    hw = SW_HPG * SW_HD
    last = seq // dil // SW_BLK - 1
    out = pl.pallas_call(
        functools.partial(_kv_rows_kernel, dil=dil),
        grid=(batch,),
        in_specs=[pl.BlockSpec((None, dil, SW_BLK, hw), lambda b: (b, 0, last, 1)),
                  pl.BlockSpec((None, dil, SW_BLK, hw), lambda b: (b, 0, last, 2))],
        out_specs=pl.BlockSpec((None, 2 * SW_HPG, win, SW_HD), lambda b: (b, 0, 0, 0)),
        out_shape=jax.ShapeDtypeStruct((batch, 2 * SW_HPG, win, SW_HD), F32),
        compiler_params=_cparams("parallel"),
        name=f"kv_rows_w{win}",
    )(qkv, qkv)
    return out.reshape(batch, 2, SW_HPG, win, SW_HD).transpose(0, 3, 1, 2, 4)


def _cm_kernel(u_ref, v_ref, gv_ref, ws_ref, bs_ref, o_ref):
    ch = CM_CHUNK
    row = lax.broadcasted_iota(jnp.int32, (ch, ch), 0)
    col = lax.broadcasted_iota(jnp.int32, (ch, ch), 1)
    tri = col <= row
    for g in range(CM_GROUPS):
        sl = slice(g * CM_GD, (g + 1) * CM_GD)
        w = jnp.where(tri, ws_ref[g], 0.0).astype(BF)
        for c in range(u_ref.shape[0] // ch):
            rows = slice(c * ch, (c + 1) * ch)
            vg = v_ref[rows, sl].astype(F32)
            vn = vg * lax.rsqrt(jnp.mean(vg * vg, axis=1, keepdims=True) + RMS_EPS) * gv_ref[g:g + 1, :]
            mixed = jnp.dot(w, vn.astype(BF), preferred_element_type=F32) + bs_ref[:, g:g + 1]
            o_ref[rows, sl] = (u_ref[rows, sl].astype(F32) * mixed).astype(o_ref.dtype)


def _cm_prompt(p, g_v, w_s, b_s, l):
    m_rows = p.shape[0]
    ch = CM_CHUNK
    width = CM_GROUPS * CM_GD
    rows = 8 * ch
    return pl.pallas_call(
        _cm_kernel,
        grid=(m_rows // rows,),
        in_specs=[pl.BlockSpec((rows, width), lambda i: (i, C_UC // width)),
                  pl.BlockSpec((rows, width), lambda i: (i, C_VC // width)),
                  pl.BlockSpec((None, CM_GROUPS, CM_GD), lambda i: (l, 0, 0)),
                  pl.BlockSpec((None, CM_GROUPS, ch, ch), lambda i: (l, 0, 0, 0)),
                  pl.BlockSpec((ch, CM_GROUPS), lambda i: (0, 0))],
        out_specs=pl.BlockSpec((rows, width), lambda i: (i, 0)),
        out_shape=jax.ShapeDtypeStruct((m_rows, width), BF),
        compiler_params=_cparams("parallel"),
        name="cm_prompt",
    )(p, p, g_v, w_s, b_s[l].T)


def _mlstm_step_kernel(qr_ref, kr_ref, vr_ref, or_ref, qkc_ref, g_ref, brow_ref, m0_ref, c0_ref, n0_ref,
                       xa_ref, c1_ref, n1_ref, m1_ref):
    scale = ML_DQK ** -0.5
    gates = g_ref[...] + brow_ref[...]
    m0 = m0_ref[...]
    for h in range(ML_HEADS):
        ig = gates[:, h:h + 1]
        lf = _log_sigmoid(gates[:, ML_HEADS + h:ML_HEADS + h + 1])
        m_prev = m0[:, h:h + 1]
        inter = lf + m_prev
        m_t = jnp.maximum(inter, ig)
        d_w = jnp.exp(ig - m_t)
        w_inter = jnp.exp(inter - m_t)
        q_row = qr_ref[h:h + 1, :]
        k_row = kr_ref[h:h + 1, :]
        v_row = vr_ref[h:h + 1, :]
        q_col = qkc_ref[:, h:h + 1]
        k_col = qkc_ref[:, ML_HEADS + h:ML_HEADS + h + 1]
        c_old = c0_ref[h]
        n_old = n0_ref[h:h + 1, :]
        s = jnp.sum(q_row * k_row, axis=1, keepdims=True) * scale * d_w
        qc = jnp.sum(q_col * c_old, axis=0, keepdims=True)
        num = s * v_row + w_inter * qc
        den = s + w_inter * jnp.sum(q_row * n_old, axis=1, keepdims=True)
        hh = num / jnp.maximum(jnp.abs(den), jnp.exp(-m_t))
        xa_ref[:, h * ML_DV:(h + 1) * ML_DV] = _sigmoid(or_ref[h:h + 1, :]) * hh
        w_last = d_w * scale
        decay = w_inter
        c1_ref[h] = decay * c_old + (k_col * w_last) * v_row
        n1_ref[h:h + 1, :] = decay * n_old + w_last * k_row
        m1_ref[h:h + 1, :] = jnp.broadcast_to(m_t, (1, ML_DQK))


def _mlstm_sample(ps, gates, b_i, b_f, c0, n0, m0, l):
    nb = c0.shape[1]
    q = ps[:nb, C_QA:C_QA + 1024].reshape(nb, ML_HEADS, ML_DQK)
    k = ps[:nb, C_KA:C_KA + 1024].reshape(nb, ML_HEADS, ML_DQK)
    v = ps[:nb, C_VA:C_VA + 2048].reshape(nb, ML_HEADS, ML_DV)
    o = ps[:nb, C_OA:C_OA + 2048].reshape(nb, ML_HEADS, ML_DV)
    qk_col = jnp.concatenate([q, k], axis=1).transpose(0, 2, 1)
    bias = jnp.concatenate([b_i, b_f]).astype(F32)
    bias_row = jnp.zeros((1, N_GATE), F32).at[0, :2 * ML_HEADS].set(bias)
    g3 = gates[:nb].reshape(nb, 1, N_GATE)
    m03 = m0[l].reshape(nb, 1, ML_HEADS)
    per_b3 = lambda shape: pl.BlockSpec((None,) + shape, lambda b: (b, 0, 0))
    return pl.pallas_call(
        _mlstm_step_kernel,
        grid=(nb,),
        in_specs=[per_b3((ML_HEADS, ML_DQK)), per_b3((ML_HEADS, ML_DQK)),
                  per_b3((ML_HEADS, ML_DV)), per_b3((ML_HEADS, ML_DV)),
                  per_b3((ML_DQK, 2 * ML_HEADS)), per_b3((1, N_GATE)),
                  pl.BlockSpec((1, N_GATE), lambda b: (0, 0)),
                  per_b3((1, ML_HEADS)),
                  pl.BlockSpec((None, None, ML_HEADS, ML_DQK, ML_DV), lambda b: (l, b, 0, 0, 0)),
                  pl.BlockSpec((None, None, ML_HEADS, ML_DQK), lambda b: (l, b, 0, 0))],
        out_specs=[per_b3((1, ML_HEADS * ML_DV)),
                   pl.BlockSpec((None, ML_HEADS, ML_DQK, ML_DV), lambda b: (b, 0, 0, 0)),
                   per_b3((ML_HEADS, ML_DQK)), per_b3((ML_HEADS, ML_DQK))],
        out_shape=[jax.ShapeDtypeStruct((nb, 1, ML_HEADS * ML_DV), F32),
                   jax.ShapeDtypeStruct((nb, ML_HEADS, ML_DQK, ML_DV), F32),
                   jax.ShapeDtypeStruct((nb, ML_HEADS, ML_DQK), F32),
                   jax.ShapeDtypeStruct((nb, ML_HEADS, ML_DQK), F32)],
        compiler_params=_cparams("parallel"),
        name="mlstm_sample",
    )(q, k, v, o, qk_col, g3, bias_row, m03, c0, n0)


def _swa_step_kernel(g0_ref, g1_ref, g2_ref, c0_ref, c1_ref, c2_ref, xb_ref):
    hw = SW_HPG * SW_HD
    per_pos = 2 * SW_HPG
    news = (g0_ref, g1_ref, g2_ref)
    caches = (c0_ref, c1_ref, c2_ref)
    for hh in range(SW_HPG):
        m_run = l_run = acc = None
        for gi in range(len(SW_GROUPS)):
            lo = hh * SW_HD
            q = news[gi][:, lo:lo + SW_HD]
            k_new = news[gi][:, hw + lo:hw + lo + SW_HD]
            v_new = news[gi][:, 2 * hw + lo:2 * hw + lo + SW_HD]
            kc = caches[gi][:, hh, :]
            vc = caches[gi][:, SW_HPG + hh, :]
            s_c = jnp.sum(kc * q, axis=1, keepdims=True)
            s_n = jnp.sum(k_new * q, axis=1, keepdims=True)
            mx = jnp.maximum(jnp.max(s_c, axis=0, keepdims=True), s_n)
            p_c = jnp.exp(s_c - mx)
            p_n = jnp.exp(s_n - mx)
            den = jnp.sum(p_c, axis=0, keepdims=True) + p_n
            part = jnp.sum(p_c * vc, axis=0, keepdims=True) + p_n * v_new
            if gi == 0:
                m_run, l_run, acc = mx, den, part
            else:
                top = jnp.maximum(m_run, mx)
                a, b = jnp.exp(m_run - top), jnp.exp(mx - top)
                m_run, l_run, acc = top, a * l_run + b * den, a * acc + b * part
        xb_ref[:, hh * SW_HD:(hh + 1) * SW_HD] = acc / l_run


def _swa_sample(qkvs, caches, l):
    nb = caches[0].shape[1]
    hw = SW_HPG * SW_HD
    per_pos = 2 * SW_HPG
    news, views, specs = [], [], []
    for (win, dil), cache, qkv in zip(SW_GROUPS, caches, qkvs):
        assert cache.shape[2] == win and win // dil == SW_BLK
        news.append(qkv[0, 0, :nb].reshape(nb, 1, 3 * hw))
        views.append(cache.reshape(cache.shape[0], nb, win // dil, dil * per_pos, SW_HD))
        specs.append(pl.BlockSpec((None, None, SW_BLK, per_pos, SW_HD), lambda b: (l, b, 0, 0, 0)))
    row3 = pl.BlockSpec((None, 1, 3 * hw), lambda b: (b, 0, 0))
    xb = pl.pallas_call(
        _swa_step_kernel,
        grid=(nb,),
        in_specs=[row3, row3, row3] + specs,
        out_specs=pl.BlockSpec((None, 1, hw), lambda b: (b, 0, 0)),
        out_shape=jax.ShapeDtypeStruct((nb, 1, hw), F32),
        compiler_params=_cparams("parallel"),
        name="swa_sample",
    )(*news, *views)
    kvs = [jnp.stack([n[:, :, hw:2 * hw].reshape(nb, 1, SW_HPG, SW_HD),
                      n[:, :, 2 * hw:].reshape(nb, 1, SW_HPG, SW_HD)], axis=2) for n in news]
    return xb.reshape(nb, hw), kvs


def _cm_step_kernel(u_ref, v_ref, gv_ref, ws_ref, bs_ref, o_ref, vn_ref):
    for g in range(CM_GROUPS):
        sl = slice(g * CM_GD, (g + 1) * CM_GD)
        vg = v_ref[:, sl]
        vn = vg * lax.rsqrt(jnp.mean(vg * vg, axis=1, keepdims=True) + RMS_EPS) * gv_ref[g:g + 1, :]
        vn_ref[:, sl] = vn
        w00 = ws_ref[g][0:1, 0:1]
        o_ref[:, sl] = u_ref[:, sl] * (w00 * vn + bs_ref[g:g + 1, 0:1])


def _cm_sample(ps, g_v, w_s, b_s, l):
    rows = ps.shape[0]
    width = CM_GROUPS * CM_GD
    ch = CM_CHUNK
    return pl.pallas_call(
        _cm_step_kernel,
        grid=(1,),
        in_specs=[pl.BlockSpec((rows, width), lambda i: (0, C_UC // width)),
                  pl.BlockSpec((rows, width), lambda i: (0, C_VC // width)),
                  pl.BlockSpec((None, CM_GROUPS, CM_GD), lambda i: (l, 0, 0)),
                  pl.BlockSpec((None, CM_GROUPS, ch, ch), lambda i: (l, 0, 0, 0)),
                  pl.BlockSpec((None, CM_GROUPS, ch), lambda i: (l, 0, 0))],
        out_specs=[pl.BlockSpec((rows, width), lambda i: (0, 0))] * 2,
        out_shape=[jax.ShapeDtypeStruct((rows, width), F32)] * 2,
        compiler_params=_cparams("arbitrary"),
        name="cm_sample",
    )(ps, ps, g_v, w_s, b_s)


def _in_proj(hp, hs, w_in, l, tabs_p, tabs_s, batch, dils):
    wt = jnp.swapaxes(w_in, 1, 2)
    p_a, s_a = _mm_t(hp, hs, wt, l, 1024, 0, O_GATES // 1024, BF, "in_proj_a")
    p_g, s_g = _mm_t(hp, hs, wt, l, N_GATE, O_GATES, 1, F32, "gate_proj")
    p_qkv = [_qkv_proj(hp, wt, l, gi, dil, tabs_p, batch, BF, f"qkv_proj{gi}") for gi, dil in enumerate(dils)]
    s_qkv = [_qkv_proj(hs, wt, l, gi, 1, tabs_s, 1, F32, f"qkv_proj{gi}_s") for gi in range(len(dils))]
    p_r, s_r = _mm_t(hp, hs, wt, l, 1024, O_UC, N_REST // 1024, BF, "in_proj_r")
    return (p_a, p_g, p_qkv, p_r), (s_a, s_g, s_qkv, s_r)


def _dense_tail(xs, mixers, ps, l, mods, rows_per_batch, wts, gain, mods_next):
    w_down = _cast_bf16(wts["w_ffn_down"], l)
    merged = _merge(mixers[0] + (ps[0],), mixers[1] + (ps[1],), wts["w_br_a"], wts["w_br_b"], wts["w_br_c"], l)
    (x_p, h2_p), (x_s, h2_s) = _mm_res_norm(merged, wts["w_out"], l, xs, mods, 2, rows_per_batch,
                                            wts["g_norm2"][l], mods, 4, 3, "out_proj", ride=False)
    acts = _ffn1(h2_p, h2_s, wts["w_ffn_gate_up"], l)
    return _mm_res_norm(acts, w_down, l, (x_p, x_s), mods, 5, rows_per_batch, gain, mods_next, 1, 0, "ffn_down",
                        ride=False)


def kernel(x_prompt, x_sample, state_mlstm_C, state_mlstm_n, state_mlstm_m, cache_swa_kv_w128, cache_swa_kv_w512, cache_swa_kv_w2048, c_prompt, c_sample, w_ada, b_ada, g_norm1, g_norm2, g_final, w_in, b_ml_i, b_ml_f, g_cm_v, w_s, b_s, w_br_a, w_br_b, w_br_c, w_out, w_ffn_gate_up, w_ffn_down):
    bp, tp, d = x_prompt.shape
    ns, ts, _ = x_sample.shape
    depth = w_in.shape[0]
    past_len = 16384
    assert d == D_MODEL and ts == 1 and ns == 8 and tp % max(ML_L, SW_GROUPS[-1][0]) == 0
    s_rows = 16
    wts = dict(w_br_a=w_br_a, w_br_b=w_br_b, w_br_c=w_br_c, w_out=w_out, g_norm2=g_norm2,
               w_ffn_gate_up=w_ffn_gate_up, w_ffn_down=w_ffn_down)

    c_all = jnp.zeros((s_rows, d), F32).at[:ns].set(c_sample).at[ns:ns + bp].set(c_prompt)
    mod_all = _ada(c_all, w_ada, b_ada)
    tabs_p = _rope_table(jnp.arange(tp, dtype=jnp.int32))
    tabs_s = _rope_table(jnp.full((s_rows,), past_len, jnp.int32))
    caches = (cache_swa_kv_w128, cache_swa_kv_w512, cache_swa_kv_w2048)
    dils = [dil for _, dil in SW_GROUPS]

    xp = x_prompt.reshape(bp * tp, d)
    xs = jnp.zeros((s_rows, d), F32).at[:ns].set(x_sample.reshape(ns, d))
    p_c, p_n, p_m, p_kv = [], [], [], [[], [], []]
    s_c, s_n, s_m, s_kv, s_v = [], [], [], [[], [], []], []
    mods_p = [mod_all[l, ns:ns + bp].reshape(bp, 1, 6 * d) for l in range(depth)]
    mods_s = [mod_all[l] for l in range(depth)]
    hp = _norm_mod(xp, g_norm1[0], mods_p[0], 1, 0, tp)
    hs = _norm_mod(xs, g_norm1[0], mods_s[0], 1, 0, None)
    for l in range(depth):
        last = l == depth - 1
        (p_a, gates, qkvs, p_r), (s_a, s_gates, s_qkvs, s_r) = _in_proj(hp, hs, w_in, l, tabs_p, tabs_s, bp, dils)
        xa, c1, n1, m1 = _mlstm_prompt(p_a, gates, b_ml_i[l], b_ml_f[l], bp, tp)
        xb = _swa_prompt(qkvs, bp, tp)
        kvs = [_kv_rows(qkvs[gi], gi, bp, tp) for gi in range(len(SW_GROUPS))]
        xc = _cm_prompt(p_r, g_cm_v, w_s, b_s, l)
        mix_p = (xa, xb, xc)
        p_c.append(c1)
        p_n.append(n1)
        p_m.append(m1[:, 0, :ML_HEADS])
        for gi in range(3):
            p_kv[gi].append(kvs[gi])
        xa, c1, n1, m1 = _mlstm_sample(s_a, s_gates, b_ml_i[l], b_ml_f[l], state_mlstm_C, state_mlstm_n,
                                       state_mlstm_m, l)
        xb, kvs = _swa_sample(s_qkvs, caches, l)
        xc, vn = _cm_sample(s_r, g_cm_v, w_s, b_s, l)
        pad = lambda a: jnp.zeros((s_rows, a.shape[-1]), F32).at[:ns].set(a.reshape(ns, -1))
        mix_s = (pad(xa), pad(xb), xc)
        outs = _dense_tail((xp, xs), (mix_p, mix_s), (p_r, s_r), l, (mods_p[l], mods_s[l]), tp, wts,
                           g_final if last else g_norm1[l + 1],
                           None if last else (mods_p[l + 1], mods_s[l + 1]))
        if last:
            y_p = outs[0].reshape(bp, tp, d)
            y_s = outs[1][:ns].reshape(ns, ts, d)
        else:
            (xp, hp), (xs, hs) = outs
        s_c.append(c1)
        s_n.append(n1)
        s_m.append(m1[:, :, 0])
        for gi in range(3):
            s_kv[gi].append(kvs[gi])
        s_v.append(vn[:ns].reshape(ns, 1, CM_GROUPS, CM_GD))
    st = jnp.stack
    return (y_p, y_s, st(p_c), st(p_n), st(p_m), st(p_kv[0]), st(p_kv[1]), st(p_kv[2]),
            st(s_c), st(s_n), st(s_m), st(s_kv[0]), st(s_kv[1]), st(s_kv[2]), st(s_v))
```
